```python
import jax, jax.numpy as jnp
from jax import lax
import numpy as np

D_MODEL = 2048
BATCH = 4
SEQ = 2048
DEPTH = 1
DEC_BATCH = 128
DEC_SEQ = 4
PAST_LEN = 16384
PAGE_SIZE = 128

A_HEADS = 8
A_HEAD_DIM = 128
A_WIDTH = A_HEADS * A_HEAD_DIM
A_CHUNK = 128
B_QK_HEADS = 4
B_V_HEADS = 8
B_HEAD_DIM = 128
B_KEY_WIDTH = B_QK_HEADS * B_HEAD_DIM
B_VAL_WIDTH = B_V_HEADS * B_HEAD_DIM
B_CONV = 4
B_CONV_CH = 2 * B_KEY_WIDTH + B_VAL_WIDTH
DN_CHUNK = 64
MIX_WIDTH = A_WIDTH + B_VAL_WIDTH
IN_WIDTH = 2 * A_WIDTH + B_CONV_CH + B_VAL_WIDTH + 2 * B_V_HEADS
N_GROUPS = 4
EXPERTS_PER_GROUP = 8
N_EXPERTS = N_GROUPS * EXPERTS_PER_GROUP
TOP_K = 2
D_EXPERT = 512
MOE_BLOCK = 128
ALPHA = (2 * DEPTH) ** 0.25
BETA_INIT = (8 * DEPTH) ** -0.25
LN_EPS = 1e-5
RMS_EPS = 1e-6

kernel_name = 'hymba_gmlp_gdn_hmoe_step'


def _layer_norm(x, g, b):
    xf = x.astype(jnp.float32)
    mu = jnp.mean(xf, -1, keepdims=True)
    var = jnp.mean(jnp.square(xf - mu), -1, keepdims=True)
    return ((xf - mu) * lax.rsqrt(var + LN_EPS) * g.astype(jnp.float32) + b.astype(jnp.float32)).astype(x.dtype)


def _rms_norm(x, g):
    xf = x.astype(jnp.float32)
    return xf * lax.rsqrt(jnp.mean(jnp.square(xf), -1, keepdims=True) + RMS_EPS) * g.astype(jnp.float32)


def _l2norm(x):
    xf = x.astype(jnp.float32)
    return xf * lax.rsqrt(jnp.sum(jnp.square(xf), -1, keepdims=True) + 1e-6)


def _causal_conv(x, buf, w):
    t = x.shape[1]
    xx = jnp.concatenate([buf.astype(x.dtype), x], axis=1)
    y = xx[:, 0:t] * w[0]
    for i in range(1, B_CONV):
        y = y + xx[:, i:i + t] * w[i]
    return jax.nn.silu(y), xx[:, -(B_CONV - 1):]


def _chunk_mlp_mixer(xa, a_ws, a_bs, a_norm_g, a_norm_b):
    bn, t, _ = xa.shape
    u = jax.nn.gelu(xa[..., :A_WIDTH])
    v = _layer_norm(jax.nn.gelu(xa[..., A_WIDTH:]), a_norm_g, a_norm_b)
    v = v.reshape(bn, t, A_HEADS, A_HEAD_DIM)
    c = min(A_CHUNK, t)
    pad = (-t) % c
    nc = (t + pad) // c
    vc = jnp.pad(v, ((0, 0), (0, pad), (0, 0), (0, 0))).reshape(bn, nc, c, A_HEADS, A_HEAD_DIM)
    w = jnp.where(jnp.tril(jnp.ones((c, c), bool)), a_ws[:, :c, :c], 0.0).astype(v.dtype)
    s = jnp.einsum('hij,bnjhd->bnihd', w, vc) + a_bs[:, :c].T[None, None, :, :, None]
    s = s.reshape(bn, nc * c, A_WIDTH)[:, :t]
    return u * s, v


def _gated_delta_rule(q, k, v, g, beta, s0):
    bn, t, h, dk = q.shape
    dv = v.shape[-1]
    c = min(DN_CHUNK, t)
    pad = (-t) % c
    nc = (t + pad) // c
    p4 = ((0, 0), (0, pad), (0, 0), (0, 0))
    p3 = ((0, 0), (0, pad), (0, 0))

    def chunks(a):
        a = a.reshape((bn, nc, c) + a.shape[2:])
        return jnp.moveaxis(jnp.moveaxis(a, 3, 2), 1, 0)

    qc = chunks(jnp.pad(q, p4))
    kc = chunks(jnp.pad(k, p4))
    vc = chunks(jnp.pad(v, p4))
    gc = jnp.cumsum(chunks(jnp.pad(g, p3)), axis=-1)
    bc = chunks(jnp.pad(beta, p3))
    causal = jnp.tril(jnp.ones((c, c), bool))
    strict = jnp.tril(jnp.ones((c, c), bool), -1)
    diff = gc[..., :, None] - gc[..., None, :]
    decay = jnp.where(causal, jnp.exp(jnp.where(causal, diff, 0.0)), 0.0)
    kb = kc * bc[..., None]
    a_mat = jnp.where(strict, jnp.einsum('nbhid,nbhjd->nbhij', kb, kc) * decay, 0.0)
    rhs = jnp.concatenate([vc * bc[..., None], kb * jnp.exp(gc)[..., None]], axis=-1)
    sol = lax.linalg.triangular_solve(a_mat, rhs, left_side=True, lower=True, unit_diagonal=True)
    u_val, w_k = sol[..., :dv], sol[..., dv:]
    qk = jnp.einsum('nbhid,nbhjd->nbhij', qc, kc) * decay
    q_dec = qc * jnp.exp(gc)[..., None]
    k_dec = kc * jnp.exp(gc[..., -1:] - gc)[..., None]
    g_last = jnp.exp(gc[..., -1])

    def step(s, inp):
        qk_c, u_c, w_c, qd_c, kd_c, gl_c = inp
        v_new = u_c - jnp.einsum('bhid,bhde->bhie', w_c, s)
        o = jnp.einsum('bhid,bhde->bhie', qd_c, s) + jnp.einsum('bhij,bhje->bhie', qk_c, v_new)
        s = s * gl_c[..., None, None] + jnp.einsum('bhid,bhie->bhde', kd_c, v_new)
        return s, o

    s_fin, o = lax.scan(step, s0, (qk, u_val, w_k, q_dec, k_dec, g_last))
    o = jnp.moveaxis(jnp.moveaxis(o, 0, 1), 2, 3).reshape(bn, nc * c, h, dv)[:, :t]
    return o, s_fin


def _gated_delta_mixer(xb, conv_buf, s0, conv_w, a_log, dt_bias, onorm_g):
    bn, t, _ = xb.shape
    qkv, new_buf = _causal_conv(xb[..., :B_CONV_CH], conv_buf, conv_w)
    o0 = B_CONV_CH + B_VAL_WIDTH
    z = xb[..., B_CONV_CH:o0]
    b_logit = xb[..., o0:o0 + B_V_HEADS]
    a_logit = xb[..., o0 + B_V_HEADS:o0 + 2 * B_V_HEADS]
    rep = B_V_HEADS // B_QK_HEADS
    q = _l2norm(qkv[..., :B_KEY_WIDTH].reshape(bn, t, B_QK_HEADS, B_HEAD_DIM)) * B_HEAD_DIM ** -0.5
    k = _l2norm(qkv[..., B_KEY_WIDTH:2 * B_KEY_WIDTH].reshape(bn, t, B_QK_HEADS, B_HEAD_DIM))
    v = qkv[..., 2 * B_KEY_WIDTH:].reshape(bn, t, B_V_HEADS, B_HEAD_DIM).astype(jnp.float32)
    q = jnp.repeat(q, rep, axis=2)
    k = jnp.repeat(k, rep, axis=2)
    beta = jax.nn.sigmoid(b_logit.astype(jnp.float32))
    g = -jnp.exp(a_log.astype(jnp.float32)) * jax.nn.softplus(a_logit.astype(jnp.float32) + dt_bias.astype(jnp.float32))
    o, s_new = _gated_delta_rule(q, k, v, g, beta, s0.astype(jnp.float32))
    o = _rms_norm(o, onorm_g) * jax.nn.silu(z.reshape(bn, t, B_V_HEADS, B_HEAD_DIM).astype(jnp.float32))
    return o.reshape(bn, t, B_VAL_WIDTH).astype(xb.dtype), new_buf, s_new


def _hier_moe(h, w_router_g, b_router_g, w_router_e, b_router_e, w_gate, w_up, w_down):
    bn, t, d = h.shape
    xf = h.reshape(-1, d)
    n = xf.shape[0]
    tok = jnp.arange(n, dtype=jnp.int32)
    lg = (xf @ w_router_g).astype(jnp.float32)
    pg = jax.nn.softmax(lg, axis=-1)
    sel_g = jnp.argmax(lg + b_router_g.astype(jnp.float32), axis=-1).astype(jnp.int32)
    p_sel = pg[tok, sel_g]
    le = (xf @ w_router_e).astype(jnp.float32).reshape(n, N_GROUPS, EXPERTS_PER_GROUP)
    pe = jax.nn.softmax(le[tok, sel_g], axis=-1)
    bias_e = b_router_e.astype(jnp.float32).reshape(N_GROUPS, EXPERTS_PER_GROUP)[sel_g]
    _, loc = lax.top_k(pe + bias_e, TOP_K)
    w_top = jnp.take_along_axis(pe, loc, axis=-1)
    gate = w_top / jnp.sum(w_top, -1, keepdims=True) * p_sel[:, None]
    eid = (sel_g[:, None] * EXPERTS_PER_GROUP + loc).astype(jnp.int32)
    m = n * TOP_K
    flat_e = eid.reshape(-1)
    flat_tok = jnp.repeat(tok, TOP_K)
    order = jnp.argsort(flat_e)
    se, stok, sgate = flat_e[order], flat_tok[order], gate.reshape(-1)[order]
    counts = jnp.zeros((N_EXPERTS,), jnp.int32).at[flat_e].add(1)
    padded = (counts + MOE_BLOCK - 1) // MOE_BLOCK * MOE_BLOCK
    start = jnp.cumsum(counts) - counts
    pend = jnp.cumsum(padded)
    pstart = pend - padded
    dest = pstart[se] + (jnp.arange(m, dtype=jnp.int32) - start[se])
    n_blocks = -(-m // MOE_BLOCK) + N_EXPERTS
    n_slots = n_blocks * MOE_BLOCK
    slot_tok = jnp.full((n_slots,), n, jnp.int32).at[dest].set(stok)
    x_pad = jnp.concatenate([xf, jnp.zeros((1, d), xf.dtype)], axis=0)
    xb = x_pad[slot_tok].reshape(n_blocks, MOE_BLOCK, d)
    block_e = jnp.minimum(jnp.searchsorted(pend, jnp.arange(n_blocks, dtype=jnp.int32) * MOE_BLOCK, side='right'), N_EXPERTS - 1)

    def expert_block(args):
        xblk, e = args
        return (jax.nn.silu(xblk @ w_gate[e]) * (xblk @ w_up[e])) @ w_down[e]

    yb = lax.map(expert_block, (xb, block_e)).reshape(n_slots, d)
    contrib = yb[dest] * sgate[:, None].astype(yb.dtype)
    out = jnp.zeros((n, d), yb.dtype).at[stok].add(contrib)
    return out.reshape(bn, t, d)


def _layer(x, c, conv_buf, s0, w_ada, b_ada, w_in, a_ws, a_bs, a_norm_g, a_norm_b, b_conv_w, b_a_log,
           b_dt_bias, b_onorm_g, w_out, ln1_g, ln1_b, w_router_g, b_router_g, w_router_e, b_router_e,
           w_gate, w_up, w_down, ln2_g, ln2_b):
    mod = jax.nn.silu(c) @ w_ada + b_ada
    sh1, sc1, g1, sh2, sc2, g2 = jnp.split(mod[:, None, :], 6, axis=-1)
    h = x * (1 + sc1) + sh1
    proj = h @ w_in
    a_out, a_v = _chunk_mlp_mixer(proj[..., :2 * A_WIDTH], a_ws, a_bs, a_norm_g, a_norm_b)
    b_out, new_buf, s_new = _gated_delta_mixer(proj[..., 2 * A_WIDTH:], conv_buf, s0, b_conv_w, b_a_log, b_dt_bias, b_onorm_g)
    mix = jnp.concatenate([a_out, b_out], axis=-1) @ w_out
    x = _layer_norm(ALPHA * x + g1 * mix, ln1_g, ln1_b)
    h2 = x * (1 + sc2) + sh2
    ff = _hier_moe(h2, w_router_g, b_router_g, w_router_e, b_router_e, w_gate, w_up, w_down)
    x = _layer_norm(ALPHA * x + g2 * ff, ln2_g, ln2_b)
    return x, a_v, new_buf, s_new


def setup_inputs(seed: int = 0) -> dict:
    key = jax.random.key(seed)
    ks = jax.random.split(key, 32)
    f32 = jnp.float32
    L = DEPTH
    D = D_MODEL

    def nrm(k, shape, scale):
        return jax.random.normal(k, shape, f32) * scale

    return {
        'x_prompt': nrm(ks[0], (BATCH, SEQ, D), 1.0),
        'x_sample': nrm(ks[1], (DEC_BATCH, DEC_SEQ, D), 1.0),
        'state_conv': nrm(ks[2], (L, DEC_BATCH, B_CONV - 1, B_CONV_CH), 1.0),
        'state_ssm': nrm(ks[3], (L, DEC_BATCH, B_V_HEADS, B_HEAD_DIM, B_HEAD_DIM), B_HEAD_DIM ** -0.5),
        'c_prompt': nrm(ks[4], (BATCH, D), 1.0),
        'c_sample': nrm(ks[5], (DEC_BATCH, D), 1.0),
        'w_ada': nrm(ks[6], (L, D, 6 * D), D ** -0.5),
        'b_ada': nrm(ks[7], (L, 6 * D), 0.02),
        'w_in': nrm(ks[8], (L, D, IN_WIDTH), D ** -0.5),
        'a_ws': nrm(ks[9], (L, A_HEADS, A_CHUNK, A_CHUNK), A_CHUNK ** -0.5),
        'a_bs': 1.0 + nrm(ks[10], (L, A_HEADS, A_CHUNK), 0.02),
        'a_norm_g': 1.0 + nrm(ks[11], (L, A_WIDTH), 0.02),
        'a_norm_b': nrm(ks[12], (L, A_WIDTH), 0.02),
        'b_conv_w': nrm(ks[13], (L, B_CONV, B_CONV_CH), B_CONV ** -0.5),
        'b_a_log': jnp.log(jax.random.uniform(ks[14], (L, B_V_HEADS), f32, 0.05, 0.5)),
        'b_dt_bias': -2.0 + nrm(ks[15], (L, B_V_HEADS), 0.1),
        'b_onorm_g': 1.0 + nrm(ks[16], (L, B_HEAD_DIM), 0.02),
        'w_out': nrm(ks[17], (L, MIX_WIDTH, D), MIX_WIDTH ** -0.5 * BETA_INIT),
        'ln1_g': 1.0 + nrm(ks[18], (L, D), 0.02),
        'ln1_b': nrm(ks[19], (L, D), 0.02),
        'w_router_g': nrm(ks[20], (L, D, N_GROUPS), D ** -0.5),
        'b_router_g': nrm(ks[21], (L, N_GROUPS), 0.01),
        'w_router_e': nrm(ks[22], (L, D, N_EXPERTS), D ** -0.5),
        'b_router_e': nrm(ks[23], (L, N_EXPERTS), 0.01),
        'w_gate': nrm(ks[24], (L, N_EXPERTS, D, D_EXPERT), D ** -0.5),
        'w_up': nrm(ks[25], (L, N_EXPERTS, D, D_EXPERT), D ** -0.5),
        'w_down': nrm(ks[26], (L, N_EXPERTS, D_EXPERT, D), D_EXPERT ** -0.5 * BETA_INIT),
        'ln2_g': 1.0 + nrm(ks[27], (L, D), 0.02),
        'ln2_b': nrm(ks[28], (L, D), 0.02),
    }


def reference(x_prompt, x_sample, state_conv, state_ssm, c_prompt, c_sample, w_ada, b_ada, w_in, a_ws, a_bs,
              a_norm_g, a_norm_b, b_conv_w, b_a_log, b_dt_bias, b_onorm_g, w_out, ln1_g, ln1_b, w_router_g,
              b_router_g, w_router_e, b_router_e, w_gate, w_up, w_down, ln2_g, ln2_b):
    yp, ys = x_prompt, x_sample
    conv_p, ssm_p, conv_s, ssm_s, chunkv_s = [], [], [], [], []
    bp_n = x_prompt.shape[0]
    for l in range(DEPTH):
        lp = (w_ada[l], b_ada[l], w_in[l], a_ws[l], a_bs[l], a_norm_g[l], a_norm_b[l], b_conv_w[l], b_a_log[l],
              b_dt_bias[l], b_onorm_g[l], w_out[l], ln1_g[l], ln1_b[l], w_router_g[l], b_router_g[l],
              w_router_e[l], b_router_e[l], w_gate[l], w_up[l], w_down[l], ln2_g[l], ln2_b[l])
        buf0 = jnp.zeros((bp_n, B_CONV - 1, B_CONV_CH), x_prompt.dtype)
        s0 = jnp.zeros((bp_n, B_V_HEADS, B_HEAD_DIM, B_HEAD_DIM), jnp.float32)
        yp, _, bp, sp = _layer(yp, c_prompt, buf0, s0, *lp)
        ys, vs, bs_, ss = _layer(ys, c_sample, state_conv[l], state_ssm[l], *lp)
        conv_p.append(bp)
        ssm_p.append(sp)
        conv_s.append(bs_)
        ssm_s.append(ss)
        chunkv_s.append(vs)
    return (yp, ys, jnp.stack(conv_p), jnp.stack(ssm_p), jnp.stack(conv_s), jnp.stack(ssm_s), jnp.stack(chunkv_s))
```

```python
import functools

import jax
import jax.numpy as jnp
from jax import lax
from jax.experimental import pallas as pl
from jax.experimental.pallas import tpu as pltpu

F32 = jnp.float32
BF16 = jnp.bfloat16

D_MODEL = 2048
DEPTH = 1
A_HEADS = 8
HEAD_DIM = 128
A_WIDTH = 1024
A_CHUNK = 128
B_QK_HEADS = 4
B_V_HEADS = 8
B_KEY_WIDTH = 512
B_VAL_WIDTH = 1024
B_CONV = 4
B_CONV_CH = 2048
DN_CHUNK = 64
MAIN_WIDTH = 2 * A_WIDTH + B_CONV_CH + B_VAL_WIDTH
N_GROUPS = 4
EXPERTS_PER_GROUP = 8
N_EXPERTS = 32
D_EXPERT = 512
ALPHA = (2 * DEPTH) ** 0.25
LN_EPS = 1e-5
RMS_EPS = 1e-6
L2_EPS = 1e-6

LANES = 128
VMEM_LIMIT = 56 * 1024 * 1024
MOE_ROWS = 256
ROUTE_TILE = 512
GATHER_WINDOW = 32


def _params(*sem):
    return pltpu.CompilerParams(dimension_semantics=sem, vmem_limit_bytes=VMEM_LIMIT)


def _mm(a, b):
    return jnp.dot(a.astype(BF16), b.astype(BF16), preferred_element_type=F32)


def _mm_nt(a, b):
    return lax.dot_general(a.astype(BF16), b.astype(BF16), (((1,), (1,)), ((), ())),
                           preferred_element_type=F32)


def _mm_tn(a, b):
    return lax.dot_general(a.astype(BF16), b.astype(BF16), (((0,), (0,)), ((), ())),
                           preferred_element_type=F32)


def _split(x):
    hi = x.astype(BF16)
    lo = (x - hi.astype(F32)).astype(BF16)
    return hi, lo


def _mm_exact_lhs(a_bf16, b):
    hi, lo = _split(b)
    return (jnp.dot(a_bf16, hi, preferred_element_type=F32)
            + jnp.dot(a_bf16, lo, preferred_element_type=F32))


def _mm3(a, b):
    ah, al = _split(a)
    bh, bl = _split(b)
    return (jnp.dot(ah, bh, preferred_element_type=F32) + jnp.dot(ah, bl, preferred_element_type=F32)
            + jnp.dot(al, bh, preferred_element_type=F32))


def _softplus(x):
    return jnp.maximum(x, 0.0) + jnp.log1p(jnp.exp(-jnp.abs(x)))


def _layer_norm_rows(x, g, b):
    mu = jnp.mean(x, -1, keepdims=True)
    xc = x - mu
    var = jnp.mean(xc * xc, -1, keepdims=True)
    return xc * lax.rsqrt(var + LN_EPS) * g + b


def _ada_kernel(c_ref, w_ref, b_ref, o_ref):
    a = jax.nn.silu(c_ref[...]).astype(BF16)
    o_ref[...] = jnp.dot(a, w_ref[...].astype(BF16), preferred_element_type=F32) + b_ref[...]


def _ada(c_all, w_ada, b_ada):
    rows = c_all.shape[0]
    tn = 1024
    return pl.pallas_call(
        _ada_kernel,
        grid=(6 * D_MODEL // tn,),
        in_specs=[pl.BlockSpec((rows, D_MODEL), lambda j: (0, 0)),
                  pl.BlockSpec((D_MODEL, tn), lambda j: (0, j)),
                  pl.BlockSpec((1, tn), lambda j: (0, j))],
        out_specs=pl.BlockSpec((rows, tn), lambda j: (0, j)),
        out_shape=jax.ShapeDtypeStruct((rows, 6 * D_MODEL), F32),
        compiler_params=_params("arbitrary"),
        name="ada",
    )(c_all, w_ada, b_ada)


def _inproj_kernel(x_ref, sc_ref, sh_ref, w_ref, wbd_ref, o_ref, bd_ref, h_ref):
    @pl.when(pl.program_id(1) == 0)
    def _():
        h = (x_ref[...] * (1.0 + sc_ref[...]) + sh_ref[...]).astype(BF16)
        h_ref[...] = h
        bd_ref[...] = jnp.dot(h, wbd_ref[...], preferred_element_type=F32)

    o_ref[...] = jnp.dot(h_ref[...], w_ref[...], preferred_element_type=F32)


def _mod_spec(mod, col, tm, rows_per_mod):
    if rows_per_mod:
        tiles = rows_per_mod // tm
        return pl.BlockSpec((None, 1, D_MODEL), lambda i, *_: (i // tiles, 0, col))
    return pl.BlockSpec((tm, D_MODEL), lambda i, *_: (0, col))


def _inproj(x, mod, w_main, w_bd, tm, rows_per_mod):
    m = x.shape[0]
    tn = 1024
    return pl.pallas_call(
        _inproj_kernel,
        grid=(m // tm, MAIN_WIDTH // tn),
        in_specs=[pl.BlockSpec((tm, D_MODEL), lambda i, j: (i, 0)),
                  _mod_spec(mod, 1, tm, rows_per_mod),
                  _mod_spec(mod, 0, tm, rows_per_mod),
                  pl.BlockSpec((D_MODEL, tn), lambda i, j: (0, j)),
                  pl.BlockSpec((D_MODEL, LANES), lambda i, j: (0, 0))],
        out_specs=[pl.BlockSpec((tm, tn), lambda i, j: (i, j)),
                   pl.BlockSpec((tm, LANES), lambda i, j: (i, 0))],
        out_shape=[jax.ShapeDtypeStruct((m, MAIN_WIDTH), F32),
                   jax.ShapeDtypeStruct((m, LANES), F32)],
        scratch_shapes=[pltpu.VMEM((tm, D_MODEL), BF16)],
        compiler_params=_params("arbitrary", "arbitrary"),
        name="inproj",
    )(x, mod, mod, w_main, w_bd)


def _mixa_prompt_kernel(p_ref, ws_ref, bias_ref, ng_ref, nb_ref, o_ref):
    rows = p_ref.shape[0]
    u = jax.nn.gelu(p_ref[:, :A_WIDTH])
    v = _layer_norm_rows(jax.nn.gelu(p_ref[:, A_WIDTH:]), ng_ref[...], nb_ref[...])
    ri = lax.broadcasted_iota(jnp.int32, (A_CHUNK, A_CHUNK), 0)
    ci = lax.broadcasted_iota(jnp.int32, (A_CHUNK, A_CHUNK), 1)
    for h in range(A_HEADS):
        w = jnp.where(ri >= ci, ws_ref[h], 0.0).astype(BF16)
        cols = slice(h * HEAD_DIM, (h + 1) * HEAD_DIM)
        for c in range(rows // A_CHUNK):
            rs = slice(c * A_CHUNK, (c + 1) * A_CHUNK)
            s = jnp.dot(w, v[rs, cols].astype(BF16), preferred_element_type=F32) + bias_ref[:, cols]
            o_ref[rs, cols] = u[rs, cols] * s


def _mixa_prompt(proj, a_ws, bias_tile, ng, nb, tm):
    m = proj.shape[0]
    return pl.pallas_call(
        _mixa_prompt_kernel,
        grid=(m // tm,),
        in_specs=[pl.BlockSpec((tm, 2 * A_WIDTH), lambda i: (i, 0)),
                  pl.BlockSpec((A_HEADS, A_CHUNK, A_CHUNK), lambda i: (0, 0, 0)),
                  pl.BlockSpec((A_CHUNK, A_WIDTH), lambda i: (0, 0)),
                  pl.BlockSpec((1, A_WIDTH), lambda i: (0, 0)),
                  pl.BlockSpec((1, A_WIDTH), lambda i: (0, 0))],
        out_specs=pl.BlockSpec((tm, A_WIDTH), lambda i: (i, 0)),
        out_shape=jax.ShapeDtypeStruct((m, A_WIDTH), F32),
        compiler_params=_params("arbitrary"),
        name="mixa_prompt",
    )(proj, a_ws, bias_tile, ng, nb)


def _mixa_sample_kernel(p_ref, coef_ref, bias_ref, ng_ref, nb_ref, o_ref, v_ref, *, steps, nb_rows):
    u = jax.nn.gelu(p_ref[:, :A_WIDTH])
    v = _layer_norm_rows(jax.nn.gelu(p_ref[:, A_WIDTH:]), ng_ref[...], nb_ref[...])
    v_ref[...] = v
    for t in range(steps):
        s = bias_ref[t:t + 1, :]
        for j in range(t + 1):
            s = s + coef_ref[t * steps + j:t * steps + j + 1, :] * v[j * nb_rows:(j + 1) * nb_rows, :]
        rs = slice(t * nb_rows, (t + 1) * nb_rows)
        o_ref[rs, :] = u[rs, :] * s


def _mixa_sample(proj, coef, bias, ng, nb, steps, nb_rows):
    m = proj.shape[0]
    kern = functools.partial(_mixa_sample_kernel, steps=steps, nb_rows=nb_rows)
    return pl.pallas_call(
        kern,
        grid=(1,),
        in_specs=[pl.BlockSpec((m, 2 * A_WIDTH), lambda i: (0, 0)),
                  pl.BlockSpec(coef.shape, lambda i: (0, 0)),
                  pl.BlockSpec(bias.shape, lambda i: (0, 0)),
                  pl.BlockSpec((1, A_WIDTH), lambda i: (0, 0)),
                  pl.BlockSpec((1, A_WIDTH), lambda i: (0, 0))],
        out_specs=[pl.BlockSpec((m, A_WIDTH), lambda i: (0, 0)),
                   pl.BlockSpec((m, A_WIDTH), lambda i: (0, 0))],
        out_shape=[jax.ShapeDtypeStruct((m, A_WIDTH), F32),
                   jax.ShapeDtypeStruct((m, A_WIDTH), F32)],
        compiler_params=_params("arbitrary"),
        name="mixa_sample",
    )(proj, coef, bias, ng, nb)


def _unit_lower_inverse(a_strict, ri, ci):
    c = a_strict.shape[0]
    eye = (ri == ci).astype(F32)
    pair = (lax.shift_right_logical(ri, 1) == lax.shift_right_logical(ci, 1)) & ((ri & 1) == 1) & ((ci & 1) == 0)
    t = eye - jnp.where(pair, a_strict, 0.0)
    n = 2
    while n < c:
        sh = n.bit_length()
        m = ((lax.shift_right_logical(ri, sh) == lax.shift_right_logical(ci, sh))
             & ((ri & n) != 0) & ((ci & n) == 0))
        an = jnp.where(m, a_strict, 0.0)
        t = t - _mm3(_mm3(t, an), t)
        n *= 2
    return t


def _mixb_prompt_kernel(qkv_ref, z_ref, bd_ref, cw_ref, nega_ref, dtb_ref, og_ref,
                        o_ref, sfin_ref, s_ref, cbuf_ref):
    c = pl.program_id(1)
    C = DN_CHUNK

    @pl.when(c == 0)
    def _init():
        s_ref[...] = jnp.zeros_like(s_ref)
        cbuf_ref[0:8, :] = jnp.zeros((8, B_CONV_CH), F32)

    x = qkv_ref[...]
    cbuf_ref[8:8 + C, :] = x
    cw = cw_ref[...]
    y = cbuf_ref[5:5 + C, :] * cw[0:1]
    y = y + cbuf_ref[6:6 + C, :] * cw[1:2]
    y = y + cbuf_ref[7:7 + C, :] * cw[2:3]
    y = y + x * cw[3:4]
    cbuf_ref[0:8, :] = x[C - 8:C, :]
    act = jax.nn.silu(y)

    qn, kn, gkk, gqk = [], [], [], []
    for qh in range(B_QK_HEADS):
        qs = act[:, qh * HEAD_DIM:(qh + 1) * HEAD_DIM]
        ks = act[:, B_KEY_WIDTH + qh * HEAD_DIM:B_KEY_WIDTH + (qh + 1) * HEAD_DIM]
        q = qs * lax.rsqrt(jnp.sum(qs * qs, -1, keepdims=True) + L2_EPS) * (HEAD_DIM ** -0.5)
        k = ks * lax.rsqrt(jnp.sum(ks * ks, -1, keepdims=True) + L2_EPS)
        qn.append(q)
        kn.append(k)
        gkk.append(_mm_nt(k, k))
        gqk.append(_mm_nt(q, k))

    bd = bd_ref[...]
    beta_all = jax.nn.sigmoid(bd)
    g_all = nega_ref[...] * _softplus(bd + dtb_ref[...])

    ri = lax.broadcasted_iota(jnp.int32, (C, C), 0)
    ci = lax.broadcasted_iota(jnp.int32, (C, C), 1)
    tril = ri >= ci
    strict = ri > ci
    tril_b = tril.astype(BF16)
    ones_b = jnp.ones((C, C), BF16)
    cum_lhs = jnp.concatenate([tril_b, ones_b], axis=0)
    triu_f = (ri <= ci).astype(F32)
    og = og_ref[...]

    for h in range(B_V_HEADS):
        qh = h // (B_V_HEADS // B_QK_HEADS)
        gcol = jnp.broadcast_to(g_all[:, B_V_HEADS + h:B_V_HEADS + h + 1], (C, HEAD_DIM))
        bcol = jnp.broadcast_to(beta_all[:, h:h + 1], (C, HEAD_DIM))
        gg = _mm_exact_lhs(cum_lhs, gcol)
        gc = gg[:C]
        gl = gg[C:]
        grow = _mm_exact_lhs(ones_b, gcol[:, :C] * triu_f)
        diff = gc[:, :C] - grow
        decay = jnp.where(tril, jnp.exp(jnp.where(tril, diff, 0.0)), 0.0)
        k = kn[qh]
        q = qn[qh]
        v = act[:, 2 * B_KEY_WIDTH + h * HEAD_DIM:2 * B_KEY_WIDTH + (h + 1) * HEAD_DIM]
        a_mat = jnp.where(strict, bcol[:, :C] * gkk[qh] * decay, 0.0)
        eg = jnp.exp(gc)
        kb = k * bcol
        rhs = jnp.concatenate([v * bcol, kb * eg], axis=1)
        t_inv = _unit_lower_inverse(a_mat, ri, ci)
        sol = _mm3(t_inv, rhs)
        u_val = sol[:, :HEAD_DIM]
        w_k = sol[:, HEAD_DIM:]
        qk = gqk[qh] * decay
        q_dec = q * eg
        k_dec = k * jnp.exp(gl - gc)
        s = s_ref[h]
        ws = _mm(jnp.concatenate([w_k, q_dec], axis=0), s)
        v_new = u_val - ws[:C]
        o = ws[C:] + _mm(qk, v_new)
        s_ref[h] = s * jnp.exp(gl[0:1, :]) + _mm_tn(k_dec, v_new)
        on = o * lax.rsqrt(jnp.mean(o * o, -1, keepdims=True) + RMS_EPS) * og
        cols = slice(h * HEAD_DIM, (h + 1) * HEAD_DIM)
        o_ref[:, cols] = on * jax.nn.silu(z_ref[:, cols])

    @pl.when(c == pl.num_programs(1) - 1)
    def _fin():
        sfin_ref[0] = s_ref[...]


def _mixb_prompt(proj, bd, conv_w, nega, dtb, og, batch, seq):
    nc = seq // DN_CHUNK
    qkv_blk = 2 * A_WIDTH // B_CONV_CH
    z_blk = (2 * A_WIDTH + B_CONV_CH) // B_VAL_WIDTH
    return pl.pallas_call(
        _mixb_prompt_kernel,
        grid=(batch, nc),
        in_specs=[pl.BlockSpec((DN_CHUNK, B_CONV_CH), lambda b, c: (b * nc + c, qkv_blk)),
                  pl.BlockSpec((DN_CHUNK, B_VAL_WIDTH), lambda b, c: (b * nc + c, z_blk)),
                  pl.BlockSpec((DN_CHUNK, LANES), lambda b, c: (b * nc + c, 0)),
                  pl.BlockSpec((B_CONV, B_CONV_CH), lambda b, c: (0, 0)),
                  pl.BlockSpec((1, LANES), lambda b, c: (0, 0)),
                  pl.BlockSpec((1, LANES), lambda b, c: (0, 0)),
                  pl.BlockSpec((1, HEAD_DIM), lambda b, c: (0, 0))],
        out_specs=[pl.BlockSpec((DN_CHUNK, B_VAL_WIDTH), lambda b, c: (b * nc + c, 0)),
                   pl.BlockSpec((1, B_V_HEADS, HEAD_DIM, HEAD_DIM), lambda b, c: (b, 0, 0, 0))],
        out_shape=[jax.ShapeDtypeStruct((batch * seq, B_VAL_WIDTH), F32),
                   jax.ShapeDtypeStruct((batch, B_V_HEADS, HEAD_DIM, HEAD_DIM), F32)],
        scratch_shapes=[pltpu.VMEM((B_V_HEADS, HEAD_DIM, HEAD_DIM), F32),
                        pltpu.VMEM((DN_CHUNK + 8, B_CONV_CH), F32)],
        compiler_params=_params("arbitrary", "arbitrary"),
        name="mixb_prompt",
    )(proj, proj, bd, conv_w, nega, dtb, og)


def _mixb_sample_pre_kernel(qkv_ref, buf_ref, bd_ref, cw_ref, nega_ref, dtb_ref,
                            act_ref, beta_ref, g_ref, *, steps, nb_rows):
    cw = cw_ref[...]

    def slab(j):
        if j < B_CONV - 1:
            return buf_ref[j * nb_rows:(j + 1) * nb_rows, :]
        jj = j - (B_CONV - 1)
        return qkv_ref[jj * nb_rows:(jj + 1) * nb_rows, :]

    for t in range(steps):
        y = slab(t) * cw[0:1]
        for i in range(1, B_CONV):
            y = y + slab(t + i) * cw[i:i + 1]
        act = jax.nn.silu(y)
        rs = slice(t * nb_rows, (t + 1) * nb_rows)
        for qh in range(B_QK_HEADS):
            cq = slice(qh * HEAD_DIM, (qh + 1) * HEAD_DIM)
            ck = slice(B_KEY_WIDTH + qh * HEAD_DIM, B_KEY_WIDTH + (qh + 1) * HEAD_DIM)
            qs = act[:, cq]
            ks = act[:, ck]
            act_ref[rs, cq] = qs * lax.rsqrt(jnp.sum(qs * qs, -1, keepdims=True) + L2_EPS) * (HEAD_DIM ** -0.5)
            act_ref[rs, ck] = ks * lax.rsqrt(jnp.sum(ks * ks, -1, keepdims=True) + L2_EPS)
        act_ref[rs, 2 * B_KEY_WIDTH:] = act[:, 2 * B_KEY_WIDTH:]
    bd = bd_ref[...]
    beta_ref[...] = jax.nn.sigmoid(bd)
    g_ref[...] = nega_ref[...] * _softplus(bd + dtb_ref[...])


def _mixb_sample_pre(proj, buf, bd, conv_w, nega, dtb, steps, nb_rows):
    m = proj.shape[0]
    qkv_blk = 2 * A_WIDTH // B_CONV_CH
    kern = functools.partial(_mixb_sample_pre_kernel, steps=steps, nb_rows=nb_rows)
    return pl.pallas_call(
        kern,
        grid=(1,),
        in_specs=[pl.BlockSpec((m, B_CONV_CH), lambda i: (0, qkv_blk)),
                  pl.BlockSpec(buf.shape, lambda i: (0, 0)),
                  pl.BlockSpec((m, LANES), lambda i: (0, 0)),
                  pl.BlockSpec((B_CONV, B_CONV_CH), lambda i: (0, 0)),
                  pl.BlockSpec((1, LANES), lambda i: (0, 0)),
                  pl.BlockSpec((1, LANES), lambda i: (0, 0))],
        out_specs=[pl.BlockSpec((m, B_CONV_CH), lambda i: (0, 0)),
                   pl.BlockSpec((m, LANES), lambda i: (0, 0)),
                   pl.BlockSpec((m, LANES), lambda i: (0, 0))],
        out_shape=[jax.ShapeDtypeStruct((m, B_CONV_CH), F32),
                   jax.ShapeDtypeStruct((m, LANES), F32),
                   jax.ShapeDtypeStruct((m, LANES), F32)],
        compiler_params=_params("arbitrary"),
        name="mixb_sample_pre",
    )(proj, buf, bd, conv_w, nega, dtb)


def _mixb_sample_rec_kernel(kq_ref, v_ref, z_ref, beta_ref, g_ref, s0_ref, og_ref,
                            o_ref, s_out_ref, *, steps, pairs):
    og = og_ref[...]
    zpad = jnp.zeros((HEAD_DIM - 8, HEAD_DIM), F32)

    def body(i, carry):
        bi = i // B_V_HEADS
        hi = i % B_V_HEADS
        kq = kq_ref[bi, hi]
        s0 = s0_ref[bi, hi]
        kqs = _mm(kq, s0)
        g = g_ref[bi, hi]
        beta = beta_ref[bi, hi]
        v = v_ref[bi, hi]
        gc = [g[0:1]]
        for t in range(1, steps):
            gc.append(gc[-1] + g[t:t + 1])
        k = [kq[t:t + 1] for t in range(steps)]
        q = [kq[4 + t:5 + t] for t in range(steps)]
        d = []
        for t in range(steps):
            acc = v[t:t + 1] - jnp.exp(gc[t]) * kqs[t:t + 1]
            for j in range(t):
                kk = jnp.sum(k[j] * k[t], -1, keepdims=True)
                acc = acc - jnp.exp(gc[t] - gc[j]) * kk * d[j]
            d.append(beta[t:t + 1] * acc)
        outs = []
        for t in range(steps):
            o = jnp.exp(gc[t]) * kqs[4 + t:5 + t]
            for j in range(t + 1):
                qk = jnp.sum(k[j] * q[t], -1, keepdims=True)
                o = o + jnp.exp(gc[t] - gc[j]) * qk * d[j]
            outs.append(o * lax.rsqrt(jnp.mean(o * o, -1, keepdims=True) + RMS_EPS) * og)
        o8 = jnp.concatenate(outs + [jnp.zeros((8 - steps, HEAD_DIM), F32)], axis=0)
        o_ref[bi, hi] = o8 * jax.nn.silu(z_ref[bi, hi])
        k_dec = jnp.concatenate([jnp.exp(gc[-1] - gc[j]) * k[j] for j in range(steps)]
                                + [jnp.zeros((8 - steps, HEAD_DIM), F32)], axis=0)
        d8 = jnp.concatenate(d + [jnp.zeros((8 - steps, HEAD_DIM), F32)], axis=0)
        k_pad = jnp.concatenate([k_dec, zpad], axis=0)
        d_pad = jnp.concatenate([d8, zpad], axis=0)
        s_out_ref[bi, hi] = s0 * jnp.exp(gc[-1]) + _mm(k_pad.T, d_pad)
        return carry

    lax.fori_loop(0, pairs, body, 0)


def _mixb_sample_rec(kq, v, z, beta, g, s0, og, steps, bb):
    nb = kq.shape[0]
    tile = pl.BlockSpec((bb, B_V_HEADS, 8, HEAD_DIM), lambda i: (i, 0, 0, 0))
    st = pl.BlockSpec((bb, B_V_HEADS, HEAD_DIM, HEAD_DIM), lambda i: (i, 0, 0, 0))
    kern = functools.partial(_mixb_sample_rec_kernel, steps=steps, pairs=bb * B_V_HEADS)
    return pl.pallas_call(
        kern,
        grid=(nb // bb,),
        in_specs=[tile, tile, tile, tile, tile, st, pl.BlockSpec((1, HEAD_DIM), lambda i: (0, 0))],
        out_specs=[tile, st],
        out_shape=[jax.ShapeDtypeStruct((nb, B_V_HEADS, 8, HEAD_DIM), F32),
                   jax.ShapeDtypeStruct((nb, B_V_HEADS, HEAD_DIM, HEAD_DIM), F32)],
        compiler_params=_params("arbitrary"),
        name="mixb_sample_rec",
    )(kq, v, z, beta, g, s0, og)


def _outproj_kernel(a_ref, b_ref, x_ref, g1_ref, sc2_ref, sh2_ref, w_ref, lg_ref, lb_ref, wr_ref,
                    x1_ref, h2_ref, logit_ref):
    mix = (jnp.dot(a_ref[...].astype(BF16), w_ref[:A_WIDTH, :], preferred_element_type=F32)
           + jnp.dot(b_ref[...].astype(BF16), w_ref[A_WIDTH:, :], preferred_element_type=F32))
    x1 = _layer_norm_rows(ALPHA * x_ref[...] + g1_ref[...] * mix, lg_ref[...], lb_ref[...])
    x1_ref[...] = x1
    h2 = x1 * (1.0 + sc2_ref[...]) + sh2_ref[...]
    h2_ref[...] = h2
    logit_ref[...] = _mm3(h2, wr_ref[...])


def _outproj(a_out, b_out, x, mod, w_out, lg, lb, w_router, tm, rows_per_mod):
    m = x.shape[0]
    row = lambda w: pl.BlockSpec((tm, w), lambda i: (i, 0))
    full = lambda s: pl.BlockSpec(s, lambda i: (0, 0))
    return pl.pallas_call(
        _outproj_kernel,
        grid=(m // tm,),
        in_specs=[row(A_WIDTH), row(B_VAL_WIDTH), row(D_MODEL),
                  _mod_spec(mod, 2, tm, rows_per_mod), _mod_spec(mod, 4, tm, rows_per_mod),
                  _mod_spec(mod, 3, tm, rows_per_mod),
                  full((D_MODEL, D_MODEL)), full((1, D_MODEL)), full((1, D_MODEL)),
                  full((D_MODEL, LANES))],
        out_specs=[row(D_MODEL), row(D_MODEL), row(LANES)],
        out_shape=[jax.ShapeDtypeStruct((m, D_MODEL), F32),
                   jax.ShapeDtypeStruct((m, D_MODEL), F32),
                   jax.ShapeDtypeStruct((m, LANES), F32)],
        compiler_params=_params("arbitrary"),
        name="outproj",
    )(a_out, b_out, x, mod, mod, mod, w_out, lg, lb, w_router)


def _router_kernel(logit_ref, bias_ref, route_ref, count_ref, base_ref):
    i = pl.program_id(0)

    @pl.when(i == 0)
    def _():
        base_ref[...] = jnp.zeros_like(base_ref)

    lg = logit_ref[...]
    tm = lg.shape[0]
    bias = bias_ref[...]
    lane = lax.broadcasted_iota(jnp.int32, lg.shape, 1)
    neg = -jnp.inf

    def first_argmax(score):
        mx = jnp.max(score, -1, keepdims=True)
        return jnp.min(jnp.where(score == mx, lane, LANES), -1, keepdims=True)

    def pick(vals, idx):
        return jnp.sum(jnp.where(lane == idx, vals, 0.0), -1, keepdims=True)

    gmask = lane < N_GROUPS
    mg = jnp.max(jnp.where(gmask, lg, neg), -1, keepdims=True)
    eg = jnp.where(gmask, jnp.exp(jnp.where(gmask, lg - mg, 0.0)), 0.0)
    pg = eg / jnp.sum(eg, -1, keepdims=True)
    sel_g = first_argmax(jnp.where(gmask, lg + bias, neg))
    p_sel = pick(pg, sel_g)

    lo = N_GROUPS + sel_g * EXPERTS_PER_GROUP
    emask = (lane >= lo) & (lane < lo + EXPERTS_PER_GROUP)
    me = jnp.max(jnp.where(emask, lg, neg), -1, keepdims=True)
    ee = jnp.where(emask, jnp.exp(jnp.where(emask, lg - me, 0.0)), 0.0)
    pe = ee / jnp.sum(ee, -1, keepdims=True)
    score = jnp.where(emask, pe + bias, neg)
    i1 = first_argmax(score)
    i2 = first_argmax(jnp.where(lane == i1, neg, score))
    w1 = pick(pe, i1)
    w2 = pick(pe, i2)
    wsum = w1 + w2
    gate1 = w1 / wsum * p_sel
    gate2 = w2 / wsum * p_sel

    hot = ((lane == i1) | (lane == i2)).astype(BF16)
    ri = lax.broadcasted_iota(jnp.int32, (tm, tm), 0)
    ci = lax.broadcasted_iota(jnp.int32, (tm, tm), 1)
    before = jnp.dot((ri > ci).astype(BF16), hot, preferred_element_type=F32) + base_ref[...]
    rank1 = pick(before, i1)
    rank2 = pick(before, i2)
    base_ref[...] = base_ref[...] + jnp.sum(hot.astype(F32), 0, keepdims=True)
    count_ref[...] = base_ref[...]

    out = jnp.where(lane == 0, (i1 - N_GROUPS).astype(F32), 0.0)
    out = jnp.where(lane == 1, (i2 - N_GROUPS).astype(F32), out)
    out = jnp.where(lane == 2, gate1, out)
    out = jnp.where(lane == 3, gate2, out)
    out = jnp.where(lane == 4, rank1, out)
    out = jnp.where(lane == 5, rank2, out)
    route_ref[...] = out


def _router(logits, bias):
    m = logits.shape[0]
    return pl.pallas_call(
        _router_kernel,
        grid=(m // ROUTE_TILE,),
        in_specs=[pl.BlockSpec((ROUTE_TILE, LANES), lambda i: (i, 0)),
                  pl.BlockSpec((1, LANES), lambda i: (0, 0))],
        out_specs=[pl.BlockSpec((ROUTE_TILE, LANES), lambda i: (i, 0)),
                   pl.BlockSpec((1, LANES), lambda i: (0, 0))],
        out_shape=[jax.ShapeDtypeStruct((m, LANES), F32),
                   jax.ShapeDtypeStruct((1, LANES), F32)],
        scratch_shapes=[pltpu.VMEM((1, LANES), F32)],
        compiler_params=_params("arbitrary"),
        name="router",
    )(logits, bias)


def _gather_kernel(idx_ref, src_ref, out_ref, sem, *, n_rows):
    def row_copy(i):
        return pltpu.make_async_copy(src_ref.at[idx_ref[i]], out_ref.at[i], sem)

    def body(i, carry):
        @pl.when(i >= GATHER_WINDOW)
        def _():
            row_copy(i - GATHER_WINDOW).wait()

        row_copy(i).start()
        return carry

    lax.fori_loop(0, n_rows, body, 0)

    def drain(i, carry):
        row_copy(i).wait()
        return carry

    lax.fori_loop(n_rows - GATHER_WINDOW, n_rows, drain, 0)


def _gather_rows(idx, src):
    n = idx.shape[0]
    return pl.pallas_call(
        functools.partial(_gather_kernel, n_rows=n),
        grid_spec=pltpu.PrefetchScalarGridSpec(
            num_scalar_prefetch=1,
            grid=(1,),
            in_specs=[pl.BlockSpec(memory_space=pl.ANY)],
            out_specs=pl.BlockSpec(memory_space=pl.ANY),
            scratch_shapes=[pltpu.SemaphoreType.DMA(())]),
        out_shape=jax.ShapeDtypeStruct((n, src.shape[1]), src.dtype),
        compiler_params=_params("arbitrary"),
        name="gather_rows",
    )(idx, src)


def _expert_kernel(be_ref, nv_ref, x_ref, wg_ref, wu_ref, wd_ref, y_ref, wg_s, wu_s, wd_s):
    i = pl.program_id(0)
    prev = be_ref[jnp.maximum(i - 1, 0)]
    valid = i < nv_ref[0]

    @pl.when(valid & ((i == 0) | (be_ref[i] != prev)))
    def _():
        wg_s[...] = wg_ref[0].astype(BF16)
        wu_s[...] = wu_ref[0].astype(BF16)
        wd_s[...] = wd_ref[0].astype(BF16)

    @pl.when(valid)
    def _():
        x = x_ref[...].astype(BF16)
        hg = jnp.dot(x, wg_s[...], preferred_element_type=F32)
        hu = jnp.dot(x, wu_s[...], preferred_element_type=F32)
        hid = (jax.nn.silu(hg) * hu).astype(BF16)
        y_ref[...] = jnp.dot(hid, wd_s[...], preferred_element_type=F32)

    @pl.when(jnp.logical_not(valid))
    def _():
        y_ref[...] = jnp.zeros_like(y_ref)


def _experts(block_e, n_valid, xb, w_gate, w_up, w_down):
    n_slots = xb.shape[0]
    return pl.pallas_call(
        _expert_kernel,
        grid_spec=pltpu.PrefetchScalarGridSpec(
            num_scalar_prefetch=2,
            grid=(n_slots // MOE_ROWS,),
            in_specs=[pl.BlockSpec((MOE_ROWS, D_MODEL), lambda i, be, nv: (i, 0)),
                      pl.BlockSpec((1, D_MODEL, D_EXPERT), lambda i, be, nv: (be[i], 0, 0)),
                      pl.BlockSpec((1, D_MODEL, D_EXPERT), lambda i, be, nv: (be[i], 0, 0)),
                      pl.BlockSpec((1, D_EXPERT, D_MODEL), lambda i, be, nv: (be[i], 0, 0))],
            out_specs=pl.BlockSpec((MOE_ROWS, D_MODEL), lambda i, be, nv: (i, 0)),
            scratch_shapes=[pltpu.VMEM((D_MODEL, D_EXPERT), BF16),
                            pltpu.VMEM((D_MODEL, D_EXPERT), BF16),
                            pltpu.VMEM((D_EXPERT, D_MODEL), BF16)]),
        out_shape=jax.ShapeDtypeStruct((n_slots, D_MODEL), F32),
        compiler_params=_params("arbitrary"),
        name="experts",
    )(block_e, n_valid, xb, w_gate, w_up, w_down)


def _final_kernel(x1_ref, y0_ref, y1_ref, route_ref, g2_ref, lg_ref, lb_ref, o_ref):
    route = route_ref[...]
    ff = y0_ref[...] * route[:, 2:3] + y1_ref[...] * route[:, 3:4]
    o_ref[...] = _layer_norm_rows(ALPHA * x1_ref[...] + g2_ref[...] * ff, lg_ref[...], lb_ref[...])


def _final(x1, y01, route, mod, lg, lb, tm, rows_per_mod, row0, n_rows, n_tok):
    off = row0 // tm
    off1 = (n_tok + row0) // tm
    return pl.pallas_call(
        _final_kernel,
        grid=(n_rows // tm,),
        in_specs=[pl.BlockSpec((tm, D_MODEL), lambda i: (off + i, 0)),
                  pl.BlockSpec((tm, D_MODEL), lambda i: (off + i, 0)),
                  pl.BlockSpec((tm, D_MODEL), lambda i: (off1 + i, 0)),
                  pl.BlockSpec((tm, LANES), lambda i: (off + i, 0)),
                  _mod_spec(mod, 5, tm, rows_per_mod),
                  pl.BlockSpec((1, D_MODEL), lambda i: (0, 0)),
                  pl.BlockSpec((1, D_MODEL), lambda i: (0, 0))],
        out_specs=pl.BlockSpec((tm, D_MODEL), lambda i: (i, 0)),
        out_shape=jax.ShapeDtypeStruct((n_rows, D_MODEL), F32),
        compiler_params=_params("arbitrary"),
        name="final",
    )(x1, y01, y01, route, mod, lg, lb)


def _lane_row(vec, offset):
    return jnp.zeros((1, LANES), F32).at[0, offset:offset + vec.shape[0]].set(vec.astype(F32))


def _moe_dispatch(route, n_tok):
    eid = route[:, 0:2].astype(jnp.int32)
    rank = route[:, 4:6].astype(jnp.int32)
    return eid, rank


def kernel(x_prompt, x_sample, state_conv, state_ssm, c_prompt, c_sample, w_ada, b_ada, w_in, a_ws, a_bs, a_norm_g, a_norm_b, b_conv_w, b_a_log, b_dt_bias, b_onorm_g, w_out, ln1_g, ln1_b, w_router_g, b_router_g, w_router_e, b_router_e, w_gate, w_up, w_down, ln2_g, ln2_b):
    batch, seq, d = x_prompt.shape
    nb, steps, _ = x_sample.shape
    n_p = batch * seq
    n_s = nb * steps
    n_tok = n_p + n_s
    l = 0

    c_rows = batch + nb
    c_pad = (-c_rows) % 8
    c_all = jnp.concatenate([c_prompt, c_sample, jnp.zeros((c_pad, d), F32)], axis=0)
    mod = _ada(c_all, w_ada[l], b_ada[l].reshape(1, -1))
    mod_p = mod[:batch].reshape(batch, 1, 6 * d)
    mod_s = mod[batch:batch + nb]

    w_in_l = w_in[l]
    w_main = w_in_l[:, :MAIN_WIDTH].astype(BF16)
    w_bd = jnp.zeros((d, LANES), BF16).at[:, :2 * B_V_HEADS].set(w_in_l[:, MAIN_WIDTH:].astype(BF16))
    xp = x_prompt.reshape(n_p, d)
    xs = jnp.swapaxes(x_sample, 0, 1).reshape(n_s, d)
    proj_p, bd_p = _inproj(xp, mod_p, w_main, w_bd, 512, seq)
    proj_s, bd_s = _inproj(xs, mod_s, w_main, w_bd, nb, 0)

    ng = a_norm_g[l].reshape(1, -1)
    nbias = a_norm_b[l].reshape(1, -1)
    bias_tile = jnp.repeat(a_bs[l].T, HEAD_DIM, axis=1)
    a_out_p = _mixa_prompt(proj_p, a_ws[l], bias_tile, ng, nbias, 256)
    coef = jnp.repeat(jnp.transpose(a_ws[l][:, :steps, :steps], (1, 2, 0)).reshape(steps * steps, A_HEADS),
                      HEAD_DIM, axis=1)
    a_out_s, chunkv = _mixa_sample(proj_s, coef, bias_tile[:steps], ng, nbias, steps, nb)

    nega = _lane_row(-jnp.exp(b_a_log[l].astype(F32)), B_V_HEADS)
    dtb = _lane_row(b_dt_bias[l], B_V_HEADS)
    og = b_onorm_g[l].reshape(1, -1)
    b_out_p, ssm_p = _mixb_prompt(proj_p, bd_p, b_conv_w[l], nega, dtb, og, batch, seq)
    qkv0 = 2 * A_WIDTH
    conv_p = proj_p.reshape(batch, seq, MAIN_WIDTH)[:, seq - (B_CONV - 1):, qkv0:qkv0 + B_CONV_CH]

    buf_s = jnp.swapaxes(state_conv[l], 0, 1).reshape((B_CONV - 1) * nb, B_CONV_CH)
    act_s, beta_s, g_s = _mixb_sample_pre(proj_s, buf_s, bd_s, b_conv_w[l], nega, dtb, steps, nb)
    rep = B_V_HEADS // B_QK_HEADS
    act4 = act_s.reshape(steps, nb, B_CONV_CH)
    q8 = jnp.repeat(act4[..., :B_KEY_WIDTH].reshape(steps, nb, B_QK_HEADS, HEAD_DIM), rep, axis=2)
    k8 = jnp.repeat(act4[..., B_KEY_WIDTH:2 * B_KEY_WIDTH].reshape(steps, nb, B_QK_HEADS, HEAD_DIM), rep, axis=2)

    def to_tiles(a):
        a = jnp.transpose(a, (1, 2, 0, 3))
        return jnp.pad(a, ((0, 0), (0, 0), (0, 8 - a.shape[2]), (0, 0)))

    kq_t = jnp.concatenate([to_tiles(k8)[:, :, :4], to_tiles(q8)[:, :, :4]], axis=2)
    v_t = to_tiles(act4[..., 2 * B_KEY_WIDTH:].reshape(steps, nb, B_V_HEADS, HEAD_DIM))
    z_t = to_tiles(proj_s[:, qkv0 + B_CONV_CH:].reshape(steps, nb, B_V_HEADS, HEAD_DIM))

    def gate_tiles(a, off):
        a = a[:, off:off + B_V_HEADS].reshape(steps, nb, B_V_HEADS, 1)
        return to_tiles(jnp.broadcast_to(a, (steps, nb, B_V_HEADS, HEAD_DIM)))

    o_t, ssm_s = _mixb_sample_rec(kq_t, v_t, z_t, gate_tiles(beta_s, 0), gate_tiles(g_s, B_V_HEADS),
                                  state_ssm[l], og, steps, 8)
    b_out_s = jnp.transpose(o_t[:, :, :steps], (2, 0, 1, 3)).reshape(n_s, B_VAL_WIDTH)
    conv_s = jnp.swapaxes(proj_s.reshape(steps, nb, MAIN_WIDTH)[steps - (B_CONV - 1):, :, qkv0:qkv0 + B_CONV_CH], 0, 1)

    w_out_b = w_out[l].astype(BF16)
    lg1 = ln1_g[l].reshape(1, -1)
    lb1 = ln1_b[l].reshape(1, -1)
    w_router = jnp.zeros((d, LANES), F32)
    w_router = w_router.at[:, :N_GROUPS].set(w_router_g[l]).at[:, N_GROUPS:N_GROUPS + N_EXPERTS].set(w_router_e[l])
    x1_p, h2_p, lgt_p = _outproj(a_out_p, b_out_p, xp, mod_p, w_out_b, lg1, lb1, w_router, 256, seq)
    x1_s, h2_s, lgt_s = _outproj(a_out_s, b_out_s, xs, mod_s, w_out_b, lg1, lb1, w_router, nb, 0)
    x1 = jnp.concatenate([x1_p, x1_s], axis=0)
    h2 = jnp.concatenate([h2_p, h2_s], axis=0)
    logits = jnp.concatenate([lgt_p, lgt_s], axis=0)

    r_bias = jnp.zeros((1, LANES), F32)
    r_bias = r_bias.at[0, :N_GROUPS].set(b_router_g[l]).at[0, N_GROUPS:N_GROUPS + N_EXPERTS].set(b_router_e[l])
    route, counts_row = _router(logits, r_bias)
    counts = counts_row[0, N_GROUPS:N_GROUPS + N_EXPERTS].astype(jnp.int32)
    eid = route[:, 0:2].astype(jnp.int32)
    rank = route[:, 4:6].astype(jnp.int32)
    padded = (counts + MOE_ROWS - 1) // MOE_ROWS * MOE_ROWS
    pend = jnp.cumsum(padded)
    pstart = pend - padded
    dest = pstart[eid] + rank
    n_blocks = -(-(2 * n_tok) // MOE_ROWS) + N_EXPERTS
    n_slots = n_blocks * MOE_ROWS
    tok = jnp.arange(n_tok, dtype=jnp.int32)
    slot_tok = jnp.zeros((n_slots,), jnp.int32).at[dest.reshape(-1)].set(jnp.repeat(tok, 2))
    blk0 = jnp.arange(n_blocks, dtype=jnp.int32) * MOE_ROWS
    block_e = jnp.minimum(jnp.searchsorted(pend, blk0, side='right'), N_EXPERTS - 1).astype(jnp.int32)
    n_valid = (pend[-1:] // MOE_ROWS).astype(jnp.int32)

    xb = _gather_rows(slot_tok, h2)
    yb = _experts(block_e, n_valid, xb, w_gate[l], w_up[l], w_down[l])
    y01 = _gather_rows(jnp.concatenate([dest[:, 0], dest[:, 1]]), yb)

    lg2 = ln2_g[l].reshape(1, -1)
    lb2 = ln2_b[l].reshape(1, -1)
    y_p = _final(x1, y01, route, mod_p, lg2, lb2, 256, seq, 0, n_p, n_tok)
    y_s = _final(x1, y01, route, mod_s, lg2, lb2, nb, 0, n_p, n_s, n_tok)

    y_prompt = y_p.reshape(batch, seq, d)
    y_sample = jnp.swapaxes(y_s.reshape(steps, nb, d), 0, 1)
    chunkv_s = jnp.swapaxes(chunkv.reshape(steps, nb, A_HEADS, HEAD_DIM), 0, 1)
    return (y_prompt, y_sample, conv_p[None], ssm_p[None], conv_s[None], ssm_s[None], chunkv_s[None])
```

```python
import functools

import jax
import jax.numpy as jnp
from jax import lax
from jax.experimental import pallas as pl
from jax.experimental.pallas import tpu as pltpu

F32 = jnp.float32
BF16 = jnp.bfloat16

D_MODEL = 2048
DEPTH = 1
A_HEADS = 8
HEAD_DIM = 128
A_WIDTH = 1024
A_CHUNK = 128
B_QK_HEADS = 4
B_V_HEADS = 8
B_KEY_WIDTH = 512
B_VAL_WIDTH = 1024
B_CONV = 4
B_CONV_CH = 2048
DN_CHUNK = 64
MAIN_WIDTH = 2 * A_WIDTH + B_CONV_CH + B_VAL_WIDTH
N_GROUPS = 4
EXPERTS_PER_GROUP = 8
N_EXPERTS = 32
D_EXPERT = 512
ALPHA = (2 * DEPTH) ** 0.25
LN_EPS = 1e-5
RMS_EPS = 1e-6
L2_EPS = 1e-6

LANES = 128
VMEM_LIMIT = 56 * 1024 * 1024
MOE_ROWS = 256
ROUTE_TILE = 512


def _params(*sem):
    return pltpu.CompilerParams(dimension_semantics=sem, vmem_limit_bytes=VMEM_LIMIT)


def _mm(a, b):
    return jnp.dot(a.astype(BF16), b.astype(BF16), preferred_element_type=F32)


def _mm_nt(a, b):
    return lax.dot_general(a.astype(BF16), b.astype(BF16), (((1,), (1,)), ((), ())),
                           preferred_element_type=F32)


def _mm_tn(a, b):
    return lax.dot_general(a.astype(BF16), b.astype(BF16), (((0,), (0,)), ((), ())),
                           preferred_element_type=F32)


def _split(x):
    hi = x.astype(BF16)
    lo = (x - hi.astype(F32)).astype(BF16)
    return hi, lo


def _mm_exact_lhs(a_bf16, b):
    hi, lo = _split(b)
    return (jnp.dot(a_bf16, hi, preferred_element_type=F32)
            + jnp.dot(a_bf16, lo, preferred_element_type=F32))


def _mm3(a, b):
    ah, al = _split(a)
    bh, bl = _split(b)
    return (jnp.dot(ah, bh, preferred_element_type=F32) + jnp.dot(ah, bl, preferred_element_type=F32)
            + jnp.dot(al, bh, preferred_element_type=F32))


def _softplus(x):
    return jnp.maximum(x, 0.0) + jnp.log1p(jnp.exp(-jnp.abs(x)))


def _layer_norm_rows(x, g, b):
    mu = jnp.mean(x, -1, keepdims=True)
    xc = x - mu
    var = jnp.mean(xc * xc, -1, keepdims=True)
    return xc * lax.rsqrt(var + LN_EPS) * g + b


def _ada_kernel(c_ref, w_ref, b_ref, o_ref):
    a = jax.nn.silu(c_ref[...]).astype(BF16)
    o_ref[...] = jnp.dot(a, w_ref[...].astype(BF16), preferred_element_type=F32) + b_ref[...]


def _ada(c_all, w_ada, b_ada):
    rows = c_all.shape[0]
    tn = 1024
    return pl.pallas_call(
        _ada_kernel,
        grid=(6 * D_MODEL // tn,),
        in_specs=[pl.BlockSpec((rows, D_MODEL), lambda j: (0, 0)),
                  pl.BlockSpec((D_MODEL, tn), lambda j: (0, j)),
                  pl.BlockSpec((1, tn), lambda j: (0, j))],
        out_specs=pl.BlockSpec((rows, tn), lambda j: (0, j)),
        out_shape=jax.ShapeDtypeStruct((rows, 6 * D_MODEL), F32),
        compiler_params=_params("arbitrary"),
        name="ada",
    )(c_all, w_ada, b_ada)


def _inproj_kernel(x_ref, sc_ref, sh_ref, w_ref, wbd_ref, o_ref, bd_ref, h_ref):
    @pl.when(pl.program_id(1) == 0)
    def _():
        h = (x_ref[...] * (1.0 + sc_ref[...]) + sh_ref[...]).astype(BF16)
        h_ref[...] = h
        bd_ref[...] = jnp.dot(h, wbd_ref[...], preferred_element_type=F32)

    o_ref[...] = jnp.dot(h_ref[...], w_ref[...], preferred_element_type=F32)


def _mod_spec(mod, col, tm, rows_per_mod):
    if rows_per_mod:
        tiles = rows_per_mod // tm
        return pl.BlockSpec((None, 1, D_MODEL), lambda i, *_: (i // tiles, 0, col))
    return pl.BlockSpec((tm, D_MODEL), lambda i, *_: (0, col))


def _inproj(x, mod, w_main, w_bd, tm, rows_per_mod):
    m = x.shape[0]
    tn = 1024
    return pl.pallas_call(
        _inproj_kernel,
        grid=(m // tm, MAIN_WIDTH // tn),
        in_specs=[pl.BlockSpec((tm, D_MODEL), lambda i, j: (i, 0)),
                  _mod_spec(mod, 1, tm, rows_per_mod),
                  _mod_spec(mod, 0, tm, rows_per_mod),
                  pl.BlockSpec((D_MODEL, tn), lambda i, j: (0, j)),
                  pl.BlockSpec((D_MODEL, LANES), lambda i, j: (0, 0))],
        out_specs=[pl.BlockSpec((tm, tn), lambda i, j: (i, j)),
                   pl.BlockSpec((tm, LANES), lambda i, j: (i, 0))],
        out_shape=[jax.ShapeDtypeStruct((m, MAIN_WIDTH), F32),
                   jax.ShapeDtypeStruct((m, LANES), F32)],
        scratch_shapes=[pltpu.VMEM((tm, D_MODEL), BF16)],
        compiler_params=_params("arbitrary", "arbitrary"),
        name="inproj",
    )(x, mod, mod, w_main, w_bd)


def _mixa_prompt_kernel(p_ref, ws_ref, bias_ref, ng_ref, nb_ref, o_ref):
    rows = p_ref.shape[0]
    u = jax.nn.gelu(p_ref[:, :A_WIDTH])
    v = _layer_norm_rows(jax.nn.gelu(p_ref[:, A_WIDTH:]), ng_ref[...], nb_ref[...])
    ri = lax.broadcasted_iota(jnp.int32, (A_CHUNK, A_CHUNK), 0)
    ci = lax.broadcasted_iota(jnp.int32, (A_CHUNK, A_CHUNK), 1)
    for h in range(A_HEADS):
        w = jnp.where(ri >= ci, ws_ref[h], 0.0).astype(BF16)
        cols = slice(h * HEAD_DIM, (h + 1) * HEAD_DIM)
        for c in range(rows // A_CHUNK):
            rs = slice(c * A_CHUNK, (c + 1) * A_CHUNK)
            s = jnp.dot(w, v[rs, cols].astype(BF16), preferred_element_type=F32) + bias_ref[:, cols]
            o_ref[rs, cols] = u[rs, cols] * s


def _mixa_prompt(proj, a_ws, bias_tile, ng, nb, tm):
    m = proj.shape[0]
    return pl.pallas_call(
        _mixa_prompt_kernel,
        grid=(m // tm,),
        in_specs=[pl.BlockSpec((tm, 2 * A_WIDTH), lambda i: (i, 0)),
                  pl.BlockSpec((A_HEADS, A_CHUNK, A_CHUNK), lambda i: (0, 0, 0)),
                  pl.BlockSpec((A_CHUNK, A_WIDTH), lambda i: (0, 0)),
                  pl.BlockSpec((1, A_WIDTH), lambda i: (0, 0)),
                  pl.BlockSpec((1, A_WIDTH), lambda i: (0, 0))],
        out_specs=pl.BlockSpec((tm, A_WIDTH), lambda i: (i, 0)),
        out_shape=jax.ShapeDtypeStruct((m, A_WIDTH), F32),
        compiler_params=_params("arbitrary"),
        name="mixa_prompt",
    )(proj, a_ws, bias_tile, ng, nb)


def _mixa_sample_kernel(p_ref, coef_ref, bias_ref, ng_ref, nb_ref, o_ref, v_ref, *, steps, nb_rows):
    u = jax.nn.gelu(p_ref[:, :A_WIDTH])
    v = _layer_norm_rows(jax.nn.gelu(p_ref[:, A_WIDTH:]), ng_ref[...], nb_ref[...])
    v_ref[...] = v
    for t in range(steps):
        s = bias_ref[t:t + 1, :]
        for j in range(t + 1):
            s = s + coef_ref[t * steps + j:t * steps + j + 1, :] * v[j * nb_rows:(j + 1) * nb_rows, :]
        rs = slice(t * nb_rows, (t + 1) * nb_rows)
        o_ref[rs, :] = u[rs, :] * s


def _mixa_sample(proj, coef, bias, ng, nb, steps, nb_rows):
    m = proj.shape[0]
    kern = functools.partial(_mixa_sample_kernel, steps=steps, nb_rows=nb_rows)
    return pl.pallas_call(
        kern,
        grid=(1,),
        in_specs=[pl.BlockSpec((m, 2 * A_WIDTH), lambda i: (0, 0)),
                  pl.BlockSpec(coef.shape, lambda i: (0, 0)),
                  pl.BlockSpec(bias.shape, lambda i: (0, 0)),
                  pl.BlockSpec((1, A_WIDTH), lambda i: (0, 0)),
                  pl.BlockSpec((1, A_WIDTH), lambda i: (0, 0))],
        out_specs=[pl.BlockSpec((m, A_WIDTH), lambda i: (0, 0)),
                   pl.BlockSpec((m, A_WIDTH), lambda i: (0, 0))],
        out_shape=[jax.ShapeDtypeStruct((m, A_WIDTH), F32),
                   jax.ShapeDtypeStruct((m, A_WIDTH), F32)],
        compiler_params=_params("arbitrary"),
        name="mixa_sample",
    )(proj, coef, bias, ng, nb)


def _unit_lower_inverse(a_strict, ri, ci):
    c = a_strict.shape[0]
    eye = (ri == ci).astype(F32)
    pair = (lax.shift_right_logical(ri, 1) == lax.shift_right_logical(ci, 1)) & ((ri & 1) == 1) & ((ci & 1) == 0)
    t = eye - jnp.where(pair, a_strict, 0.0)
    n = 2
    while n < c:
        sh = n.bit_length()
        m = ((lax.shift_right_logical(ri, sh) == lax.shift_right_logical(ci, sh))
             & ((ri & n) != 0) & ((ci & n) == 0))
        an = jnp.where(m, a_strict, 0.0)
        t = t - _mm3(_mm3(t, an), t)
        n *= 2
    return t


def _mixb_prompt_kernel(qkv_ref, z_ref, bd_ref, cw_ref, nega_ref, dtb_ref, og_ref,
                        o_ref, sfin_ref, s_ref, cbuf_ref):
    c = pl.program_id(1)
    C = DN_CHUNK

    @pl.when(c == 0)
    def _init():
        s_ref[...] = jnp.zeros_like(s_ref)
        cbuf_ref[0:8, :] = jnp.zeros((8, B_CONV_CH), F32)

    x = qkv_ref[...]
    cbuf_ref[8:8 + C, :] = x
    cw = cw_ref[...]
    y = cbuf_ref[5:5 + C, :] * cw[0:1]
    y = y + cbuf_ref[6:6 + C, :] * cw[1:2]
    y = y + cbuf_ref[7:7 + C, :] * cw[2:3]
    y = y + x * cw[3:4]
    cbuf_ref[0:8, :] = x[C - 8:C, :]
    act = jax.nn.silu(y)

    qn, kn, gkk, gqk = [], [], [], []
    for qh in range(B_QK_HEADS):
        qs = act[:, qh * HEAD_DIM:(qh + 1) * HEAD_DIM]
        ks = act[:, B_KEY_WIDTH + qh * HEAD_DIM:B_KEY_WIDTH + (qh + 1) * HEAD_DIM]
        q = qs * lax.rsqrt(jnp.sum(qs * qs, -1, keepdims=True) + L2_EPS) * (HEAD_DIM ** -0.5)
        k = ks * lax.rsqrt(jnp.sum(ks * ks, -1, keepdims=True) + L2_EPS)
        qn.append(q)
        kn.append(k)
        gkk.append(_mm_nt(k, k))
        gqk.append(_mm_nt(q, k))

    bd = bd_ref[...]
    beta_all = jax.nn.sigmoid(bd)
    g_all = nega_ref[...] * _softplus(bd + dtb_ref[...])

    ri = lax.broadcasted_iota(jnp.int32, (C, C), 0)
    ci = lax.broadcasted_iota(jnp.int32, (C, C), 1)
    tril = ri >= ci
    strict = ri > ci
    tril_b = tril.astype(BF16)
    ones_b = jnp.ones((C, C), BF16)
    cum_lhs = jnp.concatenate([tril_b, ones_b], axis=0)
    triu_f = (ri <= ci).astype(F32)
    og = og_ref[...]

    for h in range(B_V_HEADS):
        qh = h // (B_V_HEADS // B_QK_HEADS)
        gcol = jnp.broadcast_to(g_all[:, B_V_HEADS + h:B_V_HEADS + h + 1], (C, HEAD_DIM))
        bcol = jnp.broadcast_to(beta_all[:, h:h + 1], (C, HEAD_DIM))
        gg = _mm_exact_lhs(cum_lhs, gcol)
        gc = gg[:C]
        gl = gg[C:]
        grow = _mm_exact_lhs(ones_b, gcol[:, :C] * triu_f)
        diff = gc[:, :C] - grow
        decay = jnp.where(tril, jnp.exp(jnp.where(tril, diff, 0.0)), 0.0)
        k = kn[qh]
        q = qn[qh]
        v = act[:, 2 * B_KEY_WIDTH + h * HEAD_DIM:2 * B_KEY_WIDTH + (h + 1) * HEAD_DIM]
        a_mat = jnp.where(strict, bcol[:, :C] * gkk[qh] * decay, 0.0)
        eg = jnp.exp(gc)
        kb = k * bcol
        rhs = jnp.concatenate([v * bcol, kb * eg], axis=1)
        t_inv = _unit_lower_inverse(a_mat, ri, ci)
        sol = _mm3(t_inv, rhs)
        u_val = sol[:, :HEAD_DIM]
        w_k = sol[:, HEAD_DIM:]
        qk = gqk[qh] * decay
        q_dec = q * eg
        k_dec = k * jnp.exp(gl - gc)
        s = s_ref[h]
        ws = _mm(jnp.concatenate([w_k, q_dec], axis=0), s)
        v_new = u_val - ws[:C]
        o = ws[C:] + _mm(qk, v_new)
        s_ref[h] = s * jnp.exp(gl[0:1, :]) + _mm_tn(k_dec, v_new)
        on = o * lax.rsqrt(jnp.mean(o * o, -1, keepdims=True) + RMS_EPS) * og
        cols = slice(h * HEAD_DIM, (h + 1) * HEAD_DIM)
        o_ref[:, cols] = on * jax.nn.silu(z_ref[:, cols])

    @pl.when(c == pl.num_programs(1) - 1)
    def _fin():
        sfin_ref[0] = s_ref[...]


def _mixb_prompt(proj, bd, conv_w, nega, dtb, og, batch, seq):
    nc = seq // DN_CHUNK
    qkv_blk = 2 * A_WIDTH // B_CONV_CH
    z_blk = (2 * A_WIDTH + B_CONV_CH) // B_VAL_WIDTH
    return pl.pallas_call(
        _mixb_prompt_kernel,
        grid=(batch, nc),
        in_specs=[pl.BlockSpec((DN_CHUNK, B_CONV_CH), lambda b, c: (b * nc + c, qkv_blk)),
                  pl.BlockSpec((DN_CHUNK, B_VAL_WIDTH), lambda b, c: (b * nc + c, z_blk)),
                  pl.BlockSpec((DN_CHUNK, LANES), lambda b, c: (b * nc + c, 0)),
                  pl.BlockSpec((B_CONV, B_CONV_CH), lambda b, c: (0, 0)),
                  pl.BlockSpec((1, LANES), lambda b, c: (0, 0)),
                  pl.BlockSpec((1, LANES), lambda b, c: (0, 0)),
                  pl.BlockSpec((1, HEAD_DIM), lambda b, c: (0, 0))],
        out_specs=[pl.BlockSpec((DN_CHUNK, B_VAL_WIDTH), lambda b, c: (b * nc + c, 0)),
                   pl.BlockSpec((1, B_V_HEADS, HEAD_DIM, HEAD_DIM), lambda b, c: (b, 0, 0, 0))],
        out_shape=[jax.ShapeDtypeStruct((batch * seq, B_VAL_WIDTH), F32),
                   jax.ShapeDtypeStruct((batch, B_V_HEADS, HEAD_DIM, HEAD_DIM), F32)],
        scratch_shapes=[pltpu.VMEM((B_V_HEADS, HEAD_DIM, HEAD_DIM), F32),
                        pltpu.VMEM((DN_CHUNK + 8, B_CONV_CH), F32)],
        compiler_params=_params("arbitrary", "arbitrary"),
        name="mixb_prompt",
    )(proj, proj, bd, conv_w, nega, dtb, og)


def _mixb_sample_pre_kernel(qkv_ref, buf_ref, bd_ref, cw_ref, nega_ref, dtb_ref,
                            act_ref, beta_ref, g_ref, *, steps, nb_rows):
    cw = cw_ref[...]

    def slab(j):
        if j < B_CONV - 1:
            return buf_ref[j * nb_rows:(j + 1) * nb_rows, :]
        jj = j - (B_CONV - 1)
        return qkv_ref[jj * nb_rows:(jj + 1) * nb_rows, :]

    for t in range(steps):
        y = slab(t) * cw[0:1]
        for i in range(1, B_CONV):
            y = y + slab(t + i) * cw[i:i + 1]
        act = jax.nn.silu(y)
        rs = slice(t * nb_rows, (t + 1) * nb_rows)
        for qh in range(B_QK_HEADS):
            cq = slice(qh * HEAD_DIM, (qh + 1) * HEAD_DIM)
            ck = slice(B_KEY_WIDTH + qh * HEAD_DIM, B_KEY_WIDTH + (qh + 1) * HEAD_DIM)
            qs = act[:, cq]
            ks = act[:, ck]
            act_ref[rs, cq] = qs * lax.rsqrt(jnp.sum(qs * qs, -1, keepdims=True) + L2_EPS) * (HEAD_DIM ** -0.5)
            act_ref[rs, ck] = ks * lax.rsqrt(jnp.sum(ks * ks, -1, keepdims=True) + L2_EPS)
        act_ref[rs, 2 * B_KEY_WIDTH:] = act[:, 2 * B_KEY_WIDTH:]
    bd = bd_ref[...]
    beta_ref[...] = jax.nn.sigmoid(bd)
    g_ref[...] = nega_ref[...] * _softplus(bd + dtb_ref[...])


def _mixb_sample_pre(proj, buf, bd, conv_w, nega, dtb, steps, nb_rows):
    m = proj.shape[0]
    qkv_blk = 2 * A_WIDTH // B_CONV_CH
    kern = functools.partial(_mixb_sample_pre_kernel, steps=steps, nb_rows=nb_rows)
    return pl.pallas_call(
        kern,
        grid=(1,),
        in_specs=[pl.BlockSpec((m, B_CONV_CH), lambda i: (0, qkv_blk)),
                  pl.BlockSpec(buf.shape, lambda i: (0, 0)),
                  pl.BlockSpec((m, LANES), lambda i: (0, 0)),
                  pl.BlockSpec((B_CONV, B_CONV_CH), lambda i: (0, 0)),
                  pl.BlockSpec((1, LANES), lambda i: (0, 0)),
                  pl.BlockSpec((1, LANES), lambda i: (0, 0))],
        out_specs=[pl.BlockSpec((m, B_CONV_CH), lambda i: (0, 0)),
                   pl.BlockSpec((m, LANES), lambda i: (0, 0)),
                   pl.BlockSpec((m, LANES), lambda i: (0, 0))],
        out_shape=[jax.ShapeDtypeStruct((m, B_CONV_CH), F32),
                   jax.ShapeDtypeStruct((m, LANES), F32),
                   jax.ShapeDtypeStruct((m, LANES), F32)],
        compiler_params=_params("arbitrary"),
        name="mixb_sample_pre",
    )(proj, buf, bd, conv_w, nega, dtb)


def _mixb_sample_rec_kernel(kq_ref, v_ref, z_ref, beta_ref, g_ref, s0_ref, og_ref,
                            o_ref, s_out_ref, *, steps, pairs):
    og = og_ref[...]
    zpad = jnp.zeros((HEAD_DIM - 8, HEAD_DIM), F32)

    def body(i, carry):
        bi = i // B_V_HEADS
        hi = i % B_V_HEADS
        kq = kq_ref[bi, hi]
        s0 = s0_ref[bi, hi]
        kqs = _mm(kq, s0)
        g = g_ref[bi, hi]
        beta = beta_ref[bi, hi]
        v = v_ref[bi, hi]
        gc = [g[0:1]]
        for t in range(1, steps):
            gc.append(gc[-1] + g[t:t + 1])
        k = [kq[t:t + 1] for t in range(steps)]
        q = [kq[4 + t:5 + t] for t in range(steps)]
        d = []
        for t in range(steps):
            acc = v[t:t + 1] - jnp.exp(gc[t]) * kqs[t:t + 1]
            for j in range(t):
                kk = jnp.sum(k[j] * k[t], -1, keepdims=True)
                acc = acc - jnp.exp(gc[t] - gc[j]) * kk * d[j]
            d.append(beta[t:t + 1] * acc)
        outs = []
        for t in range(steps):
            o = jnp.exp(gc[t]) * kqs[4 + t:5 + t]
            for j in range(t + 1):
                qk = jnp.sum(k[j] * q[t], -1, keepdims=True)
                o = o + jnp.exp(gc[t] - gc[j]) * qk * d[j]
            outs.append(o * lax.rsqrt(jnp.mean(o * o, -1, keepdims=True) + RMS_EPS) * og)
        o8 = jnp.concatenate(outs + [jnp.zeros((8 - steps, HEAD_DIM), F32)], axis=0)
        o_ref[bi, hi] = o8 * jax.nn.silu(z_ref[bi, hi])
        k_dec = jnp.concatenate([jnp.exp(gc[-1] - gc[j]) * k[j] for j in range(steps)]
                                + [jnp.zeros((8 - steps, HEAD_DIM), F32)], axis=0)
        d8 = jnp.concatenate(d + [jnp.zeros((8 - steps, HEAD_DIM), F32)], axis=0)
        k_pad = jnp.concatenate([k_dec, zpad], axis=0)
        d_pad = jnp.concatenate([d8, zpad], axis=0)
        s_out_ref[bi, hi] = s0 * jnp.exp(gc[-1]) + _mm(k_pad.T, d_pad)
        return carry

    lax.fori_loop(0, pairs, body, 0)


def _mixb_sample_rec(kq, v, z, beta, g, s0, og, steps, bb):
    nb = kq.shape[0]
    tile = pl.BlockSpec((bb, B_V_HEADS, 8, HEAD_DIM), lambda i: (i, 0, 0, 0))
    st = pl.BlockSpec((bb, B_V_HEADS, HEAD_DIM, HEAD_DIM), lambda i: (i, 0, 0, 0))
    kern = functools.partial(_mixb_sample_rec_kernel, steps=steps, pairs=bb * B_V_HEADS)
    return pl.pallas_call(
        kern,
        grid=(nb // bb,),
        in_specs=[tile, tile, tile, tile, tile, st, pl.BlockSpec((1, HEAD_DIM), lambda i: (0, 0))],
        out_specs=[tile, st],
        out_shape=[jax.ShapeDtypeStruct((nb, B_V_HEADS, 8, HEAD_DIM), F32),
                   jax.ShapeDtypeStruct((nb, B_V_HEADS, HEAD_DIM, HEAD_DIM), F32)],
        compiler_params=_params("arbitrary"),
        name="mixb_sample_rec",
    )(kq, v, z, beta, g, s0, og)


def _outproj_kernel(a_ref, b_ref, x_ref, g1_ref, sc2_ref, sh2_ref, w_ref, lg_ref, lb_ref, wr_ref,
                    x1_ref, h2_ref, logit_ref):
    mix = (jnp.dot(a_ref[...].astype(BF16), w_ref[:A_WIDTH, :], preferred_element_type=F32)
           + jnp.dot(b_ref[...].astype(BF16), w_ref[A_WIDTH:, :], preferred_element_type=F32))
    x1 = _layer_norm_rows(ALPHA * x_ref[...] + g1_ref[...] * mix, lg_ref[...], lb_ref[...])
    x1_ref[...] = x1
    h2 = x1 * (1.0 + sc2_ref[...]) + sh2_ref[...]
    h2_ref[...] = h2
    logit_ref[...] = _mm3(h2, wr_ref[...])


def _outproj(a_out, b_out, x, mod, w_out, lg, lb, w_router, tm, rows_per_mod):
    m = x.shape[0]
    row = lambda w: pl.BlockSpec((tm, w), lambda i: (i, 0))
    full = lambda s: pl.BlockSpec(s, lambda i: (0, 0))
    return pl.pallas_call(
        _outproj_kernel,
        grid=(m // tm,),
        in_specs=[row(A_WIDTH), row(B_VAL_WIDTH), row(D_MODEL),
                  _mod_spec(mod, 2, tm, rows_per_mod), _mod_spec(mod, 4, tm, rows_per_mod),
                  _mod_spec(mod, 3, tm, rows_per_mod),
                  full((D_MODEL, D_MODEL)), full((1, D_MODEL)), full((1, D_MODEL)),
                  full((D_MODEL, LANES))],
        out_specs=[row(D_MODEL), row(D_MODEL), row(LANES)],
        out_shape=[jax.ShapeDtypeStruct((m, D_MODEL), F32),
                   jax.ShapeDtypeStruct((m, D_MODEL), F32),
                   jax.ShapeDtypeStruct((m, LANES), F32)],
        compiler_params=_params("arbitrary"),
        name="outproj",
    )(a_out, b_out, x, mod, mod, mod, w_out, lg, lb, w_router)


def _router_kernel(logit_ref, bias_ref, route_ref, count_ref, base_ref):
    i = pl.program_id(0)

    @pl.when(i == 0)
    def _():
        base_ref[...] = jnp.zeros_like(base_ref)

    lg = logit_ref[...]
    tm = lg.shape[0]
    bias = bias_ref[...]
    lane = lax.broadcasted_iota(jnp.int32, lg.shape, 1)
    neg = -jnp.inf

    def first_argmax(score):
        mx = jnp.max(score, -1, keepdims=True)
        return jnp.min(jnp.where(score == mx, lane, LANES), -1, keepdims=True)

    def pick(vals, idx):
        return jnp.sum(jnp.where(lane == idx, vals, 0.0), -1, keepdims=True)

    gmask = lane < N_GROUPS
    mg = jnp.max(jnp.where(gmask, lg, neg), -1, keepdims=True)
    eg = jnp.where(gmask, jnp.exp(jnp.where(gmask, lg - mg, 0.0)), 0.0)
    pg = eg / jnp.sum(eg, -1, keepdims=True)
    sel_g = first_argmax(jnp.where(gmask, lg + bias, neg))
    p_sel = pick(pg, sel_g)

    lo = N_GROUPS + sel_g * EXPERTS_PER_GROUP
    emask = (lane >= lo) & (lane < lo + EXPERTS_PER_GROUP)
    me = jnp.max(jnp.where(emask, lg, neg), -1, keepdims=True)
    ee = jnp.where(emask, jnp.exp(jnp.where(emask, lg - me, 0.0)), 0.0)
    pe = ee / jnp.sum(ee, -1, keepdims=True)
    score = jnp.where(emask, pe + bias, neg)
    i1 = first_argmax(score)
    i2 = first_argmax(jnp.where(lane == i1, neg, score))
    w1 = pick(pe, i1)
    w2 = pick(pe, i2)
    wsum = w1 + w2
    gate1 = w1 / wsum * p_sel
    gate2 = w2 / wsum * p_sel

    hot = ((lane == i1) | (lane == i2)).astype(BF16)
    ri = lax.broadcasted_iota(jnp.int32, (tm, tm), 0)
    ci = lax.broadcasted_iota(jnp.int32, (tm, tm), 1)
    before = jnp.dot((ri > ci).astype(BF16), hot, preferred_element_type=F32) + base_ref[...]
    rank1 = pick(before, i1)
    rank2 = pick(before, i2)
    base_ref[...] = base_ref[...] + jnp.sum(hot.astype(F32), 0, keepdims=True)
    count_ref[...] = base_ref[...]

    out = jnp.where(lane == 0, (i1 - N_GROUPS).astype(F32), 0.0)
    out = jnp.where(lane == 1, (i2 - N_GROUPS).astype(F32), out)
    out = jnp.where(lane == 2, gate1, out)
    out = jnp.where(lane == 3, gate2, out)
    out = jnp.where(lane == 4, rank1, out)
    out = jnp.where(lane == 5, rank2, out)
    route_ref[...] = out


def _router(logits, bias):
    m = logits.shape[0]
    return pl.pallas_call(
        _router_kernel,
        grid=(m // ROUTE_TILE,),
        in_specs=[pl.BlockSpec((ROUTE_TILE, LANES), lambda i: (i, 0)),
                  pl.BlockSpec((1, LANES), lambda i: (0, 0))],
        out_specs=[pl.BlockSpec((ROUTE_TILE, LANES), lambda i: (i, 0)),
                   pl.BlockSpec((1, LANES), lambda i: (0, 0))],
        out_shape=[jax.ShapeDtypeStruct((m, LANES), F32),
                   jax.ShapeDtypeStruct((1, LANES), F32)],
        scratch_shapes=[pltpu.VMEM((1, LANES), F32)],
        compiler_params=_params("arbitrary"),
        name="router",
    )(logits, bias)


def _expert_kernel(be_ref, nv_ref, src_ref, dst_ref, h_ref, wg_ref, wu_ref, wd_ref, y_ref,
                   xbuf, ybuf, wg_s, wu_s, wd_s, gsem, ssem):
    i = pl.program_id(0)
    nv = nv_ref[0]
    slot = i % 2

    def gather_start(blk, s):
        def body(r, carry):
            tok = src_ref[blk * MOE_ROWS + r]
            pltpu.make_async_copy(h_ref.at[pl.ds(tok, 1)], xbuf.at[s, pl.ds(r, 1)], gsem.at[s]).start()
            return carry

        lax.fori_loop(0, MOE_ROWS, body, 0, unroll=8)

    def scatter_start(blk, s):
        def body(r, carry):
            row = dst_ref[blk * MOE_ROWS + r]
            pltpu.make_async_copy(ybuf.at[s, pl.ds(r, 1)], y_ref.at[pl.ds(row, 1)], ssem.at[s]).start()
            return carry

        lax.fori_loop(0, MOE_ROWS, body, 0, unroll=8)

    def gather_wait(s):
        pltpu.make_async_copy(h_ref.at[pl.ds(0, MOE_ROWS)], xbuf.at[s], gsem.at[s]).wait()

    def scatter_wait(s):
        pltpu.make_async_copy(ybuf.at[s], y_ref.at[pl.ds(0, MOE_ROWS)], ssem.at[s]).wait()

    @pl.when(i == 0)
    def _():
        ybuf[...] = jnp.zeros_like(ybuf)
        base = y_ref.shape[0] - 2 * MOE_ROWS
        fills = [pltpu.make_async_copy(ybuf.at[s], y_ref.at[pl.ds(base + s * MOE_ROWS, MOE_ROWS)], ssem.at[s])
                 for s in range(2)]
        for f in fills:
            f.start()
        for f in fills:
            f.wait()

    @pl.when((i == 0) & (nv > 0))
    def _():
        gather_start(0, 0)

    @pl.when(i + 1 < nv)
    def _():
        gather_start(i + 1, 1 - slot)

    prev = be_ref[jnp.maximum(i - 1, 0)]

    @pl.when((i < nv) & ((i == 0) | (be_ref[i] != prev)))
    def _():
        wg_s[...] = wg_ref[0].astype(BF16)
        wu_s[...] = wu_ref[0].astype(BF16)
        wd_s[...] = wd_ref[0].astype(BF16)

    @pl.when(i < nv)
    def _():
        gather_wait(slot)
        x = xbuf[slot].astype(BF16)
        hg = jnp.dot(x, wg_s[...], preferred_element_type=F32)
        hu = jnp.dot(x, wu_s[...], preferred_element_type=F32)
        hid = (jax.nn.silu(hg) * hu).astype(BF16)
        y = jnp.dot(hid, wd_s[...], preferred_element_type=F32)

        @pl.when(i >= 2)
        def _():
            scatter_wait(slot)

        ybuf[slot] = y
        scatter_start(i, slot)

    @pl.when(i == pl.num_programs(0) - 1)
    def _():
        @pl.when(nv >= 2)
        def _():
            scatter_wait(nv % 2)

        @pl.when(nv >= 1)
        def _():
            scatter_wait((nv - 1) % 2)


def _experts(block_e, n_valid, slot_src, slot_dst, h2, w_gate, w_up, w_down, n_out_rows):
    n_blocks = block_e.shape[0]
    return pl.pallas_call(
        _expert_kernel,
        grid_spec=pltpu.PrefetchScalarGridSpec(
            num_scalar_prefetch=4,
            grid=(n_blocks,),
            in_specs=[pl.BlockSpec(memory_space=pl.ANY),
                      pl.BlockSpec((1, D_MODEL, D_EXPERT), lambda i, be, *_: (be[i], 0, 0)),
                      pl.BlockSpec((1, D_MODEL, D_EXPERT), lambda i, be, *_: (be[i], 0, 0)),
                      pl.BlockSpec((1, D_EXPERT, D_MODEL), lambda i, be, *_: (be[i], 0, 0))],
            out_specs=pl.BlockSpec(memory_space=pl.ANY),
            scratch_shapes=[pltpu.VMEM((2, MOE_ROWS, D_MODEL), F32),
                            pltpu.VMEM((2, MOE_ROWS, D_MODEL), F32),
                            pltpu.VMEM((D_MODEL, D_EXPERT), BF16),
                            pltpu.VMEM((D_MODEL, D_EXPERT), BF16),
                            pltpu.VMEM((D_EXPERT, D_MODEL), BF16),
                            pltpu.SemaphoreType.DMA((2,)),
                            pltpu.SemaphoreType.DMA((2,))]),
        out_shape=jax.ShapeDtypeStruct((n_out_rows, D_MODEL), F32),
        compiler_params=_params("arbitrary"),
        name="experts",
    )(block_e, n_valid, slot_src, slot_dst, h2, w_gate, w_up, w_down)


def _final_kernel(x1_ref, y0_ref, y1_ref, route_ref, g2_ref, lg_ref, lb_ref, o_ref):
    route = route_ref[...]
    ff = y0_ref[...] * route[:, 2:3] + y1_ref[...] * route[:, 3:4]
    o_ref[...] = _layer_norm_rows(ALPHA * x1_ref[...] + g2_ref[...] * ff, lg_ref[...], lb_ref[...])


def _final(x1, y01, route, mod, lg, lb, tm, rows_per_mod, row0, n_rows, n_tok):
    off = row0 // tm
    off1 = (n_tok + row0) // tm
    return pl.pallas_call(
        _final_kernel,
        grid=(n_rows // tm,),
        in_specs=[pl.BlockSpec((tm, D_MODEL), lambda i: (off + i, 0)),
                  pl.BlockSpec((tm, D_MODEL), lambda i: (off + i, 0)),
                  pl.BlockSpec((tm, D_MODEL), lambda i: (off1 + i, 0)),
                  pl.BlockSpec((tm, LANES), lambda i: (off + i, 0)),
                  _mod_spec(mod, 5, tm, rows_per_mod),
                  pl.BlockSpec((1, D_MODEL), lambda i: (0, 0)),
                  pl.BlockSpec((1, D_MODEL), lambda i: (0, 0))],
        out_specs=pl.BlockSpec((tm, D_MODEL), lambda i: (i, 0)),
        out_shape=jax.ShapeDtypeStruct((n_rows, D_MODEL), F32),
        compiler_params=_params("arbitrary"),
        name="final",
    )(x1, y01, y01, route, mod, lg, lb)


def _lane_row(vec, offset):
    return jnp.zeros((1, LANES), F32).at[0, offset:offset + vec.shape[0]].set(vec.astype(F32))


def kernel(x_prompt, x_sample, state_conv, state_ssm, c_prompt, c_sample, w_ada, b_ada, w_in, a_ws, a_bs, a_norm_g, a_norm_b, b_conv_w, b_a_log, b_dt_bias, b_onorm_g, w_out, ln1_g, ln1_b, w_router_g, b_router_g, w_router_e, b_router_e, w_gate, w_up, w_down, ln2_g, ln2_b):
    batch, seq, d = x_prompt.shape
    nb, steps, _ = x_sample.shape
    n_p = batch * seq
    n_s = nb * steps
    n_tok = n_p + n_s
    l = 0

    c_rows = batch + nb
    c_pad = (-c_rows) % 8
    c_all = jnp.concatenate([c_prompt, c_sample, jnp.zeros((c_pad, d), F32)], axis=0)
    mod = _ada(c_all, w_ada[l], b_ada[l].reshape(1, -1))
    mod_p = mod[:batch].reshape(batch, 1, 6 * d)
    mod_s = mod[batch:batch + nb]

    w_in_l = w_in[l]
    w_main = w_in_l[:, :MAIN_WIDTH].astype(BF16)
    w_bd = jnp.zeros((d, LANES), BF16).at[:, :2 * B_V_HEADS].set(w_in_l[:, MAIN_WIDTH:].astype(BF16))
    xp = x_prompt.reshape(n_p, d)
    xs = jnp.swapaxes(x_sample, 0, 1).reshape(n_s, d)
    proj_p, bd_p = _inproj(xp, mod_p, w_main, w_bd, 512, seq)
    proj_s, bd_s = _inproj(xs, mod_s, w_main, w_bd, nb, 0)

    ng = a_norm_g[l].reshape(1, -1)
    nbias = a_norm_b[l].reshape(1, -1)
    bias_tile = jnp.repeat(a_bs[l].T, HEAD_DIM, axis=1)
    a_out_p = _mixa_prompt(proj_p, a_ws[l], bias_tile, ng, nbias, 256)
    coef = jnp.repeat(jnp.transpose(a_ws[l][:, :steps, :steps], (1, 2, 0)).reshape(steps * steps, A_HEADS),
                      HEAD_DIM, axis=1)
    a_out_s, chunkv = _mixa_sample(proj_s, coef, bias_tile[:steps], ng, nbias, steps, nb)

    nega = _lane_row(-jnp.exp(b_a_log[l].astype(F32)), B_V_HEADS)
    dtb = _lane_row(b_dt_bias[l], B_V_HEADS)
    og = b_onorm_g[l].reshape(1, -1)
    b_out_p, ssm_p = _mixb_prompt(proj_p, bd_p, b_conv_w[l], nega, dtb, og, batch, seq)
    qkv0 = 2 * A_WIDTH
    conv_p = proj_p.reshape(batch, seq, MAIN_WIDTH)[:, seq - (B_CONV - 1):, qkv0:qkv0 + B_CONV_CH]

    buf_s = jnp.swapaxes(state_conv[l], 0, 1).reshape((B_CONV - 1) * nb, B_CONV_CH)
    act_s, beta_s, g_s = _mixb_sample_pre(proj_s, buf_s, bd_s, b_conv_w[l], nega, dtb, steps, nb)
    rep = B_V_HEADS // B_QK_HEADS
    act4 = act_s.reshape(steps, nb, B_CONV_CH)
    q8 = jnp.repeat(act4[..., :B_KEY_WIDTH].reshape(steps, nb, B_QK_HEADS, HEAD_DIM), rep, axis=2)
    k8 = jnp.repeat(act4[..., B_KEY_WIDTH:2 * B_KEY_WIDTH].reshape(steps, nb, B_QK_HEADS, HEAD_DIM), rep, axis=2)

    def to_tiles(a):
        a = jnp.transpose(a, (1, 2, 0, 3))
        return jnp.pad(a, ((0, 0), (0, 0), (0, 8 - a.shape[2]), (0, 0)))

    kq_t = jnp.concatenate([to_tiles(k8)[:, :, :4], to_tiles(q8)[:, :, :4]], axis=2)
    v_t = to_tiles(act4[..., 2 * B_KEY_WIDTH:].reshape(steps, nb, B_V_HEADS, HEAD_DIM))
    z_t = to_tiles(proj_s[:, qkv0 + B_CONV_CH:].reshape(steps, nb, B_V_HEADS, HEAD_DIM))

    def gate_tiles(a, off):
        a = a[:, off:off + B_V_HEADS].reshape(steps, nb, B_V_HEADS, 1)
        return to_tiles(jnp.broadcast_to(a, (steps, nb, B_V_HEADS, HEAD_DIM)))

    o_t, ssm_s = _mixb_sample_rec(kq_t, v_t, z_t, gate_tiles(beta_s, 0), gate_tiles(g_s, B_V_HEADS),
                                  state_ssm[l], og, steps, 8)
    b_out_s = jnp.transpose(o_t[:, :, :steps], (2, 0, 1, 3)).reshape(n_s, B_VAL_WIDTH)
    conv_s = jnp.swapaxes(proj_s.reshape(steps, nb, MAIN_WIDTH)[steps - (B_CONV - 1):, :, qkv0:qkv0 + B_CONV_CH], 0, 1)

    w_out_b = w_out[l].astype(BF16)
    lg1 = ln1_g[l].reshape(1, -1)
    lb1 = ln1_b[l].reshape(1, -1)
    w_router = jnp.zeros((d, LANES), F32)
    w_router = w_router.at[:, :N_GROUPS].set(w_router_g[l]).at[:, N_GROUPS:N_GROUPS + N_EXPERTS].set(w_router_e[l])
    x1_p, h2_p, lgt_p = _outproj(a_out_p, b_out_p, xp, mod_p, w_out_b, lg1, lb1, w_router, 256, seq)
    x1_s, h2_s, lgt_s = _outproj(a_out_s, b_out_s, xs, mod_s, w_out_b, lg1, lb1, w_router, nb, 0)
    x1 = jnp.concatenate([x1_p, x1_s], axis=0)
    h2 = jnp.concatenate([h2_p, h2_s], axis=0)
    logits = jnp.concatenate([lgt_p, lgt_s], axis=0)

    r_bias = jnp.zeros((1, LANES), F32)
    r_bias = r_bias.at[0, :N_GROUPS].set(b_router_g[l]).at[0, N_GROUPS:N_GROUPS + N_EXPERTS].set(b_router_e[l])
    route, counts_row = _router(logits, r_bias)
    counts = counts_row[0, N_GROUPS:N_GROUPS + N_EXPERTS].astype(jnp.int32)
    eid = route[:, 0:2].astype(jnp.int32)
    rank = route[:, 4:6].astype(jnp.int32)
    padded = (counts + MOE_ROWS - 1) // MOE_ROWS * MOE_ROWS
    pend = jnp.cumsum(padded)
    pstart = pend - padded
    dest = pstart[eid] + rank
    n_blocks = -(-(2 * n_tok) // MOE_ROWS) + N_EXPERTS
    n_slots = n_blocks * MOE_ROWS
    tok = jnp.arange(n_tok, dtype=jnp.int32)
    out_row = jnp.stack([tok, n_tok + tok], axis=1)
    slot_row = jnp.full((n_slots,), -1, jnp.int32).at[dest.reshape(-1)].set(out_row.reshape(-1))
    slot_id = jnp.arange(n_slots, dtype=jnp.int32)
    trash = 2 * n_tok + (slot_id // MOE_ROWS % 2) * MOE_ROWS + slot_id % MOE_ROWS
    slot_src = jnp.where(slot_row < 0, 0, slot_row % n_tok)
    slot_dst = jnp.where(slot_row < 0, trash, slot_row)
    blk0 = jnp.arange(n_blocks, dtype=jnp.int32) * MOE_ROWS
    block_e = jnp.minimum(jnp.searchsorted(pend, blk0, side='right'), N_EXPERTS - 1).astype(jnp.int32)
    n_valid = (pend[-1:] // MOE_ROWS).astype(jnp.int32)

    y01 = _experts(block_e, n_valid, slot_src, slot_dst, h2, w_gate[l], w_up[l], w_down[l],
                   2 * n_tok + 2 * MOE_ROWS)

    lg2 = ln2_g[l].reshape(1, -1)
    lb2 = ln2_b[l].reshape(1, -1)
    y_p = _final(x1, y01, route, mod_p, lg2, lb2, 256, seq, 0, n_p, n_tok)
    y_s = _final(x1, y01, route, mod_s, lg2, lb2, nb, 0, n_p, n_s, n_tok)

    y_prompt = y_p.reshape(batch, seq, d)
    y_sample = jnp.swapaxes(y_s.reshape(steps, nb, d), 0, 1)
    chunkv_s = jnp.swapaxes(chunkv.reshape(steps, nb, A_HEADS, HEAD_DIM), 0, 1)
    return (y_prompt, y_sample, conv_p[None], ssm_p[None], conv_s[None], ssm_s[None], chunkv_s[None])
```

```python
import functools

import jax
import jax.numpy as jnp
from jax import lax
from jax.experimental import pallas as pl
from jax.experimental.pallas import tpu as pltpu

F32 = jnp.float32
BF16 = jnp.bfloat16

D_MODEL = 2048
DEPTH = 1
A_HEADS = 8
HEAD_DIM = 128
A_WIDTH = 1024
A_CHUNK = 128
B_QK_HEADS = 4
B_V_HEADS = 8
B_KEY_WIDTH = 512
B_VAL_WIDTH = 1024
B_CONV = 4
B_CONV_CH = 2048
DN_CHUNK = 64
MAIN_WIDTH = 2 * A_WIDTH + B_CONV_CH + B_VAL_WIDTH
N_GROUPS = 4
EXPERTS_PER_GROUP = 8
N_EXPERTS = 32
D_EXPERT = 512
ALPHA = (2 * DEPTH) ** 0.25
LN_EPS = 1e-5
RMS_EPS = 1e-6
L2_EPS = 1e-6

LANES = 128
VMEM_LIMIT = 56 * 1024 * 1024
MOE_ROWS = 256
ROUTE_TILE = 512
MIXB_GROUP = 4


def _params(*sem):
    return pltpu.CompilerParams(dimension_semantics=sem, vmem_limit_bytes=VMEM_LIMIT)


def _mm(a, b):
    return jnp.dot(a.astype(BF16), b.astype(BF16), preferred_element_type=F32)


def _mm_nt(a, b):
    return lax.dot_general(a.astype(BF16), b.astype(BF16), (((1,), (1,)), ((), ())),
                           preferred_element_type=F32)


def _mm_tn(a, b):
    return lax.dot_general(a.astype(BF16), b.astype(BF16), (((0,), (0,)), ((), ())),
                           preferred_element_type=F32)


def _split(x):
    hi = x.astype(BF16)
    lo = (x - hi.astype(F32)).astype(BF16)
    return hi, lo


def _mm_exact_lhs(a_bf16, b):
    hi, lo = _split(b)
    return (jnp.dot(a_bf16, hi, preferred_element_type=F32)
            + jnp.dot(a_bf16, lo, preferred_element_type=F32))


def _mm3(a, b):
    ah, al = _split(a)
    bh, bl = _split(b)
    return (jnp.dot(ah, bh, preferred_element_type=F32) + jnp.dot(ah, bl, preferred_element_type=F32)
            + jnp.dot(al, bh, preferred_element_type=F32))


def _softplus(x):
    return jnp.maximum(x, 0.0) + jnp.log1p(jnp.exp(-jnp.abs(x)))


def _layer_norm_rows(x, g, b):
    mu = jnp.mean(x, -1, keepdims=True)
    xc = x - mu
    var = jnp.mean(xc * xc, -1, keepdims=True)
    return xc * lax.rsqrt(var + LN_EPS) * g + b


def _ada_kernel(c_ref, w_ref, b_ref, o_ref):
    a = jax.nn.silu(c_ref[...]).astype(BF16)
    o_ref[...] = jnp.dot(a, w_ref[...].astype(BF16), preferred_element_type=F32) + b_ref[...]


def _ada(c_all, w_ada, b_ada):
    rows = c_all.shape[0]
    tn = 1024
    return pl.pallas_call(
        _ada_kernel,
        grid=(6 * D_MODEL // tn,),
        in_specs=[pl.BlockSpec((rows, D_MODEL), lambda j: (0, 0)),
                  pl.BlockSpec((D_MODEL, tn), lambda j: (0, j)),
                  pl.BlockSpec((1, tn), lambda j: (0, j))],
        out_specs=pl.BlockSpec((rows, tn), lambda j: (0, j)),
        out_shape=jax.ShapeDtypeStruct((rows, 6 * D_MODEL), F32),
        compiler_params=_params("arbitrary"),
        name="ada",
    )(c_all, w_ada, b_ada)


def _inproj_kernel(x_ref, sc_ref, sh_ref, w_ref, wbd_ref, o_ref, bd_ref, h_ref):
    @pl.when(pl.program_id(1) == 0)
    def _():
        h = (x_ref[...] * (1.0 + sc_ref[...]) + sh_ref[...]).astype(BF16)
        h_ref[...] = h
        bd_ref[...] = jnp.dot(h, wbd_ref[...], preferred_element_type=F32)

    o_ref[...] = jnp.dot(h_ref[...], w_ref[...], preferred_element_type=F32)


def _mod_spec(mod, col, tm, rows_per_mod):
    if rows_per_mod:
        tiles = rows_per_mod // tm
        return pl.BlockSpec((None, 1, D_MODEL), lambda i, *_: (i // tiles, 0, col))
    return pl.BlockSpec((tm, D_MODEL), lambda i, *_: (0, col))


def _inproj(x, mod, w_main, w_bd, tm, rows_per_mod):
    m = x.shape[0]
    tn = 1024
    return pl.pallas_call(
        _inproj_kernel,
        grid=(m // tm, MAIN_WIDTH // tn),
        in_specs=[pl.BlockSpec((tm, D_MODEL), lambda i, j: (i, 0)),
                  _mod_spec(mod, 1, tm, rows_per_mod),
                  _mod_spec(mod, 0, tm, rows_per_mod),
                  pl.BlockSpec((D_MODEL, tn), lambda i, j: (0, j)),
                  pl.BlockSpec((D_MODEL, LANES), lambda i, j: (0, 0))],
        out_specs=[pl.BlockSpec((tm, tn), lambda i, j: (i, j)),
                   pl.BlockSpec((tm, LANES), lambda i, j: (i, 0))],
        out_shape=[jax.ShapeDtypeStruct((m, MAIN_WIDTH), F32),
                   jax.ShapeDtypeStruct((m, LANES), F32)],
        scratch_shapes=[pltpu.VMEM((tm, D_MODEL), BF16)],
        compiler_params=_params("arbitrary", "arbitrary"),
        name="inproj",
    )(x, mod, mod, w_main, w_bd)


def _mixa_prompt_kernel(p_ref, ws_ref, bias_ref, ng_ref, nb_ref, o_ref):
    rows = p_ref.shape[0]
    u = jax.nn.gelu(p_ref[:, :A_WIDTH])
    v = _layer_norm_rows(jax.nn.gelu(p_ref[:, A_WIDTH:]), ng_ref[...], nb_ref[...])
    ri = lax.broadcasted_iota(jnp.int32, (A_CHUNK, A_CHUNK), 0)
    ci = lax.broadcasted_iota(jnp.int32, (A_CHUNK, A_CHUNK), 1)
    for h in range(A_HEADS):
        w = jnp.where(ri >= ci, ws_ref[h], 0.0).astype(BF16)
        cols = slice(h * HEAD_DIM, (h + 1) * HEAD_DIM)
        for c in range(rows // A_CHUNK):
            rs = slice(c * A_CHUNK, (c + 1) * A_CHUNK)
            s = jnp.dot(w, v[rs, cols].astype(BF16), preferred_element_type=F32) + bias_ref[:, cols]
            o_ref[rs, cols] = u[rs, cols] * s


def _mixa_prompt(proj, a_ws, bias_tile, ng, nb, tm):
    m = proj.shape[0]
    return pl.pallas_call(
        _mixa_prompt_kernel,
        grid=(m // tm,),
        in_specs=[pl.BlockSpec((tm, 2 * A_WIDTH), lambda i: (i, 0)),
                  pl.BlockSpec((A_HEADS, A_CHUNK, A_CHUNK), lambda i: (0, 0, 0)),
                  pl.BlockSpec((A_CHUNK, A_WIDTH), lambda i: (0, 0)),
                  pl.BlockSpec((1, A_WIDTH), lambda i: (0, 0)),
                  pl.BlockSpec((1, A_WIDTH), lambda i: (0, 0))],
        out_specs=pl.BlockSpec((tm, A_WIDTH), lambda i: (i, 0)),
        out_shape=jax.ShapeDtypeStruct((m, A_WIDTH), F32),
        compiler_params=_params("arbitrary"),
        name="mixa_prompt",
    )(proj, a_ws, bias_tile, ng, nb)


def _mixa_sample_kernel(p_ref, coef_ref, bias_ref, ng_ref, nb_ref, o_ref, v_ref, *, steps, nb_rows):
    u = jax.nn.gelu(p_ref[:, :A_WIDTH])
    v = _layer_norm_rows(jax.nn.gelu(p_ref[:, A_WIDTH:]), ng_ref[...], nb_ref[...])
    v_ref[...] = v
    for t in range(steps):
        s = bias_ref[t:t + 1, :]
        for j in range(t + 1):
            s = s + coef_ref[t * steps + j:t * steps + j + 1, :] * v[j * nb_rows:(j + 1) * nb_rows, :]
        rs = slice(t * nb_rows, (t + 1) * nb_rows)
        o_ref[rs, :] = u[rs, :] * s


def _mixa_sample(proj, coef, bias, ng, nb, steps, nb_rows):
    m = proj.shape[0]
    kern = functools.partial(_mixa_sample_kernel, steps=steps, nb_rows=nb_rows)
    return pl.pallas_call(
        kern,
        grid=(1,),
        in_specs=[pl.BlockSpec((m, 2 * A_WIDTH), lambda i: (0, 0)),
                  pl.BlockSpec(coef.shape, lambda i: (0, 0)),
                  pl.BlockSpec(bias.shape, lambda i: (0, 0)),
                  pl.BlockSpec((1, A_WIDTH), lambda i: (0, 0)),
                  pl.BlockSpec((1, A_WIDTH), lambda i: (0, 0))],
        out_specs=[pl.BlockSpec((m, A_WIDTH), lambda i: (0, 0)),
                   pl.BlockSpec((m, A_WIDTH), lambda i: (0, 0))],
        out_shape=[jax.ShapeDtypeStruct((m, A_WIDTH), F32),
                   jax.ShapeDtypeStruct((m, A_WIDTH), F32)],
        compiler_params=_params("arbitrary"),
        name="mixa_sample",
    )(proj, coef, bias, ng, nb)


def _unit_lower_inverse(a_strict, ri, ci, block):
    eye = (ri == ci).astype(F32)
    pair = (lax.shift_right_logical(ri, 1) == lax.shift_right_logical(ci, 1)) & ((ri & 1) == 1) & ((ci & 1) == 0)
    t = eye - jnp.where(pair, a_strict, 0.0)
    n = 2
    while n < block:
        sh = n.bit_length()
        m = ((lax.shift_right_logical(ri, sh) == lax.shift_right_logical(ci, sh))
             & ((ri & n) != 0) & ((ci & n) == 0))
        an = jnp.where(m, a_strict, 0.0)
        t = t - _mm(_mm(t, an), t)
        n *= 2
    return t


def _mixb_prompt_kernel(qkv_ref, z_ref, bd_ref, cw_ref, nega_ref, dtb_ref, og_ref,
                        o_ref, sfin_ref, s_ref, cbuf_ref):
    c = pl.program_id(1)
    C = DN_CHUNK

    @pl.when(c == 0)
    def _init():
        s_ref[...] = jnp.zeros_like(s_ref)
        cbuf_ref[0:8, :] = jnp.zeros((8, B_CONV_CH), F32)

    x = qkv_ref[...]
    cbuf_ref[8:8 + C, :] = x
    cw = cw_ref[...]
    y = cbuf_ref[5:5 + C, :] * cw[0:1]
    y = y + cbuf_ref[6:6 + C, :] * cw[1:2]
    y = y + cbuf_ref[7:7 + C, :] * cw[2:3]
    y = y + x * cw[3:4]
    cbuf_ref[0:8, :] = x[C - 8:C, :]
    act = jax.nn.silu(y)

    qn, kn = [], []
    for qh in range(B_QK_HEADS):
        qs = act[:, qh * HEAD_DIM:(qh + 1) * HEAD_DIM]
        ks = act[:, B_KEY_WIDTH + qh * HEAD_DIM:B_KEY_WIDTH + (qh + 1) * HEAD_DIM]
        qn.append(qs * lax.rsqrt(jnp.sum(qs * qs, -1, keepdims=True) + L2_EPS) * (HEAD_DIM ** -0.5))
        kn.append(ks * lax.rsqrt(jnp.sum(ks * ks, -1, keepdims=True) + L2_EPS))

    bd = bd_ref[...]
    beta_all = jax.nn.sigmoid(bd)
    g_all = nega_ref[...] * _softplus(bd + dtb_ref[...])
    ri64 = lax.broadcasted_iota(jnp.int32, (C, C), 0)
    ci64 = lax.broadcasted_iota(jnp.int32, (C, C), 1)
    gc_all = _mm_exact_lhs((ri64 >= ci64).astype(BF16), g_all)
    og = og_ref[...]

    R = MIXB_GROUP * C
    ri = lax.broadcasted_iota(jnp.int32, (R, R), 0)
    ci = lax.broadcasted_iota(jnp.int32, (R, R), 1)
    same = lax.shift_right_logical(ri, C.bit_length() - 1) == lax.shift_right_logical(ci, C.bit_length() - 1)
    tril = same & (ri >= ci)
    strict = same & (ri > ci)
    rep = B_V_HEADS // B_QK_HEADS

    def lane_col(a, lane):
        return jnp.broadcast_to(a[:, lane:lane + 1], (a.shape[0], HEAD_DIM))

    for grp in range(B_V_HEADS // MIXB_GROUP):
        heads = list(range(grp * MIXB_GROUP, (grp + 1) * MIXB_GROUP))

        def stack(fn):
            return jnp.concatenate([fn(h) for h in heads], axis=0)

        kst = stack(lambda h: kn[h // rep])
        qst = stack(lambda h: qn[h // rep])
        vst = stack(lambda h: act[:, 2 * B_KEY_WIDTH + h * HEAD_DIM:2 * B_KEY_WIDTH + (h + 1) * HEAD_DIM])
        gcol = stack(lambda h: lane_col(gc_all, B_V_HEADS + h))
        glcol = stack(lambda h: jnp.broadcast_to(gc_all[C - 1:C, B_V_HEADS + h:B_V_HEADS + h + 1], (C, HEAD_DIM)))
        bcol = stack(lambda h: lane_col(beta_all, h))
        grow = gcol.T[0:1, :]
        diff = jnp.concatenate([gcol, gcol], axis=1) - grow
        decay = jnp.where(tril, jnp.exp(jnp.where(tril, diff, 0.0)), 0.0)
        a_mat = jnp.where(strict, jnp.concatenate([bcol, bcol], axis=1) * _mm_nt(kst, kst) * decay, 0.0)
        t_inv = _unit_lower_inverse(a_mat, ri, ci, C)
        eg = jnp.exp(gcol)
        kb = kst * bcol
        sol = _mm(t_inv, jnp.concatenate([vst * bcol, kb * eg], axis=1))
        u_val = sol[:, :HEAD_DIM]
        w_k = sol[:, HEAD_DIM:]
        qk = _mm_nt(qst, kst) * decay
        q_dec = qst * eg
        k_dec = kst * jnp.exp(glcol - gcol)
        ws = [_mm(jnp.concatenate([w_k[i * C:(i + 1) * C], q_dec[i * C:(i + 1) * C]], axis=0), s_ref[h])
              for i, h in enumerate(heads)]
        v_new = u_val - jnp.concatenate([w[:C] for w in ws], axis=0)
        o = jnp.concatenate([w[C:] for w in ws], axis=0) + _mm(qk, v_new)
        on = o * lax.rsqrt(jnp.mean(o * o, -1, keepdims=True) + RMS_EPS) * og
        for i, h in enumerate(heads):
            rs = slice(i * C, (i + 1) * C)
            s_ref[h] = s_ref[h] * jnp.exp(glcol[i * C:i * C + 1, :]) + _mm_tn(k_dec[rs], v_new[rs])
            cols = slice(h * HEAD_DIM, (h + 1) * HEAD_DIM)
            o_ref[:, cols] = on[rs] * jax.nn.silu(z_ref[:, cols])

    @pl.when(c == pl.num_programs(1) - 1)
    def _fin():
        sfin_ref[0] = s_ref[...]


def _mixb_prompt(proj, bd, conv_w, nega, dtb, og, batch, seq):
    nc = seq // DN_CHUNK
    qkv_blk = 2 * A_WIDTH // B_CONV_CH
    z_blk = (2 * A_WIDTH + B_CONV_CH) // B_VAL_WIDTH
    return pl.pallas_call(
        _mixb_prompt_kernel,
        grid=(batch, nc),
        in_specs=[pl.BlockSpec((DN_CHUNK, B_CONV_CH), lambda b, c: (b * nc + c, qkv_blk)),
                  pl.BlockSpec((DN_CHUNK, B_VAL_WIDTH), lambda b, c: (b * nc + c, z_blk)),
                  pl.BlockSpec((DN_CHUNK, LANES), lambda b, c: (b * nc + c, 0)),
                  pl.BlockSpec((B_CONV, B_CONV_CH), lambda b, c: (0, 0)),
                  pl.BlockSpec((1, LANES), lambda b, c: (0, 0)),
                  pl.BlockSpec((1, LANES), lambda b, c: (0, 0)),
                  pl.BlockSpec((1, HEAD_DIM), lambda b, c: (0, 0))],
        out_specs=[pl.BlockSpec((DN_CHUNK, B_VAL_WIDTH), lambda b, c: (b * nc + c, 0)),
                   pl.BlockSpec((1, B_V_HEADS, HEAD_DIM, HEAD_DIM), lambda b, c: (b, 0, 0, 0))],
        out_shape=[jax.ShapeDtypeStruct((batch * seq, B_VAL_WIDTH), F32),
                   jax.ShapeDtypeStruct((batch, B_V_HEADS, HEAD_DIM, HEAD_DIM), F32)],
        scratch_shapes=[pltpu.VMEM((B_V_HEADS, HEAD_DIM, HEAD_DIM), F32),
                        pltpu.VMEM((DN_CHUNK + 8, B_CONV_CH), F32)],
        compiler_params=_params("arbitrary", "arbitrary"),
        name="mixb_prompt",
    )(proj, proj, bd, conv_w, nega, dtb, og)


def _mixb_sample_pre_kernel(qkv_ref, buf_ref, bd_ref, cw_ref, nega_ref, dtb_ref,
                            act_ref, beta_ref, g_ref, *, steps, nb_rows):
    cw = cw_ref[...]

    def slab(j):
        if j < B_CONV - 1:
            return buf_ref[j * nb_rows:(j + 1) * nb_rows, :]
        jj = j - (B_CONV - 1)
        return qkv_ref[jj * nb_rows:(jj + 1) * nb_rows, :]

    for t in range(steps):
        y = slab(t) * cw[0:1]
        for i in range(1, B_CONV):
            y = y + slab(t + i) * cw[i:i + 1]
        act = jax.nn.silu(y)
        rs = slice(t * nb_rows, (t + 1) * nb_rows)
        for qh in range(B_QK_HEADS):
            cq = slice(qh * HEAD_DIM, (qh + 1) * HEAD_DIM)
            ck = slice(B_KEY_WIDTH + qh * HEAD_DIM, B_KEY_WIDTH + (qh + 1) * HEAD_DIM)
            qs = act[:, cq]
            ks = act[:, ck]
            act_ref[rs, cq] = qs * lax.rsqrt(jnp.sum(qs * qs, -1, keepdims=True) + L2_EPS) * (HEAD_DIM ** -0.5)
            act_ref[rs, ck] = ks * lax.rsqrt(jnp.sum(ks * ks, -1, keepdims=True) + L2_EPS)
        act_ref[rs, 2 * B_KEY_WIDTH:] = act[:, 2 * B_KEY_WIDTH:]
    bd = bd_ref[...]
    beta_ref[...] = jax.nn.sigmoid(bd)
    g_ref[...] = nega_ref[...] * _softplus(bd + dtb_ref[...])


def _mixb_sample_pre(proj, buf, bd, conv_w, nega, dtb, steps, nb_rows):
    m = proj.shape[0]
    qkv_blk = 2 * A_WIDTH // B_CONV_CH
    kern = functools.partial(_mixb_sample_pre_kernel, steps=steps, nb_rows=nb_rows)
    return pl.pallas_call(
        kern,
        grid=(1,),
        in_specs=[pl.BlockSpec((m, B_CONV_CH), lambda i: (0, qkv_blk)),
                  pl.BlockSpec(buf.shape, lambda i: (0, 0)),
                  pl.BlockSpec((m, LANES), lambda i: (0, 0)),
                  pl.BlockSpec((B_CONV, B_CONV_CH), lambda i: (0, 0)),
                  pl.BlockSpec((1, LANES), lambda i: (0, 0)),
                  pl.BlockSpec((1, LANES), lambda i: (0, 0))],
        out_specs=[pl.BlockSpec((m, B_CONV_CH), lambda i: (0, 0)),
                   pl.BlockSpec((m, LANES), lambda i: (0, 0)),
                   pl.BlockSpec((m, LANES), lambda i: (0, 0))],
        out_shape=[jax.ShapeDtypeStruct((m, B_CONV_CH), F32),
                   jax.ShapeDtypeStruct((m, LANES), F32),
                   jax.ShapeDtypeStruct((m, LANES), F32)],
        compiler_params=_params("arbitrary"),
        name="mixb_sample_pre",
    )(proj, buf, bd, conv_w, nega, dtb)


def _mixb_sample_rec_kernel(kq_ref, v_ref, z_ref, beta_ref, g_ref, s0_ref, og_ref,
                            o_ref, s_out_ref, *, steps, pairs):
    og = og_ref[...]
    zpad = jnp.zeros((HEAD_DIM - 8, HEAD_DIM), F32)

    def body(i, carry):
        bi = i // B_V_HEADS
        hi = i % B_V_HEADS
        kq = kq_ref[bi, hi]
        s0 = s0_ref[bi, hi]
        kqs = _mm(kq, s0)
        g = g_ref[bi, hi]
        beta = beta_ref[bi, hi]
        v = v_ref[bi, hi]
        gc = [g[0:1]]
        for t in range(1, steps):
            gc.append(gc[-1] + g[t:t + 1])
        k = [kq[t:t + 1] for t in range(steps)]
        q = [kq[4 + t:5 + t] for t in range(steps)]
        d = []
        for t in range(steps):
            acc = v[t:t + 1] - jnp.exp(gc[t]) * kqs[t:t + 1]
            for j in range(t):
                kk = jnp.sum(k[j] * k[t], -1, keepdims=True)
                acc = acc - jnp.exp(gc[t] - gc[j]) * kk * d[j]
            d.append(beta[t:t + 1] * acc)
        outs = []
        for t in range(steps):
            o = jnp.exp(gc[t]) * kqs[4 + t:5 + t]
            for j in range(t + 1):
                qk = jnp.sum(k[j] * q[t], -1, keepdims=True)
                o = o + jnp.exp(gc[t] - gc[j]) * qk * d[j]
            outs.append(o * lax.rsqrt(jnp.mean(o * o, -1, keepdims=True) + RMS_EPS) * og)
        o8 = jnp.concatenate(outs + [jnp.zeros((8 - steps, HEAD_DIM), F32)], axis=0)
        o_ref[bi, hi] = o8 * jax.nn.silu(z_ref[bi, hi])
        k_dec = jnp.concatenate([jnp.exp(gc[-1] - gc[j]) * k[j] for j in range(steps)]
                                + [jnp.zeros((8 - steps, HEAD_DIM), F32)], axis=0)
        d8 = jnp.concatenate(d + [jnp.zeros((8 - steps, HEAD_DIM), F32)], axis=0)
        k_pad = jnp.concatenate([k_dec, zpad], axis=0)
        d_pad = jnp.concatenate([d8, zpad], axis=0)
        s_out_ref[bi, hi] = s0 * jnp.exp(gc[-1]) + _mm(k_pad.T, d_pad)
        return carry

    lax.fori_loop(0, pairs, body, 0, unroll=4)


def _mixb_sample_rec(kq, v, z, beta, g, s0, og, steps, bb):
    nb = kq.shape[0]
    tile = pl.BlockSpec((bb, B_V_HEADS, 8, HEAD_DIM), lambda i: (i, 0, 0, 0))
    st = pl.BlockSpec((bb, B_V_HEADS, HEAD_DIM, HEAD_DIM), lambda i: (i, 0, 0, 0))
    kern = functools.partial(_mixb_sample_rec_kernel, steps=steps, pairs=bb * B_V_HEADS)
    return pl.pallas_call(
        kern,
        grid=(nb // bb,),
        in_specs=[tile, tile, tile, tile, tile, st, pl.BlockSpec((1, HEAD_DIM), lambda i: (0, 0))],
        out_specs=[tile, st],
        out_shape=[jax.ShapeDtypeStruct((nb, B_V_HEADS, 8, HEAD_DIM), F32),
                   jax.ShapeDtypeStruct((nb, B_V_HEADS, HEAD_DIM, HEAD_DIM), F32)],
        compiler_params=_params("arbitrary"),
        name="mixb_sample_rec",
    )(kq, v, z, beta, g, s0, og)


def _outproj_kernel(a_ref, b_ref, x_ref, g1_ref, sc2_ref, sh2_ref, w_ref, lg_ref, lb_ref, wr_ref,
                    x1_ref, h2_ref, logit_ref):
    mix = (jnp.dot(a_ref[...].astype(BF16), w_ref[:A_WIDTH, :], preferred_element_type=F32)
           + jnp.dot(b_ref[...].astype(BF16), w_ref[A_WIDTH:, :], preferred_element_type=F32))
    x1 = _layer_norm_rows(ALPHA * x_ref[...] + g1_ref[...] * mix, lg_ref[...], lb_ref[...])
    x1_ref[...] = x1
    h2 = x1 * (1.0 + sc2_ref[...]) + sh2_ref[...]
    h2_ref[...] = h2
    logit_ref[...] = _mm3(h2, wr_ref[...])


def _outproj(a_out, b_out, x, mod, w_out, lg, lb, w_router, tm, rows_per_mod):
    m = x.shape[0]
    row = lambda w: pl.BlockSpec((tm, w), lambda i: (i, 0))
    full = lambda s: pl.BlockSpec(s, lambda i: (0, 0))
    return pl.pallas_call(
        _outproj_kernel,
        grid=(m // tm,),
        in_specs=[row(A_WIDTH), row(B_VAL_WIDTH), row(D_MODEL),
                  _mod_spec(mod, 2, tm, rows_per_mod), _mod_spec(mod, 4, tm, rows_per_mod),
                  _mod_spec(mod, 3, tm, rows_per_mod),
                  full((D_MODEL, D_MODEL)), full((1, D_MODEL)), full((1, D_MODEL)),
                  full((D_MODEL, LANES))],
        out_specs=[row(D_MODEL), row(D_MODEL), row(LANES)],
        out_shape=[jax.ShapeDtypeStruct((m, D_MODEL), F32),
                   jax.ShapeDtypeStruct((m, D_MODEL), F32),
                   jax.ShapeDtypeStruct((m, LANES), F32)],
        compiler_params=_params("arbitrary"),
        name="outproj",
    )(a_out, b_out, x, mod, mod, mod, w_out, lg, lb, w_router)


def _router_kernel(logit_ref, bias_ref, route_ref, count_ref, base_ref):
    i = pl.program_id(0)

    @pl.when(i == 0)
    def _():
        base_ref[...] = jnp.zeros_like(base_ref)

    lg = logit_ref[...]
    tm = lg.shape[0]
    bias = bias_ref[...]
    lane = lax.broadcasted_iota(jnp.int32, lg.shape, 1)
    neg = -jnp.inf

    def first_argmax(score):
        mx = jnp.max(score, -1, keepdims=True)
        return jnp.min(jnp.where(score == mx, lane, LANES), -1, keepdims=True)

    def pick(vals, idx):
        return jnp.sum(jnp.where(lane == idx, vals, 0.0), -1, keepdims=True)

    gmask = lane < N_GROUPS
    mg = jnp.max(jnp.where(gmask, lg, neg), -1, keepdims=True)
    eg = jnp.where(gmask, jnp.exp(jnp.where(gmask, lg - mg, 0.0)), 0.0)
    pg = eg / jnp.sum(eg, -1, keepdims=True)
    sel_g = first_argmax(jnp.where(gmask, lg + bias, neg))
    p_sel = pick(pg, sel_g)

    lo = N_GROUPS + sel_g * EXPERTS_PER_GROUP
    emask = (lane >= lo) & (lane < lo + EXPERTS_PER_GROUP)
    me = jnp.max(jnp.where(emask, lg, neg), -1, keepdims=True)
    ee = jnp.where(emask, jnp.exp(jnp.where(emask, lg - me, 0.0)), 0.0)
    pe = ee / jnp.sum(ee, -1, keepdims=True)
    score = jnp.where(emask, pe + bias, neg)
    i1 = first_argmax(score)
    i2 = first_argmax(jnp.where(lane == i1, neg, score))
    w1 = pick(pe, i1)
    w2 = pick(pe, i2)
    wsum = w1 + w2
    gate1 = w1 / wsum * p_sel
    gate2 = w2 / wsum * p_sel

    hot = ((lane == i1) | (lane == i2)).astype(BF16)
    ri = lax.broadcasted_iota(jnp.int32, (tm, tm), 0)
    ci = lax.broadcasted_iota(jnp.int32, (tm, tm), 1)
    before = jnp.dot((ri > ci).astype(BF16), hot, preferred_element_type=F32) + base_ref[...]
    rank1 = pick(before, i1)
    rank2 = pick(before, i2)
    base_ref[...] = base_ref[...] + jnp.sum(hot.astype(F32), 0, keepdims=True)
    count_ref[...] = base_ref[...]

    out = jnp.where(lane == 0, (i1 - N_GROUPS).astype(F32), 0.0)
    out = jnp.where(lane == 1, (i2 - N_GROUPS).astype(F32), out)
    out = jnp.where(lane == 2, gate1, out)
    out = jnp.where(lane == 3, gate2, out)
    out = jnp.where(lane == 4, rank1, out)
    out = jnp.where(lane == 5, rank2, out)
    route_ref[...] = out


def _router(logits, bias):
    m = logits.shape[0]
    return pl.pallas_call(
        _router_kernel,
        grid=(m // ROUTE_TILE,),
        in_specs=[pl.BlockSpec((ROUTE_TILE, LANES), lambda i: (i, 0)),
                  pl.BlockSpec((1, LANES), lambda i: (0, 0))],
        out_specs=[pl.BlockSpec((ROUTE_TILE, LANES), lambda i: (i, 0)),
                   pl.BlockSpec((1, LANES), lambda i: (0, 0))],
        out_shape=[jax.ShapeDtypeStruct((m, LANES), F32),
                   jax.ShapeDtypeStruct((1, LANES), F32)],
        scratch_shapes=[pltpu.VMEM((1, LANES), F32)],
        compiler_params=_params("arbitrary"),
        name="router",
    )(logits, bias)


def _expert_kernel(be_ref, nv_ref, src_ref, dst_ref, h_ref, wg_ref, wu_ref, wd_ref, y_ref,
                   xbuf, ybuf, wg_s, wu_s, wd_s, gsem, ssem):
    i = pl.program_id(0)
    nv = nv_ref[0]
    slot = i % 2
    other = 1 - slot

    def gather_row(blk, r, s):
        tok = src_ref[blk * MOE_ROWS + r]
        return pltpu.make_async_copy(h_ref.at[pl.ds(tok, 1)], xbuf.at[s, pl.ds(r, 1)], gsem.at[s])

    def scatter_row(blk, r, s):
        row = dst_ref[(blk + 1) * MOE_ROWS + r]
        return pltpu.make_async_copy(ybuf.at[s, pl.ds(r, 1)], y_ref.at[pl.ds(row, 1)], ssem.at[s])

    def gather_wait(s):
        pltpu.make_async_copy(h_ref.at[pl.ds(0, MOE_ROWS)], xbuf.at[s], gsem.at[s]).wait()

    def scatter_wait(s):
        pltpu.make_async_copy(ybuf.at[s], y_ref.at[pl.ds(0, MOE_ROWS)], ssem.at[s]).wait()

    @pl.when(i == 0)
    def _prologue():
        ybuf[...] = jnp.zeros_like(ybuf)
        base = y_ref.shape[0] - 2 * MOE_ROWS
        pltpu.make_async_copy(ybuf.at[0], y_ref.at[pl.ds(base, MOE_ROWS)], ssem.at[0]).start()

        def body(r, carry):
            gather_row(0, r, 0).start()
            return carry

        lax.fori_loop(0, MOE_ROWS, body, 0, unroll=8)

    prev = be_ref[jnp.maximum(i - 1, 0)]

    @pl.when((i < nv) & ((i == 0) | (be_ref[i] != prev)))
    def _():
        wg_s[...] = wg_ref[0].astype(BF16)
        wu_s[...] = wu_ref[0].astype(BF16)
        wd_s[...] = wd_ref[0].astype(BF16)

    @pl.when(i < nv)
    def _block():
        gather_wait(slot)
        for r in range(MOE_ROWS):
            gather_row(i + 1, r, other).start()
            scatter_row(i - 1, r, other).start()
        x = xbuf[slot].astype(BF16)
        hg = jnp.dot(x, wg_s[...], preferred_element_type=F32)
        hu = jnp.dot(x, wu_s[...], preferred_element_type=F32)
        hid = (jax.nn.silu(hg) * hu).astype(BF16)
        y = jnp.dot(hid, wd_s[...], preferred_element_type=F32)
        scatter_wait(slot)
        ybuf[slot] = y

    @pl.when(i == nv - 1)
    def _tail():
        def body(r, carry):
            scatter_row(i, r, slot).start()
            return carry

        lax.fori_loop(0, MOE_ROWS, body, 0, unroll=8)
        scatter_wait(slot)
        scatter_wait(other)
        gather_wait(other)


def _experts(block_e, n_valid, slot_src, slot_dst, h2, w_gate, w_up, w_down, n_out_rows):
    n_blocks = block_e.shape[0]
    return pl.pallas_call(
        _expert_kernel,
        grid_spec=pltpu.PrefetchScalarGridSpec(
            num_scalar_prefetch=4,
            grid=(n_blocks,),
            in_specs=[pl.BlockSpec(memory_space=pl.ANY),
                      pl.BlockSpec((1, D_MODEL, D_EXPERT), lambda i, be, *_: (be[i], 0, 0)),
                      pl.BlockSpec((1, D_MODEL, D_EXPERT), lambda i, be, *_: (be[i], 0, 0)),
                      pl.BlockSpec((1, D_EXPERT, D_MODEL), lambda i, be, *_: (be[i], 0, 0))],
            out_specs=pl.BlockSpec(memory_space=pl.ANY),
            scratch_shapes=[pltpu.VMEM((2, MOE_ROWS, D_MODEL), F32),
                            pltpu.VMEM((2, MOE_ROWS, D_MODEL), F32),
                            pltpu.VMEM((D_MODEL, D_EXPERT), BF16),
                            pltpu.VMEM((D_MODEL, D_EXPERT), BF16),
                            pltpu.VMEM((D_EXPERT, D_MODEL), BF16),
                            pltpu.SemaphoreType.DMA((2,)),
                            pltpu.SemaphoreType.DMA((2,))]),
        out_shape=jax.ShapeDtypeStruct((n_out_rows, D_MODEL), F32),
        compiler_params=_params("arbitrary"),
        name="experts",
    )(block_e, n_valid, slot_src, slot_dst, h2, w_gate, w_up, w_down)


def _final_kernel(x1_ref, y0_ref, y1_ref, route_ref, g2_ref, lg_ref, lb_ref, o_ref):
    route = route_ref[...]
    ff = y0_ref[...] * route[:, 2:3] + y1_ref[...] * route[:, 3:4]
    o_ref[...] = _layer_norm_rows(ALPHA * x1_ref[...] + g2_ref[...] * ff, lg_ref[...], lb_ref[...])


def _final(x1, y01, route, mod, lg, lb, tm, rows_per_mod, row0, n_rows, n_tok):
    off = row0 // tm
    off1 = (n_tok + row0) // tm
    return pl.pallas_call(
        _final_kernel,
        grid=(n_rows // tm,),
        in_specs=[pl.BlockSpec((tm, D_MODEL), lambda i: (off + i, 0)),
                  pl.BlockSpec((tm, D_MODEL), lambda i: (off + i, 0)),
                  pl.BlockSpec((tm, D_MODEL), lambda i: (off1 + i, 0)),
                  pl.BlockSpec((tm, LANES), lambda i: (off + i, 0)),
                  _mod_spec(mod, 5, tm, rows_per_mod),
                  pl.BlockSpec((1, D_MODEL), lambda i: (0, 0)),
                  pl.BlockSpec((1, D_MODEL), lambda i: (0, 0))],
        out_specs=pl.BlockSpec((tm, D_MODEL), lambda i: (i, 0)),
        out_shape=jax.ShapeDtypeStruct((n_rows, D_MODEL), F32),
        compiler_params=_params("arbitrary"),
        name="final",
    )(x1, y01, y01, route, mod, lg, lb)


def _lane_row(vec, offset):
    return jnp.zeros((1, LANES), F32).at[0, offset:offset + vec.shape[0]].set(vec.astype(F32))


def kernel(x_prompt, x_sample, state_conv, state_ssm, c_prompt, c_sample, w_ada, b_ada, w_in, a_ws, a_bs, a_norm_g, a_norm_b, b_conv_w, b_a_log, b_dt_bias, b_onorm_g, w_out, ln1_g, ln1_b, w_router_g, b_router_g, w_router_e, b_router_e, w_gate, w_up, w_down, ln2_g, ln2_b):
    batch, seq, d = x_prompt.shape
    nb, steps, _ = x_sample.shape
    n_p = batch * seq
    n_s = nb * steps
    n_tok = n_p + n_s
    l = 0

    c_rows = batch + nb
    c_pad = (-c_rows) % 8
    c_all = jnp.concatenate([c_prompt, c_sample, jnp.zeros((c_pad, d), F32)], axis=0)
    mod = _ada(c_all, w_ada[l], b_ada[l].reshape(1, -1))
    mod_p = mod[:batch].reshape(batch, 1, 6 * d)
    mod_s = mod[batch:batch + nb]

    w_in_l = w_in[l]
    w_main = w_in_l[:, :MAIN_WIDTH].astype(BF16)
    w_bd = jnp.zeros((d, LANES), BF16).at[:, :2 * B_V_HEADS].set(w_in_l[:, MAIN_WIDTH:].astype(BF16))
    xp = x_prompt.reshape(n_p, d)
    xs = jnp.swapaxes(x_sample, 0, 1).reshape(n_s, d)
    proj_p, bd_p = _inproj(xp, mod_p, w_main, w_bd, 512, seq)
    proj_s, bd_s = _inproj(xs, mod_s, w_main, w_bd, nb, 0)

    ng = a_norm_g[l].reshape(1, -1)
    nbias = a_norm_b[l].reshape(1, -1)
    bias_tile = jnp.repeat(a_bs[l].T, HEAD_DIM, axis=1)
    a_out_p = _mixa_prompt(proj_p, a_ws[l], bias_tile, ng, nbias, 256)
    coef = jnp.repeat(jnp.transpose(a_ws[l][:, :steps, :steps], (1, 2, 0)).reshape(steps * steps, A_HEADS),
                      HEAD_DIM, axis=1)
    a_out_s, chunkv = _mixa_sample(proj_s, coef, bias_tile[:steps], ng, nbias, steps, nb)

    nega = _lane_row(-jnp.exp(b_a_log[l].astype(F32)), B_V_HEADS)
    dtb = _lane_row(b_dt_bias[l], B_V_HEADS)
    og = b_onorm_g[l].reshape(1, -1)
    b_out_p, ssm_p = _mixb_prompt(proj_p, bd_p, b_conv_w[l], nega, dtb, og, batch, seq)
    qkv0 = 2 * A_WIDTH
    conv_p = proj_p.reshape(batch, seq, MAIN_WIDTH)[:, seq - (B_CONV - 1):, qkv0:qkv0 + B_CONV_CH]

    buf_s = jnp.swapaxes(state_conv[l], 0, 1).reshape((B_CONV - 1) * nb, B_CONV_CH)
    act_s, beta_s, g_s = _mixb_sample_pre(proj_s, buf_s, bd_s, b_conv_w[l], nega, dtb, steps, nb)
    rep = B_V_HEADS // B_QK_HEADS
    act4 = act_s.reshape(steps, nb, B_CONV_CH)
    q8 = jnp.repeat(act4[..., :B_KEY_WIDTH].reshape(steps, nb, B_QK_HEADS, HEAD_DIM), rep, axis=2)
    k8 = jnp.repeat(act4[..., B_KEY_WIDTH:2 * B_KEY_WIDTH].reshape(steps, nb, B_QK_HEADS, HEAD_DIM), rep, axis=2)

    def to_tiles(a):
        a = jnp.transpose(a, (1, 2, 0, 3))
        return jnp.pad(a, ((0, 0), (0, 0), (0, 8 - a.shape[2]), (0, 0)))

    kq_t = jnp.concatenate([to_tiles(k8)[:, :, :4], to_tiles(q8)[:, :, :4]], axis=2)
    v_t = to_tiles(act4[..., 2 * B_KEY_WIDTH:].reshape(steps, nb, B_V_HEADS, HEAD_DIM))
    z_t = to_tiles(proj_s[:, qkv0 + B_CONV_CH:].reshape(steps, nb, B_V_HEADS, HEAD_DIM))

    def gate_tiles(a, off):
        a = a[:, off:off + B_V_HEADS].reshape(steps, nb, B_V_HEADS, 1)
        return to_tiles(jnp.broadcast_to(a, (steps, nb, B_V_HEADS, HEAD_DIM)))

    o_t, ssm_s = _mixb_sample_rec(kq_t, v_t, z_t, gate_tiles(beta_s, 0), gate_tiles(g_s, B_V_HEADS),
                                  state_ssm[l], og, steps, 8)
    b_out_s = jnp.transpose(o_t[:, :, :steps], (2, 0, 1, 3)).reshape(n_s, B_VAL_WIDTH)
    conv_s = jnp.swapaxes(proj_s.reshape(steps, nb, MAIN_WIDTH)[steps - (B_CONV - 1):, :, qkv0:qkv0 + B_CONV_CH], 0, 1)

    w_out_b = w_out[l].astype(BF16)
    lg1 = ln1_g[l].reshape(1, -1)
    lb1 = ln1_b[l].reshape(1, -1)
    w_router = jnp.zeros((d, LANES), F32)
    w_router = w_router.at[:, :N_GROUPS].set(w_router_g[l]).at[:, N_GROUPS:N_GROUPS + N_EXPERTS].set(w_router_e[l])
    x1_p, h2_p, lgt_p = _outproj(a_out_p, b_out_p, xp, mod_p, w_out_b, lg1, lb1, w_router, 256, seq)
    x1_s, h2_s, lgt_s = _outproj(a_out_s, b_out_s, xs, mod_s, w_out_b, lg1, lb1, w_router, nb, 0)
    x1 = jnp.concatenate([x1_p, x1_s], axis=0)
    h2 = jnp.concatenate([h2_p, h2_s], axis=0)
    logits = jnp.concatenate([lgt_p, lgt_s], axis=0)

    r_bias = jnp.zeros((1, LANES), F32)
    r_bias = r_bias.at[0, :N_GROUPS].set(b_router_g[l]).at[0, N_GROUPS:N_GROUPS + N_EXPERTS].set(b_router_e[l])
    route, counts_row = _router(logits, r_bias)
    counts = counts_row[0, N_GROUPS:N_GROUPS + N_EXPERTS].astype(jnp.int32)
    eid = route[:, 0:2].astype(jnp.int32)
    rank = route[:, 4:6].astype(jnp.int32)
    padded = (counts + MOE_ROWS - 1) // MOE_ROWS * MOE_ROWS
    pend = jnp.cumsum(padded)
    pstart = pend - padded
    dest = pstart[eid] + rank
    n_blocks = -(-(2 * n_tok) // MOE_ROWS) + N_EXPERTS
    n_slots = n_blocks * MOE_ROWS
    tok = jnp.arange(n_tok, dtype=jnp.int32)
    out_row = jnp.stack([tok, n_tok + tok], axis=1)
    slot_row = jnp.full((n_slots,), -1, jnp.int32).at[dest.reshape(-1)].set(out_row.reshape(-1))
    slot_id = jnp.arange(n_slots, dtype=jnp.int32)
    trash = 2 * n_tok + (slot_id // MOE_ROWS % 2) * MOE_ROWS + slot_id % MOE_ROWS
    row_id = jnp.arange(MOE_ROWS, dtype=jnp.int32)
    slot_src = jnp.concatenate([jnp.where(slot_row < 0, 0, slot_row % n_tok),
                                jnp.zeros((MOE_ROWS,), jnp.int32)])
    slot_dst = jnp.concatenate([2 * n_tok + MOE_ROWS + row_id,
                                jnp.where(slot_row < 0, trash, slot_row)])
    blk0 = jnp.arange(n_blocks, dtype=jnp.int32) * MOE_ROWS
    block_e = jnp.minimum(jnp.searchsorted(pend, blk0, side='right'), N_EXPERTS - 1).astype(jnp.int32)
    n_valid = (pend[-1:] // MOE_ROWS).astype(jnp.int32)

    y01 = _experts(block_e, n_valid, slot_src, slot_dst, h2, w_gate[l], w_up[l], w_down[l],
                   2 * n_tok + 2 * MOE_ROWS)

    lg2 = ln2_g[l].reshape(1, -1)
    lb2 = ln2_b[l].reshape(1, -1)
    y_p = _final(x1, y01, route, mod_p, lg2, lb2, 256, seq, 0, n_p, n_tok)
    y_s = _final(x1, y01, route, mod_s, lg2, lb2, nb, 0, n_p, n_s, n_tok)

    y_prompt = y_p.reshape(batch, seq, d)
    y_sample = jnp.swapaxes(y_s.reshape(steps, nb, d), 0, 1)
    chunkv_s = jnp.swapaxes(chunkv.reshape(steps, nb, A_HEADS, HEAD_DIM), 0, 1)
    return (y_prompt, y_sample, conv_p[None], ssm_p[None], conv_s[None], ssm_s[None], chunkv_s[None])
```

```python
import functools

import jax
import jax.numpy as jnp
from jax import lax
from jax.experimental import pallas as pl
from jax.experimental.pallas import tpu as pltpu

F32 = jnp.float32
BF16 = jnp.bfloat16

D_MODEL = 2048
DEPTH = 1
A_HEADS = 8
HEAD_DIM = 128
A_WIDTH = 1024
A_CHUNK = 128
B_QK_HEADS = 4
B_V_HEADS = 8
B_KEY_WIDTH = 512
B_VAL_WIDTH = 1024
B_CONV = 4
B_CONV_CH = 2048
DN_CHUNK = 64
MAIN_WIDTH = 2 * A_WIDTH + B_CONV_CH + B_VAL_WIDTH
N_GROUPS = 4
EXPERTS_PER_GROUP = 8
N_EXPERTS = 32
D_EXPERT = 512
ALPHA = (2 * DEPTH) ** 0.25
LN_EPS = 1e-5
RMS_EPS = 1e-6
L2_EPS = 1e-6

LANES = 128
VMEM_LIMIT = 56 * 1024 * 1024
MOE_ROWS = 256
ROUTE_TILE = 512
MIXB_GROUP = 4


def _params(*sem):
    return pltpu.CompilerParams(dimension_semantics=sem, vmem_limit_bytes=VMEM_LIMIT)


def _mm(a, b):
    return jnp.dot(a.astype(BF16), b.astype(BF16), preferred_element_type=F32)


def _mm_nt(a, b):
    return lax.dot_general(a.astype(BF16), b.astype(BF16), (((1,), (1,)), ((), ())),
                           preferred_element_type=F32)


def _mm_tn(a, b):
    return lax.dot_general(a.astype(BF16), b.astype(BF16), (((0,), (0,)), ((), ())),
                           preferred_element_type=F32)


def _split(x):
    hi = x.astype(BF16)
    lo = (x - hi.astype(F32)).astype(BF16)
    return hi, lo


def _mm_exact_lhs(a_bf16, b):
    hi, lo = _split(b)
    return (jnp.dot(a_bf16, hi, preferred_element_type=F32)
            + jnp.dot(a_bf16, lo, preferred_element_type=F32))


def _mm3(a, b):
    ah, al = _split(a)
    bh, bl = _split(b)
    return (jnp.dot(ah, bh, preferred_element_type=F32) + jnp.dot(ah, bl, preferred_element_type=F32)
            + jnp.dot(al, bh, preferred_element_type=F32))


def _softplus(x):
    return jnp.maximum(x, 0.0) + jnp.log1p(jnp.exp(-jnp.abs(x)))


def _layer_norm_rows(x, g, b):
    mu = jnp.mean(x, -1, keepdims=True)
    xc = x - mu
    var = jnp.mean(xc * xc, -1, keepdims=True)
    return xc * lax.rsqrt(var + LN_EPS) * g + b


def _ada_kernel(c_ref, w_ref, b_ref, o_ref):
    a = jax.nn.silu(c_ref[...]).astype(BF16)
    o_ref[...] = jnp.dot(a, w_ref[...].astype(BF16), preferred_element_type=F32) + b_ref[...]


def _ada(c_all, w_ada, b_ada):
    rows = c_all.shape[0]
    tn = 1024
    return pl.pallas_call(
        _ada_kernel,
        grid=(6 * D_MODEL // tn,),
        in_specs=[pl.BlockSpec((rows, D_MODEL), lambda j: (0, 0)),
                  pl.BlockSpec((D_MODEL, tn), lambda j: (0, j)),
                  pl.BlockSpec((1, tn), lambda j: (0, j))],
        out_specs=pl.BlockSpec((rows, tn), lambda j: (0, j)),
        out_shape=jax.ShapeDtypeStruct((rows, 6 * D_MODEL), F32),
        compiler_params=_params("arbitrary"),
        name="ada",
    )(c_all, w_ada, b_ada)


def _inproj_kernel(x_ref, sc_ref, sh_ref, w_ref, wbd_ref, o_ref, bd_ref, wb_ref):
    j = pl.program_id(0)

    @pl.when(pl.program_id(1) == 0)
    def _():
        wb_ref[...] = w_ref[...].astype(BF16)

    h = (x_ref[...] * (1.0 + sc_ref[...]) + sh_ref[...]).astype(BF16)
    o_ref[...] = jnp.dot(h, wb_ref[...], preferred_element_type=F32)

    @pl.when(j == 0)
    def _():
        bd_ref[0] = jnp.dot(h, wbd_ref[...], preferred_element_type=F32)

    @pl.when(j != 0)
    def _():
        bd_ref[0] = jnp.zeros(bd_ref.shape[1:], F32)


def _mod_spec(mod, col, tm, rows_per_mod, axis=0):
    if rows_per_mod:
        tiles = rows_per_mod // tm
        return pl.BlockSpec((None, 1, D_MODEL), lambda *g: (g[axis] // tiles, 0, col))
    return pl.BlockSpec((tm, D_MODEL), lambda *g: (0, col))


def _inproj(x, mod, w_in, w_bd, tm, rows_per_mod):
    m = x.shape[0]
    tn = 1024
    nj = MAIN_WIDTH // tn
    proj, bd = pl.pallas_call(
        _inproj_kernel,
        grid=(nj, m // tm),
        in_specs=[pl.BlockSpec((tm, D_MODEL), lambda j, i: (i, 0)),
                  _mod_spec(mod, 1, tm, rows_per_mod, axis=1),
                  _mod_spec(mod, 0, tm, rows_per_mod, axis=1),
                  pl.BlockSpec((D_MODEL, tn), lambda j, i: (0, j)),
                  pl.BlockSpec((D_MODEL, LANES), lambda j, i: (0, 0))],
        out_specs=[pl.BlockSpec((tm, tn), lambda j, i: (i, j)),
                   pl.BlockSpec((1, tm, LANES), lambda j, i: (j, i, 0))],
        out_shape=[jax.ShapeDtypeStruct((m, MAIN_WIDTH), F32),
                   jax.ShapeDtypeStruct((nj, m, LANES), F32)],
        scratch_shapes=[pltpu.VMEM((D_MODEL, tn), BF16)],
        compiler_params=_params("arbitrary", "arbitrary"),
        name="inproj",
    )(x, mod, mod, w_in, w_bd)
    return proj, bd[0]


def _mixa_prompt_kernel(p_ref, ws_ref, bias_ref, ng_ref, nb_ref, o_ref):
    rows = p_ref.shape[0]
    u = jax.nn.gelu(p_ref[:, :A_WIDTH])
    v = _layer_norm_rows(jax.nn.gelu(p_ref[:, A_WIDTH:]), ng_ref[...], nb_ref[...])
    ri = lax.broadcasted_iota(jnp.int32, (A_CHUNK, A_CHUNK), 0)
    ci = lax.broadcasted_iota(jnp.int32, (A_CHUNK, A_CHUNK), 1)
    for h in range(A_HEADS):
        w = jnp.where(ri >= ci, ws_ref[h], 0.0).astype(BF16)
        cols = slice(h * HEAD_DIM, (h + 1) * HEAD_DIM)
        for c in range(rows // A_CHUNK):
            rs = slice(c * A_CHUNK, (c + 1) * A_CHUNK)
            s = jnp.dot(w, v[rs, cols].astype(BF16), preferred_element_type=F32) + bias_ref[:, cols]
            o_ref[rs, cols] = u[rs, cols] * s


def _mixa_prompt(proj, a_ws, bias_tile, ng, nb, tm):
    m = proj.shape[0]
    return pl.pallas_call(
        _mixa_prompt_kernel,
        grid=(m // tm,),
        in_specs=[pl.BlockSpec((tm, 2 * A_WIDTH), lambda i: (i, 0)),
                  pl.BlockSpec((A_HEADS, A_CHUNK, A_CHUNK), lambda i: (0, 0, 0)),
                  pl.BlockSpec((A_CHUNK, A_WIDTH), lambda i: (0, 0)),
                  pl.BlockSpec((1, A_WIDTH), lambda i: (0, 0)),
                  pl.BlockSpec((1, A_WIDTH), lambda i: (0, 0))],
        out_specs=pl.BlockSpec((tm, A_WIDTH), lambda i: (i, 0)),
        out_shape=jax.ShapeDtypeStruct((m, A_WIDTH), F32),
        compiler_params=_params("arbitrary"),
        name="mixa_prompt",
    )(proj, a_ws, bias_tile, ng, nb)


def _mixa_sample_kernel(p_ref, coef_ref, bias_ref, ng_ref, nb_ref, o_ref, v_ref, *, steps, nb_rows):
    u = jax.nn.gelu(p_ref[:, :A_WIDTH])
    v = _layer_norm_rows(jax.nn.gelu(p_ref[:, A_WIDTH:]), ng_ref[...], nb_ref[...])
    v_ref[...] = v
    for t in range(steps):
        s = bias_ref[t:t + 1, :]
        for j in range(t + 1):
            s = s + coef_ref[t * steps + j:t * steps + j + 1, :] * v[j * nb_rows:(j + 1) * nb_rows, :]
        rs = slice(t * nb_rows, (t + 1) * nb_rows)
        o_ref[rs, :] = u[rs, :] * s


def _mixa_sample(proj, coef, bias, ng, nb, steps, nb_rows):
    m = proj.shape[0]
    kern = functools.partial(_mixa_sample_kernel, steps=steps, nb_rows=nb_rows)
    return pl.pallas_call(
        kern,
        grid=(1,),
        in_specs=[pl.BlockSpec((m, 2 * A_WIDTH), lambda i: (0, 0)),
                  pl.BlockSpec(coef.shape, lambda i: (0, 0)),
                  pl.BlockSpec(bias.shape, lambda i: (0, 0)),
                  pl.BlockSpec((1, A_WIDTH), lambda i: (0, 0)),
                  pl.BlockSpec((1, A_WIDTH), lambda i: (0, 0))],
        out_specs=[pl.BlockSpec((m, A_WIDTH), lambda i: (0, 0)),
                   pl.BlockSpec((m, A_WIDTH), lambda i: (0, 0))],
        out_shape=[jax.ShapeDtypeStruct((m, A_WIDTH), F32),
                   jax.ShapeDtypeStruct((m, A_WIDTH), F32)],
        compiler_params=_params("arbitrary"),
        name="mixa_sample",
    )(proj, coef, bias, ng, nb)


def _unit_lower_inverse_many(a_list, ri, ci, block):
    eye = (ri == ci).astype(F32)
    pair = (lax.shift_right_logical(ri, 1) == lax.shift_right_logical(ci, 1)) & ((ri & 1) == 1) & ((ci & 1) == 0)
    ts = [eye - jnp.where(pair, a, 0.0) for a in a_list]
    n = 2
    while n < block:
        sh = n.bit_length()
        m = ((lax.shift_right_logical(ri, sh) == lax.shift_right_logical(ci, sh))
             & ((ri & n) != 0) & ((ci & n) == 0))
        xs = [_mm(t, jnp.where(m, a, 0.0)) for t, a in zip(ts, a_list)]
        ts = [t - _mm(x, t) for t, x in zip(ts, xs)]
        n *= 2
    return ts


def _mixb_prompt_kernel(qkv_ref, z_ref, bd_ref, cw_ref, nega_ref, dtb_ref, og_ref,
                        o_ref, sfin_ref, s_ref, cbuf_ref):
    c = pl.program_id(0)
    C = DN_CHUNK
    nseq = qkv_ref.shape[0]

    @pl.when(c == 0)
    def _init():
        s_ref[...] = jnp.zeros_like(s_ref)
        cbuf_ref[:, 0:8, :] = jnp.zeros((nseq, 8, B_CONV_CH), F32)

    cw = cw_ref[...]
    og = og_ref[...]
    ri64 = lax.broadcasted_iota(jnp.int32, (C, C), 0)
    ci64 = lax.broadcasted_iota(jnp.int32, (C, C), 1)
    cum_lhs = (ri64 >= ci64).astype(BF16)

    R = MIXB_GROUP * C
    ri = lax.broadcasted_iota(jnp.int32, (R, R), 0)
    ci = lax.broadcasted_iota(jnp.int32, (R, R), 1)
    same = lax.shift_right_logical(ri, C.bit_length() - 1) == lax.shift_right_logical(ci, C.bit_length() - 1)
    tril = same & (ri >= ci)
    strict = same & (ri > ci)
    rep = B_V_HEADS // B_QK_HEADS

    def lane_col(a, lane):
        return jnp.broadcast_to(a[:, lane:lane + 1], (a.shape[0], HEAD_DIM))

    groups = B_V_HEADS // MIXB_GROUP
    units = [(b, grp) for b in range(nseq) for grp in range(groups)]
    heads_of = lambda grp: list(range(grp * MIXB_GROUP, (grp + 1) * MIXB_GROUP))

    acts, gcs, betas = [], [], []
    for b in range(nseq):
        x = qkv_ref[b]
        cbuf_ref[b, 8:8 + C, :] = x
        y = cbuf_ref[b, 5:5 + C, :] * cw[0:1]
        y = y + cbuf_ref[b, 6:6 + C, :] * cw[1:2]
        y = y + cbuf_ref[b, 7:7 + C, :] * cw[2:3]
        y = y + x * cw[3:4]
        cbuf_ref[b, 0:8, :] = x[C - 8:C, :]
        acts.append(jax.nn.silu(y))
        bd = bd_ref[b]
        betas.append(jax.nn.sigmoid(bd))
        g_all = nega_ref[...] * _softplus(bd + dtb_ref[...])
        gcs.append(_mm_exact_lhs(cum_lhs, g_all))

    kst, qst, gcol, glcol, bcol, decay, a_mat, rhs = {}, {}, {}, {}, {}, {}, {}, {}
    for u in units:
        b, grp = u
        act, gc_all, beta_all = acts[b], gcs[b], betas[b]

        def stack(fn):
            return jnp.concatenate([fn(h) for h in heads_of(grp)], axis=0)

        def l2n(cols0, h):
            s = act[:, cols0 + (h // rep) * HEAD_DIM:cols0 + (h // rep + 1) * HEAD_DIM]
            return s * lax.rsqrt(jnp.sum(s * s, -1, keepdims=True) + L2_EPS)

        kst[u] = stack(lambda h: l2n(B_KEY_WIDTH, h))
        qst[u] = stack(lambda h: l2n(0, h) * (HEAD_DIM ** -0.5))
        vst = stack(lambda h: act[:, 2 * B_KEY_WIDTH + h * HEAD_DIM:2 * B_KEY_WIDTH + (h + 1) * HEAD_DIM])
        gcol[u] = stack(lambda h: lane_col(gc_all, B_V_HEADS + h))
        glcol[u] = stack(lambda h: jnp.broadcast_to(gc_all[C - 1:C, B_V_HEADS + h:B_V_HEADS + h + 1], (C, HEAD_DIM)))
        bcol[u] = stack(lambda h: lane_col(beta_all, h))
        grow = gcol[u].T[0:1, :]
        diff = jnp.concatenate([gcol[u], gcol[u]], axis=1) - grow
        decay[u] = jnp.where(tril, jnp.exp(jnp.where(tril, diff, 0.0)), 0.0)
        rhs[u] = jnp.concatenate([vst * bcol[u], kst[u] * bcol[u] * jnp.exp(gcol[u])], axis=1)
    for u in units:
        a_mat[u] = jnp.where(strict, jnp.concatenate([bcol[u], bcol[u]], axis=1) * _mm_nt(kst[u], kst[u]) * decay[u], 0.0)

    t_inv = _unit_lower_inverse_many([a_mat[u] for u in units], ri, ci, C)
    sol = [_mm(t, rhs[u]) for t, u in zip(t_inv, units)]
    qk = [_mm_nt(qst[u], kst[u]) * decay[u] for u in units]
    ws = []
    for sl, u in zip(sol, units):
        b, grp = u
        q_dec = qst[u] * jnp.exp(gcol[u])
        ws.append([_mm(jnp.concatenate([sl[i * C:(i + 1) * C, HEAD_DIM:], q_dec[i * C:(i + 1) * C]], axis=0),
                       s_ref[b * B_V_HEADS + h]) for i, h in enumerate(heads_of(grp))])
    v_new = [sl[:, :HEAD_DIM] - jnp.concatenate([w[:C] for w in wl], axis=0) for sl, wl in zip(sol, ws)]
    outs = [jnp.concatenate([w[C:] for w in wl], axis=0) + _mm(q, v) for wl, q, v in zip(ws, qk, v_new)]
    for u, v, o in zip(units, v_new, outs):
        b, grp = u
        k_dec = kst[u] * jnp.exp(glcol[u] - gcol[u])
        on = o * lax.rsqrt(jnp.mean(o * o, -1, keepdims=True) + RMS_EPS) * og
        for i, h in enumerate(heads_of(grp)):
            rs = slice(i * C, (i + 1) * C)
            si = b * B_V_HEADS + h
            s_ref[si] = s_ref[si] * jnp.exp(glcol[u][i * C:i * C + 1, :]) + _mm_tn(k_dec[rs], v[rs])
            cols = slice(h * HEAD_DIM, (h + 1) * HEAD_DIM)
            o_ref[b, :, cols] = on[rs] * jax.nn.silu(z_ref[b, :, cols])

    @pl.when(c == pl.num_programs(0) - 1)
    def _fin():
        sfin_ref[...] = s_ref[...]


def _mixb_prompt(proj, bd, conv_w, nega, dtb, og, batch, seq):
    nc = seq // DN_CHUNK
    qkv_blk = 2 * A_WIDTH // B_CONV_CH
    z_blk = (2 * A_WIDTH + B_CONV_CH) // B_VAL_WIDTH
    proj3 = proj.reshape(batch, seq, proj.shape[-1])
    bd3 = bd.reshape(batch, seq, LANES)
    o, s_fin = pl.pallas_call(
        _mixb_prompt_kernel,
        grid=(nc,),
        in_specs=[pl.BlockSpec((batch, DN_CHUNK, B_CONV_CH), lambda c: (0, c, qkv_blk)),
                  pl.BlockSpec((batch, DN_CHUNK, B_VAL_WIDTH), lambda c: (0, c, z_blk)),
                  pl.BlockSpec((batch, DN_CHUNK, LANES), lambda c: (0, c, 0)),
                  pl.BlockSpec((B_CONV, B_CONV_CH), lambda c: (0, 0)),
                  pl.BlockSpec((1, LANES), lambda c: (0, 0)),
                  pl.BlockSpec((1, LANES), lambda c: (0, 0)),
                  pl.BlockSpec((1, HEAD_DIM), lambda c: (0, 0))],
        out_specs=[pl.BlockSpec((batch, DN_CHUNK, B_VAL_WIDTH), lambda c: (0, c, 0)),
                   pl.BlockSpec((batch * B_V_HEADS, HEAD_DIM, HEAD_DIM), lambda c: (0, 0, 0))],
        out_shape=[jax.ShapeDtypeStruct((batch, seq, B_VAL_WIDTH), F32),
                   jax.ShapeDtypeStruct((batch * B_V_HEADS, HEAD_DIM, HEAD_DIM), F32)],
        scratch_shapes=[pltpu.VMEM((batch * B_V_HEADS, HEAD_DIM, HEAD_DIM), F32),
                        pltpu.VMEM((batch, DN_CHUNK + 8, B_CONV_CH), F32)],
        compiler_params=_params("arbitrary"),
        name="mixb_prompt",
    )(proj3, proj3, bd3, conv_w, nega, dtb, og)
    return (o.reshape(batch * seq, B_VAL_WIDTH),
            s_fin.reshape(batch, B_V_HEADS, HEAD_DIM, HEAD_DIM))


def _mixb_sample_pre_kernel(qkv_ref, buf_ref, bd_ref, cw_ref, nega_ref, dtb_ref,
                            act_ref, beta_ref, g_ref, *, steps, nb_rows):
    cw = cw_ref[...]

    def slab(j):
        if j < B_CONV - 1:
            return buf_ref[j * nb_rows:(j + 1) * nb_rows, :]
        jj = j - (B_CONV - 1)
        return qkv_ref[jj * nb_rows:(jj + 1) * nb_rows, :]

    for t in range(steps):
        y = slab(t) * cw[0:1]
        for i in range(1, B_CONV):
            y = y + slab(t + i) * cw[i:i + 1]
        act = jax.nn.silu(y)
        rs = slice(t * nb_rows, (t + 1) * nb_rows)
        for qh in range(B_QK_HEADS):
            cq = slice(qh * HEAD_DIM, (qh + 1) * HEAD_DIM)
            ck = slice(B_KEY_WIDTH + qh * HEAD_DIM, B_KEY_WIDTH + (qh + 1) * HEAD_DIM)
            qs = act[:, cq]
            ks = act[:, ck]
            act_ref[rs, cq] = qs * lax.rsqrt(jnp.sum(qs * qs, -1, keepdims=True) + L2_EPS) * (HEAD_DIM ** -0.5)
            act_ref[rs, ck] = ks * lax.rsqrt(jnp.sum(ks * ks, -1, keepdims=True) + L2_EPS)
        act_ref[rs, 2 * B_KEY_WIDTH:] = act[:, 2 * B_KEY_WIDTH:]
    bd = bd_ref[...]
    beta_ref[...] = jax.nn.sigmoid(bd)
    g_ref[...] = nega_ref[...] * _softplus(bd + dtb_ref[...])


def _mixb_sample_pre(proj, buf, bd, conv_w, nega, dtb, steps, nb_rows):
    m = proj.shape[0]
    qkv_blk = 2 * A_WIDTH // B_CONV_CH
    kern = functools.partial(_mixb_sample_pre_kernel, steps=steps, nb_rows=nb_rows)
    return pl.pallas_call(
        kern,
        grid=(1,),
        in_specs=[pl.BlockSpec((m, B_CONV_CH), lambda i: (0, qkv_blk)),
                  pl.BlockSpec(buf.shape, lambda i: (0, 0)),
                  pl.BlockSpec((m, LANES), lambda i: (0, 0)),
                  pl.BlockSpec((B_CONV, B_CONV_CH), lambda i: (0, 0)),
                  pl.BlockSpec((1, LANES), lambda i: (0, 0)),
                  pl.BlockSpec((1, LANES), lambda i: (0, 0))],
        out_specs=[pl.BlockSpec((m, B_CONV_CH), lambda i: (0, 0)),
                   pl.BlockSpec((m, LANES), lambda i: (0, 0)),
                   pl.BlockSpec((m, LANES), lambda i: (0, 0))],
        out_shape=[jax.ShapeDtypeStruct((m, B_CONV_CH), F32),
                   jax.ShapeDtypeStruct((m, LANES), F32),
                   jax.ShapeDtypeStruct((m, LANES), F32)],
        compiler_params=_params("arbitrary"),
        name="mixb_sample_pre",
    )(proj, buf, bd, conv_w, nega, dtb)


def _mixb_sample_rec_kernel(kq_ref, v_ref, z_ref, beta_ref, g_ref, s0_ref, og_ref,
                            o_ref, s_out_ref, *, steps, pairs):
    og = og_ref[...]
    zpad = jnp.zeros((HEAD_DIM - 8, HEAD_DIM), F32)
    zrows = jnp.zeros((8 - steps, HEAD_DIM), F32)
    heads = range(B_V_HEADS)

    def body(bi, carry):
        kqs = [_mm(kq_ref[bi, h], s0_ref[bi, h]) for h in heads]
        pending = []
        for h in heads:
            kq = kq_ref[bi, h]
            g = g_ref[bi, h]
            beta = beta_ref[bi, h]
            v = v_ref[bi, h]
            gc = [g[0:1]]
            for t in range(1, steps):
                gc.append(gc[-1] + g[t:t + 1])
            k = [kq[t:t + 1] for t in range(steps)]
            q = [kq[4 + t:5 + t] for t in range(steps)]
            d = []
            for t in range(steps):
                acc = v[t:t + 1] - jnp.exp(gc[t]) * kqs[h][t:t + 1]
                for j in range(t):
                    kk = jnp.sum(k[j] * k[t], -1, keepdims=True)
                    acc = acc - jnp.exp(gc[t] - gc[j]) * kk * d[j]
                d.append(beta[t:t + 1] * acc)
            outs = []
            for t in range(steps):
                o = jnp.exp(gc[t]) * kqs[h][4 + t:5 + t]
                for j in range(t + 1):
                    qk = jnp.sum(k[j] * q[t], -1, keepdims=True)
                    o = o + jnp.exp(gc[t] - gc[j]) * qk * d[j]
                outs.append(o * lax.rsqrt(jnp.mean(o * o, -1, keepdims=True) + RMS_EPS) * og)
            o_ref[bi, h] = jnp.concatenate(outs + [zrows], axis=0) * jax.nn.silu(z_ref[bi, h])
            k_dec = jnp.concatenate([jnp.exp(gc[-1] - gc[j]) * k[j] for j in range(steps)] + [zrows], axis=0)
            k_pad = jnp.concatenate([k_dec, zpad], axis=0)
            d_pad = jnp.concatenate(d + [zrows, zpad], axis=0)
            pending.append((k_pad.T, d_pad, jnp.exp(gc[-1])))
        for h, (k_t, d_pad, decay_last) in zip(heads, pending):
            s_out_ref[bi, h] = s0_ref[bi, h] * decay_last + _mm(k_t, d_pad)
        return carry

    lax.fori_loop(0, pairs // B_V_HEADS, body, 0)


def _mixb_sample_rec(kq, v, z, beta, g, s0, og, steps, bb):
    nb = kq.shape[0]
    tile = pl.BlockSpec((bb, B_V_HEADS, 8, HEAD_DIM), lambda i: (i, 0, 0, 0))
    st = pl.BlockSpec((bb, B_V_HEADS, HEAD_DIM, HEAD_DIM), lambda i: (i, 0, 0, 0))
    kern = functools.partial(_mixb_sample_rec_kernel, steps=steps, pairs=bb * B_V_HEADS)
    return pl.pallas_call(
        kern,
        grid=(nb // bb,),
        in_specs=[tile, tile, tile, tile, tile, st, pl.BlockSpec((1, HEAD_DIM), lambda i: (0, 0))],
        out_specs=[tile, st],
        out_shape=[jax.ShapeDtypeStruct((nb, B_V_HEADS, 8, HEAD_DIM), F32),
                   jax.ShapeDtypeStruct((nb, B_V_HEADS, HEAD_DIM, HEAD_DIM), F32)],
        compiler_params=_params("arbitrary"),
        name="mixb_sample_rec",
    )(kq, v, z, beta, g, s0, og)


def _outproj_kernel(a_ref, b_ref, x_ref, g1_ref, sc2_ref, sh2_ref, w_ref, lg_ref, lb_ref, wr_ref,
                    x1_ref, h2_ref, logit_ref):
    mix = (jnp.dot(a_ref[...].astype(BF16), w_ref[:A_WIDTH, :], preferred_element_type=F32)
           + jnp.dot(b_ref[...].astype(BF16), w_ref[A_WIDTH:, :], preferred_element_type=F32))
    x1 = _layer_norm_rows(ALPHA * x_ref[...] + g1_ref[...] * mix, lg_ref[...], lb_ref[...])
    x1_ref[...] = x1
    h2 = x1 * (1.0 + sc2_ref[...]) + sh2_ref[...]
    h2_ref[...] = h2
    logit_ref[...] = _mm3(h2, wr_ref[...])


def _outproj(a_out, b_out, x, mod, w_out, lg, lb, w_router, tm, rows_per_mod):
    m = x.shape[0]
    row = lambda w: pl.BlockSpec((tm, w), lambda i: (i, 0))
    full = lambda s: pl.BlockSpec(s, lambda i: (0, 0))
    return pl.pallas_call(
        _outproj_kernel,
        grid=(m // tm,),
        in_specs=[row(A_WIDTH), row(B_VAL_WIDTH), row(D_MODEL),
                  _mod_spec(mod, 2, tm, rows_per_mod), _mod_spec(mod, 4, tm, rows_per_mod),
                  _mod_spec(mod, 3, tm, rows_per_mod),
                  full((D_MODEL, D_MODEL)), full((1, D_MODEL)), full((1, D_MODEL)),
                  full((D_MODEL, LANES))],
        out_specs=[row(D_MODEL), row(D_MODEL), row(LANES)],
        out_shape=[jax.ShapeDtypeStruct((m, D_MODEL), F32),
                   jax.ShapeDtypeStruct((m, D_MODEL), F32),
                   jax.ShapeDtypeStruct((m, LANES), F32)],
        compiler_params=_params("arbitrary"),
        name="outproj",
    )(a_out, b_out, x, mod, mod, mod, w_out, lg, lb, w_router)


def _router_kernel(logit_ref, bias_ref, route_ref, count_ref, base_ref):
    i = pl.program_id(0)

    @pl.when(i == 0)
    def _():
        base_ref[...] = jnp.zeros_like(base_ref)

    lg = logit_ref[...]
    tm = lg.shape[0]
    bias = bias_ref[...]
    lane = lax.broadcasted_iota(jnp.int32, lg.shape, 1)
    neg = -jnp.inf

    def first_argmax(score):
        mx = jnp.max(score, -1, keepdims=True)
        return jnp.min(jnp.where(score == mx, lane, LANES), -1, keepdims=True)

    def pick(vals, idx):
        return jnp.sum(jnp.where(lane == idx, vals, 0.0), -1, keepdims=True)

    gmask = lane < N_GROUPS
    mg = jnp.max(jnp.where(gmask, lg, neg), -1, keepdims=True)
    eg = jnp.where(gmask, jnp.exp(jnp.where(gmask, lg - mg, 0.0)), 0.0)
    pg = eg / jnp.sum(eg, -1, keepdims=True)
    sel_g = first_argmax(jnp.where(gmask, lg + bias, neg))
    p_sel = pick(pg, sel_g)

    lo = N_GROUPS + sel_g * EXPERTS_PER_GROUP
    emask = (lane >= lo) & (lane < lo + EXPERTS_PER_GROUP)
    me = jnp.max(jnp.where(emask, lg, neg), -1, keepdims=True)
    ee = jnp.where(emask, jnp.exp(jnp.where(emask, lg - me, 0.0)), 0.0)
    pe = ee / jnp.sum(ee, -1, keepdims=True)
    score = jnp.where(emask, pe + bias, neg)
    i1 = first_argmax(score)
    i2 = first_argmax(jnp.where(lane == i1, neg, score))
    w1 = pick(pe, i1)
    w2 = pick(pe, i2)
    wsum = w1 + w2
    gate1 = w1 / wsum * p_sel
    gate2 = w2 / wsum * p_sel

    hot = ((lane == i1) | (lane == i2)).astype(BF16)
    ri = lax.broadcasted_iota(jnp.int32, (tm, tm), 0)
    ci = lax.broadcasted_iota(jnp.int32, (tm, tm), 1)
    before = jnp.dot((ri > ci).astype(BF16), hot, preferred_element_type=F32) + base_ref[...]
    rank1 = pick(before, i1)
    rank2 = pick(before, i2)
    base_ref[...] = base_ref[...] + jnp.sum(hot.astype(F32), 0, keepdims=True)
    count_ref[...] = base_ref[...]

    out = jnp.where(lane == 0, (i1 - N_GROUPS).astype(F32), 0.0)
    out = jnp.where(lane == 1, (i2 - N_GROUPS).astype(F32), out)
    out = jnp.where(lane == 2, gate1, out)
    out = jnp.where(lane == 3, gate2, out)
    out = jnp.where(lane == 4, rank1, out)
    out = jnp.where(lane == 5, rank2, out)
    route_ref[...] = out


def _router(logits, bias):
    m = logits.shape[0]
    return pl.pallas_call(
        _router_kernel,
        grid=(m // ROUTE_TILE,),
        in_specs=[pl.BlockSpec((ROUTE_TILE, LANES), lambda i: (i, 0)),
                  pl.BlockSpec((1, LANES), lambda i: (0, 0))],
        out_specs=[pl.BlockSpec((ROUTE_TILE, LANES), lambda i: (i, 0)),
                   pl.BlockSpec((1, LANES), lambda i: (0, 0))],
        out_shape=[jax.ShapeDtypeStruct((m, LANES), F32),
                   jax.ShapeDtypeStruct((1, LANES), F32)],
        scratch_shapes=[pltpu.VMEM((1, LANES), F32)],
        compiler_params=_params("arbitrary"),
        name="router",
    )(logits, bias)


def _expert_kernel(be_ref, nv_ref, src_ref, dst_ref, h_ref, wg_ref, wu_ref, wd_ref, y_ref,
                   xbuf, ybuf, wg_s, wu_s, wd_s, gsem, ssem):
    i = pl.program_id(0)
    nv = nv_ref[0]
    slot = i % 2
    other = 1 - slot

    def gather_row(blk, r, s):
        tok = src_ref[blk * MOE_ROWS + r]
        return pltpu.make_async_copy(h_ref.at[pl.ds(tok, 1)], xbuf.at[s, pl.ds(r, 1)], gsem.at[s])

    def scatter_row(blk, r, s):
        row = dst_ref[(blk + 1) * MOE_ROWS + r]
        return pltpu.make_async_copy(ybuf.at[s, pl.ds(r, 1)], y_ref.at[pl.ds(row, 1)], ssem.at[s])

    def gather_wait(s):
        pltpu.make_async_copy(h_ref.at[pl.ds(0, MOE_ROWS)], xbuf.at[s], gsem.at[s]).wait()

    def scatter_wait(s):
        pltpu.make_async_copy(ybuf.at[s], y_ref.at[pl.ds(0, MOE_ROWS)], ssem.at[s]).wait()

    @pl.when(i == 0)
    def _prologue():
        ybuf[...] = jnp.zeros_like(ybuf)
        base = y_ref.shape[0] - 2 * MOE_ROWS
        pltpu.make_async_copy(ybuf.at[0], y_ref.at[pl.ds(base, MOE_ROWS)], ssem.at[0]).start()

        def body(r, carry):
            gather_row(0, r, 0).start()
            return carry

        lax.fori_loop(0, MOE_ROWS, body, 0, unroll=8)

    prev = be_ref[jnp.maximum(i - 1, 0)]

    @pl.when((i < nv) & ((i == 0) | (be_ref[i] != prev)))
    def _():
        wg_s[...] = wg_ref[0].astype(BF16)
        wu_s[...] = wu_ref[0].astype(BF16)
        wd_s[...] = wd_ref[0].astype(BF16)

    @pl.when(i < nv)
    def _block():
        gather_wait(slot)
        for r in range(MOE_ROWS):
            gather_row(i + 1, r, other).start()
            scatter_row(i - 1, r, other).start()
        x = xbuf[slot].astype(BF16)
        hg = jnp.dot(x, wg_s[...], preferred_element_type=F32)
        hu = jnp.dot(x, wu_s[...], preferred_element_type=F32)
        hid = (jax.nn.silu(hg) * hu).astype(BF16)
        y = jnp.dot(hid, wd_s[...], preferred_element_type=F32)
        scatter_wait(slot)
        ybuf[slot] = y

    @pl.when(i == nv - 1)
    def _tail():
        def body(r, carry):
            scatter_row(i, r, slot).start()
            return carry

        lax.fori_loop(0, MOE_ROWS, body, 0, unroll=8)
        scatter_wait(slot)
        scatter_wait(other)
        gather_wait(other)


def _experts(block_e, n_valid, slot_src, slot_dst, h2, w_gate, w_up, w_down, n_out_rows):
    n_blocks = block_e.shape[0]
    return pl.pallas_call(
        _expert_kernel,
        grid_spec=pltpu.PrefetchScalarGridSpec(
            num_scalar_prefetch=4,
            grid=(n_blocks,),
            in_specs=[pl.BlockSpec(memory_space=pl.ANY),
                      pl.BlockSpec((1, D_MODEL, D_EXPERT), lambda i, be, *_: (be[i], 0, 0)),
                      pl.BlockSpec((1, D_MODEL, D_EXPERT), lambda i, be, *_: (be[i], 0, 0)),
                      pl.BlockSpec((1, D_EXPERT, D_MODEL), lambda i, be, *_: (be[i], 0, 0))],
            out_specs=pl.BlockSpec(memory_space=pl.ANY),
            scratch_shapes=[pltpu.VMEM((2, MOE_ROWS, D_MODEL), F32),
                            pltpu.VMEM((2, MOE_ROWS, D_MODEL), F32),
                            pltpu.VMEM((D_MODEL, D_EXPERT), BF16),
                            pltpu.VMEM((D_MODEL, D_EXPERT), BF16),
                            pltpu.VMEM((D_EXPERT, D_MODEL), BF16),
                            pltpu.SemaphoreType.DMA((2,)),
                            pltpu.SemaphoreType.DMA((2,))]),
        out_shape=jax.ShapeDtypeStruct((n_out_rows, D_MODEL), F32),
        compiler_params=_params("arbitrary"),
        name="experts",
    )(block_e, n_valid, slot_src, slot_dst, h2, w_gate, w_up, w_down)


def _final_kernel(x1_ref, y0_ref, y1_ref, route_ref, g2_ref, lg_ref, lb_ref, o_ref):
    route = route_ref[...]
    ff = y0_ref[...] * route[:, 2:3] + y1_ref[...] * route[:, 3:4]
    o_ref[...] = _layer_norm_rows(ALPHA * x1_ref[...] + g2_ref[...] * ff, lg_ref[...], lb_ref[...])


def _final(x1, y01, route, mod, lg, lb, tm, rows_per_mod, row0, n_rows, n_tok):
    off = row0 // tm
    off1 = (n_tok + row0) // tm
    return pl.pallas_call(
        _final_kernel,
        grid=(n_rows // tm,),
        in_specs=[pl.BlockSpec((tm, D_MODEL), lambda i: (off + i, 0)),
                  pl.BlockSpec((tm, D_MODEL), lambda i: (off + i, 0)),
                  pl.BlockSpec((tm, D_MODEL), lambda i: (off1 + i, 0)),
                  pl.BlockSpec((tm, LANES), lambda i: (off + i, 0)),
                  _mod_spec(mod, 5, tm, rows_per_mod),
                  pl.BlockSpec((1, D_MODEL), lambda i: (0, 0)),
                  pl.BlockSpec((1, D_MODEL), lambda i: (0, 0))],
        out_specs=pl.BlockSpec((tm, D_MODEL), lambda i: (i, 0)),
        out_shape=jax.ShapeDtypeStruct((n_rows, D_MODEL), F32),
        compiler_params=_params("arbitrary"),
        name="final",
    )(x1, y01, y01, route, mod, lg, lb)


def _lane_row(vec, offset):
    return jnp.zeros((1, LANES), F32).at[0, offset:offset + vec.shape[0]].set(vec.astype(F32))


def kernel(x_prompt, x_sample, state_conv, state_ssm, c_prompt, c_sample, w_ada, b_ada, w_in, a_ws, a_bs, a_norm_g, a_norm_b, b_conv_w, b_a_log, b_dt_bias, b_onorm_g, w_out, ln1_g, ln1_b, w_router_g, b_router_g, w_router_e, b_router_e, w_gate, w_up, w_down, ln2_g, ln2_b):
    batch, seq, d = x_prompt.shape
    nb, steps, _ = x_sample.shape
    n_p = batch * seq
    n_s = nb * steps
    n_tok = n_p + n_s
    l = 0

    c_rows = batch + nb
    c_pad = (-c_rows) % 8
    c_all = jnp.concatenate([c_prompt, c_sample, jnp.zeros((c_pad, d), F32)], axis=0)
    mod = _ada(c_all, w_ada[l], b_ada[l].reshape(1, -1))
    mod_p = mod[:batch].reshape(batch, 1, 6 * d)
    mod_s = mod[batch:batch + nb]

    w_in_l = w_in[l]
    w_bd = jnp.pad(w_in_l[:, MAIN_WIDTH:], ((0, 0), (0, LANES - 2 * B_V_HEADS))).astype(BF16)
    xp = x_prompt.reshape(n_p, d)
    xs = jnp.swapaxes(x_sample, 0, 1).reshape(n_s, d)
    proj_p, bd_p = _inproj(xp, mod_p, w_in_l, w_bd, 512, seq)
    proj_s, bd_s = _inproj(xs, mod_s, w_in_l, w_bd, nb, 0)

    ng = a_norm_g[l].reshape(1, -1)
    nbias = a_norm_b[l].reshape(1, -1)
    bias_tile = jnp.repeat(a_bs[l].T, HEAD_DIM, axis=1)
    a_out_p = _mixa_prompt(proj_p, a_ws[l], bias_tile, ng, nbias, 256)
    coef = jnp.repeat(jnp.transpose(a_ws[l][:, :steps, :steps], (1, 2, 0)).reshape(steps * steps, A_HEADS),
                      HEAD_DIM, axis=1)
    a_out_s, chunkv = _mixa_sample(proj_s, coef, bias_tile[:steps], ng, nbias, steps, nb)

    nega = _lane_row(-jnp.exp(b_a_log[l].astype(F32)), B_V_HEADS)
    dtb = _lane_row(b_dt_bias[l], B_V_HEADS)
    og = b_onorm_g[l].reshape(1, -1)
    b_out_p, ssm_p = _mixb_prompt(proj_p, bd_p, b_conv_w[l], nega, dtb, og, batch, seq)
    qkv0 = 2 * A_WIDTH
    conv_p = proj_p.reshape(batch, seq, MAIN_WIDTH)[:, seq - (B_CONV - 1):, qkv0:qkv0 + B_CONV_CH]

    buf_s = jnp.swapaxes(state_conv[l], 0, 1).reshape((B_CONV - 1) * nb, B_CONV_CH)
    act_s, beta_s, g_s = _mixb_sample_pre(proj_s, buf_s, bd_s, b_conv_w[l], nega, dtb, steps, nb)
    rep = B_V_HEADS // B_QK_HEADS
    act4 = act_s.reshape(steps, nb, B_CONV_CH)
    q8 = jnp.repeat(act4[..., :B_KEY_WIDTH].reshape(steps, nb, B_QK_HEADS, HEAD_DIM), rep, axis=2)
    k8 = jnp.repeat(act4[..., B_KEY_WIDTH:2 * B_KEY_WIDTH].reshape(steps, nb, B_QK_HEADS, HEAD_DIM), rep, axis=2)

    def to_tiles(a):
        a = jnp.transpose(a, (1, 2, 0, 3))
        return jnp.pad(a, ((0, 0), (0, 0), (0, 8 - a.shape[2]), (0, 0)))

    kq_t = jnp.concatenate([to_tiles(k8)[:, :, :4], to_tiles(q8)[:, :, :4]], axis=2)
    v_t = to_tiles(act4[..., 2 * B_KEY_WIDTH:].reshape(steps, nb, B_V_HEADS, HEAD_DIM))
    z_t = to_tiles(proj_s[:, qkv0 + B_CONV_CH:].reshape(steps, nb, B_V_HEADS, HEAD_DIM))

    def gate_tiles(a, off):
        a = a[:, off:off + B_V_HEADS].reshape(steps, nb, B_V_HEADS, 1)
        return to_tiles(jnp.broadcast_to(a, (steps, nb, B_V_HEADS, HEAD_DIM)))

    o_t, ssm_s = _mixb_sample_rec(kq_t, v_t, z_t, gate_tiles(beta_s, 0), gate_tiles(g_s, B_V_HEADS),
                                  state_ssm[l], og, steps, 8)
    b_out_s = jnp.transpose(o_t[:, :, :steps], (2, 0, 1, 3)).reshape(n_s, B_VAL_WIDTH)
    conv_s = jnp.swapaxes(proj_s.reshape(steps, nb, MAIN_WIDTH)[steps - (B_CONV - 1):, :, qkv0:qkv0 + B_CONV_CH], 0, 1)

    w_out_b = w_out[l].astype(BF16)
    lg1 = ln1_g[l].reshape(1, -1)
    lb1 = ln1_b[l].reshape(1, -1)
    w_router = jnp.zeros((d, LANES), F32)
    w_router = w_router.at[:, :N_GROUPS].set(w_router_g[l]).at[:, N_GROUPS:N_GROUPS + N_EXPERTS].set(w_router_e[l])
    x1_p, h2_p, lgt_p = _outproj(a_out_p, b_out_p, xp, mod_p, w_out_b, lg1, lb1, w_router, 256, seq)
    x1_s, h2_s, lgt_s = _outproj(a_out_s, b_out_s, xs, mod_s, w_out_b, lg1, lb1, w_router, nb, 0)
    x1 = jnp.concatenate([x1_p, x1_s], axis=0)
    h2 = jnp.concatenate([h2_p, h2_s], axis=0)
    logits = jnp.concatenate([lgt_p, lgt_s], axis=0)

    r_bias = jnp.zeros((1, LANES), F32)
    r_bias = r_bias.at[0, :N_GROUPS].set(b_router_g[l]).at[0, N_GROUPS:N_GROUPS + N_EXPERTS].set(b_router_e[l])
    route, counts_row = _router(logits, r_bias)
    counts = counts_row[0, N_GROUPS:N_GROUPS + N_EXPERTS].astype(jnp.int32)
    eid = route[:, 0:2].astype(jnp.int32)
    rank = route[:, 4:6].astype(jnp.int32)
    padded = (counts + MOE_ROWS - 1) // MOE_ROWS * MOE_ROWS
    pend = jnp.cumsum(padded)
    pstart = pend - padded
    dest = pstart[eid] + rank
    n_blocks = -(-(2 * n_tok) // MOE_ROWS) + N_EXPERTS
    n_slots = n_blocks * MOE_ROWS
    tok = jnp.arange(n_tok, dtype=jnp.int32)
    out_row = jnp.stack([tok, n_tok + tok], axis=1)
    slot_row = jnp.full((n_slots,), -1, jnp.int32).at[dest.reshape(-1)].set(out_row.reshape(-1))
    slot_id = jnp.arange(n_slots, dtype=jnp.int32)
    trash = 2 * n_tok + (slot_id // MOE_ROWS % 2) * MOE_ROWS + slot_id % MOE_ROWS
    row_id = jnp.arange(MOE_ROWS, dtype=jnp.int32)
    slot_src = jnp.concatenate([jnp.where(slot_row < 0, 0, slot_row % n_tok),
                                jnp.zeros((MOE_ROWS,), jnp.int32)])
    slot_dst = jnp.concatenate([2 * n_tok + MOE_ROWS + row_id,
                                jnp.where(slot_row < 0, trash, slot_row)])
    blk0 = jnp.arange(n_blocks, dtype=jnp.int32) * MOE_ROWS
    block_e = jnp.minimum(jnp.searchsorted(pend, blk0, side='right'), N_EXPERTS - 1).astype(jnp.int32)
    n_valid = (pend[-1:] // MOE_ROWS).astype(jnp.int32)

    y01 = _experts(block_e, n_valid, slot_src, slot_dst, h2, w_gate[l], w_up[l], w_down[l],
                   2 * n_tok + 2 * MOE_ROWS)

    lg2 = ln2_g[l].reshape(1, -1)
    lb2 = ln2_b[l].reshape(1, -1)
    y_p = _final(x1, y01, route, mod_p, lg2, lb2, 256, seq, 0, n_p, n_tok)
    y_s = _final(x1, y01, route, mod_s, lg2, lb2, nb, 0, n_p, n_s, n_tok)

    y_prompt = y_p.reshape(batch, seq, d)
    y_sample = jnp.swapaxes(y_s.reshape(steps, nb, d), 0, 1)
    chunkv_s = jnp.swapaxes(chunkv.reshape(steps, nb, A_HEADS, HEAD_DIM), 0, 1)
    return (y_prompt, y_sample, conv_p[None], ssm_p[None], conv_s[None], ssm_s[None], chunkv_s[None])
```

```python
import functools

import jax
import jax.numpy as jnp
from jax import lax
from jax.experimental import pallas as pl
from jax.experimental.pallas import tpu as pltpu

F32 = jnp.float32
BF16 = jnp.bfloat16

D_MODEL = 2048
DEPTH = 1
A_HEADS = 8
HEAD_DIM = 128
A_WIDTH = 1024
A_CHUNK = 128
B_QK_HEADS = 4
B_V_HEADS = 8
B_KEY_WIDTH = 512
B_VAL_WIDTH = 1024
B_CONV = 4
B_CONV_CH = 2048
DN_CHUNK = 64
MAIN_WIDTH = 2 * A_WIDTH + B_CONV_CH + B_VAL_WIDTH
N_GROUPS = 4
EXPERTS_PER_GROUP = 8
N_EXPERTS = 32
D_EXPERT = 512
ALPHA = (2 * DEPTH) ** 0.25
LN_EPS = 1e-5
RMS_EPS = 1e-6
L2_EPS = 1e-6

LANES = 128
VMEM_LIMIT = 56 * 1024 * 1024
MOE_ROWS = 256
ROUTE_TILE = 512
MIXB_GROUP = 4


def _params(*sem):
    return pltpu.CompilerParams(dimension_semantics=sem, vmem_limit_bytes=VMEM_LIMIT)


def _mm(a, b):
    return jnp.dot(a.astype(BF16), b.astype(BF16), preferred_element_type=F32)


def _mm_nt(a, b):
    return lax.dot_general(a.astype(BF16), b.astype(BF16), (((1,), (1,)), ((), ())),
                           preferred_element_type=F32)


def _mm_tn(a, b):
    return lax.dot_general(a.astype(BF16), b.astype(BF16), (((0,), (0,)), ((), ())),
                           preferred_element_type=F32)


def _split(x):
    hi = x.astype(BF16)
    lo = (x - hi.astype(F32)).astype(BF16)
    return hi, lo


def _mm_exact_lhs(a_bf16, b):
    hi, lo = _split(b)
    return (jnp.dot(a_bf16, hi, preferred_element_type=F32)
            + jnp.dot(a_bf16, lo, preferred_element_type=F32))


def _mm3(a, b):
    ah, al = _split(a)
    bh, bl = _split(b)
    return (jnp.dot(ah, bh, preferred_element_type=F32) + jnp.dot(ah, bl, preferred_element_type=F32)
            + jnp.dot(al, bh, preferred_element_type=F32))


ROW_TILES = D_MODEL // LANES


def _store_token_rows(ref, val):
    n = val.shape[0]
    for j in range(ROW_TILES):
        ref[pl.ds(j, n, stride=ROW_TILES), :] = val[:, j * LANES:(j + 1) * LANES]


def _load_token_rows(ref, n):
    return jnp.concatenate([ref[pl.ds(j, n, stride=ROW_TILES), :] for j in range(ROW_TILES)], axis=1)


def _softplus(x):
    return jnp.maximum(x, 0.0) + jnp.log1p(jnp.exp(-jnp.abs(x)))


def _layer_norm_rows(x, g, b):
    mu = jnp.mean(x, -1, keepdims=True)
    xc = x - mu
    var = jnp.mean(xc * xc, -1, keepdims=True)
    return xc * lax.rsqrt(var + LN_EPS) * g + b


def _ada_kernel(c_ref, w_ref, b_ref, o_ref):
    a = jax.nn.silu(c_ref[...]).astype(BF16)
    o_ref[...] = jnp.dot(a, w_ref[...].astype(BF16), preferred_element_type=F32) + b_ref[...]


def _ada(c_all, w_ada, b_ada):
    rows = c_all.shape[0]
    tn = 1024
    return pl.pallas_call(
        _ada_kernel,
        grid=(6 * D_MODEL // tn,),
        in_specs=[pl.BlockSpec((rows, D_MODEL), lambda j: (0, 0)),
                  pl.BlockSpec((D_MODEL, tn), lambda j: (0, j)),
                  pl.BlockSpec((1, tn), lambda j: (0, j))],
        out_specs=pl.BlockSpec((rows, tn), lambda j: (0, j)),
        out_shape=jax.ShapeDtypeStruct((rows, 6 * D_MODEL), F32),
        compiler_params=_params("arbitrary"),
        name="ada",
    )(c_all, w_ada, b_ada)


def _inproj_kernel(x_ref, sc_ref, sh_ref, w_ref, wbd_ref, o_ref, bd_ref, wb_ref):
    j = pl.program_id(0)

    @pl.when(pl.program_id(1) == 0)
    def _():
        wb_ref[...] = w_ref[...].astype(BF16)

    h = (x_ref[...] * (1.0 + sc_ref[...]) + sh_ref[...]).astype(BF16)
    o_ref[...] = _mm_nt(h, wb_ref[...])

    @pl.when(j == 0)
    def _():
        bd_ref[0] = _mm_nt(h, wbd_ref[...])

    @pl.when(j != 0)
    def _():
        bd_ref[0] = jnp.zeros(bd_ref.shape[1:], F32)


def _mod_spec(mod, col, tm, rows_per_mod, axis=0):
    if rows_per_mod:
        tiles = rows_per_mod // tm
        return pl.BlockSpec((None, 1, D_MODEL), lambda *g: (g[axis] // tiles, 0, col))
    return pl.BlockSpec((tm, D_MODEL), lambda *g: (0, col))


def _inproj(x, mod, w_in, w_bd, tm, rows_per_mod):
    m = x.shape[0]
    tn = 1024
    nj = MAIN_WIDTH // tn
    proj, bd = pl.pallas_call(
        _inproj_kernel,
        grid=(nj, m // tm),
        in_specs=[pl.BlockSpec((tm, D_MODEL), lambda j, i: (i, 0)),
                  _mod_spec(mod, 1, tm, rows_per_mod, axis=1),
                  _mod_spec(mod, 0, tm, rows_per_mod, axis=1),
                  pl.BlockSpec((tn, D_MODEL), lambda j, i: (j, 0)),
                  pl.BlockSpec((LANES, D_MODEL), lambda j, i: (0, 0))],
        out_specs=[pl.BlockSpec((tm, tn), lambda j, i: (i, j)),
                   pl.BlockSpec((1, tm, LANES), lambda j, i: (j, i, 0))],
        out_shape=[jax.ShapeDtypeStruct((m, MAIN_WIDTH), F32),
                   jax.ShapeDtypeStruct((nj, m, LANES), F32)],
        scratch_shapes=[pltpu.VMEM((tn, D_MODEL), BF16)],
        compiler_params=_params("arbitrary", "arbitrary"),
        name="inproj",
    )(x, mod, mod, w_in, w_bd)
    return proj, bd[0]


def _mixa_prompt_kernel(p_ref, ws_ref, bias_ref, ng_ref, nb_ref, o_ref):
    rows = p_ref.shape[0]
    u = jax.nn.gelu(p_ref[:, :A_WIDTH])
    v = _layer_norm_rows(jax.nn.gelu(p_ref[:, A_WIDTH:]), ng_ref[...], nb_ref[...])
    ri = lax.broadcasted_iota(jnp.int32, (A_CHUNK, A_CHUNK), 0)
    ci = lax.broadcasted_iota(jnp.int32, (A_CHUNK, A_CHUNK), 1)
    for h in range(A_HEADS):
        w = jnp.where(ri >= ci, ws_ref[h], 0.0).astype(BF16)
        cols = slice(h * HEAD_DIM, (h + 1) * HEAD_DIM)
        for c in range(rows // A_CHUNK):
            rs = slice(c * A_CHUNK, (c + 1) * A_CHUNK)
            s = jnp.dot(w, v[rs, cols].astype(BF16), preferred_element_type=F32) + bias_ref[:, cols]
            o_ref[rs, cols] = u[rs, cols] * s


def _mixa_prompt(proj, a_ws, bias_tile, ng, nb, tm):
    m = proj.shape[0]
    return pl.pallas_call(
        _mixa_prompt_kernel,
        grid=(m // tm,),
        in_specs=[pl.BlockSpec((tm, 2 * A_WIDTH), lambda i: (i, 0)),
                  pl.BlockSpec((A_HEADS, A_CHUNK, A_CHUNK), lambda i: (0, 0, 0)),
                  pl.BlockSpec((A_CHUNK, A_WIDTH), lambda i: (0, 0)),
                  pl.BlockSpec((1, A_WIDTH), lambda i: (0, 0)),
                  pl.BlockSpec((1, A_WIDTH), lambda i: (0, 0))],
        out_specs=pl.BlockSpec((tm, A_WIDTH), lambda i: (i, 0)),
        out_shape=jax.ShapeDtypeStruct((m, A_WIDTH), F32),
        compiler_params=_params("arbitrary"),
        name="mixa_prompt",
    )(proj, a_ws, bias_tile, ng, nb)


def _mixa_sample_kernel(p_ref, coef_ref, bias_ref, ng_ref, nb_ref, o_ref, v_ref, *, steps, nb_rows):
    u = jax.nn.gelu(p_ref[:, :A_WIDTH])
    v = _layer_norm_rows(jax.nn.gelu(p_ref[:, A_WIDTH:]), ng_ref[...], nb_ref[...])
    v_ref[...] = v
    for t in range(steps):
        s = bias_ref[t:t + 1, :]
        for j in range(t + 1):
            s = s + coef_ref[t * steps + j:t * steps + j + 1, :] * v[j * nb_rows:(j + 1) * nb_rows, :]
        rs = slice(t * nb_rows, (t + 1) * nb_rows)
        o_ref[rs, :] = u[rs, :] * s


def _mixa_sample(proj, coef, bias, ng, nb, steps, nb_rows):
    m = proj.shape[0]
    kern = functools.partial(_mixa_sample_kernel, steps=steps, nb_rows=nb_rows)
    return pl.pallas_call(
        kern,
        grid=(1,),
        in_specs=[pl.BlockSpec((m, 2 * A_WIDTH), lambda i: (0, 0)),
                  pl.BlockSpec(coef.shape, lambda i: (0, 0)),
                  pl.BlockSpec(bias.shape, lambda i: (0, 0)),
                  pl.BlockSpec((1, A_WIDTH), lambda i: (0, 0)),
                  pl.BlockSpec((1, A_WIDTH), lambda i: (0, 0))],
        out_specs=[pl.BlockSpec((m, A_WIDTH), lambda i: (0, 0)),
                   pl.BlockSpec((m, A_WIDTH), lambda i: (0, 0))],
        out_shape=[jax.ShapeDtypeStruct((m, A_WIDTH), F32),
                   jax.ShapeDtypeStruct((m, A_WIDTH), F32)],
        compiler_params=_params("arbitrary"),
        name="mixa_sample",
    )(proj, coef, bias, ng, nb)


def _unit_lower_inverse_many(a_list, ri, ci, block):
    eye = (ri == ci).astype(F32)
    pair = (lax.shift_right_logical(ri, 1) == lax.shift_right_logical(ci, 1)) & ((ri & 1) == 1) & ((ci & 1) == 0)
    ts = [eye - jnp.where(pair, a, 0.0) for a in a_list]
    n = 2
    while n < block:
        sh = n.bit_length()
        m = ((lax.shift_right_logical(ri, sh) == lax.shift_right_logical(ci, sh))
             & ((ri & n) != 0) & ((ci & n) == 0))
        xs = [_mm(t, jnp.where(m, a, 0.0)) for t, a in zip(ts, a_list)]
        ts = [t - _mm(x, t) for t, x in zip(ts, xs)]
        n *= 2
    return ts


def _mixb_prompt_kernel(qkv_ref, z_ref, bd_ref, cw_ref, nega_ref, dtb_ref, og_ref,
                        o_ref, sfin_ref, s_ref, cbuf_ref):
    c = pl.program_id(0)
    C = DN_CHUNK
    nseq = qkv_ref.shape[0]

    @pl.when(c == 0)
    def _init():
        s_ref[...] = jnp.zeros_like(s_ref)
        cbuf_ref[:, 0:8, :] = jnp.zeros((nseq, 8, B_CONV_CH), F32)

    cw = cw_ref[...]
    og = og_ref[...]
    ri64 = lax.broadcasted_iota(jnp.int32, (C, C), 0)
    ci64 = lax.broadcasted_iota(jnp.int32, (C, C), 1)
    cum_lhs = (ri64 >= ci64).astype(BF16)

    R = MIXB_GROUP * C
    ri = lax.broadcasted_iota(jnp.int32, (R, R), 0)
    ci = lax.broadcasted_iota(jnp.int32, (R, R), 1)
    same = lax.shift_right_logical(ri, C.bit_length() - 1) == lax.shift_right_logical(ci, C.bit_length() - 1)
    tril = same & (ri >= ci)
    strict = same & (ri > ci)
    rep = B_V_HEADS // B_QK_HEADS

    def lane_col(a, lane):
        return jnp.broadcast_to(a[:, lane:lane + 1], (a.shape[0], HEAD_DIM))

    groups = B_V_HEADS // MIXB_GROUP
    units = [(b, grp) for b in range(nseq) for grp in range(groups)]
    heads_of = lambda grp: list(range(grp * MIXB_GROUP, (grp + 1) * MIXB_GROUP))

    acts, gcs, betas = [], [], []
    for b in range(nseq):
        x = qkv_ref[b]
        cbuf_ref[b, 8:8 + C, :] = x
        y = cbuf_ref[b, 5:5 + C, :] * cw[0:1]
        y = y + cbuf_ref[b, 6:6 + C, :] * cw[1:2]
        y = y + cbuf_ref[b, 7:7 + C, :] * cw[2:3]
        y = y + x * cw[3:4]
        cbuf_ref[b, 0:8, :] = x[C - 8:C, :]
        acts.append(jax.nn.silu(y))
        bd = bd_ref[b]
        betas.append(jax.nn.sigmoid(bd))
        g_all = nega_ref[...] * _softplus(bd + dtb_ref[...])
        gcs.append(_mm_exact_lhs(cum_lhs, g_all))

    kst, qst, gcol, glcol, bcol, decay, a_mat, rhs = {}, {}, {}, {}, {}, {}, {}, {}
    for u in units:
        b, grp = u
        act, gc_all, beta_all = acts[b], gcs[b], betas[b]

        def stack(fn):
            return jnp.concatenate([fn(h) for h in heads_of(grp)], axis=0)

        def l2n(cols0, h):
            s = act[:, cols0 + (h // rep) * HEAD_DIM:cols0 + (h // rep + 1) * HEAD_DIM]
            return s * lax.rsqrt(jnp.sum(s * s, -1, keepdims=True) + L2_EPS)

        kst[u] = stack(lambda h: l2n(B_KEY_WIDTH, h))
        qst[u] = stack(lambda h: l2n(0, h) * (HEAD_DIM ** -0.5))
        vst = stack(lambda h: act[:, 2 * B_KEY_WIDTH + h * HEAD_DIM:2 * B_KEY_WIDTH + (h + 1) * HEAD_DIM])
        gcol[u] = stack(lambda h: lane_col(gc_all, B_V_HEADS + h))
        glcol[u] = stack(lambda h: jnp.broadcast_to(gc_all[C - 1:C, B_V_HEADS + h:B_V_HEADS + h + 1], (C, HEAD_DIM)))
        bcol[u] = stack(lambda h: lane_col(beta_all, h))
        grow = gcol[u].T[0:1, :]
        diff = jnp.concatenate([gcol[u], gcol[u]], axis=1) - grow
        decay[u] = jnp.where(tril, jnp.exp(jnp.where(tril, diff, 0.0)), 0.0)
        rhs[u] = jnp.concatenate([vst * bcol[u], kst[u] * bcol[u] * jnp.exp(gcol[u])], axis=1)
    for u in units:
        a_mat[u] = jnp.where(strict, jnp.concatenate([bcol[u], bcol[u]], axis=1) * _mm_nt(kst[u], kst[u]) * decay[u], 0.0)

    t_inv = _unit_lower_inverse_many([a_mat[u] for u in units], ri, ci, C)
    sol = [_mm(t, rhs[u]) for t, u in zip(t_inv, units)]
    qk = [_mm_nt(qst[u], kst[u]) * decay[u] for u in units]
    ws = []
    for sl, u in zip(sol, units):
        b, grp = u
        q_dec = qst[u] * jnp.exp(gcol[u])
        ws.append([_mm(jnp.concatenate([sl[i * C:(i + 1) * C, HEAD_DIM:], q_dec[i * C:(i + 1) * C]], axis=0),
                       s_ref[b * B_V_HEADS + h]) for i, h in enumerate(heads_of(grp))])
    v_new = [sl[:, :HEAD_DIM] - jnp.concatenate([w[:C] for w in wl], axis=0) for sl, wl in zip(sol, ws)]
    outs = [jnp.concatenate([w[C:] for w in wl], axis=0) + _mm(q, v) for wl, q, v in zip(ws, qk, v_new)]
    for u, v, o in zip(units, v_new, outs):
        b, grp = u
        k_dec = kst[u] * jnp.exp(glcol[u] - gcol[u])
        on = o * lax.rsqrt(jnp.mean(o * o, -1, keepdims=True) + RMS_EPS) * og
        for i, h in enumerate(heads_of(grp)):
            rs = slice(i * C, (i + 1) * C)
            si = b * B_V_HEADS + h
            s_ref[si] = s_ref[si] * jnp.exp(glcol[u][i * C:i * C + 1, :]) + _mm_tn(k_dec[rs], v[rs])
            cols = slice(h * HEAD_DIM, (h + 1) * HEAD_DIM)
            o_ref[b, :, cols] = on[rs] * jax.nn.silu(z_ref[b, :, cols])

    @pl.when(c == pl.num_programs(0) - 1)
    def _fin():
        sfin_ref[...] = s_ref[...]


def _mixb_prompt(proj, bd, conv_w, nega, dtb, og, batch, seq):
    nc = seq // DN_CHUNK
    qkv_blk = 2 * A_WIDTH // B_CONV_CH
    z_blk = (2 * A_WIDTH + B_CONV_CH) // B_VAL_WIDTH
    proj3 = proj.reshape(batch, seq, proj.shape[-1])
    bd3 = bd.reshape(batch, seq, LANES)
    o, s_fin = pl.pallas_call(
        _mixb_prompt_kernel,
        grid=(nc,),
        in_specs=[pl.BlockSpec((batch, DN_CHUNK, B_CONV_CH), lambda c: (0, c, qkv_blk)),
                  pl.BlockSpec((batch, DN_CHUNK, B_VAL_WIDTH), lambda c: (0, c, z_blk)),
                  pl.BlockSpec((batch, DN_CHUNK, LANES), lambda c: (0, c, 0)),
                  pl.BlockSpec((B_CONV, B_CONV_CH), lambda c: (0, 0)),
                  pl.BlockSpec((1, LANES), lambda c: (0, 0)),
                  pl.BlockSpec((1, LANES), lambda c: (0, 0)),
                  pl.BlockSpec((1, HEAD_DIM), lambda c: (0, 0))],
        out_specs=[pl.BlockSpec((batch, DN_CHUNK, B_VAL_WIDTH), lambda c: (0, c, 0)),
                   pl.BlockSpec((batch * B_V_HEADS, HEAD_DIM, HEAD_DIM), lambda c: (0, 0, 0))],
        out_shape=[jax.ShapeDtypeStruct((batch, seq, B_VAL_WIDTH), F32),
                   jax.ShapeDtypeStruct((batch * B_V_HEADS, HEAD_DIM, HEAD_DIM), F32)],
        scratch_shapes=[pltpu.VMEM((batch * B_V_HEADS, HEAD_DIM, HEAD_DIM), F32),
                        pltpu.VMEM((batch, DN_CHUNK + 8, B_CONV_CH), F32)],
        compiler_params=_params("arbitrary"),
        name="mixb_prompt",
    )(proj3, proj3, bd3, conv_w, nega, dtb, og)
    return (o.reshape(batch * seq, B_VAL_WIDTH),
            s_fin.reshape(batch, B_V_HEADS, HEAD_DIM, HEAD_DIM))


def _mixb_sample_pre_kernel(qkv_ref, buf_ref, bd_ref, cw_ref, nega_ref, dtb_ref,
                            act_ref, beta_ref, g_ref, *, steps, nb_rows):
    cw = cw_ref[...]

    def slab(j):
        if j < B_CONV - 1:
            return buf_ref[j * nb_rows:(j + 1) * nb_rows, :]
        jj = j - (B_CONV - 1)
        return qkv_ref[jj * nb_rows:(jj + 1) * nb_rows, :]

    for t in range(steps):
        y = slab(t) * cw[0:1]
        for i in range(1, B_CONV):
            y = y + slab(t + i) * cw[i:i + 1]
        act = jax.nn.silu(y)
        rs = slice(t * nb_rows, (t + 1) * nb_rows)
        for qh in range(B_QK_HEADS):
            cq = slice(qh * HEAD_DIM, (qh + 1) * HEAD_DIM)
            ck = slice(B_KEY_WIDTH + qh * HEAD_DIM, B_KEY_WIDTH + (qh + 1) * HEAD_DIM)
            qs = act[:, cq]
            ks = act[:, ck]
            act_ref[rs, cq] = qs * lax.rsqrt(jnp.sum(qs * qs, -1, keepdims=True) + L2_EPS) * (HEAD_DIM ** -0.5)
            act_ref[rs, ck] = ks * lax.rsqrt(jnp.sum(ks * ks, -1, keepdims=True) + L2_EPS)
        act_ref[rs, 2 * B_KEY_WIDTH:] = act[:, 2 * B_KEY_WIDTH:]
    bd = bd_ref[...]
    beta_ref[...] = jax.nn.sigmoid(bd)
    g_ref[...] = nega_ref[...] * _softplus(bd + dtb_ref[...])


def _mixb_sample_pre(proj, buf, bd, conv_w, nega, dtb, steps, nb_rows):
    m = proj.shape[0]
    qkv_blk = 2 * A_WIDTH // B_CONV_CH
    kern = functools.partial(_mixb_sample_pre_kernel, steps=steps, nb_rows=nb_rows)
    return pl.pallas_call(
        kern,
        grid=(1,),
        in_specs=[pl.BlockSpec((m, B_CONV_CH), lambda i: (0, qkv_blk)),
                  pl.BlockSpec(buf.shape, lambda i: (0, 0)),
                  pl.BlockSpec((m, LANES), lambda i: (0, 0)),
                  pl.BlockSpec((B_CONV, B_CONV_CH), lambda i: (0, 0)),
                  pl.BlockSpec((1, LANES), lambda i: (0, 0)),
                  pl.BlockSpec((1, LANES), lambda i: (0, 0))],
        out_specs=[pl.BlockSpec((m, B_CONV_CH), lambda i: (0, 0)),
                   pl.BlockSpec((m, LANES), lambda i: (0, 0)),
                   pl.BlockSpec((m, LANES), lambda i: (0, 0))],
        out_shape=[jax.ShapeDtypeStruct((m, B_CONV_CH), F32),
                   jax.ShapeDtypeStruct((m, LANES), F32),
                   jax.ShapeDtypeStruct((m, LANES), F32)],
        compiler_params=_params("arbitrary"),
        name="mixb_sample_pre",
    )(proj, buf, bd, conv_w, nega, dtb)


def _mixb_sample_rec_kernel(kq_ref, v_ref, z_ref, beta_ref, g_ref, s0_ref, og_ref,
                            o_ref, s_out_ref, *, steps, pairs):
    og = og_ref[...]
    zpad = jnp.zeros((HEAD_DIM - 8, HEAD_DIM), F32)
    zrows = jnp.zeros((8 - steps, HEAD_DIM), F32)
    heads = range(B_V_HEADS)

    def body(bi, carry):
        kqs = [_mm(kq_ref[bi, h], s0_ref[bi, h]) for h in heads]
        pending = []
        for h in heads:
            kq = kq_ref[bi, h]
            g = g_ref[bi, h]
            beta = beta_ref[bi, h]
            v = v_ref[bi, h]
            gc = [g[0:1]]
            for t in range(1, steps):
                gc.append(gc[-1] + g[t:t + 1])
            k = [kq[t:t + 1] for t in range(steps)]
            q = [kq[4 + t:5 + t] for t in range(steps)]
            d = []
            for t in range(steps):
                acc = v[t:t + 1] - jnp.exp(gc[t]) * kqs[h][t:t + 1]
                for j in range(t):
                    kk = jnp.sum(k[j] * k[t], -1, keepdims=True)
                    acc = acc - jnp.exp(gc[t] - gc[j]) * kk * d[j]
                d.append(beta[t:t + 1] * acc)
            outs = []
            for t in range(steps):
                o = jnp.exp(gc[t]) * kqs[h][4 + t:5 + t]
                for j in range(t + 1):
                    qk = jnp.sum(k[j] * q[t], -1, keepdims=True)
                    o = o + jnp.exp(gc[t] - gc[j]) * qk * d[j]
                outs.append(o * lax.rsqrt(jnp.mean(o * o, -1, keepdims=True) + RMS_EPS) * og)
            o_ref[bi, h] = jnp.concatenate(outs + [zrows], axis=0) * jax.nn.silu(z_ref[bi, h])
            k_dec = jnp.concatenate([jnp.exp(gc[-1] - gc[j]) * k[j] for j in range(steps)] + [zrows], axis=0)
            k_pad = jnp.concatenate([k_dec, zpad], axis=0)
            d_pad = jnp.concatenate(d + [zrows, zpad], axis=0)
            pending.append((k_pad.T, d_pad, jnp.exp(gc[-1])))
        for h, (k_t, d_pad, decay_last) in zip(heads, pending):
            s_out_ref[bi, h] = s0_ref[bi, h] * decay_last + _mm(k_t, d_pad)
        return carry

    lax.fori_loop(0, pairs // B_V_HEADS, body, 0)


def _mixb_sample_rec(kq, v, z, beta, g, s0, og, steps, bb):
    nb = kq.shape[0]
    tile = pl.BlockSpec((bb, B_V_HEADS, 8, HEAD_DIM), lambda i: (i, 0, 0, 0))
    st = pl.BlockSpec((bb, B_V_HEADS, HEAD_DIM, HEAD_DIM), lambda i: (i, 0, 0, 0))
    kern = functools.partial(_mixb_sample_rec_kernel, steps=steps, pairs=bb * B_V_HEADS)
    return pl.pallas_call(
        kern,
        grid=(nb // bb,),
        in_specs=[tile, tile, tile, tile, tile, st, pl.BlockSpec((1, HEAD_DIM), lambda i: (0, 0))],
        out_specs=[tile, st],
        out_shape=[jax.ShapeDtypeStruct((nb, B_V_HEADS, 8, HEAD_DIM), F32),
                   jax.ShapeDtypeStruct((nb, B_V_HEADS, HEAD_DIM, HEAD_DIM), F32)],
        compiler_params=_params("arbitrary"),
        name="mixb_sample_rec",
    )(kq, v, z, beta, g, s0, og)


def _outproj_kernel(ap_ref, bp_ref, xp_ref, g1p_ref, sc2p_ref, sh2p_ref,
                    as_ref, bs_ref, xs_ref, g1s_ref, sc2s_ref, sh2s_ref,
                    w_ref, lg_ref, lb_ref, wr_ref, x1_ref, h2_ref, logit_ref, *, prompt_tiles):
    def tile(a_ref, b_ref, x_ref, g1_ref, sc2_ref, sh2_ref):
        mix = (jnp.dot(a_ref[...].astype(BF16), w_ref[:A_WIDTH, :], preferred_element_type=F32)
               + jnp.dot(b_ref[...].astype(BF16), w_ref[A_WIDTH:, :], preferred_element_type=F32))
        x1 = _layer_norm_rows(ALPHA * x_ref[...] + g1_ref[...] * mix, lg_ref[...], lb_ref[...])
        x1_ref[...] = x1
        h2 = x1 * (1.0 + sc2_ref[...]) + sh2_ref[...]
        _store_token_rows(h2_ref, h2)
        logit_ref[...] = _mm3(h2, wr_ref[...])

    is_prompt = pl.program_id(0) < prompt_tiles

    @pl.when(is_prompt)
    def _():
        tile(ap_ref, bp_ref, xp_ref, g1p_ref, sc2p_ref, sh2p_ref)

    @pl.when(jnp.logical_not(is_prompt))
    def _():
        tile(as_ref, bs_ref, xs_ref, g1s_ref, sc2s_ref, sh2s_ref)


def _outproj(prompt, sample, mod_p, mod_s, w_out, lg, lb, w_router, tm, seq):
    n_p = prompt[2].shape[0]
    n_s = sample[2].shape[0]
    pt = n_p // tm
    m = n_p + n_s
    tiles_per_seq = seq // tm
    p_row = lambda w: pl.BlockSpec((tm, w), lambda i: (jnp.minimum(i, pt - 1), 0))
    s_row = lambda w: pl.BlockSpec((tm, w), lambda i: (jnp.maximum(i - pt, 0), 0))
    p_mod = lambda col: pl.BlockSpec((None, 1, D_MODEL), lambda i: (jnp.minimum(i, pt - 1) // tiles_per_seq, 0, col))
    s_mod = lambda col: pl.BlockSpec((tm, D_MODEL), lambda i: (0, col))
    row = lambda w: pl.BlockSpec((tm, w), lambda i: (i, 0))
    full = lambda s: pl.BlockSpec(s, lambda i: (0, 0))
    return pl.pallas_call(
        functools.partial(_outproj_kernel, prompt_tiles=pt),
        grid=(m // tm,),
        in_specs=[p_row(A_WIDTH), p_row(B_VAL_WIDTH), p_row(D_MODEL), p_mod(2), p_mod(4), p_mod(3),
                  s_row(A_WIDTH), s_row(B_VAL_WIDTH), s_row(D_MODEL), s_mod(2), s_mod(4), s_mod(3),
                  full((D_MODEL, D_MODEL)), full((1, D_MODEL)), full((1, D_MODEL)),
                  full((D_MODEL, LANES))],
        out_specs=[row(D_MODEL), pl.BlockSpec((tm * ROW_TILES, LANES), lambda i: (i, 0)), row(LANES)],
        out_shape=[jax.ShapeDtypeStruct((m, D_MODEL), F32),
                   jax.ShapeDtypeStruct((m * ROW_TILES, LANES), F32),
                   jax.ShapeDtypeStruct((m, LANES), F32)],
        compiler_params=_params("arbitrary"),
        name="outproj",
    )(*prompt, mod_p, mod_p, mod_p, *sample, mod_s, mod_s, mod_s, w_out, lg, lb, w_router)


def _router_kernel(logit_ref, bias_ref, route_ref, count_ref, base_ref):
    i = pl.program_id(0)

    @pl.when(i == 0)
    def _():
        base_ref[...] = jnp.zeros_like(base_ref)

    lg = logit_ref[...]
    tm = lg.shape[0]
    bias = bias_ref[...]
    lane = lax.broadcasted_iota(jnp.int32, lg.shape, 1)
    neg = -jnp.inf

    def first_argmax(score):
        mx = jnp.max(score, -1, keepdims=True)
        return jnp.min(jnp.where(score == mx, lane, LANES), -1, keepdims=True)

    def pick(vals, idx):
        return jnp.sum(jnp.where(lane == idx, vals, 0.0), -1, keepdims=True)

    gmask = lane < N_GROUPS
    mg = jnp.max(jnp.where(gmask, lg, neg), -1, keepdims=True)
    eg = jnp.where(gmask, jnp.exp(jnp.where(gmask, lg - mg, 0.0)), 0.0)
    pg = eg / jnp.sum(eg, -1, keepdims=True)
    sel_g = first_argmax(jnp.where(gmask, lg + bias, neg))
    p_sel = pick(pg, sel_g)

    lo = N_GROUPS + sel_g * EXPERTS_PER_GROUP
    emask = (lane >= lo) & (lane < lo + EXPERTS_PER_GROUP)
    me = jnp.max(jnp.where(emask, lg, neg), -1, keepdims=True)
    ee = jnp.where(emask, jnp.exp(jnp.where(emask, lg - me, 0.0)), 0.0)
    pe = ee / jnp.sum(ee, -1, keepdims=True)
    score = jnp.where(emask, pe + bias, neg)
    i1 = first_argmax(score)
    i2 = first_argmax(jnp.where(lane == i1, neg, score))
    w1 = pick(pe, i1)
    w2 = pick(pe, i2)
    wsum = w1 + w2
    gate1 = w1 / wsum * p_sel
    gate2 = w2 / wsum * p_sel

    hot = ((lane == i1) | (lane == i2)).astype(BF16)
    ri = lax.broadcasted_iota(jnp.int32, (tm, tm), 0)
    ci = lax.broadcasted_iota(jnp.int32, (tm, tm), 1)
    before = jnp.dot((ri > ci).astype(BF16), hot, preferred_element_type=F32) + base_ref[...]
    rank1 = pick(before, i1)
    rank2 = pick(before, i2)
    base_ref[...] = base_ref[...] + jnp.sum(hot.astype(F32), 0, keepdims=True)
    count_ref[...] = base_ref[...]

    out = jnp.where(lane == 0, (i1 - N_GROUPS).astype(F32), 0.0)
    out = jnp.where(lane == 1, (i2 - N_GROUPS).astype(F32), out)
    out = jnp.where(lane == 2, gate1, out)
    out = jnp.where(lane == 3, gate2, out)
    out = jnp.where(lane == 4, rank1, out)
    out = jnp.where(lane == 5, rank2, out)
    route_ref[...] = out


def _router(logits, bias):
    m = logits.shape[0]
    return pl.pallas_call(
        _router_kernel,
        grid=(m // ROUTE_TILE,),
        in_specs=[pl.BlockSpec((ROUTE_TILE, LANES), lambda i: (i, 0)),
                  pl.BlockSpec((1, LANES), lambda i: (0, 0))],
        out_specs=[pl.BlockSpec((ROUTE_TILE, LANES), lambda i: (i, 0)),
                   pl.BlockSpec((1, LANES), lambda i: (0, 0))],
        out_shape=[jax.ShapeDtypeStruct((m, LANES), F32),
                   jax.ShapeDtypeStruct((1, LANES), F32)],
        scratch_shapes=[pltpu.VMEM((1, LANES), F32)],
        compiler_params=_params("arbitrary"),
        name="router",
    )(logits, bias)


def _expert_kernel(be_ref, nv_ref, src_ref, dst_ref, first_ref, next_ref, wslot_ref,
                   h_ref, wg_ref, wu_ref, wd_ref, y_ref,
                   xbuf, ybuf, wg_st, wu_st, wd_st, wg_s, wu_s, wd_s, gsem, ssem, wsem):
    i = pl.program_id(0)
    nv = nv_ref[0]
    slot = i % 2
    other = 1 - slot

    def gather_row(blk, r, s):
        tok = src_ref[blk * MOE_ROWS + r]
        return pltpu.make_async_copy(h_ref.at[pl.ds(pl.multiple_of(tok * ROW_TILES, ROW_TILES), ROW_TILES)],
                                     xbuf.at[s, pl.ds(r * ROW_TILES, ROW_TILES)], gsem.at[s])

    def scatter_row(blk, r, s):
        row = dst_ref[(blk + 1) * MOE_ROWS + r]
        return pltpu.make_async_copy(ybuf.at[s, pl.ds(r * ROW_TILES, ROW_TILES)],
                                     y_ref.at[pl.ds(pl.multiple_of(row * ROW_TILES, ROW_TILES), ROW_TILES)], ssem.at[s])

    def gather_wait(s):
        pltpu.make_async_copy(h_ref.at[pl.ds(0, MOE_ROWS * ROW_TILES)], xbuf.at[s], gsem.at[s]).wait()

    def scatter_wait(s):
        pltpu.make_async_copy(ybuf.at[s], y_ref.at[pl.ds(0, MOE_ROWS * ROW_TILES)], ssem.at[s]).wait()

    @pl.when(i == 0)
    def _prologue():
        ybuf[...] = jnp.zeros_like(ybuf)
        base = y_ref.shape[0] - 2 * MOE_ROWS * ROW_TILES
        pltpu.make_async_copy(ybuf.at[0], y_ref.at[pl.ds(base, MOE_ROWS * ROW_TILES)], ssem.at[0]).start()

        def body(r, carry):
            gather_row(0, r, 0).start()
            return carry

        lax.fori_loop(0, MOE_ROWS, body, 0, unroll=8)

    def weight_copies(e, p):
        return [pltpu.make_async_copy(src.at[e], dst.at[p], wsem.at[p])
                for src, dst in ((wg_ref, wg_st), (wu_ref, wu_st), (wd_ref, wd_st))]

    @pl.when(i == 0)
    def _():
        for c in weight_copies(be_ref[0], 0):
            c.start()

    @pl.when((i < nv) & (first_ref[i] == 1))
    def _():
        p = wslot_ref[i]
        for c in weight_copies(be_ref[i], p):
            c.wait()
        wg_s[...] = wg_st[p].astype(BF16)
        wu_s[...] = wu_st[p].astype(BF16)
        wd_s[...] = wd_st[p].astype(BF16)

        @pl.when(next_ref[i] >= 0)
        def _():
            for c in weight_copies(next_ref[i], 1 - p):
                c.start()

    @pl.when(i < nv)
    def _block():
        gather_wait(slot)
        for r in range(MOE_ROWS):
            gather_row(i + 1, r, other).start()
            scatter_row(i - 1, r, other).start()
        x = _load_token_rows(xbuf.at[slot], MOE_ROWS).astype(BF16)
        hg = jnp.dot(x, wg_s[...], preferred_element_type=F32)
        hu = jnp.dot(x, wu_s[...], preferred_element_type=F32)
        hid = (jax.nn.silu(hg) * hu).astype(BF16)
        y = jnp.dot(hid, wd_s[...], preferred_element_type=F32)
        scatter_wait(slot)
        _store_token_rows(ybuf.at[slot], y)

    @pl.when(i == nv - 1)
    def _tail():
        def body(r, carry):
            scatter_row(i, r, slot).start()
            return carry

        lax.fori_loop(0, MOE_ROWS, body, 0, unroll=8)
        scatter_wait(slot)
        scatter_wait(other)
        gather_wait(other)


def _experts(block_e, n_valid, slot_src, slot_dst, h2, w_gate, w_up, w_down, n_out_rows):
    n_blocks = block_e.shape[0]
    idx = jnp.arange(n_blocks, dtype=jnp.int32)
    first = (idx < n_valid[0]) & ((idx == 0) | (block_e != jnp.roll(block_e, 1)))
    wslot = (jnp.cumsum(first.astype(jnp.int32)) - 1) % 2
    first_at = jnp.where(first, idx, n_blocks)
    next_first = jnp.concatenate([lax.cummin(first_at, reverse=True)[1:], jnp.full((1,), n_blocks, jnp.int32)])
    next_e = jnp.where(next_first < n_blocks, block_e[jnp.minimum(next_first, n_blocks - 1)], -1)
    any_spec = pl.BlockSpec(memory_space=pl.ANY)
    return pl.pallas_call(
        _expert_kernel,
        grid_spec=pltpu.PrefetchScalarGridSpec(
            num_scalar_prefetch=7,
            grid=(n_blocks,),
            in_specs=[any_spec, any_spec, any_spec, any_spec],
            out_specs=any_spec,
            scratch_shapes=[pltpu.VMEM((2, MOE_ROWS * ROW_TILES, LANES), F32),
                            pltpu.VMEM((2, MOE_ROWS * ROW_TILES, LANES), F32),
                            pltpu.VMEM((2, D_MODEL, D_EXPERT), F32),
                            pltpu.VMEM((2, D_MODEL, D_EXPERT), F32),
                            pltpu.VMEM((2, D_EXPERT, D_MODEL), F32),
                            pltpu.VMEM((D_MODEL, D_EXPERT), BF16),
                            pltpu.VMEM((D_MODEL, D_EXPERT), BF16),
                            pltpu.VMEM((D_EXPERT, D_MODEL), BF16),
                            pltpu.SemaphoreType.DMA((2,)),
                            pltpu.SemaphoreType.DMA((2,)),
                            pltpu.SemaphoreType.DMA((2,))]),
        out_shape=jax.ShapeDtypeStruct((n_out_rows * ROW_TILES, LANES), F32),
        compiler_params=_params("arbitrary"),
        name="experts",
    )(block_e, n_valid, slot_src, slot_dst, first.astype(jnp.int32), next_e.astype(jnp.int32),
      wslot.astype(jnp.int32), h2, w_gate, w_up, w_down)


def _final_kernel(x1_ref, y0_ref, y1_ref, route_ref, g2_ref, lg_ref, lb_ref, o_ref):
    route = route_ref[...]
    tm = route.shape[0]
    ff = _load_token_rows(y0_ref, tm) * route[:, 2:3] + _load_token_rows(y1_ref, tm) * route[:, 3:4]
    o_ref[...] = _layer_norm_rows(ALPHA * x1_ref[...] + g2_ref[...] * ff, lg_ref[...], lb_ref[...])


def _final(x1, y01, route, mod, lg, lb, tm, rows_per_mod, row0, n_rows, n_tok):
    off = row0 // tm
    off1 = (n_tok + row0) // tm
    return pl.pallas_call(
        _final_kernel,
        grid=(n_rows // tm,),
        in_specs=[pl.BlockSpec((tm, D_MODEL), lambda i: (off + i, 0)),
                  pl.BlockSpec((tm * ROW_TILES, LANES), lambda i: (off + i, 0)),
                  pl.BlockSpec((tm * ROW_TILES, LANES), lambda i: (off1 + i, 0)),
                  pl.BlockSpec((tm, LANES), lambda i: (off + i, 0)),
                  _mod_spec(mod, 5, tm, rows_per_mod),
                  pl.BlockSpec((1, D_MODEL), lambda i: (0, 0)),
                  pl.BlockSpec((1, D_MODEL), lambda i: (0, 0))],
        out_specs=pl.BlockSpec((tm, D_MODEL), lambda i: (i, 0)),
        out_shape=jax.ShapeDtypeStruct((n_rows, D_MODEL), F32),
        compiler_params=_params("arbitrary"),
        name="final",
    )(x1, y01, y01, route, mod, lg, lb)


def _lane_row(vec, offset):
    return jnp.zeros((1, LANES), F32).at[0, offset:offset + vec.shape[0]].set(vec.astype(F32))


def kernel(x_prompt, x_sample, state_conv, state_ssm, c_prompt, c_sample, w_ada, b_ada, w_in, a_ws, a_bs, a_norm_g, a_norm_b, b_conv_w, b_a_log, b_dt_bias, b_onorm_g, w_out, ln1_g, ln1_b, w_router_g, b_router_g, w_router_e, b_router_e, w_gate, w_up, w_down, ln2_g, ln2_b):
    batch, seq, d = x_prompt.shape
    nb, steps, _ = x_sample.shape
    n_p = batch * seq
    n_s = nb * steps
    n_tok = n_p + n_s
    l = 0

    c_rows = batch + nb
    c_pad = (-c_rows) % 8
    c_all = jnp.concatenate([c_prompt, c_sample, jnp.zeros((c_pad, d), F32)], axis=0)
    mod = _ada(c_all, w_ada[l], b_ada[l].reshape(1, -1))
    mod_p = mod[:batch].reshape(batch, 1, 6 * d)
    mod_s = mod[batch:batch + nb]

    w_in_t = jnp.swapaxes(w_in[l], 0, 1)
    w_bd = jnp.pad(w_in_t[MAIN_WIDTH:], ((0, LANES - 2 * B_V_HEADS), (0, 0))).astype(BF16)
    xp = x_prompt.reshape(n_p, d)
    xs = jnp.swapaxes(x_sample, 0, 1).reshape(n_s, d)
    proj_p, bd_p = _inproj(xp, mod_p, w_in_t, w_bd, 512, seq)
    proj_s, bd_s = _inproj(xs, mod_s, w_in_t, w_bd, nb, 0)

    ng = a_norm_g[l].reshape(1, -1)
    nbias = a_norm_b[l].reshape(1, -1)
    bias_tile = jnp.repeat(a_bs[l].T, HEAD_DIM, axis=1)
    a_out_p = _mixa_prompt(proj_p, a_ws[l], bias_tile, ng, nbias, 256)
    coef = jnp.repeat(jnp.transpose(a_ws[l][:, :steps, :steps], (1, 2, 0)).reshape(steps * steps, A_HEADS),
                      HEAD_DIM, axis=1)
    a_out_s, chunkv = _mixa_sample(proj_s, coef, bias_tile[:steps], ng, nbias, steps, nb)

    nega = _lane_row(-jnp.exp(b_a_log[l].astype(F32)), B_V_HEADS)
    dtb = _lane_row(b_dt_bias[l], B_V_HEADS)
    og = b_onorm_g[l].reshape(1, -1)
    b_out_p, ssm_p = _mixb_prompt(proj_p, bd_p, b_conv_w[l], nega, dtb, og, batch, seq)
    qkv0 = 2 * A_WIDTH
    conv_p = proj_p.reshape(batch, seq, MAIN_WIDTH)[:, seq - (B_CONV - 1):, qkv0:qkv0 + B_CONV_CH]

    buf_s = jnp.swapaxes(state_conv[l], 0, 1).reshape((B_CONV - 1) * nb, B_CONV_CH)
    act_s, beta_s, g_s = _mixb_sample_pre(proj_s, buf_s, bd_s, b_conv_w[l], nega, dtb, steps, nb)
    rep = B_V_HEADS // B_QK_HEADS
    act4 = act_s.reshape(steps, nb, B_CONV_CH)
    q8 = jnp.repeat(act4[..., :B_KEY_WIDTH].reshape(steps, nb, B_QK_HEADS, HEAD_DIM), rep, axis=2)
    k8 = jnp.repeat(act4[..., B_KEY_WIDTH:2 * B_KEY_WIDTH].reshape(steps, nb, B_QK_HEADS, HEAD_DIM), rep, axis=2)

    def to_tiles(a):
        a = jnp.transpose(a, (1, 2, 0, 3))
        return jnp.pad(a, ((0, 0), (0, 0), (0, 8 - a.shape[2]), (0, 0)))

    kq_t = jnp.concatenate([to_tiles(k8)[:, :, :4], to_tiles(q8)[:, :, :4]], axis=2)
    v_t = to_tiles(act4[..., 2 * B_KEY_WIDTH:].reshape(steps, nb, B_V_HEADS, HEAD_DIM))
    z_t = to_tiles(proj_s[:, qkv0 + B_CONV_CH:].reshape(steps, nb, B_V_HEADS, HEAD_DIM))

    def gate_tiles(a, off):
        a = a[:, off:off + B_V_HEADS].reshape(steps, nb, B_V_HEADS, 1)
        return to_tiles(jnp.broadcast_to(a, (steps, nb, B_V_HEADS, HEAD_DIM)))

    o_t, ssm_s = _mixb_sample_rec(kq_t, v_t, z_t, gate_tiles(beta_s, 0), gate_tiles(g_s, B_V_HEADS),
                                  state_ssm[l], og, steps, 8)
    b_out_s = jnp.transpose(o_t[:, :, :steps], (2, 0, 1, 3)).reshape(n_s, B_VAL_WIDTH)
    conv_s = jnp.swapaxes(proj_s.reshape(steps, nb, MAIN_WIDTH)[steps - (B_CONV - 1):, :, qkv0:qkv0 + B_CONV_CH], 0, 1)

    w_out_b = w_out[l].astype(BF16)
    lg1 = ln1_g[l].reshape(1, -1)
    lb1 = ln1_b[l].reshape(1, -1)
    w_router = jnp.concatenate([w_router_g[l], w_router_e[l],
                                jnp.zeros((d, LANES - N_GROUPS - N_EXPERTS), F32)], axis=1)
    out_tm = 256
    mod_s_tile = jnp.tile(mod_s, (out_tm // nb, 1))
    x1, h2, logits = _outproj((a_out_p, b_out_p, xp), (a_out_s, b_out_s, xs), mod_p, mod_s_tile,
                              w_out_b, lg1, lb1, w_router, out_tm, seq)

    r_bias = jnp.zeros((1, LANES), F32)
    r_bias = r_bias.at[0, :N_GROUPS].set(b_router_g[l]).at[0, N_GROUPS:N_GROUPS + N_EXPERTS].set(b_router_e[l])
    route, counts_row = _router(logits, r_bias)
    counts = counts_row[0, N_GROUPS:N_GROUPS + N_EXPERTS].astype(jnp.int32)
    route_t = route[:, :8].T
    eid = route_t[0:2].astype(jnp.int32)
    rank = route_t[4:6].astype(jnp.int32)
    padded = (counts + MOE_ROWS - 1) // MOE_ROWS * MOE_ROWS
    pend = jnp.cumsum(padded)
    pstart = pend - padded
    dest = rank + jnp.sum(jnp.where(eid[None] == jnp.arange(N_EXPERTS, dtype=jnp.int32)[:, None, None],
                                    pstart[:, None, None], 0), axis=0)
    n_blocks = -(-(2 * n_tok) // MOE_ROWS) + N_EXPERTS
    n_slots = n_blocks * MOE_ROWS
    slot_row = jnp.full((n_slots,), -1, jnp.int32).at[dest.reshape(-1)].set(jnp.arange(2 * n_tok, dtype=jnp.int32))
    slot_id = jnp.arange(n_slots, dtype=jnp.int32)
    trash = 2 * n_tok + (slot_id // MOE_ROWS % 2) * MOE_ROWS + slot_id % MOE_ROWS
    row_id = jnp.arange(MOE_ROWS, dtype=jnp.int32)
    slot_src = jnp.concatenate([jnp.where(slot_row < 0, 0, slot_row % n_tok),
                                jnp.zeros((MOE_ROWS,), jnp.int32)])
    slot_dst = jnp.concatenate([2 * n_tok + MOE_ROWS + row_id,
                                jnp.where(slot_row < 0, trash, slot_row)])
    blk0 = jnp.arange(n_blocks, dtype=jnp.int32) * MOE_ROWS
    block_e = jnp.minimum(jnp.sum(pend[None, :] <= blk0[:, None], axis=1), N_EXPERTS - 1).astype(jnp.int32)
    n_valid = (pend[-1:] // MOE_ROWS).astype(jnp.int32)

    y01 = _experts(block_e, n_valid, slot_src, slot_dst, h2, w_gate[l], w_up[l], w_down[l],
                   2 * n_tok + 2 * MOE_ROWS)

    lg2 = ln2_g[l].reshape(1, -1)
    lb2 = ln2_b[l].reshape(1, -1)
    y_p = _final(x1, y01, route, mod_p, lg2, lb2, 256, seq, 0, n_p, n_tok)
    y_s = _final(x1, y01, route, mod_s, lg2, lb2, nb, 0, n_p, n_s, n_tok)

    y_prompt = y_p.reshape(batch, seq, d)
    y_sample = jnp.swapaxes(y_s.reshape(steps, nb, d), 0, 1)
    chunkv_s = jnp.swapaxes(chunkv.reshape(steps, nb, A_HEADS, HEAD_DIM), 0, 1)
    return (y_prompt, y_sample, conv_p[None], ssm_p[None], conv_s[None], ssm_s[None], chunkv_s[None])
```

```python
import functools

import jax
import jax.numpy as jnp
from jax import lax
from jax.experimental import pallas as pl
from jax.experimental.pallas import tpu as pltpu

F32 = jnp.float32
BF16 = jnp.bfloat16

D_MODEL = 2048
DEPTH = 1
A_HEADS = 8
HEAD_DIM = 128
A_WIDTH = 1024
A_CHUNK = 128
B_QK_HEADS = 4
B_V_HEADS = 8
B_KEY_WIDTH = 512
B_VAL_WIDTH = 1024
B_CONV = 4
B_CONV_CH = 2048
DN_CHUNK = 64
MAIN_WIDTH = 2 * A_WIDTH + B_CONV_CH + B_VAL_WIDTH
N_GROUPS = 4
EXPERTS_PER_GROUP = 8
N_EXPERTS = 32
D_EXPERT = 512
ALPHA = (2 * DEPTH) ** 0.25
LN_EPS = 1e-5
RMS_EPS = 1e-6
L2_EPS = 1e-6

LANES = 128
VMEM_LIMIT = 56 * 1024 * 1024
MOE_ROWS = 256
MIXB_GROUP = 4


def _params(*sem):
    return pltpu.CompilerParams(dimension_semantics=sem, vmem_limit_bytes=VMEM_LIMIT)


def _mm(a, b):
    return jnp.dot(a.astype(BF16), b.astype(BF16), preferred_element_type=F32)


def _mm_nt(a, b):
    return lax.dot_general(a.astype(BF16), b.astype(BF16), (((1,), (1,)), ((), ())),
                           preferred_element_type=F32)


def _mm_tn(a, b):
    return lax.dot_general(a.astype(BF16), b.astype(BF16), (((0,), (0,)), ((), ())),
                           preferred_element_type=F32)


def _split(x):
    hi = x.astype(BF16)
    lo = (x - hi.astype(F32)).astype(BF16)
    return hi, lo


def _mm_exact_lhs(a_bf16, b):
    hi, lo = _split(b)
    return (jnp.dot(a_bf16, hi, preferred_element_type=F32)
            + jnp.dot(a_bf16, lo, preferred_element_type=F32))


def _mm3(a, b):
    ah, al = _split(a)
    bh, bl = _split(b)
    return (jnp.dot(ah, bh, preferred_element_type=F32) + jnp.dot(ah, bl, preferred_element_type=F32)
            + jnp.dot(al, bh, preferred_element_type=F32))


ROW_TILES = D_MODEL // LANES


def _store_token_rows(ref, val):
    n = val.shape[0]
    for j in range(ROW_TILES):
        ref[pl.ds(j, n, stride=ROW_TILES), :] = val[:, j * LANES:(j + 1) * LANES]


def _load_token_rows(ref, n):
    return jnp.concatenate([ref[pl.ds(j, n, stride=ROW_TILES), :] for j in range(ROW_TILES)], axis=1)


def _softplus(x):
    return jnp.maximum(x, 0.0) + jnp.log1p(jnp.exp(-jnp.abs(x)))


def _layer_norm_rows(x, g, b):
    mu = jnp.mean(x, -1, keepdims=True)
    xc = x - mu
    var = jnp.mean(xc * xc, -1, keepdims=True)
    return xc * lax.rsqrt(var + LN_EPS) * g + b


def _ada_kernel(c_ref, w_ref, b_ref, o_ref):
    a = jax.nn.silu(c_ref[...]).astype(BF16)
    o_ref[...] = jnp.dot(a, w_ref[...].astype(BF16), preferred_element_type=F32) + b_ref[...]


def _ada(c_all, w_ada, b_ada):
    rows = c_all.shape[0]
    tn = 1024
    return pl.pallas_call(
        _ada_kernel,
        grid=(6 * D_MODEL // tn,),
        in_specs=[pl.BlockSpec((rows, D_MODEL), lambda j: (0, 0)),
                  pl.BlockSpec((D_MODEL, tn), lambda j: (0, j)),
                  pl.BlockSpec((1, tn), lambda j: (0, j))],
        out_specs=pl.BlockSpec((rows, tn), lambda j: (0, j)),
        out_shape=jax.ShapeDtypeStruct((rows, 6 * D_MODEL), F32),
        compiler_params=_params("arbitrary"),
        name="ada",
    )(c_all, w_ada, b_ada)


def _inproj_kernel(xp_ref, scp_ref, shp_ref, xs_ref, scs_ref, shs_ref, w_ref, wbd_ref,
                   op_ref, bdp_ref, os_ref, bds_ref, wb_ref, *, prompt_tiles):
    j = pl.program_id(0)
    i = pl.program_id(1)

    @pl.when(i == 0)
    def _():
        wb_ref[...] = w_ref[...].astype(BF16)

    def project(h, o_ref, bd_ref):
        o_ref[...] = _mm_nt(h, wb_ref[...])

        @pl.when(j == 0)
        def _():
            bd_ref[0] = _mm_nt(h, wbd_ref[...])

        @pl.when(j != 0)
        def _():
            bd_ref[0] = jnp.zeros(bd_ref.shape[1:], F32)

    @pl.when(i < prompt_tiles)
    def _():
        project((xp_ref[...] * (1.0 + scp_ref[...]) + shp_ref[...]).astype(BF16), op_ref, bdp_ref)

    @pl.when(i >= prompt_tiles)
    def _():
        n_s, nb = xs_ref.shape[0], scs_ref.shape[0]
        x = xs_ref[...].reshape(n_s // nb, nb, D_MODEL)
        h = x * (1.0 + scs_ref[...])[None] + shs_ref[...][None]
        project(h.reshape(n_s, D_MODEL).astype(BF16), os_ref, bds_ref)


def _mod_spec(mod, col, tm, rows_per_mod, axis=0):
    if rows_per_mod:
        tiles = rows_per_mod // tm
        return pl.BlockSpec((None, 1, D_MODEL), lambda *g: (g[axis] // tiles, 0, col))
    return pl.BlockSpec((tm, D_MODEL), lambda *g: (0, col))


def _inproj(xp, xs, mod_p, mod_s, w_in_t, w_bd, tm, seq):
    n_p, n_s = xp.shape[0], xs.shape[0]
    nb = mod_s.shape[0]
    tn = 1024
    nj = MAIN_WIDTH // tn
    pt = n_p // tm
    tiles_per_seq = seq // tm
    pi = lambda i: jnp.minimum(i, pt - 1)
    p_mod = lambda col: pl.BlockSpec((None, 1, D_MODEL), lambda j, i: (pi(i) // tiles_per_seq, 0, col))
    s_mod = lambda col: pl.BlockSpec((nb, D_MODEL), lambda j, i: (0, col))
    proj_p, bd_p, proj_s, bd_s = pl.pallas_call(
        functools.partial(_inproj_kernel, prompt_tiles=pt),
        grid=(nj, pt + 1),
        in_specs=[pl.BlockSpec((tm, D_MODEL), lambda j, i: (pi(i), 0)), p_mod(1), p_mod(0),
                  pl.BlockSpec((n_s, D_MODEL), lambda j, i: (0, 0)), s_mod(1), s_mod(0),
                  pl.BlockSpec((tn, D_MODEL), lambda j, i: (j, 0)),
                  pl.BlockSpec((LANES, D_MODEL), lambda j, i: (0, 0))],
        out_specs=[pl.BlockSpec((tm, tn), lambda j, i: (pi(i), j)),
                   pl.BlockSpec((1, tm, LANES), lambda j, i: (j, pi(i), 0)),
                   pl.BlockSpec((n_s, tn), lambda j, i: (0, j)),
                   pl.BlockSpec((1, n_s, LANES), lambda j, i: (j, 0, 0))],
        out_shape=[jax.ShapeDtypeStruct((n_p, MAIN_WIDTH), F32),
                   jax.ShapeDtypeStruct((nj, n_p, LANES), F32),
                   jax.ShapeDtypeStruct((n_s, MAIN_WIDTH), F32),
                   jax.ShapeDtypeStruct((nj, n_s, LANES), F32)],
        scratch_shapes=[pltpu.VMEM((tn, D_MODEL), BF16)],
        compiler_params=_params("arbitrary", "arbitrary"),
        name="inproj",
    )(xp, mod_p, mod_p, xs, mod_s, mod_s, w_in_t, w_bd)
    return proj_p, bd_p[0], proj_s, bd_s[0]


def _mixa_prompt_kernel(p_ref, ws_ref, bias_ref, ng_ref, nb_ref, o_ref):
    rows = p_ref.shape[0]
    u = jax.nn.gelu(p_ref[:, :A_WIDTH])
    v = _layer_norm_rows(jax.nn.gelu(p_ref[:, A_WIDTH:]), ng_ref[...], nb_ref[...])
    ri = lax.broadcasted_iota(jnp.int32, (A_CHUNK, A_CHUNK), 0)
    ci = lax.broadcasted_iota(jnp.int32, (A_CHUNK, A_CHUNK), 1)
    for h in range(A_HEADS):
        w = jnp.where(ri >= ci, ws_ref[h], 0.0).astype(BF16)
        cols = slice(h * HEAD_DIM, (h + 1) * HEAD_DIM)
        for c in range(rows // A_CHUNK):
            rs = slice(c * A_CHUNK, (c + 1) * A_CHUNK)
            s = jnp.dot(w, v[rs, cols].astype(BF16), preferred_element_type=F32) + bias_ref[:, cols]
            o_ref[rs, cols] = u[rs, cols] * s


def _mixa_prompt(proj, a_ws, bias_tile, ng, nb, tm):
    m = proj.shape[0]
    return pl.pallas_call(
        _mixa_prompt_kernel,
        grid=(m // tm,),
        in_specs=[pl.BlockSpec((tm, 2 * A_WIDTH), lambda i: (i, 0)),
                  pl.BlockSpec((A_HEADS, A_CHUNK, A_CHUNK), lambda i: (0, 0, 0)),
                  pl.BlockSpec((A_CHUNK, A_WIDTH), lambda i: (0, 0)),
                  pl.BlockSpec((1, A_WIDTH), lambda i: (0, 0)),
                  pl.BlockSpec((1, A_WIDTH), lambda i: (0, 0))],
        out_specs=pl.BlockSpec((tm, A_WIDTH), lambda i: (i, 0)),
        out_shape=jax.ShapeDtypeStruct((m, A_WIDTH), F32),
        compiler_params=_params("arbitrary"),
        name="mixa_prompt",
    )(proj, a_ws, bias_tile, ng, nb)


def _mixa_sample_kernel(p_ref, coef_ref, bias_ref, ng_ref, nb_ref, o_ref, v_ref, *, steps, nb_rows):
    u = jax.nn.gelu(p_ref[:, :A_WIDTH])
    v = _layer_norm_rows(jax.nn.gelu(p_ref[:, A_WIDTH:]), ng_ref[...], nb_ref[...])
    v_ref[...] = v
    for t in range(steps):
        s = bias_ref[t:t + 1, :]
        for j in range(t + 1):
            s = s + coef_ref[t * steps + j:t * steps + j + 1, :] * v[j * nb_rows:(j + 1) * nb_rows, :]
        rs = slice(t * nb_rows, (t + 1) * nb_rows)
        o_ref[rs, :] = u[rs, :] * s


def _mixa_sample(proj, coef, bias, ng, nb, steps, nb_rows):
    m = proj.shape[0]
    kern = functools.partial(_mixa_sample_kernel, steps=steps, nb_rows=nb_rows)
    return pl.pallas_call(
        kern,
        grid=(1,),
        in_specs=[pl.BlockSpec((m, 2 * A_WIDTH), lambda i: (0, 0)),
                  pl.BlockSpec(coef.shape, lambda i: (0, 0)),
                  pl.BlockSpec(bias.shape, lambda i: (0, 0)),
                  pl.BlockSpec((1, A_WIDTH), lambda i: (0, 0)),
                  pl.BlockSpec((1, A_WIDTH), lambda i: (0, 0))],
        out_specs=[pl.BlockSpec((m, A_WIDTH), lambda i: (0, 0)),
                   pl.BlockSpec((m, A_WIDTH), lambda i: (0, 0))],
        out_shape=[jax.ShapeDtypeStruct((m, A_WIDTH), F32),
                   jax.ShapeDtypeStruct((m, A_WIDTH), F32)],
        compiler_params=_params("arbitrary"),
        name="mixa_sample",
    )(proj, coef, bias, ng, nb)


def _unit_lower_inverse_many(a_list, ri, ci, block):
    eye = (ri == ci).astype(F32)
    pair = (lax.shift_right_logical(ri, 1) == lax.shift_right_logical(ci, 1)) & ((ri & 1) == 1) & ((ci & 1) == 0)
    ts = [eye - jnp.where(pair, a, 0.0) for a in a_list]
    n = 2
    while n < block:
        sh = n.bit_length()
        m = ((lax.shift_right_logical(ri, sh) == lax.shift_right_logical(ci, sh))
             & ((ri & n) != 0) & ((ci & n) == 0))
        xs = [_mm(t, jnp.where(m, a, 0.0)) for t, a in zip(ts, a_list)]
        ts = [t - _mm(x, t) for t, x in zip(ts, xs)]
        n *= 2
    return ts


def _mixb_prompt_kernel(qkv_ref, z_ref, bd_ref, cw_ref, nega_ref, dtb_ref, og_ref,
                        o_ref, sfin_ref, s_ref, cbuf_ref):
    c = pl.program_id(0)
    C = DN_CHUNK
    nseq = qkv_ref.shape[0]

    @pl.when(c == 0)
    def _init():
        s_ref[...] = jnp.zeros_like(s_ref)
        cbuf_ref[:, 0:8, :] = jnp.zeros((nseq, 8, B_CONV_CH), F32)

    cw = cw_ref[...]
    og = og_ref[...]
    ri64 = lax.broadcasted_iota(jnp.int32, (C, C), 0)
    ci64 = lax.broadcasted_iota(jnp.int32, (C, C), 1)
    cum_lhs = (ri64 >= ci64).astype(BF16)

    R = MIXB_GROUP * C
    ri = lax.broadcasted_iota(jnp.int32, (R, R), 0)
    ci = lax.broadcasted_iota(jnp.int32, (R, R), 1)
    same = lax.shift_right_logical(ri, C.bit_length() - 1) == lax.shift_right_logical(ci, C.bit_length() - 1)
    tril = same & (ri >= ci)
    strict = same & (ri > ci)
    rep = B_V_HEADS // B_QK_HEADS

    def lane_col(a, lane):
        return jnp.broadcast_to(a[:, lane:lane + 1], (a.shape[0], HEAD_DIM))

    groups = B_V_HEADS // MIXB_GROUP
    units = [(b, grp) for b in range(nseq) for grp in range(groups)]
    heads_of = lambda grp: list(range(grp * MIXB_GROUP, (grp + 1) * MIXB_GROUP))

    acts, gcs, betas = [], [], []
    for b in range(nseq):
        x = qkv_ref[b]
        cbuf_ref[b, 8:8 + C, :] = x
        y = cbuf_ref[b, 5:5 + C, :] * cw[0:1]
        y = y + cbuf_ref[b, 6:6 + C, :] * cw[1:2]
        y = y + cbuf_ref[b, 7:7 + C, :] * cw[2:3]
        y = y + x * cw[3:4]
        cbuf_ref[b, 0:8, :] = x[C - 8:C, :]
        acts.append(jax.nn.silu(y))
        bd = bd_ref[b]
        betas.append(jax.nn.sigmoid(bd))
        g_all = nega_ref[...] * _softplus(bd + dtb_ref[...])
        gcs.append(_mm_exact_lhs(cum_lhs, g_all))

    kst, qst, gcol, glcol, bcol, decay, a_mat, rhs = {}, {}, {}, {}, {}, {}, {}, {}
    for u in units:
        b, grp = u
        act, gc_all, beta_all = acts[b], gcs[b], betas[b]

        def stack(fn):
            return jnp.concatenate([fn(h) for h in heads_of(grp)], axis=0)

        def l2n(cols0, h):
            s = act[:, cols0 + (h // rep) * HEAD_DIM:cols0 + (h // rep + 1) * HEAD_DIM]
            return s * lax.rsqrt(jnp.sum(s * s, -1, keepdims=True) + L2_EPS)

        kst[u] = stack(lambda h: l2n(B_KEY_WIDTH, h))
        qst[u] = stack(lambda h: l2n(0, h) * (HEAD_DIM ** -0.5))
        vst = stack(lambda h: act[:, 2 * B_KEY_WIDTH + h * HEAD_DIM:2 * B_KEY_WIDTH + (h + 1) * HEAD_DIM])
        gcol[u] = stack(lambda h: lane_col(gc_all, B_V_HEADS + h))
        glcol[u] = stack(lambda h: jnp.broadcast_to(gc_all[C - 1:C, B_V_HEADS + h:B_V_HEADS + h + 1], (C, HEAD_DIM)))
        bcol[u] = stack(lambda h: lane_col(beta_all, h))
        grow = gcol[u].T[0:1, :]
        diff = jnp.concatenate([gcol[u], gcol[u]], axis=1) - grow
        decay[u] = jnp.where(tril, jnp.exp(jnp.where(tril, diff, 0.0)), 0.0)
        rhs[u] = jnp.concatenate([vst * bcol[u], kst[u] * bcol[u] * jnp.exp(gcol[u])], axis=1)
    for u in units:
        a_mat[u] = jnp.where(strict, jnp.concatenate([bcol[u], bcol[u]], axis=1) * _mm_nt(kst[u], kst[u]) * decay[u], 0.0)

    t_inv = _unit_lower_inverse_many([a_mat[u] for u in units], ri, ci, C)
    sol = [_mm(t, rhs[u]) for t, u in zip(t_inv, units)]
    qk = [_mm_nt(qst[u], kst[u]) * decay[u] for u in units]
    ws = []
    for sl, u in zip(sol, units):
        b, grp = u
        q_dec = qst[u] * jnp.exp(gcol[u])
        ws.append([_mm(jnp.concatenate([sl[i * C:(i + 1) * C, HEAD_DIM:], q_dec[i * C:(i + 1) * C]], axis=0),
                       s_ref[b * B_V_HEADS + h]) for i, h in enumerate(heads_of(grp))])
    v_new = [sl[:, :HEAD_DIM] - jnp.concatenate([w[:C] for w in wl], axis=0) for sl, wl in zip(sol, ws)]
    outs = [jnp.concatenate([w[C:] for w in wl], axis=0) + _mm(q, v) for wl, q, v in zip(ws, qk, v_new)]
    for u, v, o in zip(units, v_new, outs):
        b, grp = u
        k_dec = kst[u] * jnp.exp(glcol[u] - gcol[u])
        on = o * lax.rsqrt(jnp.mean(o * o, -1, keepdims=True) + RMS_EPS) * og
        for i, h in enumerate(heads_of(grp)):
            rs = slice(i * C, (i + 1) * C)
            si = b * B_V_HEADS + h
            s_ref[si] = s_ref[si] * jnp.exp(glcol[u][i * C:i * C + 1, :]) + _mm_tn(k_dec[rs], v[rs])
            cols = slice(h * HEAD_DIM, (h + 1) * HEAD_DIM)
            o_ref[b, :, cols] = on[rs] * jax.nn.silu(z_ref[b, :, cols])

    @pl.when(c == pl.num_programs(0) - 1)
    def _fin():
        sfin_ref[...] = s_ref[...]


def _mixb_prompt(proj, bd, conv_w, nega, dtb, og, batch, seq):
    nc = seq // DN_CHUNK
    qkv_blk = 2 * A_WIDTH // B_CONV_CH
    z_blk = (2 * A_WIDTH + B_CONV_CH) // B_VAL_WIDTH
    proj3 = proj.reshape(batch, seq, proj.shape[-1])
    bd3 = bd.reshape(batch, seq, LANES)
    o, s_fin = pl.pallas_call(
        _mixb_prompt_kernel,
        grid=(nc,),
        in_specs=[pl.BlockSpec((batch, DN_CHUNK, B_CONV_CH), lambda c: (0, c, qkv_blk)),
                  pl.BlockSpec((batch, DN_CHUNK, B_VAL_WIDTH), lambda c: (0, c, z_blk)),
                  pl.BlockSpec((batch, DN_CHUNK, LANES), lambda c: (0, c, 0)),
                  pl.BlockSpec((B_CONV, B_CONV_CH), lambda c: (0, 0)),
                  pl.BlockSpec((1, LANES), lambda c: (0, 0)),
                  pl.BlockSpec((1, LANES), lambda c: (0, 0)),
                  pl.BlockSpec((1, HEAD_DIM), lambda c: (0, 0))],
        out_specs=[pl.BlockSpec((batch, DN_CHUNK, B_VAL_WIDTH), lambda c: (0, c, 0)),
                   pl.BlockSpec((batch * B_V_HEADS, HEAD_DIM, HEAD_DIM), lambda c: (0, 0, 0))],
        out_shape=[jax.ShapeDtypeStruct((batch, seq, B_VAL_WIDTH), F32),
                   jax.ShapeDtypeStruct((batch * B_V_HEADS, HEAD_DIM, HEAD_DIM), F32)],
        scratch_shapes=[pltpu.VMEM((batch * B_V_HEADS, HEAD_DIM, HEAD_DIM), F32),
                        pltpu.VMEM((batch, DN_CHUNK + 8, B_CONV_CH), F32)],
        compiler_params=_params("arbitrary"),
        name="mixb_prompt",
    )(proj3, proj3, bd3, conv_w, nega, dtb, og)
    return (o.reshape(batch * seq, B_VAL_WIDTH),
            s_fin.reshape(batch, B_V_HEADS, HEAD_DIM, HEAD_DIM))


def _mixb_sample_pre_kernel(qkv_ref, buf_ref, bd_ref, cw_ref, nega_ref, dtb_ref,
                            act_ref, beta_ref, g_ref, *, steps, nb_rows):
    cw = cw_ref[...]

    def slab(j):
        if j < B_CONV - 1:
            return buf_ref[j * nb_rows:(j + 1) * nb_rows, :]
        jj = j - (B_CONV - 1)
        return qkv_ref[jj * nb_rows:(jj + 1) * nb_rows, :]

    for t in range(steps):
        y = slab(t) * cw[0:1]
        for i in range(1, B_CONV):
            y = y + slab(t + i) * cw[i:i + 1]
        act = jax.nn.silu(y)
        rs = slice(t * nb_rows, (t + 1) * nb_rows)
        for qh in range(B_QK_HEADS):
            cq = slice(qh * HEAD_DIM, (qh + 1) * HEAD_DIM)
            ck = slice(B_KEY_WIDTH + qh * HEAD_DIM, B_KEY_WIDTH + (qh + 1) * HEAD_DIM)
            qs = act[:, cq]
            ks = act[:, ck]
            act_ref[rs, cq] = qs * lax.rsqrt(jnp.sum(qs * qs, -1, keepdims=True) + L2_EPS) * (HEAD_DIM ** -0.5)
            act_ref[rs, ck] = ks * lax.rsqrt(jnp.sum(ks * ks, -1, keepdims=True) + L2_EPS)
        act_ref[rs, 2 * B_KEY_WIDTH:] = act[:, 2 * B_KEY_WIDTH:]
    bd = bd_ref[...]
    beta_ref[...] = jax.nn.sigmoid(bd)
    g_ref[...] = nega_ref[...] * _softplus(bd + dtb_ref[...])


def _mixb_sample_pre(proj, buf, bd, conv_w, nega, dtb, steps, nb_rows):
    m = proj.shape[0]
    qkv_blk = 2 * A_WIDTH // B_CONV_CH
    kern = functools.partial(_mixb_sample_pre_kernel, steps=steps, nb_rows=nb_rows)
    return pl.pallas_call(
        kern,
        grid=(1,),
        in_specs=[pl.BlockSpec((m, B_CONV_CH), lambda i: (0, qkv_blk)),
                  pl.BlockSpec(buf.shape, lambda i: (0, 0)),
                  pl.BlockSpec((m, LANES), lambda i: (0, 0)),
                  pl.BlockSpec((B_CONV, B_CONV_CH), lambda i: (0, 0)),
                  pl.BlockSpec((1, LANES), lambda i: (0, 0)),
                  pl.BlockSpec((1, LANES), lambda i: (0, 0))],
        out_specs=[pl.BlockSpec((m, B_CONV_CH), lambda i: (0, 0)),
                   pl.BlockSpec((m, LANES), lambda i: (0, 0)),
                   pl.BlockSpec((m, LANES), lambda i: (0, 0))],
        out_shape=[jax.ShapeDtypeStruct((m, B_CONV_CH), F32),
                   jax.ShapeDtypeStruct((m, LANES), F32),
                   jax.ShapeDtypeStruct((m, LANES), F32)],
        compiler_params=_params("arbitrary"),
        name="mixb_sample_pre",
    )(proj, buf, bd, conv_w, nega, dtb)


def _mixb_sample_rec_kernel(kq_ref, v_ref, z_ref, beta_ref, g_ref, s0_ref, og_ref,
                            o_ref, s_out_ref, *, steps, pairs):
    og = og_ref[...]
    zpad = jnp.zeros((HEAD_DIM - 8, HEAD_DIM), F32)
    zrows = jnp.zeros((8 - steps, HEAD_DIM), F32)
    heads = range(B_V_HEADS)

    def body(bi, carry):
        kqs = [_mm(kq_ref[bi, h], s0_ref[bi, h]) for h in heads]
        pending = []
        for h in heads:
            kq = kq_ref[bi, h]
            g = g_ref[bi, h]
            beta = beta_ref[bi, h]
            v = v_ref[bi, h]
            gc = [g[0:1]]
            for t in range(1, steps):
                gc.append(gc[-1] + g[t:t + 1])
            k = [kq[t:t + 1] for t in range(steps)]
            q = [kq[4 + t:5 + t] for t in range(steps)]
            d = []
            for t in range(steps):
                acc = v[t:t + 1] - jnp.exp(gc[t]) * kqs[h][t:t + 1]
                for j in range(t):
                    kk = jnp.sum(k[j] * k[t], -1, keepdims=True)
                    acc = acc - jnp.exp(gc[t] - gc[j]) * kk * d[j]
                d.append(beta[t:t + 1] * acc)
            outs = []
            for t in range(steps):
                o = jnp.exp(gc[t]) * kqs[h][4 + t:5 + t]
                for j in range(t + 1):
                    qk = jnp.sum(k[j] * q[t], -1, keepdims=True)
                    o = o + jnp.exp(gc[t] - gc[j]) * qk * d[j]
                outs.append(o * lax.rsqrt(jnp.mean(o * o, -1, keepdims=True) + RMS_EPS) * og)
            o_ref[bi, h] = jnp.concatenate(outs + [zrows], axis=0) * jax.nn.silu(z_ref[bi, h])
            k_dec = jnp.concatenate([jnp.exp(gc[-1] - gc[j]) * k[j] for j in range(steps)] + [zrows], axis=0)
            k_pad = jnp.concatenate([k_dec, zpad], axis=0)
            d_pad = jnp.concatenate(d + [zrows, zpad], axis=0)
            pending.append((k_pad.T, d_pad, jnp.exp(gc[-1])))
        for h, (k_t, d_pad, decay_last) in zip(heads, pending):
            s_out_ref[bi, h] = s0_ref[bi, h] * decay_last + _mm(k_t, d_pad)
        return carry

    lax.fori_loop(0, pairs // B_V_HEADS, body, 0)


def _mixb_sample_rec(kq, v, z, beta, g, s0, og, steps, bb):
    nb = kq.shape[0]
    tile = pl.BlockSpec((bb, B_V_HEADS, 8, HEAD_DIM), lambda i: (i, 0, 0, 0))
    st = pl.BlockSpec((bb, B_V_HEADS, HEAD_DIM, HEAD_DIM), lambda i: (i, 0, 0, 0))
    kern = functools.partial(_mixb_sample_rec_kernel, steps=steps, pairs=bb * B_V_HEADS)
    return pl.pallas_call(
        kern,
        grid=(nb // bb,),
        in_specs=[tile, tile, tile, tile, tile, st, pl.BlockSpec((1, HEAD_DIM), lambda i: (0, 0))],
        out_specs=[tile, st],
        out_shape=[jax.ShapeDtypeStruct((nb, B_V_HEADS, 8, HEAD_DIM), F32),
                   jax.ShapeDtypeStruct((nb, B_V_HEADS, HEAD_DIM, HEAD_DIM), F32)],
        compiler_params=_params("arbitrary"),
        name="mixb_sample_rec",
    )(kq, v, z, beta, g, s0, og)


def _outproj_kernel(ap_ref, bp_ref, xp_ref, g1p_ref, sc2p_ref, sh2p_ref,
                    as_ref, bs_ref, xs_ref, g1s_ref, sc2s_ref, sh2s_ref,
                    w_ref, lg_ref, lb_ref, wr_ref, rb_ref, x1_ref, h2_ref, route_ref, count_ref, base_ref,
                    *, prompt_tiles):
    def tile(a_ref, b_ref, x_ref, g1_ref, sc2_ref, sh2_ref):
        mix = (jnp.dot(a_ref[...].astype(BF16), w_ref[:A_WIDTH, :], preferred_element_type=F32)
               + jnp.dot(b_ref[...].astype(BF16), w_ref[A_WIDTH:, :], preferred_element_type=F32))
        x1 = _layer_norm_rows(ALPHA * x_ref[...] + g1_ref[...] * mix, lg_ref[...], lb_ref[...])
        x1_ref[...] = x1
        h2 = x1 * (1.0 + sc2_ref[...]) + sh2_ref[...]
        _store_token_rows(h2_ref, h2)
        route_ref[...] = _route_tile(_mm3(h2, wr_ref[...]), rb_ref[...], base_ref)
        count_ref[...] = base_ref[...]

    @pl.when(pl.program_id(0) == 0)
    def _():
        base_ref[...] = jnp.zeros_like(base_ref)

    is_prompt = pl.program_id(0) < prompt_tiles

    @pl.when(is_prompt)
    def _():
        tile(ap_ref, bp_ref, xp_ref, g1p_ref, sc2p_ref, sh2p_ref)

    @pl.when(jnp.logical_not(is_prompt))
    def _():
        tile(as_ref, bs_ref, xs_ref, g1s_ref, sc2s_ref, sh2s_ref)


def _outproj(prompt, sample, mod_p, mod_s, w_out, lg, lb, w_router, r_bias, tm, seq):
    n_p = prompt[2].shape[0]
    n_s = sample[2].shape[0]
    pt = n_p // tm
    m = n_p + n_s
    tiles_per_seq = seq // tm
    p_row = lambda w: pl.BlockSpec((tm, w), lambda i: (jnp.minimum(i, pt - 1), 0))
    s_row = lambda w: pl.BlockSpec((tm, w), lambda i: (jnp.maximum(i - pt, 0), 0))
    p_mod = lambda col: pl.BlockSpec((None, 1, D_MODEL), lambda i: (jnp.minimum(i, pt - 1) // tiles_per_seq, 0, col))
    s_mod = lambda col: pl.BlockSpec((tm, D_MODEL), lambda i: (0, col))
    row = lambda w: pl.BlockSpec((tm, w), lambda i: (i, 0))
    full = lambda s: pl.BlockSpec(s, lambda i: (0, 0))
    return pl.pallas_call(
        functools.partial(_outproj_kernel, prompt_tiles=pt),
        grid=(m // tm,),
        in_specs=[p_row(A_WIDTH), p_row(B_VAL_WIDTH), p_row(D_MODEL), p_mod(2), p_mod(4), p_mod(3),
                  s_row(A_WIDTH), s_row(B_VAL_WIDTH), s_row(D_MODEL), s_mod(2), s_mod(4), s_mod(3),
                  full((D_MODEL, D_MODEL)), full((1, D_MODEL)), full((1, D_MODEL)),
                  full((D_MODEL, LANES)), full((1, LANES))],
        out_specs=[row(D_MODEL), pl.BlockSpec((tm * ROW_TILES, LANES), lambda i: (i, 0)), row(LANES),
                   full((1, LANES))],
        out_shape=[jax.ShapeDtypeStruct((m, D_MODEL), F32),
                   jax.ShapeDtypeStruct((m * ROW_TILES, LANES), F32),
                   jax.ShapeDtypeStruct((m, LANES), F32),
                   jax.ShapeDtypeStruct((1, LANES), F32)],
        scratch_shapes=[pltpu.VMEM((1, LANES), F32)],
        compiler_params=_params("arbitrary"),
        name="outproj",
    )(*prompt, mod_p, mod_p, mod_p, *sample, mod_s, mod_s, mod_s, w_out, lg, lb, w_router, r_bias)


def _route_tile(lg, bias, base_ref):
    tm = lg.shape[0]
    lane = lax.broadcasted_iota(jnp.int32, lg.shape, 1)
    neg = -jnp.inf

    def first_argmax(score):
        mx = jnp.max(score, -1, keepdims=True)
        return jnp.min(jnp.where(score == mx, lane, LANES), -1, keepdims=True)

    def pick(vals, idx):
        return jnp.sum(jnp.where(lane == idx, vals, 0.0), -1, keepdims=True)

    gmask = lane < N_GROUPS
    mg = jnp.max(jnp.where(gmask, lg, neg), -1, keepdims=True)
    eg = jnp.where(gmask, jnp.exp(jnp.where(gmask, lg - mg, 0.0)), 0.0)
    pg = eg / jnp.sum(eg, -1, keepdims=True)
    sel_g = first_argmax(jnp.where(gmask, lg + bias, neg))
    p_sel = pick(pg, sel_g)

    lo = N_GROUPS + sel_g * EXPERTS_PER_GROUP
    emask = (lane >= lo) & (lane < lo + EXPERTS_PER_GROUP)
    me = jnp.max(jnp.where(emask, lg, neg), -1, keepdims=True)
    ee = jnp.where(emask, jnp.exp(jnp.where(emask, lg - me, 0.0)), 0.0)
    pe = ee / jnp.sum(ee, -1, keepdims=True)
    score = jnp.where(emask, pe + bias, neg)
    i1 = first_argmax(score)
    i2 = first_argmax(jnp.where(lane == i1, neg, score))
    w1 = pick(pe, i1)
    w2 = pick(pe, i2)
    wsum = w1 + w2
    gate1 = w1 / wsum * p_sel
    gate2 = w2 / wsum * p_sel

    hot = ((lane == i1) | (lane == i2)).astype(BF16)
    ri = lax.broadcasted_iota(jnp.int32, (tm, tm), 0)
    ci = lax.broadcasted_iota(jnp.int32, (tm, tm), 1)
    before = jnp.dot((ri > ci).astype(BF16), hot, preferred_element_type=F32) + base_ref[...]
    rank1 = pick(before, i1)
    rank2 = pick(before, i2)
    base_ref[...] = base_ref[...] + jnp.sum(hot.astype(F32), 0, keepdims=True)

    out = jnp.where(lane == 0, (i1 - N_GROUPS).astype(F32), 0.0)
    out = jnp.where(lane == 1, (i2 - N_GROUPS).astype(F32), out)
    out = jnp.where(lane == 2, gate1, out)
    out = jnp.where(lane == 3, gate2, out)
    out = jnp.where(lane == 4, rank1, out)
    out = jnp.where(lane == 5, rank2, out)
    return out


def _expert_kernel(be_ref, nv_ref, src_ref, dst_ref, first_ref, next_ref, wslot_ref,
                   h_ref, wg_ref, wu_ref, wd_ref, y_ref,
                   xbuf, ybuf, wg_st, wu_st, wd_st, wg_s, wu_s, wd_s, gsem, ssem, wsem):
    i = pl.program_id(0)
    nv = nv_ref[0]
    slot = i % 2
    other = 1 - slot

    def gather_row(blk, r, s):
        tok = src_ref[blk * MOE_ROWS + r]
        return pltpu.make_async_copy(h_ref.at[pl.ds(pl.multiple_of(tok * ROW_TILES, ROW_TILES), ROW_TILES)],
                                     xbuf.at[s, pl.ds(r * ROW_TILES, ROW_TILES)], gsem.at[s])

    def scatter_row(blk, r, s):
        row = dst_ref[(blk + 1) * MOE_ROWS + r]
        return pltpu.make_async_copy(ybuf.at[s, pl.ds(r * ROW_TILES, ROW_TILES)],
                                     y_ref.at[pl.ds(pl.multiple_of(row * ROW_TILES, ROW_TILES), ROW_TILES)], ssem.at[s])

    def gather_wait(s):
        pltpu.make_async_copy(h_ref.at[pl.ds(0, MOE_ROWS * ROW_TILES)], xbuf.at[s], gsem.at[s]).wait()

    def scatter_wait(s):
        pltpu.make_async_copy(ybuf.at[s], y_ref.at[pl.ds(0, MOE_ROWS * ROW_TILES)], ssem.at[s]).wait()

    @pl.when(i == 0)
    def _prologue():
        ybuf[...] = jnp.zeros_like(ybuf)
        base = y_ref.shape[0] - 2 * MOE_ROWS * ROW_TILES
        pltpu.make_async_copy(ybuf.at[0], y_ref.at[pl.ds(base, MOE_ROWS * ROW_TILES)], ssem.at[0]).start()

        def body(r, carry):
            gather_row(0, r, 0).start()
            return carry

        lax.fori_loop(0, MOE_ROWS, body, 0, unroll=8)

    def weight_copies(e, p):
        return [pltpu.make_async_copy(src.at[e], dst.at[p], wsem.at[p])
                for src, dst in ((wg_ref, wg_st), (wu_ref, wu_st), (wd_ref, wd_st))]

    @pl.when(i == 0)
    def _():
        for c in weight_copies(be_ref[0], 0):
            c.start()

    @pl.when((i < nv) & (first_ref[i] == 1))
    def _():
        p = wslot_ref[i]
        for c in weight_copies(be_ref[i], p):
            c.wait()
        wg_s[...] = wg_st[p].astype(BF16)
        wu_s[...] = wu_st[p].astype(BF16)
        wd_s[...] = wd_st[p].astype(BF16)

        @pl.when(next_ref[i] >= 0)
        def _():
            for c in weight_copies(next_ref[i], 1 - p):
                c.start()

    @pl.when(i < nv)
    def _block():
        gather_wait(slot)
        for r in range(MOE_ROWS):
            gather_row(i + 1, r, other).start()
            scatter_row(i - 1, r, other).start()
        x = _load_token_rows(xbuf.at[slot], MOE_ROWS).astype(BF16)
        hg = jnp.dot(x, wg_s[...], preferred_element_type=F32)
        hu = jnp.dot(x, wu_s[...], preferred_element_type=F32)
        hid = (jax.nn.silu(hg) * hu).astype(BF16)
        y = jnp.dot(hid, wd_s[...], preferred_element_type=F32)
        scatter_wait(slot)
        _store_token_rows(ybuf.at[slot], y)

    @pl.when(i == nv - 1)
    def _tail():
        def body(r, carry):
            scatter_row(i, r, slot).start()
            return carry

        lax.fori_loop(0, MOE_ROWS, body, 0, unroll=8)
        scatter_wait(slot)
        scatter_wait(other)
        gather_wait(other)


def _experts(block_e, n_valid, slot_src, slot_dst, h2, w_gate, w_up, w_down, n_out_rows):
    n_blocks = block_e.shape[0]
    idx = jnp.arange(n_blocks, dtype=jnp.int32)
    first = (idx < n_valid[0]) & ((idx == 0) | (block_e != jnp.roll(block_e, 1)))
    wslot = (jnp.cumsum(first.astype(jnp.int32)) - 1) % 2
    first_at = jnp.where(first, idx, n_blocks)
    next_first = jnp.concatenate([lax.cummin(first_at, reverse=True)[1:], jnp.full((1,), n_blocks, jnp.int32)])
    next_e = jnp.where(next_first < n_blocks, block_e[jnp.minimum(next_first, n_blocks - 1)], -1)
    any_spec = pl.BlockSpec(memory_space=pl.ANY)
    return pl.pallas_call(
        _expert_kernel,
        grid_spec=pltpu.PrefetchScalarGridSpec(
            num_scalar_prefetch=7,
            grid=(n_blocks,),
            in_specs=[any_spec, any_spec, any_spec, any_spec],
            out_specs=any_spec,
            scratch_shapes=[pltpu.VMEM((2, MOE_ROWS * ROW_TILES, LANES), F32),
                            pltpu.VMEM((2, MOE_ROWS * ROW_TILES, LANES), F32),
                            pltpu.VMEM((2, D_MODEL, D_EXPERT), F32),
                            pltpu.VMEM((2, D_MODEL, D_EXPERT), F32),
                            pltpu.VMEM((2, D_EXPERT, D_MODEL), F32),
                            pltpu.VMEM((D_MODEL, D_EXPERT), BF16),
                            pltpu.VMEM((D_MODEL, D_EXPERT), BF16),
                            pltpu.VMEM((D_EXPERT, D_MODEL), BF16),
                            pltpu.SemaphoreType.DMA((2,)),
                            pltpu.SemaphoreType.DMA((2,)),
                            pltpu.SemaphoreType.DMA((2,))]),
        out_shape=jax.ShapeDtypeStruct((n_out_rows * ROW_TILES, LANES), F32),
        compiler_params=_params("arbitrary"),
        name="experts",
    )(block_e, n_valid, slot_src, slot_dst, first.astype(jnp.int32), next_e.astype(jnp.int32),
      wslot.astype(jnp.int32), h2, w_gate, w_up, w_down)


def _final_kernel(x1_ref, y0_ref, y1_ref, route_ref, g2_ref, lg_ref, lb_ref, o_ref):
    route = route_ref[...]
    tm = route.shape[0]
    ff = _load_token_rows(y0_ref, tm) * route[:, 2:3] + _load_token_rows(y1_ref, tm) * route[:, 3:4]
    o_ref[...] = _layer_norm_rows(ALPHA * x1_ref[...] + g2_ref[...] * ff, lg_ref[...], lb_ref[...])


def _final(x1, y01, route, mod, lg, lb, tm, rows_per_mod, row0, n_rows, n_tok):
    off = row0 // tm
    off1 = (n_tok + row0) // tm
    return pl.pallas_call(
        _final_kernel,
        grid=(n_rows // tm,),
        in_specs=[pl.BlockSpec((tm, D_MODEL), lambda i: (off + i, 0)),
                  pl.BlockSpec((tm * ROW_TILES, LANES), lambda i: (off + i, 0)),
                  pl.BlockSpec((tm * ROW_TILES, LANES), lambda i: (off1 + i, 0)),
                  pl.BlockSpec((tm, LANES), lambda i: (off + i, 0)),
                  _mod_spec(mod, 5, tm, rows_per_mod),
                  pl.BlockSpec((1, D_MODEL), lambda i: (0, 0)),
                  pl.BlockSpec((1, D_MODEL), lambda i: (0, 0))],
        out_specs=pl.BlockSpec((tm, D_MODEL), lambda i: (i, 0)),
        out_shape=jax.ShapeDtypeStruct((n_rows, D_MODEL), F32),
        compiler_params=_params("arbitrary"),
        name="final",
    )(x1, y01, y01, route, mod, lg, lb)


def _lane_row(vec, offset):
    return jnp.zeros((1, LANES), F32).at[0, offset:offset + vec.shape[0]].set(vec.astype(F32))


def kernel(x_prompt, x_sample, state_conv, state_ssm, c_prompt, c_sample, w_ada, b_ada, w_in, a_ws, a_bs, a_norm_g, a_norm_b, b_conv_w, b_a_log, b_dt_bias, b_onorm_g, w_out, ln1_g, ln1_b, w_router_g, b_router_g, w_router_e, b_router_e, w_gate, w_up, w_down, ln2_g, ln2_b):
    batch, seq, d = x_prompt.shape
    nb, steps, _ = x_sample.shape
    n_p = batch * seq
    n_s = nb * steps
    n_tok = n_p + n_s
    l = 0

    c_rows = batch + nb
    c_pad = (-c_rows) % 8
    c_all = jnp.concatenate([c_prompt, c_sample, jnp.zeros((c_pad, d), F32)], axis=0)
    mod = _ada(c_all, w_ada[l], b_ada[l].reshape(1, -1))
    mod_p = mod[:batch].reshape(batch, 1, 6 * d)
    mod_s = mod[batch:batch + nb]

    w_in_t = jnp.swapaxes(w_in[l], 0, 1)
    w_bd = jnp.pad(w_in_t[MAIN_WIDTH:], ((0, LANES - 2 * B_V_HEADS), (0, 0))).astype(BF16)
    xp = x_prompt.reshape(n_p, d)
    xs = jnp.swapaxes(x_sample, 0, 1).reshape(n_s, d)
    proj_p, bd_p, proj_s, bd_s = _inproj(xp, xs, mod_p, mod_s, w_in_t, w_bd, 512, seq)

    ng = a_norm_g[l].reshape(1, -1)
    nbias = a_norm_b[l].reshape(1, -1)
    bias_tile = jnp.repeat(a_bs[l].T, HEAD_DIM, axis=1)
    a_out_p = _mixa_prompt(proj_p, a_ws[l], bias_tile, ng, nbias, 256)
    coef = jnp.repeat(jnp.transpose(a_ws[l][:, :steps, :steps], (1, 2, 0)).reshape(steps * steps, A_HEADS),
                      HEAD_DIM, axis=1)
    a_out_s, chunkv = _mixa_sample(proj_s, coef, bias_tile[:steps], ng, nbias, steps, nb)

    nega = _lane_row(-jnp.exp(b_a_log[l].astype(F32)), B_V_HEADS)
    dtb = _lane_row(b_dt_bias[l], B_V_HEADS)
    og = b_onorm_g[l].reshape(1, -1)
    b_out_p, ssm_p = _mixb_prompt(proj_p, bd_p, b_conv_w[l], nega, dtb, og, batch, seq)
    qkv0 = 2 * A_WIDTH
    conv_p = proj_p.reshape(batch, seq, MAIN_WIDTH)[:, seq - (B_CONV - 1):, qkv0:qkv0 + B_CONV_CH]

    buf_s = jnp.swapaxes(state_conv[l], 0, 1).reshape((B_CONV - 1) * nb, B_CONV_CH)
    act_s, beta_s, g_s = _mixb_sample_pre(proj_s, buf_s, bd_s, b_conv_w[l], nega, dtb, steps, nb)
    rep = B_V_HEADS // B_QK_HEADS
    act4 = act_s.reshape(steps, nb, B_CONV_CH)
    q8 = jnp.repeat(act4[..., :B_KEY_WIDTH].reshape(steps, nb, B_QK_HEADS, HEAD_DIM), rep, axis=2)
    k8 = jnp.repeat(act4[..., B_KEY_WIDTH:2 * B_KEY_WIDTH].reshape(steps, nb, B_QK_HEADS, HEAD_DIM), rep, axis=2)

    def to_tiles(a):
        a = jnp.transpose(a, (1, 2, 0, 3))
        return jnp.pad(a, ((0, 0), (0, 0), (0, 8 - a.shape[2]), (0, 0)))

    kq_t = jnp.concatenate([to_tiles(k8)[:, :, :4], to_tiles(q8)[:, :, :4]], axis=2)
    v_t = to_tiles(act4[..., 2 * B_KEY_WIDTH:].reshape(steps, nb, B_V_HEADS, HEAD_DIM))
    z_t = to_tiles(proj_s[:, qkv0 + B_CONV_CH:].reshape(steps, nb, B_V_HEADS, HEAD_DIM))

    def gate_tiles(a, off):
        a = a[:, off:off + B_V_HEADS].reshape(steps, nb, B_V_HEADS, 1)
        return to_tiles(jnp.broadcast_to(a, (steps, nb, B_V_HEADS, HEAD_DIM)))

    o_t, ssm_s = _mixb_sample_rec(kq_t, v_t, z_t, gate_tiles(beta_s, 0), gate_tiles(g_s, B_V_HEADS),
                                  state_ssm[l], og, steps, 8)
    b_out_s = jnp.transpose(o_t[:, :, :steps], (2, 0, 1, 3)).reshape(n_s, B_VAL_WIDTH)
    conv_s = jnp.swapaxes(proj_s.reshape(steps, nb, MAIN_WIDTH)[steps - (B_CONV - 1):, :, qkv0:qkv0 + B_CONV_CH], 0, 1)

    w_out_b = w_out[l].astype(BF16)
    lg1 = ln1_g[l].reshape(1, -1)
    lb1 = ln1_b[l].reshape(1, -1)
    w_router = jnp.concatenate([w_router_g[l], w_router_e[l],
                                jnp.zeros((d, LANES - N_GROUPS - N_EXPERTS), F32)], axis=1)
    out_tm = 256
    mod_s_tile = jnp.tile(mod_s, (out_tm // nb, 1))
    r_bias = jnp.concatenate([b_router_g[l], b_router_e[l],
                              jnp.zeros((LANES - N_GROUPS - N_EXPERTS,), F32)]).reshape(1, LANES)
    x1, h2, route, counts_row = _outproj((a_out_p, b_out_p, xp), (a_out_s, b_out_s, xs), mod_p, mod_s_tile,
                                         w_out_b, lg1, lb1, w_router, r_bias, out_tm, seq)

    counts = counts_row[0, N_GROUPS:N_GROUPS + N_EXPERTS].astype(jnp.int32)
    route_t = route[:, :8].T
    eid = route_t[0:2].astype(jnp.int32)
    rank = route_t[4:6].astype(jnp.int32)
    padded = (counts + MOE_ROWS - 1) // MOE_ROWS * MOE_ROWS
    pend = jnp.cumsum(padded)
    pstart = pend - padded
    dest = rank + jnp.sum(jnp.where(eid[None] == jnp.arange(N_EXPERTS, dtype=jnp.int32)[:, None, None],
                                    pstart[:, None, None], 0), axis=0)
    n_blocks = -(-(2 * n_tok) // MOE_ROWS) + N_EXPERTS
    n_slots = n_blocks * MOE_ROWS
    slot_row = jnp.full((n_slots,), -1, jnp.int32).at[dest.reshape(-1)].set(jnp.arange(2 * n_tok, dtype=jnp.int32))
    slot_id = jnp.arange(n_slots, dtype=jnp.int32)
    trash = 2 * n_tok + (slot_id // MOE_ROWS % 2) * MOE_ROWS + slot_id % MOE_ROWS
    row_id = jnp.arange(MOE_ROWS, dtype=jnp.int32)
    slot_src = jnp.concatenate([jnp.where(slot_row < 0, 0, slot_row % n_tok),
                                jnp.zeros((MOE_ROWS,), jnp.int32)])
    slot_dst = jnp.concatenate([2 * n_tok + MOE_ROWS + row_id,
                                jnp.where(slot_row < 0, trash, slot_row)])
    blk0 = jnp.arange(n_blocks, dtype=jnp.int32) * MOE_ROWS
    block_e = jnp.minimum(jnp.sum(pend[None, :] <= blk0[:, None], axis=1), N_EXPERTS - 1).astype(jnp.int32)
    n_valid = (pend[-1:] // MOE_ROWS).astype(jnp.int32)

    y01 = _experts(block_e, n_valid, slot_src, slot_dst, h2, w_gate[l], w_up[l], w_down[l],
                   2 * n_tok + 2 * MOE_ROWS)

    lg2 = ln2_g[l].reshape(1, -1)
    lb2 = ln2_b[l].reshape(1, -1)
    y_p = _final(x1, y01, route, mod_p, lg2, lb2, 256, seq, 0, n_p, n_tok)
    y_s = _final(x1, y01, route, mod_s, lg2, lb2, nb, 0, n_p, n_s, n_tok)

    y_prompt = y_p.reshape(batch, seq, d)
    y_sample = jnp.swapaxes(y_s.reshape(steps, nb, d), 0, 1)
    chunkv_s = jnp.swapaxes(chunkv.reshape(steps, nb, A_HEADS, HEAD_DIM), 0, 1)
    return (y_prompt, y_sample, conv_p[None], ssm_p[None], conv_s[None], ssm_s[None], chunkv_s[None])
```

```python
import functools

import jax
import jax.numpy as jnp
from jax import lax
from jax.experimental import pallas as pl
from jax.experimental.pallas import tpu as pltpu

F32 = jnp.float32
BF16 = jnp.bfloat16

D_MODEL = 2048
DEPTH = 1
A_HEADS = 8
HEAD_DIM = 128
A_WIDTH = 1024
A_CHUNK = 128
B_QK_HEADS = 4
B_V_HEADS = 8
B_KEY_WIDTH = 512
B_VAL_WIDTH = 1024
B_CONV = 4
B_CONV_CH = 2048
DN_CHUNK = 64
MAIN_WIDTH = 2 * A_WIDTH + B_CONV_CH + B_VAL_WIDTH
N_GROUPS = 4
EXPERTS_PER_GROUP = 8
N_EXPERTS = 32
D_EXPERT = 512
ALPHA = (2 * DEPTH) ** 0.25
LN_EPS = 1e-5
RMS_EPS = 1e-6
L2_EPS = 1e-6

LANES = 128
VMEM_LIMIT = 56 * 1024 * 1024
MOE_ROWS = 256
MIXB_GROUP = 4


def _params(*sem):
    return pltpu.CompilerParams(dimension_semantics=sem, vmem_limit_bytes=VMEM_LIMIT)


def _mm(a, b):
    return jnp.dot(a.astype(BF16), b.astype(BF16), preferred_element_type=F32)


def _mm_nt(a, b):
    return lax.dot_general(a.astype(BF16), b.astype(BF16), (((1,), (1,)), ((), ())),
                           preferred_element_type=F32)


def _mm_tn(a, b):
    return lax.dot_general(a.astype(BF16), b.astype(BF16), (((0,), (0,)), ((), ())),
                           preferred_element_type=F32)


def _split(x):
    hi = x.astype(BF16)
    lo = (x - hi.astype(F32)).astype(BF16)
    return hi, lo


def _mm_exact_lhs(a_bf16, b):
    hi, lo = _split(b)
    return (jnp.dot(a_bf16, hi, preferred_element_type=F32)
            + jnp.dot(a_bf16, lo, preferred_element_type=F32))


def _mm3(a, b):
    ah, al = _split(a)
    bh, bl = _split(b)
    return (jnp.dot(ah, bh, preferred_element_type=F32) + jnp.dot(ah, bl, preferred_element_type=F32)
            + jnp.dot(al, bh, preferred_element_type=F32))


def _softplus(x):
    return jnp.maximum(x, 0.0) + jnp.log1p(jnp.exp(-jnp.abs(x)))


def _layer_norm_rows(x, g, b):
    mu = jnp.mean(x, -1, keepdims=True)
    xc = x - mu
    var = jnp.mean(xc * xc, -1, keepdims=True)
    return xc * lax.rsqrt(var + LN_EPS) * g + b


def _ada_kernel(c_ref, w_ref, b_ref, o_ref):
    a = jax.nn.silu(c_ref[...]).astype(BF16)
    o_ref[...] = jnp.dot(a, w_ref[...].astype(BF16), preferred_element_type=F32) + b_ref[...]


def _ada(c_all, w_ada, b_ada):
    rows = c_all.shape[0]
    tn = 1024
    return pl.pallas_call(
        _ada_kernel,
        grid=(6 * D_MODEL // tn,),
        in_specs=[pl.BlockSpec((rows, D_MODEL), lambda j: (0, 0)),
                  pl.BlockSpec((D_MODEL, tn), lambda j: (0, j)),
                  pl.BlockSpec((1, tn), lambda j: (0, j))],
        out_specs=pl.BlockSpec((rows, tn), lambda j: (0, j)),
        out_shape=jax.ShapeDtypeStruct((rows, 6 * D_MODEL), F32),
        compiler_params=_params("arbitrary"),
        name="ada",
    )(c_all, w_ada, b_ada)


def _inproj_kernel(xp_ref, scp_ref, shp_ref, xs_ref, scs_ref, shs_ref, w_ref, wbd_ref,
                   op_ref, bdp_ref, os_ref, bds_ref, wb_ref, *, prompt_tiles):
    j = pl.program_id(0)
    i = pl.program_id(1)

    @pl.when(i == 0)
    def _():
        wb_ref[...] = w_ref[...].astype(BF16)

    def project(h, o_ref, bd_ref):
        o_ref[...] = _mm_nt(h, wb_ref[...])

        @pl.when(j == 0)
        def _():
            bd_ref[0] = _mm_nt(h, wbd_ref[...])

        @pl.when(j != 0)
        def _():
            bd_ref[0] = jnp.zeros(bd_ref.shape[1:], F32)

    @pl.when(i < prompt_tiles)
    def _():
        project((xp_ref[...] * (1.0 + scp_ref[...]) + shp_ref[...]).astype(BF16), op_ref, bdp_ref)

    @pl.when(i >= prompt_tiles)
    def _():
        n_s, nb = xs_ref.shape[0], scs_ref.shape[0]
        x = xs_ref[...].reshape(n_s // nb, nb, D_MODEL)
        h = x * (1.0 + scs_ref[...])[None] + shs_ref[...][None]
        project(h.reshape(n_s, D_MODEL).astype(BF16), os_ref, bds_ref)


def _mod_spec(mod, col, tm, rows_per_mod, axis=0):
    if rows_per_mod:
        tiles = rows_per_mod // tm
        return pl.BlockSpec((None, 1, D_MODEL), lambda *g: (g[axis] // tiles, 0, col))
    return pl.BlockSpec((tm, D_MODEL), lambda *g: (0, col))


def _inproj(xp, xs, mod_p, mod_s, w_in_t, w_bd, tm, seq):
    n_p, n_s = xp.shape[0], xs.shape[0]
    nb = mod_s.shape[0]
    tn = 1024
    nj = MAIN_WIDTH // tn
    pt = n_p // tm
    tiles_per_seq = seq // tm
    pi = lambda i: jnp.minimum(i, pt - 1)
    p_mod = lambda col: pl.BlockSpec((None, 1, D_MODEL), lambda j, i: (pi(i) // tiles_per_seq, 0, col))
    s_mod = lambda col: pl.BlockSpec((nb, D_MODEL), lambda j, i: (0, col))
    proj_p, bd_p, proj_s, bd_s = pl.pallas_call(
        functools.partial(_inproj_kernel, prompt_tiles=pt),
        grid=(nj, pt + 1),
        in_specs=[pl.BlockSpec((tm, D_MODEL), lambda j, i: (pi(i), 0)), p_mod(1), p_mod(0),
                  pl.BlockSpec((n_s, D_MODEL), lambda j, i: (0, 0)), s_mod(1), s_mod(0),
                  pl.BlockSpec((tn, D_MODEL), lambda j, i: (j, 0)),
                  pl.BlockSpec((LANES, D_MODEL), lambda j, i: (0, 0))],
        out_specs=[pl.BlockSpec((tm, tn), lambda j, i: (pi(i), j)),
                   pl.BlockSpec((1, tm, LANES), lambda j, i: (j, pi(i), 0)),
                   pl.BlockSpec((n_s, tn), lambda j, i: (0, j)),
                   pl.BlockSpec((1, n_s, LANES), lambda j, i: (j, 0, 0))],
        out_shape=[jax.ShapeDtypeStruct((n_p, MAIN_WIDTH), F32),
                   jax.ShapeDtypeStruct((nj, n_p, LANES), F32),
                   jax.ShapeDtypeStruct((n_s, MAIN_WIDTH), F32),
                   jax.ShapeDtypeStruct((nj, n_s, LANES), F32)],
        scratch_shapes=[pltpu.VMEM((tn, D_MODEL), BF16)],
        compiler_params=_params("arbitrary", "arbitrary"),
        name="inproj",
    )(xp, mod_p, mod_p, xs, mod_s, mod_s, w_in_t, w_bd)
    return proj_p, bd_p[0], proj_s, bd_s[0]


def _mixa_prompt_kernel(p_ref, ws_ref, bias_ref, ng_ref, nb_ref, o_ref):
    rows = p_ref.shape[0]
    u = jax.nn.gelu(p_ref[:, :A_WIDTH])
    v = _layer_norm_rows(jax.nn.gelu(p_ref[:, A_WIDTH:]), ng_ref[...], nb_ref[...])
    ri = lax.broadcasted_iota(jnp.int32, (A_CHUNK, A_CHUNK), 0)
    ci = lax.broadcasted_iota(jnp.int32, (A_CHUNK, A_CHUNK), 1)
    for h in range(A_HEADS):
        w = jnp.where(ri >= ci, ws_ref[h], 0.0).astype(BF16)
        cols = slice(h * HEAD_DIM, (h + 1) * HEAD_DIM)
        for c in range(rows // A_CHUNK):
            rs = slice(c * A_CHUNK, (c + 1) * A_CHUNK)
            s = jnp.dot(w, v[rs, cols].astype(BF16), preferred_element_type=F32) + bias_ref[:, cols]
            o_ref[rs, cols] = u[rs, cols] * s


def _mixa_prompt(proj, a_ws, bias_tile, ng, nb, tm):
    m = proj.shape[0]
    return pl.pallas_call(
        _mixa_prompt_kernel,
        grid=(m // tm,),
        in_specs=[pl.BlockSpec((tm, 2 * A_WIDTH), lambda i: (i, 0)),
                  pl.BlockSpec((A_HEADS, A_CHUNK, A_CHUNK), lambda i: (0, 0, 0)),
                  pl.BlockSpec((A_CHUNK, A_WIDTH), lambda i: (0, 0)),
                  pl.BlockSpec((1, A_WIDTH), lambda i: (0, 0)),
                  pl.BlockSpec((1, A_WIDTH), lambda i: (0, 0))],
        out_specs=pl.BlockSpec((tm, A_WIDTH), lambda i: (i, 0)),
        out_shape=jax.ShapeDtypeStruct((m, A_WIDTH), F32),
        compiler_params=_params("arbitrary"),
        name="mixa_prompt",
    )(proj, a_ws, bias_tile, ng, nb)


def _mixa_sample_kernel(p_ref, coef_ref, bias_ref, ng_ref, nb_ref, o_ref, v_ref, *, steps, nb_rows):
    u = jax.nn.gelu(p_ref[:, :A_WIDTH])
    v = _layer_norm_rows(jax.nn.gelu(p_ref[:, A_WIDTH:]), ng_ref[...], nb_ref[...])
    v_ref[...] = v
    for t in range(steps):
        s = bias_ref[t:t + 1, :]
        for j in range(t + 1):
            s = s + coef_ref[t * steps + j:t * steps + j + 1, :] * v[j * nb_rows:(j + 1) * nb_rows, :]
        rs = slice(t * nb_rows, (t + 1) * nb_rows)
        o_ref[rs, :] = u[rs, :] * s


def _mixa_sample(proj, coef, bias, ng, nb, steps, nb_rows):
    m = proj.shape[0]
    kern = functools.partial(_mixa_sample_kernel, steps=steps, nb_rows=nb_rows)
    return pl.pallas_call(
        kern,
        grid=(1,),
        in_specs=[pl.BlockSpec((m, 2 * A_WIDTH), lambda i: (0, 0)),
                  pl.BlockSpec(coef.shape, lambda i: (0, 0)),
                  pl.BlockSpec(bias.shape, lambda i: (0, 0)),
                  pl.BlockSpec((1, A_WIDTH), lambda i: (0, 0)),
                  pl.BlockSpec((1, A_WIDTH), lambda i: (0, 0))],
        out_specs=[pl.BlockSpec((m, A_WIDTH), lambda i: (0, 0)),
                   pl.BlockSpec((m, A_WIDTH), lambda i: (0, 0))],
        out_shape=[jax.ShapeDtypeStruct((m, A_WIDTH), F32),
                   jax.ShapeDtypeStruct((m, A_WIDTH), F32)],
        compiler_params=_params("arbitrary"),
        name="mixa_sample",
    )(proj, coef, bias, ng, nb)


def _unit_lower_inverse_many(a_list, ri, ci, block):
    eye = (ri == ci).astype(F32)
    pair = (lax.shift_right_logical(ri, 1) == lax.shift_right_logical(ci, 1)) & ((ri & 1) == 1) & ((ci & 1) == 0)
    ts = [eye - jnp.where(pair, a, 0.0) for a in a_list]
    n = 2
    while n < block:
        sh = n.bit_length()
        m = ((lax.shift_right_logical(ri, sh) == lax.shift_right_logical(ci, sh))
             & ((ri & n) != 0) & ((ci & n) == 0))
        xs = [_mm(t, jnp.where(m, a, 0.0)) for t, a in zip(ts, a_list)]
        ts = [t - _mm(x, t) for t, x in zip(ts, xs)]
        n *= 2
    return ts


def _mixb_prompt_kernel(qkv_ref, z_ref, bd_ref, cw_ref, nega_ref, dtb_ref, og_ref,
                        o_ref, sfin_ref, s_ref, cbuf_ref):
    c = pl.program_id(0)
    C = DN_CHUNK
    nseq = qkv_ref.shape[0]

    @pl.when(c == 0)
    def _init():
        s_ref[...] = jnp.zeros_like(s_ref)
        cbuf_ref[:, 0:8, :] = jnp.zeros((nseq, 8, B_CONV_CH), F32)

    cw = cw_ref[...]
    og = og_ref[...]
    ri64 = lax.broadcasted_iota(jnp.int32, (C, C), 0)
    ci64 = lax.broadcasted_iota(jnp.int32, (C, C), 1)
    cum_lhs = (ri64 >= ci64).astype(BF16)

    R = MIXB_GROUP * C
    ri = lax.broadcasted_iota(jnp.int32, (R, R), 0)
    ci = lax.broadcasted_iota(jnp.int32, (R, R), 1)
    same = lax.shift_right_logical(ri, C.bit_length() - 1) == lax.shift_right_logical(ci, C.bit_length() - 1)
    tril = same & (ri >= ci)
    strict = same & (ri > ci)
    rep = B_V_HEADS // B_QK_HEADS

    def lane_col(a, lane):
        return jnp.broadcast_to(a[:, lane:lane + 1], (a.shape[0], HEAD_DIM))

    groups = B_V_HEADS // MIXB_GROUP
    units = [(b, grp) for b in range(nseq) for grp in range(groups)]
    heads_of = lambda grp: list(range(grp * MIXB_GROUP, (grp + 1) * MIXB_GROUP))

    acts, gcs, betas = [], [], []
    for b in range(nseq):
        x = qkv_ref[b]
        cbuf_ref[b, 8:8 + C, :] = x
        y = cbuf_ref[b, 5:5 + C, :] * cw[0:1]
        y = y + cbuf_ref[b, 6:6 + C, :] * cw[1:2]
        y = y + cbuf_ref[b, 7:7 + C, :] * cw[2:3]
        y = y + x * cw[3:4]
        cbuf_ref[b, 0:8, :] = x[C - 8:C, :]
        acts.append(jax.nn.silu(y))
        bd = bd_ref[b]
        betas.append(jax.nn.sigmoid(bd))
        g_all = nega_ref[...] * _softplus(bd + dtb_ref[...])
        gcs.append(_mm_exact_lhs(cum_lhs, g_all))

    kst, qst, gcol, glcol, bcol, decay, a_mat, rhs = {}, {}, {}, {}, {}, {}, {}, {}
    for u in units:
        b, grp = u
        act, gc_all, beta_all = acts[b], gcs[b], betas[b]

        def stack(fn):
            return jnp.concatenate([fn(h) for h in heads_of(grp)], axis=0)

        def l2n(cols0, h):
            s = act[:, cols0 + (h // rep) * HEAD_DIM:cols0 + (h // rep + 1) * HEAD_DIM]
            return s * lax.rsqrt(jnp.sum(s * s, -1, keepdims=True) + L2_EPS)

        kst[u] = stack(lambda h: l2n(B_KEY_WIDTH, h))
        qst[u] = stack(lambda h: l2n(0, h) * (HEAD_DIM ** -0.5))
        vst = stack(lambda h: act[:, 2 * B_KEY_WIDTH + h * HEAD_DIM:2 * B_KEY_WIDTH + (h + 1) * HEAD_DIM])
        gcol[u] = stack(lambda h: lane_col(gc_all, B_V_HEADS + h))
        glcol[u] = stack(lambda h: jnp.broadcast_to(gc_all[C - 1:C, B_V_HEADS + h:B_V_HEADS + h + 1], (C, HEAD_DIM)))
        bcol[u] = stack(lambda h: lane_col(beta_all, h))
        grow = gcol[u].T[0:1, :]
        diff = jnp.concatenate([gcol[u], gcol[u]], axis=1) - grow
        decay[u] = jnp.where(tril, jnp.exp(jnp.where(tril, diff, 0.0)), 0.0)
        rhs[u] = jnp.concatenate([vst * bcol[u], kst[u] * bcol[u] * jnp.exp(gcol[u])], axis=1)
    for u in units:
        a_mat[u] = jnp.where(strict, jnp.concatenate([bcol[u], bcol[u]], axis=1) * _mm_nt(kst[u], kst[u]) * decay[u], 0.0)

    t_inv = _unit_lower_inverse_many([a_mat[u] for u in units], ri, ci, C)
    sol = [_mm(t, rhs[u]) for t, u in zip(t_inv, units)]
    qk = [_mm_nt(qst[u], kst[u]) * decay[u] for u in units]
    ws = []
    for sl, u in zip(sol, units):
        b, grp = u
        q_dec = qst[u] * jnp.exp(gcol[u])
        ws.append([_mm(jnp.concatenate([sl[i * C:(i + 1) * C, HEAD_DIM:], q_dec[i * C:(i + 1) * C]], axis=0),
                       s_ref[b * B_V_HEADS + h]) for i, h in enumerate(heads_of(grp))])
    v_new = [sl[:, :HEAD_DIM] - jnp.concatenate([w[:C] for w in wl], axis=0) for sl, wl in zip(sol, ws)]
    outs = [jnp.concatenate([w[C:] for w in wl], axis=0) + _mm(q, v) for wl, q, v in zip(ws, qk, v_new)]
    for u, v, o in zip(units, v_new, outs):
        b, grp = u
        k_dec = kst[u] * jnp.exp(glcol[u] - gcol[u])
        on = o * lax.rsqrt(jnp.mean(o * o, -1, keepdims=True) + RMS_EPS) * og
        for i, h in enumerate(heads_of(grp)):
            rs = slice(i * C, (i + 1) * C)
            si = b * B_V_HEADS + h
            s_ref[si] = s_ref[si] * jnp.exp(glcol[u][i * C:i * C + 1, :]) + _mm_tn(k_dec[rs], v[rs])
            cols = slice(h * HEAD_DIM, (h + 1) * HEAD_DIM)
            o_ref[b, :, cols] = on[rs] * jax.nn.silu(z_ref[b, :, cols])

    @pl.when(c == pl.num_programs(0) - 1)
    def _fin():
        sfin_ref[...] = s_ref[...]


def _mixb_prompt(proj, bd, conv_w, nega, dtb, og, batch, seq):
    nc = seq // DN_CHUNK
    qkv_blk = 2 * A_WIDTH // B_CONV_CH
    z_blk = (2 * A_WIDTH + B_CONV_CH) // B_VAL_WIDTH
    proj3 = proj.reshape(batch, seq, proj.shape[-1])
    bd3 = bd.reshape(batch, seq, LANES)
    o, s_fin = pl.pallas_call(
        _mixb_prompt_kernel,
        grid=(nc,),
        in_specs=[pl.BlockSpec((batch, DN_CHUNK, B_CONV_CH), lambda c: (0, c, qkv_blk)),
                  pl.BlockSpec((batch, DN_CHUNK, B_VAL_WIDTH), lambda c: (0, c, z_blk)),
                  pl.BlockSpec((batch, DN_CHUNK, LANES), lambda c: (0, c, 0)),
                  pl.BlockSpec((B_CONV, B_CONV_CH), lambda c: (0, 0)),
                  pl.BlockSpec((1, LANES), lambda c: (0, 0)),
                  pl.BlockSpec((1, LANES), lambda c: (0, 0)),
                  pl.BlockSpec((1, HEAD_DIM), lambda c: (0, 0))],
        out_specs=[pl.BlockSpec((batch, DN_CHUNK, B_VAL_WIDTH), lambda c: (0, c, 0)),
                   pl.BlockSpec((batch * B_V_HEADS, HEAD_DIM, HEAD_DIM), lambda c: (0, 0, 0))],
        out_shape=[jax.ShapeDtypeStruct((batch, seq, B_VAL_WIDTH), F32),
                   jax.ShapeDtypeStruct((batch * B_V_HEADS, HEAD_DIM, HEAD_DIM), F32)],
        scratch_shapes=[pltpu.VMEM((batch * B_V_HEADS, HEAD_DIM, HEAD_DIM), F32),
                        pltpu.VMEM((batch, DN_CHUNK + 8, B_CONV_CH), F32)],
        compiler_params=_params("arbitrary"),
        name="mixb_prompt",
    )(proj3, proj3, bd3, conv_w, nega, dtb, og)
    return (o.reshape(batch * seq, B_VAL_WIDTH),
            s_fin.reshape(batch, B_V_HEADS, HEAD_DIM, HEAD_DIM))


def _mixb_sample_pre_kernel(qkv_ref, buf_ref, bd_ref, cw_ref, nega_ref, dtb_ref,
                            act_ref, beta_ref, g_ref, *, steps, nb_rows):
    cw = cw_ref[...]

    def slab(j):
        if j < B_CONV - 1:
            return buf_ref[j * nb_rows:(j + 1) * nb_rows, :]
        jj = j - (B_CONV - 1)
        return qkv_ref[jj * nb_rows:(jj + 1) * nb_rows, :]

    for t in range(steps):
        y = slab(t) * cw[0:1]
        for i in range(1, B_CONV):
            y = y + slab(t + i) * cw[i:i + 1]
        act = jax.nn.silu(y)
        rs = slice(t * nb_rows, (t + 1) * nb_rows)
        for qh in range(B_QK_HEADS):
            cq = slice(qh * HEAD_DIM, (qh + 1) * HEAD_DIM)
            ck = slice(B_KEY_WIDTH + qh * HEAD_DIM, B_KEY_WIDTH + (qh + 1) * HEAD_DIM)
            qs = act[:, cq]
            ks = act[:, ck]
            act_ref[rs, cq] = qs * lax.rsqrt(jnp.sum(qs * qs, -1, keepdims=True) + L2_EPS) * (HEAD_DIM ** -0.5)
            act_ref[rs, ck] = ks * lax.rsqrt(jnp.sum(ks * ks, -1, keepdims=True) + L2_EPS)
        act_ref[rs, 2 * B_KEY_WIDTH:] = act[:, 2 * B_KEY_WIDTH:]
    bd = bd_ref[...]
    beta_ref[...] = jax.nn.sigmoid(bd)
    g_ref[...] = nega_ref[...] * _softplus(bd + dtb_ref[...])


def _mixb_sample_pre(proj, buf, bd, conv_w, nega, dtb, steps, nb_rows):
    m = proj.shape[0]
    qkv_blk = 2 * A_WIDTH // B_CONV_CH
    kern = functools.partial(_mixb_sample_pre_kernel, steps=steps, nb_rows=nb_rows)
    return pl.pallas_call(
        kern,
        grid=(1,),
        in_specs=[pl.BlockSpec((m, B_CONV_CH), lambda i: (0, qkv_blk)),
                  pl.BlockSpec(buf.shape, lambda i: (0, 0)),
                  pl.BlockSpec((m, LANES), lambda i: (0, 0)),
                  pl.BlockSpec((B_CONV, B_CONV_CH), lambda i: (0, 0)),
                  pl.BlockSpec((1, LANES), lambda i: (0, 0)),
                  pl.BlockSpec((1, LANES), lambda i: (0, 0))],
        out_specs=[pl.BlockSpec((m, B_CONV_CH), lambda i: (0, 0)),
                   pl.BlockSpec((m, LANES), lambda i: (0, 0)),
                   pl.BlockSpec((m, LANES), lambda i: (0, 0))],
        out_shape=[jax.ShapeDtypeStruct((m, B_CONV_CH), F32),
                   jax.ShapeDtypeStruct((m, LANES), F32),
                   jax.ShapeDtypeStruct((m, LANES), F32)],
        compiler_params=_params("arbitrary"),
        name="mixb_sample_pre",
    )(proj, buf, bd, conv_w, nega, dtb)


def _mixb_sample_rec_kernel(kq_ref, v_ref, z_ref, beta_ref, g_ref, s0_ref, og_ref,
                            o_ref, s_out_ref, *, steps, pairs):
    og = og_ref[...]
    zpad = jnp.zeros((HEAD_DIM - 8, HEAD_DIM), F32)
    zrows = jnp.zeros((8 - steps, HEAD_DIM), F32)
    heads = range(B_V_HEADS)

    def body(bi, carry):
        kqs = [_mm(kq_ref[bi, h], s0_ref[bi, h]) for h in heads]
        pending = []
        for h in heads:
            kq = kq_ref[bi, h]
            g = g_ref[bi, h]
            beta = beta_ref[bi, h]
            v = v_ref[bi, h]
            gc = [g[0:1]]
            for t in range(1, steps):
                gc.append(gc[-1] + g[t:t + 1])
            k = [kq[t:t + 1] for t in range(steps)]
            q = [kq[4 + t:5 + t] for t in range(steps)]
            d = []
            for t in range(steps):
                acc = v[t:t + 1] - jnp.exp(gc[t]) * kqs[h][t:t + 1]
                for j in range(t):
                    kk = jnp.sum(k[j] * k[t], -1, keepdims=True)
                    acc = acc - jnp.exp(gc[t] - gc[j]) * kk * d[j]
                d.append(beta[t:t + 1] * acc)
            outs = []
            for t in range(steps):
                o = jnp.exp(gc[t]) * kqs[h][4 + t:5 + t]
                for j in range(t + 1):
                    qk = jnp.sum(k[j] * q[t], -1, keepdims=True)
                    o = o + jnp.exp(gc[t] - gc[j]) * qk * d[j]
                outs.append(o * lax.rsqrt(jnp.mean(o * o, -1, keepdims=True) + RMS_EPS) * og)
            o_ref[bi, h] = jnp.concatenate(outs + [zrows], axis=0) * jax.nn.silu(z_ref[bi, h])
            k_dec = jnp.concatenate([jnp.exp(gc[-1] - gc[j]) * k[j] for j in range(steps)] + [zrows], axis=0)
            k_pad = jnp.concatenate([k_dec, zpad], axis=0)
            d_pad = jnp.concatenate(d + [zrows, zpad], axis=0)
            pending.append((k_pad.T, d_pad, jnp.exp(gc[-1])))
        for h, (k_t, d_pad, decay_last) in zip(heads, pending):
            s_out_ref[bi, h] = s0_ref[bi, h] * decay_last + _mm(k_t, d_pad)
        return carry

    lax.fori_loop(0, pairs // B_V_HEADS, body, 0)


def _mixb_sample_rec(kq, v, z, beta, g, s0, og, steps, bb):
    nb = kq.shape[0]
    tile = pl.BlockSpec((bb, B_V_HEADS, 8, HEAD_DIM), lambda i: (i, 0, 0, 0))
    st = pl.BlockSpec((bb, B_V_HEADS, HEAD_DIM, HEAD_DIM), lambda i: (i, 0, 0, 0))
    kern = functools.partial(_mixb_sample_rec_kernel, steps=steps, pairs=bb * B_V_HEADS)
    return pl.pallas_call(
        kern,
        grid=(nb // bb,),
        in_specs=[tile, tile, tile, tile, tile, st, pl.BlockSpec((1, HEAD_DIM), lambda i: (0, 0))],
        out_specs=[tile, st],
        out_shape=[jax.ShapeDtypeStruct((nb, B_V_HEADS, 8, HEAD_DIM), F32),
                   jax.ShapeDtypeStruct((nb, B_V_HEADS, HEAD_DIM, HEAD_DIM), F32)],
        compiler_params=_params("arbitrary"),
        name="mixb_sample_rec",
    )(kq, v, z, beta, g, s0, og)


def _outproj_kernel(ap_ref, bp_ref, xp_ref, g1p_ref, sc2p_ref, sh2p_ref,
                    as_ref, bs_ref, xs_ref, g1s_ref, sc2s_ref, sh2s_ref,
                    w_ref, lg_ref, lb_ref, wr_ref, rb_ref, x1_ref, h2_ref, route_ref, count_ref, base_ref,
                    *, prompt_tiles):
    def tile(a_ref, b_ref, x_ref, g1_ref, sc2_ref, sh2_ref):
        mix = (jnp.dot(a_ref[...].astype(BF16), w_ref[:A_WIDTH, :], preferred_element_type=F32)
               + jnp.dot(b_ref[...].astype(BF16), w_ref[A_WIDTH:, :], preferred_element_type=F32))
        x1 = _layer_norm_rows(ALPHA * x_ref[...] + g1_ref[...] * mix, lg_ref[...], lb_ref[...])
        x1_ref[...] = x1
        h2 = x1 * (1.0 + sc2_ref[...]) + sh2_ref[...]
        h2_ref[...] = h2
        route_ref[...] = _route_tile(_mm3(h2, wr_ref[...]), rb_ref[...], base_ref)
        count_ref[...] = base_ref[...]

    @pl.when(pl.program_id(0) == 0)
    def _():
        base_ref[...] = jnp.zeros_like(base_ref)

    is_prompt = pl.program_id(0) < prompt_tiles

    @pl.when(is_prompt)
    def _():
        tile(ap_ref, bp_ref, xp_ref, g1p_ref, sc2p_ref, sh2p_ref)

    @pl.when(jnp.logical_not(is_prompt))
    def _():
        tile(as_ref, bs_ref, xs_ref, g1s_ref, sc2s_ref, sh2s_ref)


def _outproj(prompt, sample, mod_p, mod_s, w_out, lg, lb, w_router, r_bias, tm, seq):
    n_p = prompt[2].shape[0]
    n_s = sample[2].shape[0]
    pt = n_p // tm
    m = n_p + n_s
    tiles_per_seq = seq // tm
    p_row = lambda w: pl.BlockSpec((tm, w), lambda i: (jnp.minimum(i, pt - 1), 0))
    s_row = lambda w: pl.BlockSpec((tm, w), lambda i: (jnp.maximum(i - pt, 0), 0))
    p_mod = lambda col: pl.BlockSpec((None, 1, D_MODEL), lambda i: (jnp.minimum(i, pt - 1) // tiles_per_seq, 0, col))
    s_mod = lambda col: pl.BlockSpec((tm, D_MODEL), lambda i: (0, col))
    row = lambda w: pl.BlockSpec((tm, w), lambda i: (i, 0))
    full = lambda s: pl.BlockSpec(s, lambda i: (0, 0))
    return pl.pallas_call(
        functools.partial(_outproj_kernel, prompt_tiles=pt),
        grid=(m // tm,),
        in_specs=[p_row(A_WIDTH), p_row(B_VAL_WIDTH), p_row(D_MODEL), p_mod(2), p_mod(4), p_mod(3),
                  s_row(A_WIDTH), s_row(B_VAL_WIDTH), s_row(D_MODEL), s_mod(2), s_mod(4), s_mod(3),
                  full((D_MODEL, D_MODEL)), full((1, D_MODEL)), full((1, D_MODEL)),
                  full((D_MODEL, LANES)), full((1, LANES))],
        out_specs=[row(D_MODEL), row(D_MODEL), row(LANES),
                   full((1, LANES))],
        out_shape=[jax.ShapeDtypeStruct((m, D_MODEL), F32),
                   jax.ShapeDtypeStruct((m, D_MODEL), F32),
                   jax.ShapeDtypeStruct((m, LANES), F32),
                   jax.ShapeDtypeStruct((1, LANES), F32)],
        scratch_shapes=[pltpu.VMEM((1, LANES), F32)],
        compiler_params=_params("arbitrary"),
        name="outproj",
    )(*prompt, mod_p, mod_p, mod_p, *sample, mod_s, mod_s, mod_s, w_out, lg, lb, w_router, r_bias)


def _route_tile(lg, bias, base_ref):
    tm = lg.shape[0]
    lane = lax.broadcasted_iota(jnp.int32, lg.shape, 1)
    neg = -jnp.inf

    def first_argmax(score):
        mx = jnp.max(score, -1, keepdims=True)
        return jnp.min(jnp.where(score == mx, lane, LANES), -1, keepdims=True)

    def pick(vals, idx):
        return jnp.sum(jnp.where(lane == idx, vals, 0.0), -1, keepdims=True)

    gmask = lane < N_GROUPS
    mg = jnp.max(jnp.where(gmask, lg, neg), -1, keepdims=True)
    eg = jnp.where(gmask, jnp.exp(jnp.where(gmask, lg - mg, 0.0)), 0.0)
    pg = eg / jnp.sum(eg, -1, keepdims=True)
    sel_g = first_argmax(jnp.where(gmask, lg + bias, neg))
    p_sel = pick(pg, sel_g)

    lo = N_GROUPS + sel_g * EXPERTS_PER_GROUP
    emask = (lane >= lo) & (lane < lo + EXPERTS_PER_GROUP)
    me = jnp.max(jnp.where(emask, lg, neg), -1, keepdims=True)
    ee = jnp.where(emask, jnp.exp(jnp.where(emask, lg - me, 0.0)), 0.0)
    pe = ee / jnp.sum(ee, -1, keepdims=True)
    score = jnp.where(emask, pe + bias, neg)
    i1 = first_argmax(score)
    i2 = first_argmax(jnp.where(lane == i1, neg, score))
    w1 = pick(pe, i1)
    w2 = pick(pe, i2)
    wsum = w1 + w2
    gate1 = w1 / wsum * p_sel
    gate2 = w2 / wsum * p_sel

    hot = ((lane == i1) | (lane == i2)).astype(BF16)
    ri = lax.broadcasted_iota(jnp.int32, (tm, tm), 0)
    ci = lax.broadcasted_iota(jnp.int32, (tm, tm), 1)
    before = jnp.dot((ri > ci).astype(BF16), hot, preferred_element_type=F32) + base_ref[...]
    rank1 = pick(before, i1)
    rank2 = pick(before, i2)
    base_ref[...] = base_ref[...] + jnp.sum(hot.astype(F32), 0, keepdims=True)

    out = jnp.where(lane == 0, (i1 - N_GROUPS).astype(F32), 0.0)
    out = jnp.where(lane == 1, (i2 - N_GROUPS).astype(F32), out)
    out = jnp.where(lane == 2, gate1, out)
    out = jnp.where(lane == 3, gate2, out)
    out = jnp.where(lane == 4, rank1, out)
    out = jnp.where(lane == 5, rank2, out)
    return out


def _expert_kernel(be_ref, nv_ref, src_ref, dst_ref, first_ref, next_ref, wslot_ref,
                   h_ref, wg_ref, wu_ref, wd_ref, y_ref,
                   xbuf, ybuf, wg_st, wu_st, wd_st, wg_s, wu_s, wd_s, gsem, ssem, wsem):
    i = pl.program_id(0)
    nv = nv_ref[0]
    slot = i % 2
    other = 1 - slot

    def gather_row(blk, r, s):
        tok = src_ref[blk * MOE_ROWS + r]
        return pltpu.make_async_copy(h_ref.at[pl.ds(tok, 1)], xbuf.at[s, pl.ds(r, 1)], gsem.at[s])

    def scatter_row(blk, r, s):
        row = dst_ref[(blk + 1) * MOE_ROWS + r]
        return pltpu.make_async_copy(ybuf.at[s, pl.ds(r, 1)], y_ref.at[pl.ds(row, 1)], ssem.at[s])

    def gather_wait(s):
        pltpu.make_async_copy(h_ref.at[pl.ds(0, MOE_ROWS)], xbuf.at[s], gsem.at[s]).wait()

    def scatter_wait(s):
        pltpu.make_async_copy(ybuf.at[s], y_ref.at[pl.ds(0, MOE_ROWS)], ssem.at[s]).wait()

    @pl.when(i == 0)
    def _prologue():
        ybuf[...] = jnp.zeros_like(ybuf)
        base = y_ref.shape[0] - 2 * MOE_ROWS
        pltpu.make_async_copy(ybuf.at[0], y_ref.at[pl.ds(base, MOE_ROWS)], ssem.at[0]).start()

        def body(r, carry):
            gather_row(0, r, 0).start()
            return carry

        lax.fori_loop(0, MOE_ROWS, body, 0, unroll=8)

    def weight_copies(e, p):
        return [pltpu.make_async_copy(src.at[e], dst.at[p], wsem.at[p])
                for src, dst in ((wg_ref, wg_st), (wu_ref, wu_st), (wd_ref, wd_st))]

    @pl.when(i == 0)
    def _():
        for c in weight_copies(be_ref[0], 0):
            c.start()

    @pl.when((i < nv) & (first_ref[i] == 1))
    def _():
        p = wslot_ref[i]
        for c in weight_copies(be_ref[i], p):
            c.wait()
        wg_s[...] = wg_st[p].astype(BF16)
        wu_s[...] = wu_st[p].astype(BF16)
        wd_s[...] = wd_st[p].astype(BF16)

        @pl.when(next_ref[i] >= 0)
        def _():
            for c in weight_copies(next_ref[i], 1 - p):
                c.start()

    @pl.when(i < nv)
    def _block():
        gather_wait(slot)
        for r in range(MOE_ROWS):
            gather_row(i + 1, r, other).start()
            scatter_row(i - 1, r, other).start()
        x = xbuf[slot].astype(BF16)
        hg = jnp.dot(x, wg_s[...], preferred_element_type=F32)
        hu = jnp.dot(x, wu_s[...], preferred_element_type=F32)
        hid = (jax.nn.silu(hg) * hu).astype(BF16)
        y = jnp.dot(hid, wd_s[...], preferred_element_type=F32)
        scatter_wait(slot)
        ybuf[slot] = y

    @pl.when(i == nv - 1)
    def _tail():
        def body(r, carry):
            scatter_row(i, r, slot).start()
            return carry

        lax.fori_loop(0, MOE_ROWS, body, 0, unroll=8)
        scatter_wait(slot)
        scatter_wait(other)
        gather_wait(other)


def _experts(block_e, n_valid, slot_src, slot_dst, h2, w_gate, w_up, w_down, n_out_rows):
    n_blocks = block_e.shape[0]
    idx = jnp.arange(n_blocks, dtype=jnp.int32)
    first = (idx < n_valid[0]) & ((idx == 0) | (block_e != jnp.roll(block_e, 1)))
    wslot = (jnp.cumsum(first.astype(jnp.int32)) - 1) % 2
    first_at = jnp.where(first, idx, n_blocks)
    next_first = jnp.concatenate([lax.cummin(first_at, reverse=True)[1:], jnp.full((1,), n_blocks, jnp.int32)])
    next_e = jnp.where(next_first < n_blocks, block_e[jnp.minimum(next_first, n_blocks - 1)], -1)
    any_spec = pl.BlockSpec(memory_space=pl.ANY)
    return pl.pallas_call(
        _expert_kernel,
        grid_spec=pltpu.PrefetchScalarGridSpec(
            num_scalar_prefetch=7,
            grid=(n_blocks,),
            in_specs=[any_spec, any_spec, any_spec, any_spec],
            out_specs=any_spec,
            scratch_shapes=[pltpu.VMEM((2, MOE_ROWS, D_MODEL), F32),
                            pltpu.VMEM((2, MOE_ROWS, D_MODEL), F32),
                            pltpu.VMEM((2, D_MODEL, D_EXPERT), F32),
                            pltpu.VMEM((2, D_MODEL, D_EXPERT), F32),
                            pltpu.VMEM((2, D_EXPERT, D_MODEL), F32),
                            pltpu.VMEM((D_MODEL, D_EXPERT), BF16),
                            pltpu.VMEM((D_MODEL, D_EXPERT), BF16),
                            pltpu.VMEM((D_EXPERT, D_MODEL), BF16),
                            pltpu.SemaphoreType.DMA((2,)),
                            pltpu.SemaphoreType.DMA((2,)),
                            pltpu.SemaphoreType.DMA((2,))]),
        out_shape=jax.ShapeDtypeStruct((n_out_rows, D_MODEL), F32),
        compiler_params=_params("arbitrary"),
        name="experts",
    )(block_e, n_valid, slot_src, slot_dst, first.astype(jnp.int32), next_e.astype(jnp.int32),
      wslot.astype(jnp.int32), h2, w_gate, w_up, w_down)


def _final_kernel(x1_ref, y0_ref, y1_ref, route_ref, g2_ref, lg_ref, lb_ref, o_ref):
    route = route_ref[...]
    ff = y0_ref[...] * route[:, 2:3] + y1_ref[...] * route[:, 3:4]
    o_ref[...] = _layer_norm_rows(ALPHA * x1_ref[...] + g2_ref[...] * ff, lg_ref[...], lb_ref[...])


def _final(x1, y01, route, mod, lg, lb, tm, rows_per_mod, row0, n_rows, n_tok):
    off = row0 // tm
    off1 = (n_tok + row0) // tm
    return pl.pallas_call(
        _final_kernel,
        grid=(n_rows // tm,),
        in_specs=[pl.BlockSpec((tm, D_MODEL), lambda i: (off + i, 0)),
                  pl.BlockSpec((tm, D_MODEL), lambda i: (off + i, 0)),
                  pl.BlockSpec((tm, D_MODEL), lambda i: (off1 + i, 0)),
                  pl.BlockSpec((tm, LANES), lambda i: (off + i, 0)),
                  _mod_spec(mod, 5, tm, rows_per_mod),
                  pl.BlockSpec((1, D_MODEL), lambda i: (0, 0)),
                  pl.BlockSpec((1, D_MODEL), lambda i: (0, 0))],
        out_specs=pl.BlockSpec((tm, D_MODEL), lambda i: (i, 0)),
        out_shape=jax.ShapeDtypeStruct((n_rows, D_MODEL), F32),
        compiler_params=_params("arbitrary"),
        name="final",
    )(x1, y01, y01, route, mod, lg, lb)


def _lane_row(vec, offset):
    return jnp.zeros((1, LANES), F32).at[0, offset:offset + vec.shape[0]].set(vec.astype(F32))


def kernel(x_prompt, x_sample, state_conv, state_ssm, c_prompt, c_sample, w_ada, b_ada, w_in, a_ws, a_bs, a_norm_g, a_norm_b, b_conv_w, b_a_log, b_dt_bias, b_onorm_g, w_out, ln1_g, ln1_b, w_router_g, b_router_g, w_router_e, b_router_e, w_gate, w_up, w_down, ln2_g, ln2_b):
    batch, seq, d = x_prompt.shape
    nb, steps, _ = x_sample.shape
    n_p = batch * seq
    n_s = nb * steps
    n_tok = n_p + n_s
    l = 0

    c_rows = batch + nb
    c_pad = (-c_rows) % 8
    c_all = jnp.concatenate([c_prompt, c_sample, jnp.zeros((c_pad, d), F32)], axis=0)
    mod = _ada(c_all, w_ada[l], b_ada[l].reshape(1, -1))
    mod_p = mod[:batch].reshape(batch, 1, 6 * d)
    mod_s = mod[batch:batch + nb]

    w_in_t = jnp.swapaxes(w_in[l], 0, 1)
    w_bd = jnp.pad(w_in_t[MAIN_WIDTH:], ((0, LANES - 2 * B_V_HEADS), (0, 0))).astype(BF16)
    xp = x_prompt.reshape(n_p, d)
    xs = jnp.swapaxes(x_sample, 0, 1).reshape(n_s, d)
    proj_p, bd_p, proj_s, bd_s = _inproj(xp, xs, mod_p, mod_s, w_in_t, w_bd, 512, seq)

    ng = a_norm_g[l].reshape(1, -1)
    nbias = a_norm_b[l].reshape(1, -1)
    bias_tile = jnp.repeat(a_bs[l].T, HEAD_DIM, axis=1)
    a_out_p = _mixa_prompt(proj_p, a_ws[l], bias_tile, ng, nbias, 256)
    coef = jnp.repeat(jnp.transpose(a_ws[l][:, :steps, :steps], (1, 2, 0)).reshape(steps * steps, A_HEADS),
                      HEAD_DIM, axis=1)
    a_out_s, chunkv = _mixa_sample(proj_s, coef, bias_tile[:steps], ng, nbias, steps, nb)

    nega = _lane_row(-jnp.exp(b_a_log[l].astype(F32)), B_V_HEADS)
    dtb = _lane_row(b_dt_bias[l], B_V_HEADS)
    og = b_onorm_g[l].reshape(1, -1)
    b_out_p, ssm_p = _mixb_prompt(proj_p, bd_p, b_conv_w[l], nega, dtb, og, batch, seq)
    qkv0 = 2 * A_WIDTH
    conv_p = proj_p.reshape(batch, seq, MAIN_WIDTH)[:, seq - (B_CONV - 1):, qkv0:qkv0 + B_CONV_CH]

    buf_s = jnp.swapaxes(state_conv[l], 0, 1).reshape((B_CONV - 1) * nb, B_CONV_CH)
    act_s, beta_s, g_s = _mixb_sample_pre(proj_s, buf_s, bd_s, b_conv_w[l], nega, dtb, steps, nb)
    rep = B_V_HEADS // B_QK_HEADS
    act4 = act_s.reshape(steps, nb, B_CONV_CH)
    q8 = jnp.repeat(act4[..., :B_KEY_WIDTH].reshape(steps, nb, B_QK_HEADS, HEAD_DIM), rep, axis=2)
    k8 = jnp.repeat(act4[..., B_KEY_WIDTH:2 * B_KEY_WIDTH].reshape(steps, nb, B_QK_HEADS, HEAD_DIM), rep, axis=2)

    def to_tiles(a):
        a = jnp.transpose(a, (1, 2, 0, 3))
        return jnp.pad(a, ((0, 0), (0, 0), (0, 8 - a.shape[2]), (0, 0)))

    kq_t = jnp.concatenate([to_tiles(k8)[:, :, :4], to_tiles(q8)[:, :, :4]], axis=2)
    v_t = to_tiles(act4[..., 2 * B_KEY_WIDTH:].reshape(steps, nb, B_V_HEADS, HEAD_DIM))
    z_t = to_tiles(proj_s[:, qkv0 + B_CONV_CH:].reshape(steps, nb, B_V_HEADS, HEAD_DIM))

    def gate_tiles(a, off):
        a = a[:, off:off + B_V_HEADS].reshape(steps, nb, B_V_HEADS, 1)
        return to_tiles(jnp.broadcast_to(a, (steps, nb, B_V_HEADS, HEAD_DIM)))

    o_t, ssm_s = _mixb_sample_rec(kq_t, v_t, z_t, gate_tiles(beta_s, 0), gate_tiles(g_s, B_V_HEADS),
                                  state_ssm[l], og, steps, 8)
    b_out_s = jnp.transpose(o_t[:, :, :steps], (2, 0, 1, 3)).reshape(n_s, B_VAL_WIDTH)
    conv_s = jnp.swapaxes(proj_s.reshape(steps, nb, MAIN_WIDTH)[steps - (B_CONV - 1):, :, qkv0:qkv0 + B_CONV_CH], 0, 1)

    w_out_b = w_out[l].astype(BF16)
    lg1 = ln1_g[l].reshape(1, -1)
    lb1 = ln1_b[l].reshape(1, -1)
    w_router = jnp.concatenate([w_router_g[l], w_router_e[l],
                                jnp.zeros((d, LANES - N_GROUPS - N_EXPERTS), F32)], axis=1)
    out_tm = 256
    mod_s_tile = jnp.tile(mod_s, (out_tm // nb, 1))
    r_bias = jnp.concatenate([b_router_g[l], b_router_e[l],
                              jnp.zeros((LANES - N_GROUPS - N_EXPERTS,), F32)]).reshape(1, LANES)
    x1, h2, route, counts_row = _outproj((a_out_p, b_out_p, xp), (a_out_s, b_out_s, xs), mod_p, mod_s_tile,
                                         w_out_b, lg1, lb1, w_router, r_bias, out_tm, seq)

    counts = counts_row[0, N_GROUPS:N_GROUPS + N_EXPERTS].astype(jnp.int32)
    route_t = route[:, :8].T
    eid = route_t[0:2].astype(jnp.int32)
    rank = route_t[4:6].astype(jnp.int32)
    padded = (counts + MOE_ROWS - 1) // MOE_ROWS * MOE_ROWS
    pend = jnp.cumsum(padded)
    pstart = pend - padded
    dest = rank + jnp.sum(jnp.where(eid[None] == jnp.arange(N_EXPERTS, dtype=jnp.int32)[:, None, None],
                                    pstart[:, None, None], 0), axis=0)
    n_blocks = -(-(2 * n_tok) // MOE_ROWS) + N_EXPERTS
    n_slots = n_blocks * MOE_ROWS
    slot_row = jnp.full((n_slots,), -1, jnp.int32).at[dest.reshape(-1)].set(jnp.arange(2 * n_tok, dtype=jnp.int32))
    slot_id = jnp.arange(n_slots, dtype=jnp.int32)
    trash = 2 * n_tok + (slot_id // MOE_ROWS % 2) * MOE_ROWS + slot_id % MOE_ROWS
    row_id = jnp.arange(MOE_ROWS, dtype=jnp.int32)
    slot_src = jnp.concatenate([jnp.where(slot_row < 0, 0, slot_row % n_tok),
                                jnp.zeros((MOE_ROWS,), jnp.int32)])
    slot_dst = jnp.concatenate([2 * n_tok + MOE_ROWS + row_id,
                                jnp.where(slot_row < 0, trash, slot_row)])
    blk0 = jnp.arange(n_blocks, dtype=jnp.int32) * MOE_ROWS
    block_e = jnp.minimum(jnp.sum(pend[None, :] <= blk0[:, None], axis=1), N_EXPERTS - 1).astype(jnp.int32)
    n_valid = (pend[-1:] // MOE_ROWS).astype(jnp.int32)

    y01 = _experts(block_e, n_valid, slot_src, slot_dst, h2, w_gate[l], w_up[l], w_down[l],
                   2 * n_tok + 2 * MOE_ROWS)

    lg2 = ln2_g[l].reshape(1, -1)
    lb2 = ln2_b[l].reshape(1, -1)
    y_p = _final(x1, y01, route, mod_p, lg2, lb2, 256, seq, 0, n_p, n_tok)
    y_s = _final(x1, y01, route, mod_s, lg2, lb2, nb, 0, n_p, n_s, n_tok)

    y_prompt = y_p.reshape(batch, seq, d)
    y_sample = jnp.swapaxes(y_s.reshape(steps, nb, d), 0, 1)
    chunkv_s = jnp.swapaxes(chunkv.reshape(steps, nb, A_HEADS, HEAD_DIM), 0, 1)
    return (y_prompt, y_sample, conv_p[None], ssm_p[None], conv_s[None], ssm_s[None], chunkv_s[None])
```

```python
import functools

import jax
import jax.numpy as jnp
from jax import lax
from jax.experimental import pallas as pl
from jax.experimental.pallas import tpu as pltpu

F32 = jnp.float32
BF16 = jnp.bfloat16

D_MODEL = 2048
DEPTH = 1
A_HEADS = 8
HEAD_DIM = 128
A_WIDTH = 1024
A_CHUNK = 128
B_QK_HEADS = 4
B_V_HEADS = 8
B_KEY_WIDTH = 512
B_VAL_WIDTH = 1024
B_CONV = 4
B_CONV_CH = 2048
DN_CHUNK = 64
MAIN_WIDTH = 2 * A_WIDTH + B_CONV_CH + B_VAL_WIDTH
N_GROUPS = 4
EXPERTS_PER_GROUP = 8
N_EXPERTS = 32
D_EXPERT = 512
ALPHA = (2 * DEPTH) ** 0.25
LN_EPS = 1e-5
RMS_EPS = 1e-6
L2_EPS = 1e-6

LANES = 128
VMEM_LIMIT = 56 * 1024 * 1024
MOE_ROWS = 256
MIXB_GROUP = 4


def _params(*sem, vmem=VMEM_LIMIT):
    return pltpu.CompilerParams(dimension_semantics=sem, vmem_limit_bytes=vmem)


def _mm(a, b):
    return jnp.dot(a.astype(BF16), b.astype(BF16), preferred_element_type=F32)


def _mm_nt(a, b):
    return lax.dot_general(a.astype(BF16), b.astype(BF16), (((1,), (1,)), ((), ())),
                           preferred_element_type=F32)


def _mm_tn(a, b):
    return lax.dot_general(a.astype(BF16), b.astype(BF16), (((0,), (0,)), ((), ())),
                           preferred_element_type=F32)


def _split(x):
    hi = x.astype(BF16)
    lo = (x - hi.astype(F32)).astype(BF16)
    return hi, lo


def _mm_exact_lhs(a_bf16, b):
    hi, lo = _split(b)
    return (jnp.dot(a_bf16, hi, preferred_element_type=F32)
            + jnp.dot(a_bf16, lo, preferred_element_type=F32))


def _mm3(a, b):
    ah, al = _split(a)
    bh, bl = _split(b)
    return (jnp.dot(ah, bh, preferred_element_type=F32) + jnp.dot(ah, bl, preferred_element_type=F32)
            + jnp.dot(al, bh, preferred_element_type=F32))


def _softplus(x):
    return jnp.maximum(x, 0.0) + jnp.log1p(jnp.exp(-jnp.abs(x)))


def _layer_norm_rows(x, g, b):
    mu = jnp.mean(x, -1, keepdims=True)
    xc = x - mu
    var = jnp.mean(xc * xc, -1, keepdims=True)
    return xc * lax.rsqrt(var + LN_EPS) * g + b


def _ada_kernel(c_ref, w_ref, b_ref, o_ref):
    a = jax.nn.silu(c_ref[...]).astype(BF16)
    o_ref[...] = jnp.dot(a, w_ref[...].astype(BF16), preferred_element_type=F32) + b_ref[...]


def _ada(c_all, w_ada, b_ada):
    rows = c_all.shape[0]
    tn = 1024
    return pl.pallas_call(
        _ada_kernel,
        grid=(6 * D_MODEL // tn,),
        in_specs=[pl.BlockSpec((rows, D_MODEL), lambda j: (0, 0)),
                  pl.BlockSpec((D_MODEL, tn), lambda j: (0, j)),
                  pl.BlockSpec((1, tn), lambda j: (0, j))],
        out_specs=pl.BlockSpec((rows, tn), lambda j: (0, j)),
        out_shape=jax.ShapeDtypeStruct((rows, 6 * D_MODEL), F32),
        compiler_params=_params("arbitrary"),
        name="ada",
    )(c_all, w_ada, b_ada)


def _inproj_kernel(xp_ref, scp_ref, shp_ref, xs_ref, scs_ref, shs_ref, w_ref, wbd_ref,
                   op_ref, bdp_ref, os_ref, bds_ref, wb_ref, h_ref, *, prompt_tiles):
    i = pl.program_id(0)
    j = pl.program_id(1)
    tn = w_ref.shape[0]

    @pl.when(i == 0)
    def _():
        wb_ref[pl.ds(pl.multiple_of(j * tn, tn), tn), :] = w_ref[...].astype(BF16)

    def project(make_h, o_ref, bd_ref):
        @pl.when(j == 0)
        def _():
            h_ref[...] = make_h()
            bd_ref[...] = _mm_nt(h_ref[...], wbd_ref[...])

        o_ref[...] = _mm_nt(h_ref[...], wb_ref[pl.ds(pl.multiple_of(j * tn, tn), tn), :])

    @pl.when(i < prompt_tiles)
    def _():
        project(lambda: (xp_ref[...] * (1.0 + scp_ref[...]) + shp_ref[...]).astype(BF16), op_ref, bdp_ref)

    @pl.when(i >= prompt_tiles)
    def _():
        def make_h():
            n_s, nb = xs_ref.shape[0], scs_ref.shape[0]
            x = xs_ref[...].reshape(n_s // nb, nb, D_MODEL)
            h = x * (1.0 + scs_ref[...])[None] + shs_ref[...][None]
            return h.reshape(n_s, D_MODEL).astype(BF16)

        project(make_h, os_ref, bds_ref)


def _mod_spec(mod, col, tm, rows_per_mod, axis=0):
    if rows_per_mod:
        tiles = rows_per_mod // tm
        return pl.BlockSpec((None, 1, D_MODEL), lambda *g: (g[axis] // tiles, 0, col))
    return pl.BlockSpec((tm, D_MODEL), lambda *g: (0, col))


def _inproj(xp, xs, mod_p, mod_s, w_in_t, w_bd, tm, seq):
    n_p, n_s = xp.shape[0], xs.shape[0]
    assert n_s == tm
    nb = mod_s.shape[0]
    tn = 1024
    nj = MAIN_WIDTH // tn
    pt = n_p // tm
    tiles_per_seq = seq // tm
    pi = lambda i: jnp.minimum(i, pt - 1)
    once = pl.Buffered(1)
    p_mod = lambda col: pl.BlockSpec((None, 1, D_MODEL), lambda i, j: (pi(i) // tiles_per_seq, 0, col))
    s_mod = lambda col: pl.BlockSpec((nb, D_MODEL), lambda i, j: (0, col), pipeline_mode=once)
    return pl.pallas_call(
        functools.partial(_inproj_kernel, prompt_tiles=pt),
        grid=(pt + 1, nj),
        in_specs=[pl.BlockSpec((tm, D_MODEL), lambda i, j: (pi(i), 0), pipeline_mode=once), p_mod(1), p_mod(0),
                  pl.BlockSpec((n_s, D_MODEL), lambda i, j: (0, 0), pipeline_mode=once), s_mod(1), s_mod(0),
                  pl.BlockSpec((tn, D_MODEL), lambda i, j: (jnp.where(i == 0, j, nj - 1), 0), pipeline_mode=once),
                  pl.BlockSpec((LANES, D_MODEL), lambda i, j: (0, 0), pipeline_mode=once)],
        out_specs=[pl.BlockSpec((tm, tn), lambda i, j: (pi(i), jnp.where(i < pt, j, nj - 1))),
                   pl.BlockSpec((tm, LANES), lambda i, j: (pi(i), 0)),
                   pl.BlockSpec((n_s, tn), lambda i, j: (0, jnp.where(i < pt, 0, j))),
                   pl.BlockSpec((n_s, LANES), lambda i, j: (0, 0))],
        out_shape=[jax.ShapeDtypeStruct((n_p, MAIN_WIDTH), F32),
                   jax.ShapeDtypeStruct((n_p, LANES), F32),
                   jax.ShapeDtypeStruct((n_s, MAIN_WIDTH), F32),
                   jax.ShapeDtypeStruct((n_s, LANES), F32)],
        scratch_shapes=[pltpu.VMEM((MAIN_WIDTH, D_MODEL), BF16),
                        pltpu.VMEM((tm, D_MODEL), BF16)],
        compiler_params=_params("arbitrary", "arbitrary", vmem=54 * 1024 * 1024),
        name="inproj",
    )(xp, mod_p, mod_p, xs, mod_s, mod_s, w_in_t, w_bd)


def _mixa_prompt_kernel(p_ref, ws_ref, bias_ref, ng_ref, nb_ref, o_ref):
    rows = p_ref.shape[0]
    u = jax.nn.gelu(p_ref[:, :A_WIDTH])
    v = _layer_norm_rows(jax.nn.gelu(p_ref[:, A_WIDTH:]), ng_ref[...], nb_ref[...])
    ri = lax.broadcasted_iota(jnp.int32, (A_CHUNK, A_CHUNK), 0)
    ci = lax.broadcasted_iota(jnp.int32, (A_CHUNK, A_CHUNK), 1)
    for h in range(A_HEADS):
        w = jnp.where(ri >= ci, ws_ref[h], 0.0).astype(BF16)
        cols = slice(h * HEAD_DIM, (h + 1) * HEAD_DIM)
        for c in range(rows // A_CHUNK):
            rs = slice(c * A_CHUNK, (c + 1) * A_CHUNK)
            s = jnp.dot(w, v[rs, cols].astype(BF16), preferred_element_type=F32) + bias_ref[:, cols]
            o_ref[rs, cols] = u[rs, cols] * s


def _mixa_prompt(proj, a_ws, bias_tile, ng, nb, tm):
    m = proj.shape[0]
    return pl.pallas_call(
        _mixa_prompt_kernel,
        grid=(m // tm,),
        in_specs=[pl.BlockSpec((tm, 2 * A_WIDTH), lambda i: (i, 0)),
                  pl.BlockSpec((A_HEADS, A_CHUNK, A_CHUNK), lambda i: (0, 0, 0)),
                  pl.BlockSpec((A_CHUNK, A_WIDTH), lambda i: (0, 0)),
                  pl.BlockSpec((1, A_WIDTH), lambda i: (0, 0)),
                  pl.BlockSpec((1, A_WIDTH), lambda i: (0, 0))],
        out_specs=pl.BlockSpec((tm, A_WIDTH), lambda i: (i, 0)),
        out_shape=jax.ShapeDtypeStruct((m, A_WIDTH), F32),
        compiler_params=_params("arbitrary"),
        name="mixa_prompt",
    )(proj, a_ws, bias_tile, ng, nb)


def _mixa_sample_kernel(p_ref, coef_ref, bias_ref, ng_ref, nb_ref, o_ref, v_ref, *, steps, nb_rows):
    u = jax.nn.gelu(p_ref[:, :A_WIDTH])
    v = _layer_norm_rows(jax.nn.gelu(p_ref[:, A_WIDTH:]), ng_ref[...], nb_ref[...])
    v_ref[...] = v
    for t in range(steps):
        s = bias_ref[t:t + 1, :]
        for j in range(t + 1):
            s = s + coef_ref[t * steps + j:t * steps + j + 1, :] * v[j * nb_rows:(j + 1) * nb_rows, :]
        rs = slice(t * nb_rows, (t + 1) * nb_rows)
        o_ref[rs, :] = u[rs, :] * s


def _mixa_sample(proj, coef, bias, ng, nb, steps, nb_rows):
    m = proj.shape[0]
    kern = functools.partial(_mixa_sample_kernel, steps=steps, nb_rows=nb_rows)
    return pl.pallas_call(
        kern,
        grid=(1,),
        in_specs=[pl.BlockSpec((m, 2 * A_WIDTH), lambda i: (0, 0)),
                  pl.BlockSpec(coef.shape, lambda i: (0, 0)),
                  pl.BlockSpec(bias.shape, lambda i: (0, 0)),
                  pl.BlockSpec((1, A_WIDTH), lambda i: (0, 0)),
                  pl.BlockSpec((1, A_WIDTH), lambda i: (0, 0))],
        out_specs=[pl.BlockSpec((m, A_WIDTH), lambda i: (0, 0)),
                   pl.BlockSpec((m, A_WIDTH), lambda i: (0, 0))],
        out_shape=[jax.ShapeDtypeStruct((m, A_WIDTH), F32),
                   jax.ShapeDtypeStruct((m, A_WIDTH), F32)],
        compiler_params=_params("arbitrary"),
        name="mixa_sample",
    )(proj, coef, bias, ng, nb)


def _unit_lower_inverse_many(a_list, ri, ci, block):
    eye = (ri == ci).astype(F32)
    pair = (lax.shift_right_logical(ri, 1) == lax.shift_right_logical(ci, 1)) & ((ri & 1) == 1) & ((ci & 1) == 0)
    ts = [eye - jnp.where(pair, a, 0.0) for a in a_list]
    n = 2
    while n < block:
        sh = n.bit_length()
        m = ((lax.shift_right_logical(ri, sh) == lax.shift_right_logical(ci, sh))
             & ((ri & n) != 0) & ((ci & n) == 0))
        xs = [_mm(t, jnp.where(m, a, 0.0)) for t, a in zip(ts, a_list)]
        ts = [t - _mm(x, t) for t, x in zip(ts, xs)]
        n *= 2
    return ts


def _mixb_prompt_kernel(qkv_ref, z_ref, bd_ref, cw_ref, nega_ref, dtb_ref, og_ref,
                        o_ref, sfin_ref, s_ref, cbuf_ref):
    c = pl.program_id(0)
    C = DN_CHUNK
    nseq = qkv_ref.shape[0]

    @pl.when(c == 0)
    def _init():
        s_ref[...] = jnp.zeros_like(s_ref)
        cbuf_ref[:, 0:8, :] = jnp.zeros((nseq, 8, B_CONV_CH), F32)

    cw = cw_ref[...]
    og = og_ref[...]
    ri64 = lax.broadcasted_iota(jnp.int32, (C, C), 0)
    ci64 = lax.broadcasted_iota(jnp.int32, (C, C), 1)
    cum_lhs = (ri64 >= ci64).astype(BF16)

    R = MIXB_GROUP * C
    ri = lax.broadcasted_iota(jnp.int32, (R, R), 0)
    ci = lax.broadcasted_iota(jnp.int32, (R, R), 1)
    same = lax.shift_right_logical(ri, C.bit_length() - 1) == lax.shift_right_logical(ci, C.bit_length() - 1)
    tril = same & (ri >= ci)
    strict = same & (ri > ci)
    rep = B_V_HEADS // B_QK_HEADS

    def lane_col(a, lane):
        return jnp.broadcast_to(a[:, lane:lane + 1], (a.shape[0], HEAD_DIM))

    groups = B_V_HEADS // MIXB_GROUP
    units = [(b, grp) for b in range(nseq) for grp in range(groups)]
    heads_of = lambda grp: list(range(grp * MIXB_GROUP, (grp + 1) * MIXB_GROUP))

    acts, gcs, betas = [], [], []
    for b in range(nseq):
        x = qkv_ref[b]
        cbuf_ref[b, 8:8 + C, :] = x
        y = cbuf_ref[b, 5:5 + C, :] * cw[0:1]
        y = y + cbuf_ref[b, 6:6 + C, :] * cw[1:2]
        y = y + cbuf_ref[b, 7:7 + C, :] * cw[2:3]
        y = y + x * cw[3:4]
        cbuf_ref[b, 0:8, :] = x[C - 8:C, :]
        acts.append(jax.nn.silu(y))
        bd = bd_ref[b]
        betas.append(jax.nn.sigmoid(bd))
        g_all = nega_ref[...] * _softplus(bd + dtb_ref[...])
        gcs.append(_mm_exact_lhs(cum_lhs, g_all))

    kst, qst, gcol, glcol, bcol, decay, a_mat, rhs = {}, {}, {}, {}, {}, {}, {}, {}
    for u in units:
        b, grp = u
        act, gc_all, beta_all = acts[b], gcs[b], betas[b]

        def stack(fn):
            return jnp.concatenate([fn(h) for h in heads_of(grp)], axis=0)

        def l2n(cols0, h):
            s = act[:, cols0 + (h // rep) * HEAD_DIM:cols0 + (h // rep + 1) * HEAD_DIM]
            return s * lax.rsqrt(jnp.sum(s * s, -1, keepdims=True) + L2_EPS)

        kst[u] = stack(lambda h: l2n(B_KEY_WIDTH, h))
        qst[u] = stack(lambda h: l2n(0, h) * (HEAD_DIM ** -0.5))
        vst = stack(lambda h: act[:, 2 * B_KEY_WIDTH + h * HEAD_DIM:2 * B_KEY_WIDTH + (h + 1) * HEAD_DIM])
        gcol[u] = stack(lambda h: lane_col(gc_all, B_V_HEADS + h))
        glcol[u] = stack(lambda h: jnp.broadcast_to(gc_all[C - 1:C, B_V_HEADS + h:B_V_HEADS + h + 1], (C, HEAD_DIM)))
        bcol[u] = stack(lambda h: lane_col(beta_all, h))
        grow = gcol[u].T[0:1, :]
        diff = jnp.concatenate([gcol[u], gcol[u]], axis=1) - grow
        decay[u] = jnp.where(tril, jnp.exp(jnp.where(tril, diff, 0.0)), 0.0)
        rhs[u] = jnp.concatenate([vst * bcol[u], kst[u] * bcol[u] * jnp.exp(gcol[u])], axis=1)
    for u in units:
        a_mat[u] = jnp.where(strict, jnp.concatenate([bcol[u], bcol[u]], axis=1) * _mm_nt(kst[u], kst[u]) * decay[u], 0.0)

    t_inv = _unit_lower_inverse_many([a_mat[u] for u in units], ri, ci, C)
    sol = [_mm(t, rhs[u]) for t, u in zip(t_inv, units)]
    qk = [_mm_nt(qst[u], kst[u]) * decay[u] for u in units]
    ws = []
    for sl, u in zip(sol, units):
        b, grp = u
        q_dec = qst[u] * jnp.exp(gcol[u])
        ws.append([_mm(jnp.concatenate([sl[i * C:(i + 1) * C, HEAD_DIM:], q_dec[i * C:(i + 1) * C]], axis=0),
                       s_ref[b * B_V_HEADS + h]) for i, h in enumerate(heads_of(grp))])
    v_new = [sl[:, :HEAD_DIM] - jnp.concatenate([w[:C] for w in wl], axis=0) for sl, wl in zip(sol, ws)]
    outs = [jnp.concatenate([w[C:] for w in wl], axis=0) + _mm(q, v) for wl, q, v in zip(ws, qk, v_new)]
    for u, v, o in zip(units, v_new, outs):
        b, grp = u
        k_dec = kst[u] * jnp.exp(glcol[u] - gcol[u])
        on = o * lax.rsqrt(jnp.mean(o * o, -1, keepdims=True) + RMS_EPS) * og
        for i, h in enumerate(heads_of(grp)):
            rs = slice(i * C, (i + 1) * C)
            si = b * B_V_HEADS + h
            s_ref[si] = s_ref[si] * jnp.exp(glcol[u][i * C:i * C + 1, :]) + _mm_tn(k_dec[rs], v[rs])
            cols = slice(h * HEAD_DIM, (h + 1) * HEAD_DIM)
            o_ref[b, :, cols] = on[rs] * jax.nn.silu(z_ref[b, :, cols])

    @pl.when(c == pl.num_programs(0) - 1)
    def _fin():
        sfin_ref[...] = s_ref[...]


def _mixb_prompt(proj, bd, conv_w, nega, dtb, og, batch, seq):
    nc = seq // DN_CHUNK
    qkv_blk = 2 * A_WIDTH // B_CONV_CH
    z_blk = (2 * A_WIDTH + B_CONV_CH) // B_VAL_WIDTH
    proj3 = proj.reshape(batch, seq, proj.shape[-1])
    bd3 = bd.reshape(batch, seq, LANES)
    o, s_fin = pl.pallas_call(
        _mixb_prompt_kernel,
        grid=(nc,),
        in_specs=[pl.BlockSpec((batch, DN_CHUNK, B_CONV_CH), lambda c: (0, c, qkv_blk)),
                  pl.BlockSpec((batch, DN_CHUNK, B_VAL_WIDTH), lambda c: (0, c, z_blk)),
                  pl.BlockSpec((batch, DN_CHUNK, LANES), lambda c: (0, c, 0)),
                  pl.BlockSpec((B_CONV, B_CONV_CH), lambda c: (0, 0)),
                  pl.BlockSpec((1, LANES), lambda c: (0, 0)),
                  pl.BlockSpec((1, LANES), lambda c: (0, 0)),
                  pl.BlockSpec((1, HEAD_DIM), lambda c: (0, 0))],
        out_specs=[pl.BlockSpec((batch, DN_CHUNK, B_VAL_WIDTH), lambda c: (0, c, 0)),
                   pl.BlockSpec((batch * B_V_HEADS, HEAD_DIM, HEAD_DIM), lambda c: (0, 0, 0))],
        out_shape=[jax.ShapeDtypeStruct((batch, seq, B_VAL_WIDTH), F32),
                   jax.ShapeDtypeStruct((batch * B_V_HEADS, HEAD_DIM, HEAD_DIM), F32)],
        scratch_shapes=[pltpu.VMEM((batch * B_V_HEADS, HEAD_DIM, HEAD_DIM), F32),
                        pltpu.VMEM((batch, DN_CHUNK + 8, B_CONV_CH), F32)],
        compiler_params=_params("arbitrary"),
        name="mixb_prompt",
    )(proj3, proj3, bd3, conv_w, nega, dtb, og)
    return (o.reshape(batch * seq, B_VAL_WIDTH),
            s_fin.reshape(batch, B_V_HEADS, HEAD_DIM, HEAD_DIM))


def _mixb_sample_pre_kernel(qkv_ref, buf_ref, bd_ref, cw_ref, nega_ref, dtb_ref,
                            act_ref, beta_ref, g_ref, *, steps, nb_rows):
    cw = cw_ref[...]

    def slab(j):
        if j < B_CONV - 1:
            return buf_ref[j * nb_rows:(j + 1) * nb_rows, :]
        jj = j - (B_CONV - 1)
        return qkv_ref[jj * nb_rows:(jj + 1) * nb_rows, :]

    for t in range(steps):
        y = slab(t) * cw[0:1]
        for i in range(1, B_CONV):
            y = y + slab(t + i) * cw[i:i + 1]
        act = jax.nn.silu(y)
        rs = slice(t * nb_rows, (t + 1) * nb_rows)
        for qh in range(B_QK_HEADS):
            cq = slice(qh * HEAD_DIM, (qh + 1) * HEAD_DIM)
            ck = slice(B_KEY_WIDTH + qh * HEAD_DIM, B_KEY_WIDTH + (qh + 1) * HEAD_DIM)
            qs = act[:, cq]
            ks = act[:, ck]
            act_ref[rs, cq] = qs * lax.rsqrt(jnp.sum(qs * qs, -1, keepdims=True) + L2_EPS) * (HEAD_DIM ** -0.5)
            act_ref[rs, ck] = ks * lax.rsqrt(jnp.sum(ks * ks, -1, keepdims=True) + L2_EPS)
        act_ref[rs, 2 * B_KEY_WIDTH:] = act[:, 2 * B_KEY_WIDTH:]
    bd = bd_ref[...]
    beta_ref[...] = jax.nn.sigmoid(bd)
    g_ref[...] = nega_ref[...] * _softplus(bd + dtb_ref[...])


def _mixb_sample_pre(proj, buf, bd, conv_w, nega, dtb, steps, nb_rows):
    m = proj.shape[0]
    qkv_blk = 2 * A_WIDTH // B_CONV_CH
    kern = functools.partial(_mixb_sample_pre_kernel, steps=steps, nb_rows=nb_rows)
    return pl.pallas_call(
        kern,
        grid=(1,),
        in_specs=[pl.BlockSpec((m, B_CONV_CH), lambda i: (0, qkv_blk)),
                  pl.BlockSpec(buf.shape, lambda i: (0, 0)),
                  pl.BlockSpec((m, LANES), lambda i: (0, 0)),
                  pl.BlockSpec((B_CONV, B_CONV_CH), lambda i: (0, 0)),
                  pl.BlockSpec((1, LANES), lambda i: (0, 0)),
                  pl.BlockSpec((1, LANES), lambda i: (0, 0))],
        out_specs=[pl.BlockSpec((m, B_CONV_CH), lambda i: (0, 0)),
                   pl.BlockSpec((m, LANES), lambda i: (0, 0)),
                   pl.BlockSpec((m, LANES), lambda i: (0, 0))],
        out_shape=[jax.ShapeDtypeStruct((m, B_CONV_CH), F32),
                   jax.ShapeDtypeStruct((m, LANES), F32),
                   jax.ShapeDtypeStruct((m, LANES), F32)],
        compiler_params=_params("arbitrary"),
        name="mixb_sample_pre",
    )(proj, buf, bd, conv_w, nega, dtb)


def _mixb_sample_rec_kernel(kq_ref, v_ref, z_ref, beta_ref, g_ref, s0_ref, og_ref,
                            o_ref, s_out_ref, *, steps, pairs):
    og = og_ref[...]
    zpad = jnp.zeros((HEAD_DIM - 8, HEAD_DIM), F32)
    zrows = jnp.zeros((8 - steps, HEAD_DIM), F32)
    heads = range(B_V_HEADS)

    def body(bi, carry):
        kqs = [_mm(kq_ref[bi, h], s0_ref[bi, h]) for h in heads]
        pending = []
        for h in heads:
            kq = kq_ref[bi, h]
            g = g_ref[bi, h]
            beta = beta_ref[bi, h]
            v = v_ref[bi, h]
            gc = [g[0:1]]
            for t in range(1, steps):
                gc.append(gc[-1] + g[t:t + 1])
            k = [kq[t:t + 1] for t in range(steps)]
            q = [kq[4 + t:5 + t] for t in range(steps)]
            d = []
            for t in range(steps):
                acc = v[t:t + 1] - jnp.exp(gc[t]) * kqs[h][t:t + 1]
                for j in range(t):
                    kk = jnp.sum(k[j] * k[t], -1, keepdims=True)
                    acc = acc - jnp.exp(gc[t] - gc[j]) * kk * d[j]
                d.append(beta[t:t + 1] * acc)
            outs = []
            for t in range(steps):
                o = jnp.exp(gc[t]) * kqs[h][4 + t:5 + t]
                for j in range(t + 1):
                    qk = jnp.sum(k[j] * q[t], -1, keepdims=True)
                    o = o + jnp.exp(gc[t] - gc[j]) * qk * d[j]
                outs.append(o * lax.rsqrt(jnp.mean(o * o, -1, keepdims=True) + RMS_EPS) * og)
            o_ref[bi, h] = jnp.concatenate(outs + [zrows], axis=0) * jax.nn.silu(z_ref[bi, h])
            k_dec = jnp.concatenate([jnp.exp(gc[-1] - gc[j]) * k[j] for j in range(steps)] + [zrows], axis=0)
            k_pad = jnp.concatenate([k_dec, zpad], axis=0)
            d_pad = jnp.concatenate(d + [zrows, zpad], axis=0)
            pending.append((k_pad.T, d_pad, jnp.exp(gc[-1])))
        for h, (k_t, d_pad, decay_last) in zip(heads, pending):
            s_out_ref[bi, h] = s0_ref[bi, h] * decay_last + _mm(k_t, d_pad)
        return carry

    lax.fori_loop(0, pairs // B_V_HEADS, body, 0)


def _mixb_sample_rec(kq, v, z, beta, g, s0, og, steps, bb):
    nb = kq.shape[0]
    tile = pl.BlockSpec((bb, B_V_HEADS, 8, HEAD_DIM), lambda i: (i, 0, 0, 0))
    st = pl.BlockSpec((bb, B_V_HEADS, HEAD_DIM, HEAD_DIM), lambda i: (i, 0, 0, 0))
    kern = functools.partial(_mixb_sample_rec_kernel, steps=steps, pairs=bb * B_V_HEADS)
    return pl.pallas_call(
        kern,
        grid=(nb // bb,),
        in_specs=[tile, tile, tile, tile, tile, st, pl.BlockSpec((1, HEAD_DIM), lambda i: (0, 0))],
        out_specs=[tile, st],
        out_shape=[jax.ShapeDtypeStruct((nb, B_V_HEADS, 8, HEAD_DIM), F32),
                   jax.ShapeDtypeStruct((nb, B_V_HEADS, HEAD_DIM, HEAD_DIM), F32)],
        compiler_params=_params("arbitrary"),
        name="mixb_sample_rec",
    )(kq, v, z, beta, g, s0, og)


def _outproj_kernel(ap_ref, bp_ref, xp_ref, g1p_ref, sc2p_ref, sh2p_ref,
                    as_ref, bs_ref, xs_ref, g1s_ref, sc2s_ref, sh2s_ref,
                    w_ref, lg_ref, lb_ref, wr_ref, rb_ref, x1_ref, h2_ref, route_ref, count_ref, base_ref,
                    *, prompt_tiles):
    def tile(a_ref, b_ref, x_ref, g1_ref, sc2_ref, sh2_ref):
        mix = (jnp.dot(a_ref[...].astype(BF16), w_ref[:A_WIDTH, :], preferred_element_type=F32)
               + jnp.dot(b_ref[...].astype(BF16), w_ref[A_WIDTH:, :], preferred_element_type=F32))
        x1 = _layer_norm_rows(ALPHA * x_ref[...] + g1_ref[...] * mix, lg_ref[...], lb_ref[...])
        x1_ref[...] = x1
        h2 = x1 * (1.0 + sc2_ref[...]) + sh2_ref[...]
        h2_ref[...] = h2
        route_ref[...] = _route_tile(_mm3(h2, wr_ref[...]), rb_ref[...], base_ref)
        count_ref[...] = base_ref[...]

    @pl.when(pl.program_id(0) == 0)
    def _():
        base_ref[...] = jnp.zeros_like(base_ref)

    is_prompt = pl.program_id(0) < prompt_tiles

    @pl.when(is_prompt)
    def _():
        tile(ap_ref, bp_ref, xp_ref, g1p_ref, sc2p_ref, sh2p_ref)

    @pl.when(jnp.logical_not(is_prompt))
    def _():
        tile(as_ref, bs_ref, xs_ref, g1s_ref, sc2s_ref, sh2s_ref)


def _outproj(prompt, sample, mod_p, mod_s, w_out, lg, lb, w_router, r_bias, tm, seq):
    n_p = prompt[2].shape[0]
    n_s = sample[2].shape[0]
    pt = n_p // tm
    m = n_p + n_s
    tiles_per_seq = seq // tm
    p_row = lambda w: pl.BlockSpec((tm, w), lambda i: (jnp.minimum(i, pt - 1), 0))
    s_row = lambda w: pl.BlockSpec((tm, w), lambda i: (jnp.maximum(i - pt, 0), 0))
    p_mod = lambda col: pl.BlockSpec((None, 1, D_MODEL), lambda i: (jnp.minimum(i, pt - 1) // tiles_per_seq, 0, col))
    s_mod = lambda col: pl.BlockSpec((tm, D_MODEL), lambda i: (0, col))
    row = lambda w: pl.BlockSpec((tm, w), lambda i: (i, 0))
    full = lambda s: pl.BlockSpec(s, lambda i: (0, 0))
    return pl.pallas_call(
        functools.partial(_outproj_kernel, prompt_tiles=pt),
        grid=(m // tm,),
        in_specs=[p_row(A_WIDTH), p_row(B_VAL_WIDTH), p_row(D_MODEL), p_mod(2), p_mod(4), p_mod(3),
                  s_row(A_WIDTH), s_row(B_VAL_WIDTH), s_row(D_MODEL), s_mod(2), s_mod(4), s_mod(3),
                  full((D_MODEL, D_MODEL)), full((1, D_MODEL)), full((1, D_MODEL)),
                  full((D_MODEL, LANES)), full((1, LANES))],
        out_specs=[row(D_MODEL), row(D_MODEL), row(LANES),
                   full((1, LANES))],
        out_shape=[jax.ShapeDtypeStruct((m, D_MODEL), F32),
                   jax.ShapeDtypeStruct((m, D_MODEL), F32),
                   jax.ShapeDtypeStruct((m, LANES), F32),
                   jax.ShapeDtypeStruct((1, LANES), F32)],
        scratch_shapes=[pltpu.VMEM((1, LANES), F32)],
        compiler_params=_params("arbitrary"),
        name="outproj",
    )(*prompt, mod_p, mod_p, mod_p, *sample, mod_s, mod_s, mod_s, w_out, lg, lb, w_router, r_bias)


def _route_tile(lg, bias, base_ref):
    tm = lg.shape[0]
    lane = lax.broadcasted_iota(jnp.int32, lg.shape, 1)
    neg = -jnp.inf

    def first_argmax(score):
        mx = jnp.max(score, -1, keepdims=True)
        return jnp.min(jnp.where(score == mx, lane, LANES), -1, keepdims=True)

    def pick(vals, idx):
        return jnp.sum(jnp.where(lane == idx, vals, 0.0), -1, keepdims=True)

    gmask = lane < N_GROUPS
    mg = jnp.max(jnp.where(gmask, lg, neg), -1, keepdims=True)
    eg = jnp.where(gmask, jnp.exp(jnp.where(gmask, lg - mg, 0.0)), 0.0)
    pg = eg / jnp.sum(eg, -1, keepdims=True)
    sel_g = first_argmax(jnp.where(gmask, lg + bias, neg))
    p_sel = pick(pg, sel_g)

    lo = N_GROUPS + sel_g * EXPERTS_PER_GROUP
    emask = (lane >= lo) & (lane < lo + EXPERTS_PER_GROUP)
    me = jnp.max(jnp.where(emask, lg, neg), -1, keepdims=True)
    ee = jnp.where(emask, jnp.exp(jnp.where(emask, lg - me, 0.0)), 0.0)
    pe = ee / jnp.sum(ee, -1, keepdims=True)
    score = jnp.where(emask, pe + bias, neg)
    i1 = first_argmax(score)
    i2 = first_argmax(jnp.where(lane == i1, neg, score))
    w1 = pick(pe, i1)
    w2 = pick(pe, i2)
    wsum = w1 + w2
    gate1 = w1 / wsum * p_sel
    gate2 = w2 / wsum * p_sel

    hot = ((lane == i1) | (lane == i2)).astype(BF16)
    ri = lax.broadcasted_iota(jnp.int32, (tm, tm), 0)
    ci = lax.broadcasted_iota(jnp.int32, (tm, tm), 1)
    before = jnp.dot((ri > ci).astype(BF16), hot, preferred_element_type=F32) + base_ref[...]
    rank1 = pick(before, i1)
    rank2 = pick(before, i2)
    base_ref[...] = base_ref[...] + jnp.sum(hot.astype(F32), 0, keepdims=True)

    out = jnp.where(lane == 0, (i1 - N_GROUPS).astype(F32), 0.0)
    out = jnp.where(lane == 1, (i2 - N_GROUPS).astype(F32), out)
    out = jnp.where(lane == 2, gate1, out)
    out = jnp.where(lane == 3, gate2, out)
    out = jnp.where(lane == 4, rank1, out)
    out = jnp.where(lane == 5, rank2, out)
    return out


def _expert_kernel(be_ref, nv_ref, src_ref, dst_ref, first_ref, next_ref, wslot_ref,
                   h_ref, wg_ref, wu_ref, wd_ref, y_ref,
                   xbuf, ybuf, wg_st, wu_st, wd_st, wg_s, wu_s, wd_s, gsem, ssem, wsem):
    i = pl.program_id(0)
    nv = nv_ref[0]
    slot = i % 2
    other = 1 - slot

    def gather_row(blk, r, s):
        tok = src_ref[blk * MOE_ROWS + r]
        return pltpu.make_async_copy(h_ref.at[pl.ds(tok, 1)], xbuf.at[s, pl.ds(r, 1)], gsem.at[s])

    def scatter_row(blk, r, s):
        row = dst_ref[(blk + 1) * MOE_ROWS + r]
        return pltpu.make_async_copy(ybuf.at[s, pl.ds(r, 1)], y_ref.at[pl.ds(row, 1)], ssem.at[s])

    def gather_wait(s):
        pltpu.make_async_copy(h_ref.at[pl.ds(0, MOE_ROWS)], xbuf.at[s], gsem.at[s]).wait()

    def scatter_wait(s):
        pltpu.make_async_copy(ybuf.at[s], y_ref.at[pl.ds(0, MOE_ROWS)], ssem.at[s]).wait()

    @pl.when(i == 0)
    def _prologue():
        ybuf[...] = jnp.zeros_like(ybuf)
        base = y_ref.shape[0] - 2 * MOE_ROWS
        pltpu.make_async_copy(ybuf.at[0], y_ref.at[pl.ds(base, MOE_ROWS)], ssem.at[0]).start()

        def body(r, carry):
            gather_row(0, r, 0).start()
            return carry

        lax.fori_loop(0, MOE_ROWS, body, 0, unroll=8)

    def weight_copies(e, p):
        return [pltpu.make_async_copy(src.at[e], dst.at[p], wsem.at[p])
                for src, dst in ((wg_ref, wg_st), (wu_ref, wu_st), (wd_ref, wd_st))]

    @pl.when(i == 0)
    def _():
        for c in weight_copies(be_ref[0], 0):
            c.start()

    @pl.when((i < nv) & (first_ref[i] == 1))
    def _():
        p = wslot_ref[i]
        for c in weight_copies(be_ref[i], p):
            c.wait()
        wg_s[...] = wg_st[p].astype(BF16)
        wu_s[...] = wu_st[p].astype(BF16)
        wd_s[...] = wd_st[p].astype(BF16)

        @pl.when(next_ref[i] >= 0)
        def _():
            for c in weight_copies(next_ref[i], 1 - p):
                c.start()

    @pl.when(i < nv)
    def _block():
        gather_wait(slot)
        for r in range(MOE_ROWS):
            gather_row(i + 1, r, other).start()
            scatter_row(i - 1, r, other).start()
        x = xbuf[slot].astype(BF16)
        hg = jnp.dot(x, wg_s[...], preferred_element_type=F32)
        hu = jnp.dot(x, wu_s[...], preferred_element_type=F32)
        hid = (jax.nn.silu(hg) * hu).astype(BF16)
        y = jnp.dot(hid, wd_s[...], preferred_element_type=F32)
        scatter_wait(slot)
        ybuf[slot] = y

    @pl.when(i == nv - 1)
    def _tail():
        def body(r, carry):
            scatter_row(i, r, slot).start()
            return carry

        lax.fori_loop(0, MOE_ROWS, body, 0, unroll=8)
        scatter_wait(slot)
        scatter_wait(other)
        gather_wait(other)


def _experts(block_e, n_valid, slot_src, slot_dst, h2, w_gate, w_up, w_down, n_out_rows):
    n_blocks = block_e.shape[0]
    idx = jnp.arange(n_blocks, dtype=jnp.int32)
    first = (idx < n_valid[0]) & ((idx == 0) | (block_e != jnp.roll(block_e, 1)))
    wslot = (jnp.cumsum(first.astype(jnp.int32)) - 1) % 2
    first_at = jnp.where(first, idx, n_blocks)
    next_first = jnp.concatenate([lax.cummin(first_at, reverse=True)[1:], jnp.full((1,), n_blocks, jnp.int32)])
    next_e = jnp.where(next_first < n_blocks, block_e[jnp.minimum(next_first, n_blocks - 1)], -1)
    any_spec = pl.BlockSpec(memory_space=pl.ANY)
    return pl.pallas_call(
        _expert_kernel,
        grid_spec=pltpu.PrefetchScalarGridSpec(
            num_scalar_prefetch=7,
            grid=(n_blocks,),
            in_specs=[any_spec, any_spec, any_spec, any_spec],
            out_specs=any_spec,
            scratch_shapes=[pltpu.VMEM((2, MOE_ROWS, D_MODEL), F32),
                            pltpu.VMEM((2, MOE_ROWS, D_MODEL), F32),
                            pltpu.VMEM((2, D_MODEL, D_EXPERT), F32),
                            pltpu.VMEM((2, D_MODEL, D_EXPERT), F32),
                            pltpu.VMEM((2, D_EXPERT, D_MODEL), F32),
                            pltpu.VMEM((D_MODEL, D_EXPERT), BF16),
                            pltpu.VMEM((D_MODEL, D_EXPERT), BF16),
                            pltpu.VMEM((D_EXPERT, D_MODEL), BF16),
                            pltpu.SemaphoreType.DMA((2,)),
                            pltpu.SemaphoreType.DMA((2,)),
                            pltpu.SemaphoreType.DMA((2,))]),
        out_shape=jax.ShapeDtypeStruct((n_out_rows, D_MODEL), F32),
        compiler_params=_params("arbitrary"),
        name="experts",
    )(block_e, n_valid, slot_src, slot_dst, first.astype(jnp.int32), next_e.astype(jnp.int32),
      wslot.astype(jnp.int32), h2, w_gate, w_up, w_down)


def _final_kernel(x1_ref, y0_ref, y1_ref, route_ref, g2_ref, lg_ref, lb_ref, o_ref):
    route = route_ref[...]
    ff = y0_ref[...] * route[:, 2:3] + y1_ref[...] * route[:, 3:4]
    o_ref[...] = _layer_norm_rows(ALPHA * x1_ref[...] + g2_ref[...] * ff, lg_ref[...], lb_ref[...])


def _final(x1, y01, route, mod, lg, lb, tm, rows_per_mod, row0, n_rows, n_tok):
    off = row0 // tm
    off1 = (n_tok + row0) // tm
    return pl.pallas_call(
        _final_kernel,
        grid=(n_rows // tm,),
        in_specs=[pl.BlockSpec((tm, D_MODEL), lambda i: (off + i, 0)),
                  pl.BlockSpec((tm, D_MODEL), lambda i: (off + i, 0)),
                  pl.BlockSpec((tm, D_MODEL), lambda i: (off1 + i, 0)),
                  pl.BlockSpec((tm, LANES), lambda i: (off + i, 0)),
                  _mod_spec(mod, 5, tm, rows_per_mod),
                  pl.BlockSpec((1, D_MODEL), lambda i: (0, 0)),
                  pl.BlockSpec((1, D_MODEL), lambda i: (0, 0))],
        out_specs=pl.BlockSpec((tm, D_MODEL), lambda i: (i, 0)),
        out_shape=jax.ShapeDtypeStruct((n_rows, D_MODEL), F32),
        compiler_params=_params("arbitrary"),
        name="final",
    )(x1, y01, y01, route, mod, lg, lb)


def _lane_row(vec, offset):
    return jnp.zeros((1, LANES), F32).at[0, offset:offset + vec.shape[0]].set(vec.astype(F32))


def kernel(x_prompt, x_sample, state_conv, state_ssm, c_prompt, c_sample, w_ada, b_ada, w_in, a_ws, a_bs, a_norm_g, a_norm_b, b_conv_w, b_a_log, b_dt_bias, b_onorm_g, w_out, ln1_g, ln1_b, w_router_g, b_router_g, w_router_e, b_router_e, w_gate, w_up, w_down, ln2_g, ln2_b):
    batch, seq, d = x_prompt.shape
    nb, steps, _ = x_sample.shape
    n_p = batch * seq
    n_s = nb * steps
    n_tok = n_p + n_s
    l = 0

    c_rows = batch + nb
    c_pad = (-c_rows) % 8
    c_all = jnp.concatenate([c_prompt, c_sample, jnp.zeros((c_pad, d), F32)], axis=0)
    mod = _ada(c_all, w_ada[l], b_ada[l].reshape(1, -1))
    mod_p = mod[:batch].reshape(batch, 1, 6 * d)
    mod_s = mod[batch:batch + nb]

    w_in_t = jnp.swapaxes(w_in[l], 0, 1)
    w_bd = jnp.pad(w_in_t[MAIN_WIDTH:], ((0, LANES - 2 * B_V_HEADS), (0, 0))).astype(BF16)
    xp = x_prompt.reshape(n_p, d)
    xs = jnp.swapaxes(x_sample, 0, 1).reshape(n_s, d)
    proj_p, bd_p, proj_s, bd_s = _inproj(xp, xs, mod_p, mod_s, w_in_t, w_bd, 512, seq)

    ng = a_norm_g[l].reshape(1, -1)
    nbias = a_norm_b[l].reshape(1, -1)
    bias_tile = jnp.repeat(a_bs[l].T, HEAD_DIM, axis=1)
    a_out_p = _mixa_prompt(proj_p, a_ws[l], bias_tile, ng, nbias, 256)
    coef = jnp.repeat(jnp.transpose(a_ws[l][:, :steps, :steps], (1, 2, 0)).reshape(steps * steps, A_HEADS),
                      HEAD_DIM, axis=1)
    a_out_s, chunkv = _mixa_sample(proj_s, coef, bias_tile[:steps], ng, nbias, steps, nb)

    nega = _lane_row(-jnp.exp(b_a_log[l].astype(F32)), B_V_HEADS)
    dtb = _lane_row(b_dt_bias[l], B_V_HEADS)
    og = b_onorm_g[l].reshape(1, -1)
    b_out_p, ssm_p = _mixb_prompt(proj_p, bd_p, b_conv_w[l], nega, dtb, og, batch, seq)
    qkv0 = 2 * A_WIDTH
    conv_p = proj_p.reshape(batch, seq, MAIN_WIDTH)[:, seq - (B_CONV - 1):, qkv0:qkv0 + B_CONV_CH]

    buf_s = jnp.swapaxes(state_conv[l], 0, 1).reshape((B_CONV - 1) * nb, B_CONV_CH)
    act_s, beta_s, g_s = _mixb_sample_pre(proj_s, buf_s, bd_s, b_conv_w[l], nega, dtb, steps, nb)
    rep = B_V_HEADS // B_QK_HEADS
    act4 = act_s.reshape(steps, nb, B_CONV_CH)
    q8 = jnp.repeat(act4[..., :B_KEY_WIDTH].reshape(steps, nb, B_QK_HEADS, HEAD_DIM), rep, axis=2)
    k8 = jnp.repeat(act4[..., B_KEY_WIDTH:2 * B_KEY_WIDTH].reshape(steps, nb, B_QK_HEADS, HEAD_DIM), rep, axis=2)

    def to_tiles(a):
        a = jnp.transpose(a, (1, 2, 0, 3))
        return jnp.pad(a, ((0, 0), (0, 0), (0, 8 - a.shape[2]), (0, 0)))

    kq_t = jnp.concatenate([to_tiles(k8)[:, :, :4], to_tiles(q8)[:, :, :4]], axis=2)
    v_t = to_tiles(act4[..., 2 * B_KEY_WIDTH:].reshape(steps, nb, B_V_HEADS, HEAD_DIM))
    z_t = to_tiles(proj_s[:, qkv0 + B_CONV_CH:].reshape(steps, nb, B_V_HEADS, HEAD_DIM))

    def gate_tiles(a, off):
        a = a[:, off:off + B_V_HEADS].reshape(steps, nb, B_V_HEADS, 1)
        return to_tiles(jnp.broadcast_to(a, (steps, nb, B_V_HEADS, HEAD_DIM)))

    o_t, ssm_s = _mixb_sample_rec(kq_t, v_t, z_t, gate_tiles(beta_s, 0), gate_tiles(g_s, B_V_HEADS),
                                  state_ssm[l], og, steps, 8)
    b_out_s = jnp.transpose(o_t[:, :, :steps], (2, 0, 1, 3)).reshape(n_s, B_VAL_WIDTH)
    conv_s = jnp.swapaxes(proj_s.reshape(steps, nb, MAIN_WIDTH)[steps - (B_CONV - 1):, :, qkv0:qkv0 + B_CONV_CH], 0, 1)

    w_out_b = w_out[l].astype(BF16)
    lg1 = ln1_g[l].reshape(1, -1)
    lb1 = ln1_b[l].reshape(1, -1)
    w_router = jnp.concatenate([w_router_g[l], w_router_e[l],
                                jnp.zeros((d, LANES - N_GROUPS - N_EXPERTS), F32)], axis=1)
    out_tm = 256
    mod_s_tile = jnp.tile(mod_s, (out_tm // nb, 1))
    r_bias = jnp.concatenate([b_router_g[l], b_router_e[l],
                              jnp.zeros((LANES - N_GROUPS - N_EXPERTS,), F32)]).reshape(1, LANES)
    x1, h2, route, counts_row = _outproj((a_out_p, b_out_p, xp), (a_out_s, b_out_s, xs), mod_p, mod_s_tile,
                                         w_out_b, lg1, lb1, w_router, r_bias, out_tm, seq)

    counts = counts_row[0, N_GROUPS:N_GROUPS + N_EXPERTS].astype(jnp.int32)
    route_t = route[:, :8].T
    eid = route_t[0:2].astype(jnp.int32)
    rank = route_t[4:6].astype(jnp.int32)
    padded = (counts + MOE_ROWS - 1) // MOE_ROWS * MOE_ROWS
    pend = jnp.cumsum(padded)
    pstart = pend - padded
    dest = rank + jnp.sum(jnp.where(eid[None] == jnp.arange(N_EXPERTS, dtype=jnp.int32)[:, None, None],
                                    pstart[:, None, None], 0), axis=0)
    n_blocks = -(-(2 * n_tok) // MOE_ROWS) + N_EXPERTS
    n_slots = n_blocks * MOE_ROWS
    slot_row = jnp.full((n_slots,), -1, jnp.int32).at[dest.reshape(-1)].set(jnp.arange(2 * n_tok, dtype=jnp.int32))
    slot_id = jnp.arange(n_slots, dtype=jnp.int32)
    trash = 2 * n_tok + (slot_id // MOE_ROWS % 2) * MOE_ROWS + slot_id % MOE_ROWS
    row_id = jnp.arange(MOE_ROWS, dtype=jnp.int32)
    slot_src = jnp.concatenate([jnp.where(slot_row < 0, 0, slot_row % n_tok),
                                jnp.zeros((MOE_ROWS,), jnp.int32)])
    slot_dst = jnp.concatenate([2 * n_tok + MOE_ROWS + row_id,
                                jnp.where(slot_row < 0, trash, slot_row)])
    blk0 = jnp.arange(n_blocks, dtype=jnp.int32) * MOE_ROWS
    block_e = jnp.minimum(jnp.sum(pend[None, :] <= blk0[:, None], axis=1), N_EXPERTS - 1).astype(jnp.int32)
    n_valid = (pend[-1:] // MOE_ROWS).astype(jnp.int32)

    y01 = _experts(block_e, n_valid, slot_src, slot_dst, h2, w_gate[l], w_up[l], w_down[l],
                   2 * n_tok + 2 * MOE_ROWS)

    lg2 = ln2_g[l].reshape(1, -1)
    lb2 = ln2_b[l].reshape(1, -1)
    y_p = _final(x1, y01, route, mod_p, lg2, lb2, 256, seq, 0, n_p, n_tok)
    y_s = _final(x1, y01, route, mod_s, lg2, lb2, nb, 0, n_p, n_s, n_tok)

    y_prompt = y_p.reshape(batch, seq, d)
    y_sample = jnp.swapaxes(y_s.reshape(steps, nb, d), 0, 1)
    chunkv_s = jnp.swapaxes(chunkv.reshape(steps, nb, A_HEADS, HEAD_DIM), 0, 1)
    return (y_prompt, y_sample, conv_p[None], ssm_p[None], conv_s[None], ssm_s[None], chunkv_s[None])
```

```python
import functools

import jax
import jax.numpy as jnp
from jax import lax
from jax.experimental import pallas as pl
from jax.experimental.pallas import tpu as pltpu

F32 = jnp.float32
BF16 = jnp.bfloat16

D_MODEL = 2048
DEPTH = 1
A_HEADS = 8
HEAD_DIM = 128
A_WIDTH = 1024
A_CHUNK = 128
B_QK_HEADS = 4
B_V_HEADS = 8
B_KEY_WIDTH = 512
B_VAL_WIDTH = 1024
B_CONV = 4
B_CONV_CH = 2048
DN_CHUNK = 64
MAIN_WIDTH = 2 * A_WIDTH + B_CONV_CH + B_VAL_WIDTH
N_GROUPS = 4
EXPERTS_PER_GROUP = 8
N_EXPERTS = 32
D_EXPERT = 512
ALPHA = (2 * DEPTH) ** 0.25
LN_EPS = 1e-5
RMS_EPS = 1e-6
L2_EPS = 1e-6

LANES = 128
VMEM_LIMIT = 56 * 1024 * 1024
MOE_ROWS = 256
MIXB_GROUP = 4


def _params(*sem):
    return pltpu.CompilerParams(dimension_semantics=sem, vmem_limit_bytes=VMEM_LIMIT)


def _mm(a, b):
    return jnp.dot(a.astype(BF16), b.astype(BF16), preferred_element_type=F32)


def _mm_nt(a, b):
    return lax.dot_general(a.astype(BF16), b.astype(BF16), (((1,), (1,)), ((), ())),
                           preferred_element_type=F32)


def _mm_tn(a, b):
    return lax.dot_general(a.astype(BF16), b.astype(BF16), (((0,), (0,)), ((), ())),
                           preferred_element_type=F32)


def _split(x):
    hi = x.astype(BF16)
    lo = (x - hi.astype(F32)).astype(BF16)
    return hi, lo


def _mm_exact_lhs(a_bf16, b):
    hi, lo = _split(b)
    return (jnp.dot(a_bf16, hi, preferred_element_type=F32)
            + jnp.dot(a_bf16, lo, preferred_element_type=F32))


def _mm3(a, b):
    ah, al = _split(a)
    bh, bl = _split(b)
    return (jnp.dot(ah, bh, preferred_element_type=F32) + jnp.dot(ah, bl, preferred_element_type=F32)
            + jnp.dot(al, bh, preferred_element_type=F32))


def _softplus(x):
    return jnp.maximum(x, 0.0) + jnp.log1p(jnp.exp(-jnp.abs(x)))


def _layer_norm_rows(x, g, b):
    mu = jnp.mean(x, -1, keepdims=True)
    xc = x - mu
    var = jnp.mean(xc * xc, -1, keepdims=True)
    return xc * lax.rsqrt(var + LN_EPS) * g + b


def _ada_kernel(c_ref, w_ref, b_ref, o_ref):
    a = jax.nn.silu(c_ref[...]).astype(BF16)
    o_ref[...] = jnp.dot(a, w_ref[...].astype(BF16), preferred_element_type=F32) + b_ref[...]


def _ada(c_all, w_ada, b_ada):
    rows = c_all.shape[0]
    tn = 1024
    return pl.pallas_call(
        _ada_kernel,
        grid=(6 * D_MODEL // tn,),
        in_specs=[pl.BlockSpec((rows, D_MODEL), lambda j: (0, 0)),
                  pl.BlockSpec((D_MODEL, tn), lambda j: (0, j)),
                  pl.BlockSpec((1, tn), lambda j: (0, j))],
        out_specs=pl.BlockSpec((rows, tn), lambda j: (0, j)),
        out_shape=jax.ShapeDtypeStruct((rows, 6 * D_MODEL), F32),
        compiler_params=_params("arbitrary"),
        name="ada",
    )(c_all, w_ada, b_ada)


def _inproj_kernel(xp_ref, scp_ref, shp_ref, xs_ref, scs_ref, shs_ref, w_ref, wbd_ref,
                   op_ref, bdp_ref, os_ref, bds_ref, wb_ref, *, prompt_tiles):
    j = pl.program_id(0)
    i = pl.program_id(1)

    @pl.when(i == 0)
    def _():
        wb_ref[...] = w_ref[...].astype(BF16)

    def project(h, o_ref, bd_ref):
        o_ref[...] = _mm_nt(h, wb_ref[...])

        @pl.when(j == 0)
        def _():
            bd_ref[0] = _mm_nt(h, wbd_ref[...])

        @pl.when(j != 0)
        def _():
            bd_ref[0] = jnp.zeros(bd_ref.shape[1:], F32)

    @pl.when(i < prompt_tiles)
    def _():
        project((xp_ref[...] * (1.0 + scp_ref[...]) + shp_ref[...]).astype(BF16), op_ref, bdp_ref)

    @pl.when(i >= prompt_tiles)
    def _():
        n_s, nb = xs_ref.shape[0], scs_ref.shape[0]
        x = xs_ref[...].reshape(n_s // nb, nb, D_MODEL)
        h = x * (1.0 + scs_ref[...])[None] + shs_ref[...][None]
        project(h.reshape(n_s, D_MODEL).astype(BF16), os_ref, bds_ref)


def _mod_spec(mod, col, tm, rows_per_mod, axis=0):
    if rows_per_mod:
        tiles = rows_per_mod // tm
        return pl.BlockSpec((None, 1, D_MODEL), lambda *g: (g[axis] // tiles, 0, col))
    return pl.BlockSpec((tm, D_MODEL), lambda *g: (0, col))


def _inproj(xp, xs, mod_p, mod_s, w_in_t, w_bd, tm, seq):
    n_p, n_s = xp.shape[0], xs.shape[0]
    nb = mod_s.shape[0]
    tn = 1024
    nj = MAIN_WIDTH // tn
    pt = n_p // tm
    tiles_per_seq = seq // tm
    pi = lambda i: jnp.minimum(i, pt - 1)
    p_mod = lambda col: pl.BlockSpec((None, 1, D_MODEL), lambda j, i: (pi(i) // tiles_per_seq, 0, col))
    s_mod = lambda col: pl.BlockSpec((nb, D_MODEL), lambda j, i: (0, col))
    proj_p, bd_p, proj_s, bd_s = pl.pallas_call(
        functools.partial(_inproj_kernel, prompt_tiles=pt),
        grid=(nj, pt + 1),
        in_specs=[pl.BlockSpec((tm, D_MODEL), lambda j, i: (pi(i), 0)), p_mod(1), p_mod(0),
                  pl.BlockSpec((n_s, D_MODEL), lambda j, i: (0, 0)), s_mod(1), s_mod(0),
                  pl.BlockSpec((tn, D_MODEL), lambda j, i: (j, 0)),
                  pl.BlockSpec((LANES, D_MODEL), lambda j, i: (0, 0))],
        out_specs=[pl.BlockSpec((tm, tn), lambda j, i: (pi(i), j)),
                   pl.BlockSpec((1, tm, LANES), lambda j, i: (j, pi(i), 0)),
                   pl.BlockSpec((n_s, tn), lambda j, i: (0, j)),
                   pl.BlockSpec((1, n_s, LANES), lambda j, i: (j, 0, 0))],
        out_shape=[jax.ShapeDtypeStruct((n_p, MAIN_WIDTH), F32),
                   jax.ShapeDtypeStruct((nj, n_p, LANES), F32),
                   jax.ShapeDtypeStruct((n_s, MAIN_WIDTH), F32),
                   jax.ShapeDtypeStruct((nj, n_s, LANES), F32)],
        scratch_shapes=[pltpu.VMEM((tn, D_MODEL), BF16)],
        compiler_params=_params("arbitrary", "arbitrary"),
        name="inproj",
    )(xp, mod_p, mod_p, xs, mod_s, mod_s, w_in_t, w_bd)
    return proj_p, bd_p[0], proj_s, bd_s[0]


def _mixa_prompt_kernel(p_ref, ws_ref, bias_ref, ng_ref, nb_ref, o_ref):
    rows = p_ref.shape[0]
    u = jax.nn.gelu(p_ref[:, :A_WIDTH])
    v = _layer_norm_rows(jax.nn.gelu(p_ref[:, A_WIDTH:]), ng_ref[...], nb_ref[...])
    ri = lax.broadcasted_iota(jnp.int32, (A_CHUNK, A_CHUNK), 0)
    ci = lax.broadcasted_iota(jnp.int32, (A_CHUNK, A_CHUNK), 1)
    for h in range(A_HEADS):
        w = jnp.where(ri >= ci, ws_ref[h], 0.0).astype(BF16)
        cols = slice(h * HEAD_DIM, (h + 1) * HEAD_DIM)
        for c in range(rows // A_CHUNK):
            rs = slice(c * A_CHUNK, (c + 1) * A_CHUNK)
            s = jnp.dot(w, v[rs, cols].astype(BF16), preferred_element_type=F32) + bias_ref[:, cols]
            o_ref[rs, cols] = u[rs, cols] * s


def _mixa_prompt(proj, a_ws, bias_tile, ng, nb, tm):
    m = proj.shape[0]
    return pl.pallas_call(
        _mixa_prompt_kernel,
        grid=(m // tm,),
        in_specs=[pl.BlockSpec((tm, 2 * A_WIDTH), lambda i: (i, 0)),
                  pl.BlockSpec((A_HEADS, A_CHUNK, A_CHUNK), lambda i: (0, 0, 0)),
                  pl.BlockSpec((A_CHUNK, A_WIDTH), lambda i: (0, 0)),
                  pl.BlockSpec((1, A_WIDTH), lambda i: (0, 0)),
                  pl.BlockSpec((1, A_WIDTH), lambda i: (0, 0))],
        out_specs=pl.BlockSpec((tm, A_WIDTH), lambda i: (i, 0)),
        out_shape=jax.ShapeDtypeStruct((m, A_WIDTH), F32),
        compiler_params=_params("arbitrary"),
        name="mixa_prompt",
    )(proj, a_ws, bias_tile, ng, nb)


def _mixa_sample_kernel(p_ref, coef_ref, bias_ref, ng_ref, nb_ref, o_ref, v_ref, *, steps, nb_rows):
    u = jax.nn.gelu(p_ref[:, :A_WIDTH])
    v = _layer_norm_rows(jax.nn.gelu(p_ref[:, A_WIDTH:]), ng_ref[...], nb_ref[...])
    v_ref[...] = v
    for t in range(steps):
        s = bias_ref[t:t + 1, :]
        for j in range(t + 1):
            s = s + coef_ref[t * steps + j:t * steps + j + 1, :] * v[j * nb_rows:(j + 1) * nb_rows, :]
        rs = slice(t * nb_rows, (t + 1) * nb_rows)
        o_ref[rs, :] = u[rs, :] * s


def _mixa_sample(proj, coef, bias, ng, nb, steps, nb_rows):
    m = proj.shape[0]
    kern = functools.partial(_mixa_sample_kernel, steps=steps, nb_rows=nb_rows)
    return pl.pallas_call(
        kern,
        grid=(1,),
        in_specs=[pl.BlockSpec((m, 2 * A_WIDTH), lambda i: (0, 0)),
                  pl.BlockSpec(coef.shape, lambda i: (0, 0)),
                  pl.BlockSpec(bias.shape, lambda i: (0, 0)),
                  pl.BlockSpec((1, A_WIDTH), lambda i: (0, 0)),
                  pl.BlockSpec((1, A_WIDTH), lambda i: (0, 0))],
        out_specs=[pl.BlockSpec((m, A_WIDTH), lambda i: (0, 0)),
                   pl.BlockSpec((m, A_WIDTH), lambda i: (0, 0))],
        out_shape=[jax.ShapeDtypeStruct((m, A_WIDTH), F32),
                   jax.ShapeDtypeStruct((m, A_WIDTH), F32)],
        compiler_params=_params("arbitrary"),
        name="mixa_sample",
    )(proj, coef, bias, ng, nb)


def _unit_lower_inverse_many(a_list, ri, ci, block):
    eye = (ri == ci).astype(F32)
    pair = (lax.shift_right_logical(ri, 1) == lax.shift_right_logical(ci, 1)) & ((ri & 1) == 1) & ((ci & 1) == 0)
    ts = [eye - jnp.where(pair, a, 0.0) for a in a_list]
    n = 2
    while n < block:
        sh = n.bit_length()
        m = ((lax.shift_right_logical(ri, sh) == lax.shift_right_logical(ci, sh))
             & ((ri & n) != 0) & ((ci & n) == 0))
        xs = [_mm(t, jnp.where(m, a, 0.0)) for t, a in zip(ts, a_list)]
        ts = [t - _mm(x, t) for t, x in zip(ts, xs)]
        n *= 2
    return ts


def _mixb_prompt_kernel(qkv_ref, z_ref, bd_ref, cw_ref, nega_ref, dtb_ref, og_ref,
                        o_ref, sfin_ref, s_ref, cbuf_ref):
    c = pl.program_id(0)
    C = DN_CHUNK
    nseq = qkv_ref.shape[0]

    @pl.when(c == 0)
    def _init():
        s_ref[...] = jnp.zeros_like(s_ref)
        cbuf_ref[:, 0:8, :] = jnp.zeros((nseq, 8, B_CONV_CH), F32)

    cw = cw_ref[...]
    og = og_ref[...]
    ri64 = lax.broadcasted_iota(jnp.int32, (C, C), 0)
    ci64 = lax.broadcasted_iota(jnp.int32, (C, C), 1)
    cum_lhs = (ri64 >= ci64).astype(BF16)

    R = MIXB_GROUP * C
    ri = lax.broadcasted_iota(jnp.int32, (R, R), 0)
    ci = lax.broadcasted_iota(jnp.int32, (R, R), 1)
    same = lax.shift_right_logical(ri, C.bit_length() - 1) == lax.shift_right_logical(ci, C.bit_length() - 1)
    tril = same & (ri >= ci)
    strict = same & (ri > ci)
    rep = B_V_HEADS // B_QK_HEADS

    def lane_col(a, lane):
        return jnp.broadcast_to(a[:, lane:lane + 1], (a.shape[0], HEAD_DIM))

    groups = B_V_HEADS // MIXB_GROUP
    units = [(b, grp) for b in range(nseq) for grp in range(groups)]
    heads_of = lambda grp: list(range(grp * MIXB_GROUP, (grp + 1) * MIXB_GROUP))

    acts, gcs, betas = [], [], []
    for b in range(nseq):
        x = qkv_ref[b]
        cbuf_ref[b, 8:8 + C, :] = x
        y = cbuf_ref[b, 5:5 + C, :] * cw[0:1]
        y = y + cbuf_ref[b, 6:6 + C, :] * cw[1:2]
        y = y + cbuf_ref[b, 7:7 + C, :] * cw[2:3]
        y = y + x * cw[3:4]
        cbuf_ref[b, 0:8, :] = x[C - 8:C, :]
        acts.append(jax.nn.silu(y))
        bd = bd_ref[b]
        betas.append(jax.nn.sigmoid(bd))
        g_all = nega_ref[...] * _softplus(bd + dtb_ref[...])
        gcs.append(_mm_exact_lhs(cum_lhs, g_all))

    kst, qst, gcol, glcol, bcol, decay, a_mat, rhs = {}, {}, {}, {}, {}, {}, {}, {}
    for u in units:
        b, grp = u
        act, gc_all, beta_all = acts[b], gcs[b], betas[b]

        def stack(fn):
            return jnp.concatenate([fn(h) for h in heads_of(grp)], axis=0)

        def l2n(cols0, h):
            s = act[:, cols0 + (h // rep) * HEAD_DIM:cols0 + (h // rep + 1) * HEAD_DIM]
            return s * lax.rsqrt(jnp.sum(s * s, -1, keepdims=True) + L2_EPS)

        kst[u] = stack(lambda h: l2n(B_KEY_WIDTH, h))
        qst[u] = stack(lambda h: l2n(0, h) * (HEAD_DIM ** -0.5))
        vst = stack(lambda h: act[:, 2 * B_KEY_WIDTH + h * HEAD_DIM:2 * B_KEY_WIDTH + (h + 1) * HEAD_DIM])
        gcol[u] = stack(lambda h: lane_col(gc_all, B_V_HEADS + h))
        glcol[u] = stack(lambda h: jnp.broadcast_to(gc_all[C - 1:C, B_V_HEADS + h:B_V_HEADS + h + 1], (C, HEAD_DIM)))
        bcol[u] = stack(lambda h: lane_col(beta_all, h))
        grow = gcol[u].T[0:1, :]
        diff = jnp.concatenate([gcol[u], gcol[u]], axis=1) - grow
        decay[u] = jnp.where(tril, jnp.exp(jnp.where(tril, diff, 0.0)), 0.0)
        rhs[u] = jnp.concatenate([vst * bcol[u], kst[u] * bcol[u] * jnp.exp(gcol[u])], axis=1)
    for u in units:
        a_mat[u] = jnp.where(strict, jnp.concatenate([bcol[u], bcol[u]], axis=1) * _mm_nt(kst[u], kst[u]) * decay[u], 0.0)

    t_inv = _unit_lower_inverse_many([a_mat[u] for u in units], ri, ci, C)
    sol = [_mm(t, rhs[u]) for t, u in zip(t_inv, units)]
    qk = [_mm_nt(qst[u], kst[u]) * decay[u] for u in units]
    ws = []
    for sl, u in zip(sol, units):
        b, grp = u
        q_dec = qst[u] * jnp.exp(gcol[u])
        ws.append([_mm(jnp.concatenate([sl[i * C:(i + 1) * C, HEAD_DIM:], q_dec[i * C:(i + 1) * C]], axis=0),
                       s_ref[b * B_V_HEADS + h]) for i, h in enumerate(heads_of(grp))])
    v_new = [sl[:, :HEAD_DIM] - jnp.concatenate([w[:C] for w in wl], axis=0) for sl, wl in zip(sol, ws)]
    outs = [jnp.concatenate([w[C:] for w in wl], axis=0) + _mm(q, v) for wl, q, v in zip(ws, qk, v_new)]
    for u, v, o in zip(units, v_new, outs):
        b, grp = u
        k_dec = kst[u] * jnp.exp(glcol[u] - gcol[u])
        on = o * lax.rsqrt(jnp.mean(o * o, -1, keepdims=True) + RMS_EPS) * og
        for i, h in enumerate(heads_of(grp)):
            rs = slice(i * C, (i + 1) * C)
            si = b * B_V_HEADS + h
            s_ref[si] = s_ref[si] * jnp.exp(glcol[u][i * C:i * C + 1, :]) + _mm_tn(k_dec[rs], v[rs])
            cols = slice(h * HEAD_DIM, (h + 1) * HEAD_DIM)
            o_ref[b, :, cols] = on[rs] * jax.nn.silu(z_ref[b, :, cols])

    @pl.when(c == pl.num_programs(0) - 1)
    def _fin():
        sfin_ref[...] = s_ref[...]


def _mixb_prompt(proj, bd, conv_w, nega, dtb, og, batch, seq):
    nc = seq // DN_CHUNK
    qkv_blk = 2 * A_WIDTH // B_CONV_CH
    z_blk = (2 * A_WIDTH + B_CONV_CH) // B_VAL_WIDTH
    proj3 = proj.reshape(batch, seq, proj.shape[-1])
    bd3 = bd.reshape(batch, seq, LANES)
    o, s_fin = pl.pallas_call(
        _mixb_prompt_kernel,
        grid=(nc,),
        in_specs=[pl.BlockSpec((batch, DN_CHUNK, B_CONV_CH), lambda c: (0, c, qkv_blk)),
                  pl.BlockSpec((batch, DN_CHUNK, B_VAL_WIDTH), lambda c: (0, c, z_blk)),
                  pl.BlockSpec((batch, DN_CHUNK, LANES), lambda c: (0, c, 0)),
                  pl.BlockSpec((B_CONV, B_CONV_CH), lambda c: (0, 0)),
                  pl.BlockSpec((1, LANES), lambda c: (0, 0)),
                  pl.BlockSpec((1, LANES), lambda c: (0, 0)),
                  pl.BlockSpec((1, HEAD_DIM), lambda c: (0, 0))],
        out_specs=[pl.BlockSpec((batch, DN_CHUNK, B_VAL_WIDTH), lambda c: (0, c, 0)),
                   pl.BlockSpec((batch * B_V_HEADS, HEAD_DIM, HEAD_DIM), lambda c: (0, 0, 0))],
        out_shape=[jax.ShapeDtypeStruct((batch, seq, B_VAL_WIDTH), F32),
                   jax.ShapeDtypeStruct((batch * B_V_HEADS, HEAD_DIM, HEAD_DIM), F32)],
        scratch_shapes=[pltpu.VMEM((batch * B_V_HEADS, HEAD_DIM, HEAD_DIM), F32),
                        pltpu.VMEM((batch, DN_CHUNK + 8, B_CONV_CH), F32)],
        compiler_params=_params("arbitrary"),
        name="mixb_prompt",
    )(proj3, proj3, bd3, conv_w, nega, dtb, og)
    return (o.reshape(batch * seq, B_VAL_WIDTH),
            s_fin.reshape(batch, B_V_HEADS, HEAD_DIM, HEAD_DIM))


def _mixb_sample_pre_kernel(qkv_ref, buf_ref, bd_ref, cw_ref, nega_ref, dtb_ref,
                            act_ref, beta_ref, g_ref, *, steps, nb_rows):
    cw = cw_ref[...]

    def slab(j):
        if j < B_CONV - 1:
            return buf_ref[j * nb_rows:(j + 1) * nb_rows, :]
        jj = j - (B_CONV - 1)
        return qkv_ref[jj * nb_rows:(jj + 1) * nb_rows, :]

    for t in range(steps):
        y = slab(t) * cw[0:1]
        for i in range(1, B_CONV):
            y = y + slab(t + i) * cw[i:i + 1]
        act = jax.nn.silu(y)
        rs = slice(t * nb_rows, (t + 1) * nb_rows)
        for qh in range(B_QK_HEADS):
            cq = slice(qh * HEAD_DIM, (qh + 1) * HEAD_DIM)
            ck = slice(B_KEY_WIDTH + qh * HEAD_DIM, B_KEY_WIDTH + (qh + 1) * HEAD_DIM)
            qs = act[:, cq]
            ks = act[:, ck]
            act_ref[rs, cq] = qs * lax.rsqrt(jnp.sum(qs * qs, -1, keepdims=True) + L2_EPS) * (HEAD_DIM ** -0.5)
            act_ref[rs, ck] = ks * lax.rsqrt(jnp.sum(ks * ks, -1, keepdims=True) + L2_EPS)
        act_ref[rs, 2 * B_KEY_WIDTH:] = act[:, 2 * B_KEY_WIDTH:]
    bd = bd_ref[...]
    beta_ref[...] = jax.nn.sigmoid(bd)
    g_ref[...] = nega_ref[...] * _softplus(bd + dtb_ref[...])


def _mixb_sample_pre(proj, buf, bd, conv_w, nega, dtb, steps, nb_rows):
    m = proj.shape[0]
    qkv_blk = 2 * A_WIDTH // B_CONV_CH
    kern = functools.partial(_mixb_sample_pre_kernel, steps=steps, nb_rows=nb_rows)
    return pl.pallas_call(
        kern,
        grid=(1,),
        in_specs=[pl.BlockSpec((m, B_CONV_CH), lambda i: (0, qkv_blk)),
                  pl.BlockSpec(buf.shape, lambda i: (0, 0)),
                  pl.BlockSpec((m, LANES), lambda i: (0, 0)),
                  pl.BlockSpec((B_CONV, B_CONV_CH), lambda i: (0, 0)),
                  pl.BlockSpec((1, LANES), lambda i: (0, 0)),
                  pl.BlockSpec((1, LANES), lambda i: (0, 0))],
        out_specs=[pl.BlockSpec((m, B_CONV_CH), lambda i: (0, 0)),
                   pl.BlockSpec((m, LANES), lambda i: (0, 0)),
                   pl.BlockSpec((m, LANES), lambda i: (0, 0))],
        out_shape=[jax.ShapeDtypeStruct((m, B_CONV_CH), F32),
                   jax.ShapeDtypeStruct((m, LANES), F32),
                   jax.ShapeDtypeStruct((m, LANES), F32)],
        compiler_params=_params("arbitrary"),
        name="mixb_sample_pre",
    )(proj, buf, bd, conv_w, nega, dtb)


def _mixb_sample_rec_kernel(kq_ref, v_ref, z_ref, beta_ref, g_ref, s0_ref, og_ref,
                            o_ref, s_out_ref, *, steps, pairs):
    og = og_ref[...]
    zpad = jnp.zeros((HEAD_DIM - 8, HEAD_DIM), F32)
    zrows = jnp.zeros((8 - steps, HEAD_DIM), F32)
    heads = range(B_V_HEADS)

    def body(bi, carry):
        kqs = [_mm(kq_ref[bi, h], s0_ref[bi, h]) for h in heads]
        pending = []
        for h in heads:
            kq = kq_ref[bi, h]
            g = g_ref[bi, h]
            beta = beta_ref[bi, h]
            v = v_ref[bi, h]
            gc = [g[0:1]]
            for t in range(1, steps):
                gc.append(gc[-1] + g[t:t + 1])
            k = [kq[t:t + 1] for t in range(steps)]
            q = [kq[4 + t:5 + t] for t in range(steps)]
            d = []
            for t in range(steps):
                acc = v[t:t + 1] - jnp.exp(gc[t]) * kqs[h][t:t + 1]
                for j in range(t):
                    kk = jnp.sum(k[j] * k[t], -1, keepdims=True)
                    acc = acc - jnp.exp(gc[t] - gc[j]) * kk * d[j]
                d.append(beta[t:t + 1] * acc)
            outs = []
            for t in range(steps):
                o = jnp.exp(gc[t]) * kqs[h][4 + t:5 + t]
                for j in range(t + 1):
                    qk = jnp.sum(k[j] * q[t], -1, keepdims=True)
                    o = o + jnp.exp(gc[t] - gc[j]) * qk * d[j]
                outs.append(o * lax.rsqrt(jnp.mean(o * o, -1, keepdims=True) + RMS_EPS) * og)
            o_ref[bi, h] = jnp.concatenate(outs + [zrows], axis=0) * jax.nn.silu(z_ref[bi, h])
            k_dec = jnp.concatenate([jnp.exp(gc[-1] - gc[j]) * k[j] for j in range(steps)] + [zrows], axis=0)
            k_pad = jnp.concatenate([k_dec, zpad], axis=0)
            d_pad = jnp.concatenate(d + [zrows, zpad], axis=0)
            pending.append((k_pad.T, d_pad, jnp.exp(gc[-1])))
        for h, (k_t, d_pad, decay_last) in zip(heads, pending):
            s_out_ref[bi, h] = s0_ref[bi, h] * decay_last + _mm(k_t, d_pad)
        return carry

    lax.fori_loop(0, pairs // B_V_HEADS, body, 0)


def _mixb_sample_rec(kq, v, z, beta, g, s0, og, steps, bb):
    nb = kq.shape[0]
    tile = pl.BlockSpec((bb, B_V_HEADS, 8, HEAD_DIM), lambda i: (i, 0, 0, 0))
    st = pl.BlockSpec((bb, B_V_HEADS, HEAD_DIM, HEAD_DIM), lambda i: (i, 0, 0, 0))
    kern = functools.partial(_mixb_sample_rec_kernel, steps=steps, pairs=bb * B_V_HEADS)
    return pl.pallas_call(
        kern,
        grid=(nb // bb,),
        in_specs=[tile, tile, tile, tile, tile, st, pl.BlockSpec((1, HEAD_DIM), lambda i: (0, 0))],
        out_specs=[tile, st],
        out_shape=[jax.ShapeDtypeStruct((nb, B_V_HEADS, 8, HEAD_DIM), F32),
                   jax.ShapeDtypeStruct((nb, B_V_HEADS, HEAD_DIM, HEAD_DIM), F32)],
        compiler_params=_params("arbitrary"),
        name="mixb_sample_rec",
    )(kq, v, z, beta, g, s0, og)


def _outproj_kernel(ap_ref, bp_ref, xp_ref, g1p_ref, sc2p_ref, sh2p_ref,
                    as_ref, bs_ref, xs_ref, g1s_ref, sc2s_ref, sh2s_ref,
                    w_ref, lg_ref, lb_ref, wr_ref, rb_ref, x1_ref, h2_ref, route_ref, count_ref, base_ref,
                    *, prompt_tiles):
    def tile(a_ref, b_ref, x_ref, g1_ref, sc2_ref, sh2_ref):
        mix = (jnp.dot(a_ref[...].astype(BF16), w_ref[:A_WIDTH, :], preferred_element_type=F32)
               + jnp.dot(b_ref[...].astype(BF16), w_ref[A_WIDTH:, :], preferred_element_type=F32))
        x1 = _layer_norm_rows(ALPHA * x_ref[...] + g1_ref[...] * mix, lg_ref[...], lb_ref[...])
        x1_ref[...] = x1
        h2 = x1 * (1.0 + sc2_ref[...]) + sh2_ref[...]
        h2_ref[...] = h2
        route_ref[...] = _route_tile(_mm3(h2, wr_ref[...]), rb_ref[...], base_ref)
        count_ref[...] = base_ref[...]

    @pl.when(pl.program_id(0) == 0)
    def _():
        base_ref[...] = jnp.zeros_like(base_ref)

    is_prompt = pl.program_id(0) < prompt_tiles

    @pl.when(is_prompt)
    def _():
        tile(ap_ref, bp_ref, xp_ref, g1p_ref, sc2p_ref, sh2p_ref)

    @pl.when(jnp.logical_not(is_prompt))
    def _():
        tile(as_ref, bs_ref, xs_ref, g1s_ref, sc2s_ref, sh2s_ref)


def _outproj(prompt, sample, mod_p, mod_s, w_out, lg, lb, w_router, r_bias, tm, seq):
    n_p = prompt[2].shape[0]
    n_s = sample[2].shape[0]
    pt = n_p // tm
    m = n_p + n_s
    tiles_per_seq = seq // tm
    p_row = lambda w: pl.BlockSpec((tm, w), lambda i: (jnp.minimum(i, pt - 1), 0))
    s_row = lambda w: pl.BlockSpec((tm, w), lambda i: (jnp.maximum(i - pt, 0), 0))
    p_mod = lambda col: pl.BlockSpec((None, 1, D_MODEL), lambda i: (jnp.minimum(i, pt - 1) // tiles_per_seq, 0, col))
    s_mod = lambda col: pl.BlockSpec((tm, D_MODEL), lambda i: (0, col))
    row = lambda w: pl.BlockSpec((tm, w), lambda i: (i, 0))
    full = lambda s: pl.BlockSpec(s, lambda i: (0, 0))
    return pl.pallas_call(
        functools.partial(_outproj_kernel, prompt_tiles=pt),
        grid=(m // tm,),
        in_specs=[p_row(A_WIDTH), p_row(B_VAL_WIDTH), p_row(D_MODEL), p_mod(2), p_mod(4), p_mod(3),
                  s_row(A_WIDTH), s_row(B_VAL_WIDTH), s_row(D_MODEL), s_mod(2), s_mod(4), s_mod(3),
                  full((D_MODEL, D_MODEL)), full((1, D_MODEL)), full((1, D_MODEL)),
                  full((D_MODEL, LANES)), full((1, LANES))],
        out_specs=[row(D_MODEL), row(D_MODEL), row(LANES),
                   full((1, LANES))],
        out_shape=[jax.ShapeDtypeStruct((m, D_MODEL), F32),
                   jax.ShapeDtypeStruct((m, D_MODEL), F32),
                   jax.ShapeDtypeStruct((m, LANES), F32),
                   jax.ShapeDtypeStruct((1, LANES), F32)],
        scratch_shapes=[pltpu.VMEM((1, LANES), F32)],
        compiler_params=_params("arbitrary"),
        name="outproj",
    )(*prompt, mod_p, mod_p, mod_p, *sample, mod_s, mod_s, mod_s, w_out, lg, lb, w_router, r_bias)


def _route_tile(lg, bias, base_ref):
    tm = lg.shape[0]
    lane = lax.broadcasted_iota(jnp.int32, lg.shape, 1)
    neg = -jnp.inf

    def first_argmax(score):
        mx = jnp.max(score, -1, keepdims=True)
        return jnp.min(jnp.where(score == mx, lane, LANES), -1, keepdims=True)

    def pick(vals, idx):
        return jnp.sum(jnp.where(lane == idx, vals, 0.0), -1, keepdims=True)

    gmask = lane < N_GROUPS
    mg = jnp.max(jnp.where(gmask, lg, neg), -1, keepdims=True)
    eg = jnp.where(gmask, jnp.exp(jnp.where(gmask, lg - mg, 0.0)), 0.0)
    pg = eg / jnp.sum(eg, -1, keepdims=True)
    sel_g = first_argmax(jnp.where(gmask, lg + bias, neg))
    p_sel = pick(pg, sel_g)

    lo = N_GROUPS + sel_g * EXPERTS_PER_GROUP
    emask = (lane >= lo) & (lane < lo + EXPERTS_PER_GROUP)
    me = jnp.max(jnp.where(emask, lg, neg), -1, keepdims=True)
    ee = jnp.where(emask, jnp.exp(jnp.where(emask, lg - me, 0.0)), 0.0)
    pe = ee / jnp.sum(ee, -1, keepdims=True)
    score = jnp.where(emask, pe + bias, neg)
    i1 = first_argmax(score)
    i2 = first_argmax(jnp.where(lane == i1, neg, score))
    w1 = pick(pe, i1)
    w2 = pick(pe, i2)
    wsum = w1 + w2
    gate1 = w1 / wsum * p_sel
    gate2 = w2 / wsum * p_sel

    hot = ((lane == i1) | (lane == i2)).astype(BF16)
    ri = lax.broadcasted_iota(jnp.int32, (tm, tm), 0)
    ci = lax.broadcasted_iota(jnp.int32, (tm, tm), 1)
    before = jnp.dot((ri > ci).astype(BF16), hot, preferred_element_type=F32) + base_ref[...]
    rank1 = pick(before, i1)
    rank2 = pick(before, i2)
    base_ref[...] = base_ref[...] + jnp.sum(hot.astype(F32), 0, keepdims=True)

    out = jnp.where(lane == 0, (i1 - N_GROUPS).astype(F32), 0.0)
    out = jnp.where(lane == 1, (i2 - N_GROUPS).astype(F32), out)
    out = jnp.where(lane == 2, gate1, out)
    out = jnp.where(lane == 3, gate2, out)
    out = jnp.where(lane == 4, rank1, out)
    out = jnp.where(lane == 5, rank2, out)
    return out


DISPATCH_ROWS = 256


def _dispatch_kernel(dest_ref, last_ref, nv_ref, h_ref, xb_ref, zbuf, zsem, ssem, *, n_tok, n_blocks):
    i = pl.program_id(0)
    nv = nv_ref[0]

    def zero_block(blk):
        return pltpu.make_async_copy(zbuf, xb_ref.at[pl.ds(blk * MOE_ROWS, MOE_ROWS)], zsem)

    @pl.when(i == 0)
    def _():
        zbuf[...] = jnp.zeros_like(zbuf)
        for e in range(N_EXPERTS):
            @pl.when(last_ref[e] >= 0)
            def _():
                zero_block(last_ref[e]).start()

        def tail_start(b, carry):
            zero_block(b).start()
            return carry

        lax.fori_loop(nv, n_blocks, tail_start, 0)
        for e in range(N_EXPERTS):
            @pl.when(last_ref[e] >= 0)
            def _():
                zero_block(0).wait()

        def tail_wait(b, carry):
            zero_block(0).wait()
            return carry

        lax.fori_loop(nv, n_blocks, tail_wait, 0)

    def row_copy(r, choice):
        slot = dest_ref[choice * n_tok + i * DISPATCH_ROWS + r]
        return pltpu.make_async_copy(h_ref.at[pl.ds(r, 1)], xb_ref.at[pl.ds(slot, 1)], ssem)

    def body(r, carry):
        row_copy(r, 0).start()
        row_copy(r, 1).start()
        return carry

    lax.fori_loop(0, DISPATCH_ROWS, body, 0, unroll=8)
    for _ in range(2):
        pltpu.make_async_copy(h_ref, xb_ref.at[pl.ds(0, DISPATCH_ROWS)], ssem).wait()


def _dispatch(dest_flat, last_block, n_valid, h2, n_blocks):
    n_tok = h2.shape[0]
    kern = functools.partial(_dispatch_kernel, n_tok=n_tok, n_blocks=n_blocks)
    return pl.pallas_call(
        kern,
        grid_spec=pltpu.PrefetchScalarGridSpec(
            num_scalar_prefetch=3,
            grid=(n_tok // DISPATCH_ROWS,),
            in_specs=[pl.BlockSpec((DISPATCH_ROWS, D_MODEL), lambda i, *_: (i, 0))],
            out_specs=pl.BlockSpec(memory_space=pl.ANY),
            scratch_shapes=[pltpu.VMEM((MOE_ROWS, D_MODEL), F32),
                            pltpu.SemaphoreType.DMA(()),
                            pltpu.SemaphoreType.DMA(())]),
        out_shape=jax.ShapeDtypeStruct((n_blocks * MOE_ROWS, D_MODEL), F32),
        compiler_params=_params("arbitrary"),
        name="dispatch",
    )(dest_flat, last_block, n_valid, h2)


def _expert_kernel(be_ref, nv_ref, first_ref, next_ref, wslot_ref,
                   x_ref, wg_ref, wu_ref, wd_ref, y_ref,
                   wg_st, wu_st, wd_st, wg_s, wu_s, wd_s, wsem):
    i = pl.program_id(0)
    nv = nv_ref[0]

    def weight_copies(e, p):
        return [pltpu.make_async_copy(src.at[e], dst.at[p], wsem.at[p])
                for src, dst in ((wg_ref, wg_st), (wu_ref, wu_st), (wd_ref, wd_st))]

    @pl.when(i == 0)
    def _():
        for c in weight_copies(be_ref[0], 0):
            c.start()

    @pl.when((i < nv) & (first_ref[i] == 1))
    def _():
        p = wslot_ref[i]
        for c in weight_copies(be_ref[i], p):
            c.wait()
        wg_s[...] = wg_st[p].astype(BF16)
        wu_s[...] = wu_st[p].astype(BF16)
        wd_s[...] = wd_st[p].astype(BF16)

        @pl.when(next_ref[i] >= 0)
        def _():
            for c in weight_copies(next_ref[i], 1 - p):
                c.start()

    @pl.when(i < nv)
    def _():
        x = x_ref[...].astype(BF16)
        hg = jnp.dot(x, wg_s[...], preferred_element_type=F32)
        hu = jnp.dot(x, wu_s[...], preferred_element_type=F32)
        hid = (jax.nn.silu(hg) * hu).astype(BF16)
        y_ref[...] = jnp.dot(hid, wd_s[...], preferred_element_type=F32)

    @pl.when(i >= nv)
    def _():
        y_ref[...] = jnp.zeros_like(y_ref)


def _experts(block_e, n_valid, xb, w_gate, w_up, w_down):
    n_blocks = block_e.shape[0]
    idx = jnp.arange(n_blocks, dtype=jnp.int32)
    first = (idx < n_valid[0]) & ((idx == 0) | (block_e != jnp.roll(block_e, 1)))
    wslot = (jnp.cumsum(first.astype(jnp.int32)) - 1) % 2
    first_at = jnp.where(first, idx, n_blocks)
    next_first = jnp.concatenate([lax.cummin(first_at, reverse=True)[1:], jnp.full((1,), n_blocks, jnp.int32)])
    next_e = jnp.where(next_first < n_blocks, block_e[jnp.minimum(next_first, n_blocks - 1)], -1)
    any_spec = pl.BlockSpec(memory_space=pl.ANY)
    rows = pl.BlockSpec((MOE_ROWS, D_MODEL), lambda i, *_: (i, 0))
    return pl.pallas_call(
        _expert_kernel,
        grid_spec=pltpu.PrefetchScalarGridSpec(
            num_scalar_prefetch=5,
            grid=(n_blocks,),
            in_specs=[rows, any_spec, any_spec, any_spec],
            out_specs=rows,
            scratch_shapes=[pltpu.VMEM((2, D_MODEL, D_EXPERT), F32),
                            pltpu.VMEM((2, D_MODEL, D_EXPERT), F32),
                            pltpu.VMEM((2, D_EXPERT, D_MODEL), F32),
                            pltpu.VMEM((D_MODEL, D_EXPERT), BF16),
                            pltpu.VMEM((D_MODEL, D_EXPERT), BF16),
                            pltpu.VMEM((D_EXPERT, D_MODEL), BF16),
                            pltpu.SemaphoreType.DMA((2,))]),
        out_shape=jax.ShapeDtypeStruct((n_blocks * MOE_ROWS, D_MODEL), F32),
        compiler_params=_params("arbitrary"),
        name="experts",
    )(block_e, n_valid, first.astype(jnp.int32), next_e.astype(jnp.int32), wslot.astype(jnp.int32),
      xb, w_gate, w_up, w_down)


def _final_kernel(dest_ref, x1_ref, route_ref, g2_ref, lg_ref, lb_ref, yb_ref, o_ref, ybuf, sem,
                  *, row0, n_tok):
    i = pl.program_id(0)
    tm = o_ref.shape[0]
    slot = i % 2

    def gather_start(tile, s):
        def body(r, carry):
            tok = row0 + tile * tm + r
            for choice in range(2):
                src = dest_ref[choice * n_tok + tok]
                pltpu.make_async_copy(yb_ref.at[pl.ds(src, 1)], ybuf.at[s, choice, pl.ds(r, 1)], sem.at[s]).start()
            return carry

        lax.fori_loop(0, tm, body, 0, unroll=8)

    @pl.when(i == 0)
    def _():
        gather_start(0, 0)

    @pl.when(i + 1 < pl.num_programs(0))
    def _():
        gather_start(i + 1, 1 - slot)

    for choice in range(2):
        pltpu.make_async_copy(yb_ref.at[pl.ds(0, tm)], ybuf.at[slot, choice], sem.at[slot]).wait()
    route = route_ref[...]
    ff = ybuf[slot, 0] * route[:, 2:3] + ybuf[slot, 1] * route[:, 3:4]
    o_ref[...] = _layer_norm_rows(ALPHA * x1_ref[...] + g2_ref[...] * ff, lg_ref[...], lb_ref[...])


def _final(dest_flat, x1, yb, route, mod, lg, lb, tm, rows_per_mod, row0, n_rows):
    n_tok = x1.shape[0]
    off = row0 // tm
    return pl.pallas_call(
        functools.partial(_final_kernel, row0=row0, n_tok=n_tok),
        grid_spec=pltpu.PrefetchScalarGridSpec(
            num_scalar_prefetch=1,
            grid=(n_rows // tm,),
            in_specs=[pl.BlockSpec((tm, D_MODEL), lambda i, *_: (off + i, 0)),
                      pl.BlockSpec((tm, LANES), lambda i, *_: (off + i, 0)),
                      _mod_spec(mod, 5, tm, rows_per_mod),
                      pl.BlockSpec((1, D_MODEL), lambda i, *_: (0, 0)),
                      pl.BlockSpec((1, D_MODEL), lambda i, *_: (0, 0)),
                      pl.BlockSpec(memory_space=pl.ANY)],
            out_specs=pl.BlockSpec((tm, D_MODEL), lambda i, *_: (i, 0)),
            scratch_shapes=[pltpu.VMEM((2, 2, tm, D_MODEL), F32),
                            pltpu.SemaphoreType.DMA((2,))]),
        out_shape=jax.ShapeDtypeStruct((n_rows, D_MODEL), F32),
        compiler_params=_params("arbitrary"),
        name="final",
    )(dest_flat, x1, route, mod, lg, lb, yb)


def _lane_row(vec, offset):
    return jnp.zeros((1, LANES), F32).at[0, offset:offset + vec.shape[0]].set(vec.astype(F32))


def kernel(x_prompt, x_sample, state_conv, state_ssm, c_prompt, c_sample, w_ada, b_ada, w_in, a_ws, a_bs, a_norm_g, a_norm_b, b_conv_w, b_a_log, b_dt_bias, b_onorm_g, w_out, ln1_g, ln1_b, w_router_g, b_router_g, w_router_e, b_router_e, w_gate, w_up, w_down, ln2_g, ln2_b):
    batch, seq, d = x_prompt.shape
    nb, steps, _ = x_sample.shape
    n_p = batch * seq
    n_s = nb * steps
    n_tok = n_p + n_s
    l = 0

    c_rows = batch + nb
    c_pad = (-c_rows) % 8
    c_all = jnp.concatenate([c_prompt, c_sample, jnp.zeros((c_pad, d), F32)], axis=0)
    mod = _ada(c_all, w_ada[l], b_ada[l].reshape(1, -1))
    mod_p = mod[:batch].reshape(batch, 1, 6 * d)
    mod_s = mod[batch:batch + nb]

    w_in_t = jnp.swapaxes(w_in[l], 0, 1)
    w_bd = jnp.pad(w_in_t[MAIN_WIDTH:], ((0, LANES - 2 * B_V_HEADS), (0, 0))).astype(BF16)
    xp = x_prompt.reshape(n_p, d)
    xs = jnp.swapaxes(x_sample, 0, 1).reshape(n_s, d)
    proj_p, bd_p, proj_s, bd_s = _inproj(xp, xs, mod_p, mod_s, w_in_t, w_bd, 512, seq)

    ng = a_norm_g[l].reshape(1, -1)
    nbias = a_norm_b[l].reshape(1, -1)
    bias_tile = jnp.repeat(a_bs[l].T, HEAD_DIM, axis=1)
    a_out_p = _mixa_prompt(proj_p, a_ws[l], bias_tile, ng, nbias, 256)
    coef = jnp.repeat(jnp.transpose(a_ws[l][:, :steps, :steps], (1, 2, 0)).reshape(steps * steps, A_HEADS),
                      HEAD_DIM, axis=1)
    a_out_s, chunkv = _mixa_sample(proj_s, coef, bias_tile[:steps], ng, nbias, steps, nb)

    nega = _lane_row(-jnp.exp(b_a_log[l].astype(F32)), B_V_HEADS)
    dtb = _lane_row(b_dt_bias[l], B_V_HEADS)
    og = b_onorm_g[l].reshape(1, -1)
    b_out_p, ssm_p = _mixb_prompt(proj_p, bd_p, b_conv_w[l], nega, dtb, og, batch, seq)
    qkv0 = 2 * A_WIDTH
    conv_p = proj_p.reshape(batch, seq, MAIN_WIDTH)[:, seq - (B_CONV - 1):, qkv0:qkv0 + B_CONV_CH]

    buf_s = jnp.swapaxes(state_conv[l], 0, 1).reshape((B_CONV - 1) * nb, B_CONV_CH)
    act_s, beta_s, g_s = _mixb_sample_pre(proj_s, buf_s, bd_s, b_conv_w[l], nega, dtb, steps, nb)
    rep = B_V_HEADS // B_QK_HEADS
    act4 = act_s.reshape(steps, nb, B_CONV_CH)
    q8 = jnp.repeat(act4[..., :B_KEY_WIDTH].reshape(steps, nb, B_QK_HEADS, HEAD_DIM), rep, axis=2)
    k8 = jnp.repeat(act4[..., B_KEY_WIDTH:2 * B_KEY_WIDTH].reshape(steps, nb, B_QK_HEADS, HEAD_DIM), rep, axis=2)

    def to_tiles(a):
        a = jnp.transpose(a, (1, 2, 0, 3))
        return jnp.pad(a, ((0, 0), (0, 0), (0, 8 - a.shape[2]), (0, 0)))

    kq_t = jnp.concatenate([to_tiles(k8)[:, :, :4], to_tiles(q8)[:, :, :4]], axis=2)
    v_t = to_tiles(act4[..., 2 * B_KEY_WIDTH:].reshape(steps, nb, B_V_HEADS, HEAD_DIM))
    z_t = to_tiles(proj_s[:, qkv0 + B_CONV_CH:].reshape(steps, nb, B_V_HEADS, HEAD_DIM))

    def gate_tiles(a, off):
        a = a[:, off:off + B_V_HEADS].reshape(steps, nb, B_V_HEADS, 1)
        return to_tiles(jnp.broadcast_to(a, (steps, nb, B_V_HEADS, HEAD_DIM)))

    o_t, ssm_s = _mixb_sample_rec(kq_t, v_t, z_t, gate_tiles(beta_s, 0), gate_tiles(g_s, B_V_HEADS),
                                  state_ssm[l], og, steps, 8)
    b_out_s = jnp.transpose(o_t[:, :, :steps], (2, 0, 1, 3)).reshape(n_s, B_VAL_WIDTH)
    conv_s = jnp.swapaxes(proj_s.reshape(steps, nb, MAIN_WIDTH)[steps - (B_CONV - 1):, :, qkv0:qkv0 + B_CONV_CH], 0, 1)

    w_out_b = w_out[l].astype(BF16)
    lg1 = ln1_g[l].reshape(1, -1)
    lb1 = ln1_b[l].reshape(1, -1)
    w_router = jnp.concatenate([w_router_g[l], w_router_e[l],
                                jnp.zeros((d, LANES - N_GROUPS - N_EXPERTS), F32)], axis=1)
    out_tm = 256
    mod_s_tile = jnp.tile(mod_s, (out_tm // nb, 1))
    r_bias = jnp.concatenate([b_router_g[l], b_router_e[l],
                              jnp.zeros((LANES - N_GROUPS - N_EXPERTS,), F32)]).reshape(1, LANES)
    x1, h2, route, counts_row = _outproj((a_out_p, b_out_p, xp), (a_out_s, b_out_s, xs), mod_p, mod_s_tile,
                                         w_out_b, lg1, lb1, w_router, r_bias, out_tm, seq)

    counts = counts_row[0, N_GROUPS:N_GROUPS + N_EXPERTS].astype(jnp.int32)
    route_t = route[:, :8].T
    eid = route_t[0:2].astype(jnp.int32)
    rank = route_t[4:6].astype(jnp.int32)
    padded = (counts + MOE_ROWS - 1) // MOE_ROWS * MOE_ROWS
    pend = jnp.cumsum(padded)
    pstart = pend - padded
    dest = rank + jnp.sum(jnp.where(eid[None] == jnp.arange(N_EXPERTS, dtype=jnp.int32)[:, None, None],
                                    pstart[:, None, None], 0), axis=0)
    n_blocks = -(-(2 * n_tok) // MOE_ROWS) + N_EXPERTS
    dest_flat = dest.reshape(-1)
    last_block = jnp.where(padded > 0, pend // MOE_ROWS - 1, -1).astype(jnp.int32)
    blk0 = jnp.arange(n_blocks, dtype=jnp.int32) * MOE_ROWS
    block_e = jnp.minimum(jnp.sum(pend[None, :] <= blk0[:, None], axis=1), N_EXPERTS - 1).astype(jnp.int32)
    n_valid = (pend[-1:] // MOE_ROWS).astype(jnp.int32)

    xb = _dispatch(dest_flat, last_block, n_valid, h2, n_blocks)
    yb = _experts(block_e, n_valid, xb, w_gate[l], w_up[l], w_down[l])

    lg2 = ln2_g[l].reshape(1, -1)
    lb2 = ln2_b[l].reshape(1, -1)
    y_p = _final(dest_flat, x1, yb, route, mod_p, lg2, lb2, 256, seq, 0, n_p)
    y_s = _final(dest_flat, x1, yb, route, mod_s, lg2, lb2, nb, 0, n_p, n_s)

    y_prompt = y_p.reshape(batch, seq, d)
    y_sample = jnp.swapaxes(y_s.reshape(steps, nb, d), 0, 1)
    chunkv_s = jnp.swapaxes(chunkv.reshape(steps, nb, A_HEADS, HEAD_DIM), 0, 1)
    return (y_prompt, y_sample, conv_p[None], ssm_p[None], conv_s[None], ssm_s[None], chunkv_s[None])
```

```python
import functools

import jax
import jax.numpy as jnp
from jax import lax
from jax.experimental import pallas as pl
from jax.experimental.pallas import tpu as pltpu

F32 = jnp.float32
BF16 = jnp.bfloat16

D_MODEL = 2048
DEPTH = 1
A_HEADS = 8
HEAD_DIM = 128
A_WIDTH = 1024
A_CHUNK = 128
B_QK_HEADS = 4
B_V_HEADS = 8
B_KEY_WIDTH = 512
B_VAL_WIDTH = 1024
B_CONV = 4
B_CONV_CH = 2048
DN_CHUNK = 64
MAIN_WIDTH = 2 * A_WIDTH + B_CONV_CH + B_VAL_WIDTH
N_GROUPS = 4
EXPERTS_PER_GROUP = 8
N_EXPERTS = 32
D_EXPERT = 512
ALPHA = (2 * DEPTH) ** 0.25
LN_EPS = 1e-5
RMS_EPS = 1e-6
L2_EPS = 1e-6

LANES = 128
VMEM_LIMIT = 56 * 1024 * 1024
MOE_ROWS = 256
MIXB_GROUP = 4


def _params(*sem):
    return pltpu.CompilerParams(dimension_semantics=sem, vmem_limit_bytes=VMEM_LIMIT)


def _mm(a, b):
    return jnp.dot(a.astype(BF16), b.astype(BF16), preferred_element_type=F32)


def _mm_nt(a, b):
    return lax.dot_general(a.astype(BF16), b.astype(BF16), (((1,), (1,)), ((), ())),
                           preferred_element_type=F32)


def _mm_tn(a, b):
    return lax.dot_general(a.astype(BF16), b.astype(BF16), (((0,), (0,)), ((), ())),
                           preferred_element_type=F32)


def _split(x):
    hi = x.astype(BF16)
    lo = (x - hi.astype(F32)).astype(BF16)
    return hi, lo


def _mm_exact_lhs(a_bf16, b):
    hi, lo = _split(b)
    return (jnp.dot(a_bf16, hi, preferred_element_type=F32)
            + jnp.dot(a_bf16, lo, preferred_element_type=F32))


def _mm3(a, b):
    ah, al = _split(a)
    bh, bl = _split(b)
    return (jnp.dot(ah, bh, preferred_element_type=F32) + jnp.dot(ah, bl, preferred_element_type=F32)
            + jnp.dot(al, bh, preferred_element_type=F32))


def _softplus(x):
    return jnp.maximum(x, 0.0) + jnp.log1p(jnp.exp(-jnp.abs(x)))


def _layer_norm_rows(x, g, b):
    mu = jnp.mean(x, -1, keepdims=True)
    xc = x - mu
    var = jnp.mean(xc * xc, -1, keepdims=True)
    return xc * lax.rsqrt(var + LN_EPS) * g + b


def _ada_kernel(c_ref, w_ref, b_ref, o_ref):
    a = jax.nn.silu(c_ref[...]).astype(BF16)
    o_ref[...] = jnp.dot(a, w_ref[...].astype(BF16), preferred_element_type=F32) + b_ref[...]


def _ada(c_all, w_ada, b_ada):
    rows = c_all.shape[0]
    tn = 1024
    return pl.pallas_call(
        _ada_kernel,
        grid=(6 * D_MODEL // tn,),
        in_specs=[pl.BlockSpec((rows, D_MODEL), lambda j: (0, 0)),
                  pl.BlockSpec((D_MODEL, tn), lambda j: (0, j)),
                  pl.BlockSpec((1, tn), lambda j: (0, j))],
        out_specs=pl.BlockSpec((rows, tn), lambda j: (0, j)),
        out_shape=jax.ShapeDtypeStruct((rows, 6 * D_MODEL), F32),
        compiler_params=_params("arbitrary"),
        name="ada",
    )(c_all, w_ada, b_ada)


def _inproj_kernel(xp_ref, scp_ref, shp_ref, xs_ref, scs_ref, shs_ref, w_ref, wbd_ref,
                   op_ref, bdp_ref, os_ref, bds_ref, wb_ref, *, prompt_tiles):
    j = pl.program_id(0)
    i = pl.program_id(1)

    @pl.when(i == 0)
    def _():
        wb_ref[...] = w_ref[...].astype(BF16)

    def project(h, o_ref, bd_ref):
        o_ref[...] = _mm_nt(h, wb_ref[...])

        @pl.when(j == 0)
        def _():
            bd_ref[0] = _mm_nt(h, wbd_ref[...])

        @pl.when(j != 0)
        def _():
            bd_ref[0] = jnp.zeros(bd_ref.shape[1:], F32)

    @pl.when(i < prompt_tiles)
    def _():
        project((xp_ref[...] * (1.0 + scp_ref[...]) + shp_ref[...]).astype(BF16), op_ref, bdp_ref)

    @pl.when(i >= prompt_tiles)
    def _():
        n_s, nb = xs_ref.shape[0], scs_ref.shape[0]
        x = xs_ref[...].reshape(n_s // nb, nb, D_MODEL)
        h = x * (1.0 + scs_ref[...])[None] + shs_ref[...][None]
        project(h.reshape(n_s, D_MODEL).astype(BF16), os_ref, bds_ref)


def _mod_spec(mod, col, tm, rows_per_mod, axis=0):
    if rows_per_mod:
        tiles = rows_per_mod // tm
        return pl.BlockSpec((None, 1, D_MODEL), lambda *g: (g[axis] // tiles, 0, col))
    return pl.BlockSpec((tm, D_MODEL), lambda *g: (0, col))


def _inproj(xp, xs, mod_p, mod_s, w_in_t, w_bd, tm, seq):
    n_p, n_s = xp.shape[0], xs.shape[0]
    nb = mod_s.shape[0]
    tn = 1024
    nj = MAIN_WIDTH // tn
    pt = n_p // tm
    tiles_per_seq = seq // tm
    pi = lambda i: jnp.minimum(i, pt - 1)
    p_mod = lambda col: pl.BlockSpec((None, 1, D_MODEL), lambda j, i: (pi(i) // tiles_per_seq, 0, col))
    s_mod = lambda col: pl.BlockSpec((nb, D_MODEL), lambda j, i: (0, col))
    proj_p, bd_p, proj_s, bd_s = pl.pallas_call(
        functools.partial(_inproj_kernel, prompt_tiles=pt),
        grid=(nj, pt + 1),
        in_specs=[pl.BlockSpec((tm, D_MODEL), lambda j, i: (pi(i), 0)), p_mod(1), p_mod(0),
                  pl.BlockSpec((n_s, D_MODEL), lambda j, i: (0, 0)), s_mod(1), s_mod(0),
                  pl.BlockSpec((tn, D_MODEL), lambda j, i: (j, 0)),
                  pl.BlockSpec((LANES, D_MODEL), lambda j, i: (0, 0))],
        out_specs=[pl.BlockSpec((tm, tn), lambda j, i: (pi(i), j)),
                   pl.BlockSpec((1, tm, LANES), lambda j, i: (j, pi(i), 0)),
                   pl.BlockSpec((n_s, tn), lambda j, i: (0, j)),
                   pl.BlockSpec((1, n_s, LANES), lambda j, i: (j, 0, 0))],
        out_shape=[jax.ShapeDtypeStruct((n_p, MAIN_WIDTH), F32),
                   jax.ShapeDtypeStruct((nj, n_p, LANES), F32),
                   jax.ShapeDtypeStruct((n_s, MAIN_WIDTH), F32),
                   jax.ShapeDtypeStruct((nj, n_s, LANES), F32)],
        scratch_shapes=[pltpu.VMEM((tn, D_MODEL), BF16)],
        compiler_params=_params("arbitrary", "arbitrary"),
        name="inproj",
    )(xp, mod_p, mod_p, xs, mod_s, mod_s, w_in_t, w_bd)
    return proj_p, bd_p[0], proj_s, bd_s[0]


def _mixa_prompt_kernel(p_ref, ws_ref, bias_ref, ng_ref, nb_ref, o_ref):
    rows = p_ref.shape[0]
    u = jax.nn.gelu(p_ref[:, :A_WIDTH])
    v = _layer_norm_rows(jax.nn.gelu(p_ref[:, A_WIDTH:]), ng_ref[...], nb_ref[...])
    ri = lax.broadcasted_iota(jnp.int32, (A_CHUNK, A_CHUNK), 0)
    ci = lax.broadcasted_iota(jnp.int32, (A_CHUNK, A_CHUNK), 1)
    for h in range(A_HEADS):
        w = jnp.where(ri >= ci, ws_ref[h], 0.0).astype(BF16)
        cols = slice(h * HEAD_DIM, (h + 1) * HEAD_DIM)
        for c in range(rows // A_CHUNK):
            rs = slice(c * A_CHUNK, (c + 1) * A_CHUNK)
            s = jnp.dot(w, v[rs, cols].astype(BF16), preferred_element_type=F32) + bias_ref[:, cols]
            o_ref[rs, cols] = u[rs, cols] * s


def _mixa_prompt(proj, a_ws, bias_tile, ng, nb, tm):
    m = proj.shape[0]
    return pl.pallas_call(
        _mixa_prompt_kernel,
        grid=(m // tm,),
        in_specs=[pl.BlockSpec((tm, 2 * A_WIDTH), lambda i: (i, 0)),
                  pl.BlockSpec((A_HEADS, A_CHUNK, A_CHUNK), lambda i: (0, 0, 0)),
                  pl.BlockSpec((A_CHUNK, A_WIDTH), lambda i: (0, 0)),
                  pl.BlockSpec((1, A_WIDTH), lambda i: (0, 0)),
                  pl.BlockSpec((1, A_WIDTH), lambda i: (0, 0))],
        out_specs=pl.BlockSpec((tm, A_WIDTH), lambda i: (i, 0)),
        out_shape=jax.ShapeDtypeStruct((m, A_WIDTH), F32),
        compiler_params=_params("arbitrary"),
        name="mixa_prompt",
    )(proj, a_ws, bias_tile, ng, nb)


def _mixa_sample_kernel(p_ref, coef_ref, bias_ref, ng_ref, nb_ref, o_ref, v_ref, *, steps, nb_rows):
    u = jax.nn.gelu(p_ref[:, :A_WIDTH])
    v = _layer_norm_rows(jax.nn.gelu(p_ref[:, A_WIDTH:]), ng_ref[...], nb_ref[...])
    v_ref[...] = v
    for t in range(steps):
        s = bias_ref[t:t + 1, :]
        for j in range(t + 1):
            s = s + coef_ref[t * steps + j:t * steps + j + 1, :] * v[j * nb_rows:(j + 1) * nb_rows, :]
        rs = slice(t * nb_rows, (t + 1) * nb_rows)
        o_ref[rs, :] = u[rs, :] * s


def _mixa_sample(proj, coef, bias, ng, nb, steps, nb_rows):
    m = proj.shape[0]
    kern = functools.partial(_mixa_sample_kernel, steps=steps, nb_rows=nb_rows)
    return pl.pallas_call(
        kern,
        grid=(1,),
        in_specs=[pl.BlockSpec((m, 2 * A_WIDTH), lambda i: (0, 0)),
                  pl.BlockSpec(coef.shape, lambda i: (0, 0)),
                  pl.BlockSpec(bias.shape, lambda i: (0, 0)),
                  pl.BlockSpec((1, A_WIDTH), lambda i: (0, 0)),
                  pl.BlockSpec((1, A_WIDTH), lambda i: (0, 0))],
        out_specs=[pl.BlockSpec((m, A_WIDTH), lambda i: (0, 0)),
                   pl.BlockSpec((m, A_WIDTH), lambda i: (0, 0))],
        out_shape=[jax.ShapeDtypeStruct((m, A_WIDTH), F32),
                   jax.ShapeDtypeStruct((m, A_WIDTH), F32)],
        compiler_params=_params("arbitrary"),
        name="mixa_sample",
    )(proj, coef, bias, ng, nb)


def _unit_lower_inverse_many(a_list, ri, ci, block):
    eye = (ri == ci).astype(F32)
    pair = (lax.shift_right_logical(ri, 1) == lax.shift_right_logical(ci, 1)) & ((ri & 1) == 1) & ((ci & 1) == 0)
    ts = [eye - jnp.where(pair, a, 0.0) for a in a_list]
    n = 2
    while n < block:
        sh = n.bit_length()
        m = ((lax.shift_right_logical(ri, sh) == lax.shift_right_logical(ci, sh))
             & ((ri & n) != 0) & ((ci & n) == 0))
        xs = [_mm(t, jnp.where(m, a, 0.0)) for t, a in zip(ts, a_list)]
        ts = [t - _mm(x, t) for t, x in zip(ts, xs)]
        n *= 2
    return ts


def _mixb_prompt_kernel(qkv_ref, z_ref, bd_ref, cw_ref, nega_ref, dtb_ref, og_ref,
                        o_ref, sfin_ref, s_ref, cbuf_ref):
    c = pl.program_id(0)
    C = DN_CHUNK
    nseq = qkv_ref.shape[0]

    @pl.when(c == 0)
    def _init():
        s_ref[...] = jnp.zeros_like(s_ref)
        cbuf_ref[:, 0:8, :] = jnp.zeros((nseq, 8, B_CONV_CH), F32)

    cw = cw_ref[...]
    og = og_ref[...]
    ri64 = lax.broadcasted_iota(jnp.int32, (C, C), 0)
    ci64 = lax.broadcasted_iota(jnp.int32, (C, C), 1)
    cum_lhs = (ri64 >= ci64).astype(BF16)

    R = MIXB_GROUP * C
    ri = lax.broadcasted_iota(jnp.int32, (R, R), 0)
    ci = lax.broadcasted_iota(jnp.int32, (R, R), 1)
    same = lax.shift_right_logical(ri, C.bit_length() - 1) == lax.shift_right_logical(ci, C.bit_length() - 1)
    tril = same & (ri >= ci)
    strict = same & (ri > ci)
    rep = B_V_HEADS // B_QK_HEADS

    def lane_col(a, lane):
        return jnp.broadcast_to(a[:, lane:lane + 1], (a.shape[0], HEAD_DIM))

    groups = B_V_HEADS // MIXB_GROUP
    units = [(b, grp) for b in range(nseq) for grp in range(groups)]
    heads_of = lambda grp: list(range(grp * MIXB_GROUP, (grp + 1) * MIXB_GROUP))

    acts, gcs, betas = [], [], []
    for b in range(nseq):
        x = qkv_ref[b]
        cbuf_ref[b, 8:8 + C, :] = x
        y = cbuf_ref[b, 5:5 + C, :] * cw[0:1]
        y = y + cbuf_ref[b, 6:6 + C, :] * cw[1:2]
        y = y + cbuf_ref[b, 7:7 + C, :] * cw[2:3]
        y = y + x * cw[3:4]
        cbuf_ref[b, 0:8, :] = x[C - 8:C, :]
        acts.append(jax.nn.silu(y))
        bd = bd_ref[b]
        betas.append(jax.nn.sigmoid(bd))
        g_all = nega_ref[...] * _softplus(bd + dtb_ref[...])
        gcs.append(_mm_exact_lhs(cum_lhs, g_all))

    kst, qst, gcol, glcol, bcol, decay, a_mat, rhs = {}, {}, {}, {}, {}, {}, {}, {}
    for u in units:
        b, grp = u
        act, gc_all, beta_all = acts[b], gcs[b], betas[b]

        def stack(fn):
            return jnp.concatenate([fn(h) for h in heads_of(grp)], axis=0)

        def l2n(cols0, h):
            s = act[:, cols0 + (h // rep) * HEAD_DIM:cols0 + (h // rep + 1) * HEAD_DIM]
            return s * lax.rsqrt(jnp.sum(s * s, -1, keepdims=True) + L2_EPS)

        kst[u] = stack(lambda h: l2n(B_KEY_WIDTH, h))
        qst[u] = stack(lambda h: l2n(0, h) * (HEAD_DIM ** -0.5))
        vst = stack(lambda h: act[:, 2 * B_KEY_WIDTH + h * HEAD_DIM:2 * B_KEY_WIDTH + (h + 1) * HEAD_DIM])
        gcol[u] = stack(lambda h: lane_col(gc_all, B_V_HEADS + h))
        glcol[u] = stack(lambda h: jnp.broadcast_to(gc_all[C - 1:C, B_V_HEADS + h:B_V_HEADS + h + 1], (C, HEAD_DIM)))
        bcol[u] = stack(lambda h: lane_col(beta_all, h))
        grow = gcol[u].T[0:1, :]
        diff = jnp.concatenate([gcol[u], gcol[u]], axis=1) - grow
        decay[u] = jnp.where(tril, jnp.exp(jnp.where(tril, diff, 0.0)), 0.0)
        rhs[u] = jnp.concatenate([vst * bcol[u], kst[u] * bcol[u] * jnp.exp(gcol[u])], axis=1)
    for u in units:
        a_mat[u] = jnp.where(strict, jnp.concatenate([bcol[u], bcol[u]], axis=1) * _mm_nt(kst[u], kst[u]) * decay[u], 0.0)

    t_inv = _unit_lower_inverse_many([a_mat[u] for u in units], ri, ci, C)
    sol = [_mm(t, rhs[u]) for t, u in zip(t_inv, units)]
    qk = [_mm_nt(qst[u], kst[u]) * decay[u] for u in units]
    ws = []
    for sl, u in zip(sol, units):
        b, grp = u
        q_dec = qst[u] * jnp.exp(gcol[u])
        ws.append([_mm(jnp.concatenate([sl[i * C:(i + 1) * C, HEAD_DIM:], q_dec[i * C:(i + 1) * C]], axis=0),
                       s_ref[b * B_V_HEADS + h]) for i, h in enumerate(heads_of(grp))])
    v_new = [sl[:, :HEAD_DIM] - jnp.concatenate([w[:C] for w in wl], axis=0) for sl, wl in zip(sol, ws)]
    outs = [jnp.concatenate([w[C:] for w in wl], axis=0) + _mm(q, v) for wl, q, v in zip(ws, qk, v_new)]
    for u, v, o in zip(units, v_new, outs):
        b, grp = u
        k_dec = kst[u] * jnp.exp(glcol[u] - gcol[u])
        on = o * lax.rsqrt(jnp.mean(o * o, -1, keepdims=True) + RMS_EPS) * og
        for i, h in enumerate(heads_of(grp)):
            rs = slice(i * C, (i + 1) * C)
            si = b * B_V_HEADS + h
            s_ref[si] = s_ref[si] * jnp.exp(glcol[u][i * C:i * C + 1, :]) + _mm_tn(k_dec[rs], v[rs])
            cols = slice(h * HEAD_DIM, (h + 1) * HEAD_DIM)
            o_ref[b, :, cols] = on[rs] * jax.nn.silu(z_ref[b, :, cols])

    @pl.when(c == pl.num_programs(0) - 1)
    def _fin():
        sfin_ref[...] = s_ref[...]


def _mixb_prompt(proj, bd, conv_w, nega, dtb, og, batch, seq):
    nc = seq // DN_CHUNK
    qkv_blk = 2 * A_WIDTH // B_CONV_CH
    z_blk = (2 * A_WIDTH + B_CONV_CH) // B_VAL_WIDTH
    proj3 = proj.reshape(batch, seq, proj.shape[-1])
    bd3 = bd.reshape(batch, seq, LANES)
    o, s_fin = pl.pallas_call(
        _mixb_prompt_kernel,
        grid=(nc,),
        in_specs=[pl.BlockSpec((batch, DN_CHUNK, B_CONV_CH), lambda c: (0, c, qkv_blk)),
                  pl.BlockSpec((batch, DN_CHUNK, B_VAL_WIDTH), lambda c: (0, c, z_blk)),
                  pl.BlockSpec((batch, DN_CHUNK, LANES), lambda c: (0, c, 0)),
                  pl.BlockSpec((B_CONV, B_CONV_CH), lambda c: (0, 0)),
                  pl.BlockSpec((1, LANES), lambda c: (0, 0)),
                  pl.BlockSpec((1, LANES), lambda c: (0, 0)),
                  pl.BlockSpec((1, HEAD_DIM), lambda c: (0, 0))],
        out_specs=[pl.BlockSpec((batch, DN_CHUNK, B_VAL_WIDTH), lambda c: (0, c, 0)),
                   pl.BlockSpec((batch * B_V_HEADS, HEAD_DIM, HEAD_DIM), lambda c: (0, 0, 0))],
        out_shape=[jax.ShapeDtypeStruct((batch, seq, B_VAL_WIDTH), F32),
                   jax.ShapeDtypeStruct((batch * B_V_HEADS, HEAD_DIM, HEAD_DIM), F32)],
        scratch_shapes=[pltpu.VMEM((batch * B_V_HEADS, HEAD_DIM, HEAD_DIM), F32),
                        pltpu.VMEM((batch, DN_CHUNK + 8, B_CONV_CH), F32)],
        compiler_params=_params("arbitrary"),
        name="mixb_prompt",
    )(proj3, proj3, bd3, conv_w, nega, dtb, og)
    return (o.reshape(batch * seq, B_VAL_WIDTH),
            s_fin.reshape(batch, B_V_HEADS, HEAD_DIM, HEAD_DIM))


def _mixb_sample_pre_kernel(qkv_ref, buf_ref, bd_ref, cw_ref, nega_ref, dtb_ref,
                            act_ref, beta_ref, g_ref, *, steps, nb_rows):
    cw = cw_ref[...]

    def slab(j):
        if j < B_CONV - 1:
            return buf_ref[j * nb_rows:(j + 1) * nb_rows, :]
        jj = j - (B_CONV - 1)
        return qkv_ref[jj * nb_rows:(jj + 1) * nb_rows, :]

    for t in range(steps):
        y = slab(t) * cw[0:1]
        for i in range(1, B_CONV):
            y = y + slab(t + i) * cw[i:i + 1]
        act = jax.nn.silu(y)
        rs = slice(t * nb_rows, (t + 1) * nb_rows)
        for qh in range(B_QK_HEADS):
            cq = slice(qh * HEAD_DIM, (qh + 1) * HEAD_DIM)
            ck = slice(B_KEY_WIDTH + qh * HEAD_DIM, B_KEY_WIDTH + (qh + 1) * HEAD_DIM)
            qs = act[:, cq]
            ks = act[:, ck]
            act_ref[rs, cq] = qs * lax.rsqrt(jnp.sum(qs * qs, -1, keepdims=True) + L2_EPS) * (HEAD_DIM ** -0.5)
            act_ref[rs, ck] = ks * lax.rsqrt(jnp.sum(ks * ks, -1, keepdims=True) + L2_EPS)
        act_ref[rs, 2 * B_KEY_WIDTH:] = act[:, 2 * B_KEY_WIDTH:]
    bd = bd_ref[...]
    beta_ref[...] = jax.nn.sigmoid(bd)
    g_ref[...] = nega_ref[...] * _softplus(bd + dtb_ref[...])


def _mixb_sample_pre(proj, buf, bd, conv_w, nega, dtb, steps, nb_rows):
    m = proj.shape[0]
    qkv_blk = 2 * A_WIDTH // B_CONV_CH
    kern = functools.partial(_mixb_sample_pre_kernel, steps=steps, nb_rows=nb_rows)
    return pl.pallas_call(
        kern,
        grid=(1,),
        in_specs=[pl.BlockSpec((m, B_CONV_CH), lambda i: (0, qkv_blk)),
                  pl.BlockSpec(buf.shape, lambda i: (0, 0)),
                  pl.BlockSpec((m, LANES), lambda i: (0, 0)),
                  pl.BlockSpec((B_CONV, B_CONV_CH), lambda i: (0, 0)),
                  pl.BlockSpec((1, LANES), lambda i: (0, 0)),
                  pl.BlockSpec((1, LANES), lambda i: (0, 0))],
        out_specs=[pl.BlockSpec((m, B_CONV_CH), lambda i: (0, 0)),
                   pl.BlockSpec((m, LANES), lambda i: (0, 0)),
                   pl.BlockSpec((m, LANES), lambda i: (0, 0))],
        out_shape=[jax.ShapeDtypeStruct((m, B_CONV_CH), F32),
                   jax.ShapeDtypeStruct((m, LANES), F32),
                   jax.ShapeDtypeStruct((m, LANES), F32)],
        compiler_params=_params("arbitrary"),
        name="mixb_sample_pre",
    )(proj, buf, bd, conv_w, nega, dtb)


def _mixb_sample_rec_kernel(kq_ref, v_ref, z_ref, beta_ref, g_ref, s0_ref, og_ref,
                            o_ref, s_out_ref, *, steps, pairs):
    og = og_ref[...]
    zpad = jnp.zeros((HEAD_DIM - 8, HEAD_DIM), F32)
    zrows = jnp.zeros((8 - steps, HEAD_DIM), F32)
    heads = range(B_V_HEADS)

    def body(bi, carry):
        kqs = [_mm(kq_ref[bi, h], s0_ref[bi, h]) for h in heads]
        pending = []
        for h in heads:
            kq = kq_ref[bi, h]
            g = g_ref[bi, h]
            beta = beta_ref[bi, h]
            v = v_ref[bi, h]
            gc = [g[0:1]]
            for t in range(1, steps):
                gc.append(gc[-1] + g[t:t + 1])
            k = [kq[t:t + 1] for t in range(steps)]
            q = [kq[4 + t:5 + t] for t in range(steps)]
            d = []
            for t in range(steps):
                acc = v[t:t + 1] - jnp.exp(gc[t]) * kqs[h][t:t + 1]
                for j in range(t):
                    kk = jnp.sum(k[j] * k[t], -1, keepdims=True)
                    acc = acc - jnp.exp(gc[t] - gc[j]) * kk * d[j]
                d.append(beta[t:t + 1] * acc)
            outs = []
            for t in range(steps):
                o = jnp.exp(gc[t]) * kqs[h][4 + t:5 + t]
                for j in range(t + 1):
                    qk = jnp.sum(k[j] * q[t], -1, keepdims=True)
                    o = o + jnp.exp(gc[t] - gc[j]) * qk * d[j]
                outs.append(o * lax.rsqrt(jnp.mean(o * o, -1, keepdims=True) + RMS_EPS) * og)
            o_ref[bi, h] = jnp.concatenate(outs + [zrows], axis=0) * jax.nn.silu(z_ref[bi, h])
            k_dec = jnp.concatenate([jnp.exp(gc[-1] - gc[j]) * k[j] for j in range(steps)] + [zrows], axis=0)
            k_pad = jnp.concatenate([k_dec, zpad], axis=0)
            d_pad = jnp.concatenate(d + [zrows, zpad], axis=0)
            pending.append((k_pad.T, d_pad, jnp.exp(gc[-1])))
        for h, (k_t, d_pad, decay_last) in zip(heads, pending):
            s_out_ref[bi, h] = s0_ref[bi, h] * decay_last + _mm(k_t, d_pad)
        return carry

    lax.fori_loop(0, pairs // B_V_HEADS, body, 0)


def _mixb_sample_rec(kq, v, z, beta, g, s0, og, steps, bb):
    nb = kq.shape[0]
    tile = pl.BlockSpec((bb, B_V_HEADS, 8, HEAD_DIM), lambda i: (i, 0, 0, 0))
    st = pl.BlockSpec((bb, B_V_HEADS, HEAD_DIM, HEAD_DIM), lambda i: (i, 0, 0, 0))
    kern = functools.partial(_mixb_sample_rec_kernel, steps=steps, pairs=bb * B_V_HEADS)
    return pl.pallas_call(
        kern,
        grid=(nb // bb,),
        in_specs=[tile, tile, tile, tile, tile, st, pl.BlockSpec((1, HEAD_DIM), lambda i: (0, 0))],
        out_specs=[tile, st],
        out_shape=[jax.ShapeDtypeStruct((nb, B_V_HEADS, 8, HEAD_DIM), F32),
                   jax.ShapeDtypeStruct((nb, B_V_HEADS, HEAD_DIM, HEAD_DIM), F32)],
        compiler_params=_params("arbitrary"),
        name="mixb_sample_rec",
    )(kq, v, z, beta, g, s0, og)


def _outproj_kernel(ap_ref, bp_ref, xp_ref, g1p_ref, sc2p_ref, sh2p_ref,
                    as_ref, bs_ref, xs_ref, g1s_ref, sc2s_ref, sh2s_ref,
                    w_ref, lg_ref, lb_ref, wr_ref, rb_ref, x1_ref, h2_ref, route_ref, count_ref, base_ref,
                    *, prompt_tiles):
    def tile(a_ref, b_ref, x_ref, g1_ref, sc2_ref, sh2_ref):
        mix = (jnp.dot(a_ref[...].astype(BF16), w_ref[:A_WIDTH, :], preferred_element_type=F32)
               + jnp.dot(b_ref[...].astype(BF16), w_ref[A_WIDTH:, :], preferred_element_type=F32))
        x1 = _layer_norm_rows(ALPHA * x_ref[...] + g1_ref[...] * mix, lg_ref[...], lb_ref[...])
        x1_ref[...] = x1
        h2 = x1 * (1.0 + sc2_ref[...]) + sh2_ref[...]
        h2_ref[...] = h2
        route_ref[...] = _route_tile(_mm3(h2, wr_ref[...]), rb_ref[...], base_ref)
        count_ref[...] = base_ref[...]

    @pl.when(pl.program_id(0) == 0)
    def _():
        base_ref[...] = jnp.zeros_like(base_ref)

    is_prompt = pl.program_id(0) < prompt_tiles

    @pl.when(is_prompt)
    def _():
        tile(ap_ref, bp_ref, xp_ref, g1p_ref, sc2p_ref, sh2p_ref)

    @pl.when(jnp.logical_not(is_prompt))
    def _():
        tile(as_ref, bs_ref, xs_ref, g1s_ref, sc2s_ref, sh2s_ref)


def _outproj(prompt, sample, mod_p, mod_s, w_out, lg, lb, w_router, r_bias, tm, seq):
    n_p = prompt[2].shape[0]
    n_s = sample[2].shape[0]
    pt = n_p // tm
    m = n_p + n_s
    tiles_per_seq = seq // tm
    p_row = lambda w: pl.BlockSpec((tm, w), lambda i: (jnp.minimum(i, pt - 1), 0))
    s_row = lambda w: pl.BlockSpec((tm, w), lambda i: (jnp.maximum(i - pt, 0), 0))
    p_mod = lambda col: pl.BlockSpec((None, 1, D_MODEL), lambda i: (jnp.minimum(i, pt - 1) // tiles_per_seq, 0, col))
    s_mod = lambda col: pl.BlockSpec((tm, D_MODEL), lambda i: (0, col))
    row = lambda w: pl.BlockSpec((tm, w), lambda i: (i, 0))
    full = lambda s: pl.BlockSpec(s, lambda i: (0, 0))
    return pl.pallas_call(
        functools.partial(_outproj_kernel, prompt_tiles=pt),
        grid=(m // tm,),
        in_specs=[p_row(A_WIDTH), p_row(B_VAL_WIDTH), p_row(D_MODEL), p_mod(2), p_mod(4), p_mod(3),
                  s_row(A_WIDTH), s_row(B_VAL_WIDTH), s_row(D_MODEL), s_mod(2), s_mod(4), s_mod(3),
                  full((D_MODEL, D_MODEL)), full((1, D_MODEL)), full((1, D_MODEL)),
                  full((D_MODEL, LANES)), full((1, LANES))],
        out_specs=[row(D_MODEL), row(D_MODEL), row(LANES),
                   full((1, LANES))],
        out_shape=[jax.ShapeDtypeStruct((m, D_MODEL), F32),
                   jax.ShapeDtypeStruct((m, D_MODEL), F32),
                   jax.ShapeDtypeStruct((m, LANES), F32),
                   jax.ShapeDtypeStruct((1, LANES), F32)],
        scratch_shapes=[pltpu.VMEM((1, LANES), F32)],
        compiler_params=_params("arbitrary"),
        name="outproj",
    )(*prompt, mod_p, mod_p, mod_p, *sample, mod_s, mod_s, mod_s, w_out, lg, lb, w_router, r_bias)


def _route_tile(lg, bias, base_ref):
    tm = lg.shape[0]
    lane = lax.broadcasted_iota(jnp.int32, lg.shape, 1)
    neg = -jnp.inf

    def first_argmax(score):
        mx = jnp.max(score, -1, keepdims=True)
        return jnp.min(jnp.where(score == mx, lane, LANES), -1, keepdims=True)

    def pick(vals, idx):
        return jnp.sum(jnp.where(lane == idx, vals, 0.0), -1, keepdims=True)

    gmask = lane < N_GROUPS
    mg = jnp.max(jnp.where(gmask, lg, neg), -1, keepdims=True)
    eg = jnp.where(gmask, jnp.exp(jnp.where(gmask, lg - mg, 0.0)), 0.0)
    pg = eg / jnp.sum(eg, -1, keepdims=True)
    sel_g = first_argmax(jnp.where(gmask, lg + bias, neg))
    p_sel = pick(pg, sel_g)

    lo = N_GROUPS + sel_g * EXPERTS_PER_GROUP
    emask = (lane >= lo) & (lane < lo + EXPERTS_PER_GROUP)
    me = jnp.max(jnp.where(emask, lg, neg), -1, keepdims=True)
    ee = jnp.where(emask, jnp.exp(jnp.where(emask, lg - me, 0.0)), 0.0)
    pe = ee / jnp.sum(ee, -1, keepdims=True)
    score = jnp.where(emask, pe + bias, neg)
    i1 = first_argmax(score)
    i2 = first_argmax(jnp.where(lane == i1, neg, score))
    w1 = pick(pe, i1)
    w2 = pick(pe, i2)
    wsum = w1 + w2
    gate1 = w1 / wsum * p_sel
    gate2 = w2 / wsum * p_sel

    hot = ((lane == i1) | (lane == i2)).astype(BF16)
    ri = lax.broadcasted_iota(jnp.int32, (tm, tm), 0)
    ci = lax.broadcasted_iota(jnp.int32, (tm, tm), 1)
    before = jnp.dot((ri > ci).astype(BF16), hot, preferred_element_type=F32) + base_ref[...]
    rank1 = pick(before, i1)
    rank2 = pick(before, i2)
    base_ref[...] = base_ref[...] + jnp.sum(hot.astype(F32), 0, keepdims=True)

    out = jnp.where(lane == 0, (i1 - N_GROUPS).astype(F32), 0.0)
    out = jnp.where(lane == 1, (i2 - N_GROUPS).astype(F32), out)
    out = jnp.where(lane == 2, gate1, out)
    out = jnp.where(lane == 3, gate2, out)
    out = jnp.where(lane == 4, rank1, out)
    out = jnp.where(lane == 5, rank2, out)
    return out


DISPATCH_ROWS = 512


def _dispatch_kernel(dest_ref, last_ref, nv_ref, h_ref, xb_ref, zbuf, pbuf, zsem, ssem, *, n_tok, n_blocks):
    i = pl.program_id(0)
    nv = nv_ref[0]

    def zero_block(blk):
        return pltpu.make_async_copy(zbuf, xb_ref.at[pl.ds(blk * MOE_ROWS, MOE_ROWS)], zsem)

    @pl.when(i == 0)
    def _():
        zbuf[...] = jnp.zeros_like(zbuf)
        for e in range(N_EXPERTS):
            @pl.when(last_ref[e] >= 0)
            def _():
                zero_block(last_ref[e]).start()

        def tail_start(b, carry):
            zero_block(b).start()
            return carry

        lax.fori_loop(nv, n_blocks, tail_start, 0)
        for e in range(N_EXPERTS):
            @pl.when(last_ref[e] >= 0)
            def _():
                zero_block(0).wait()

        def tail_wait(b, carry):
            zero_block(0).wait()
            return carry

        lax.fori_loop(nv, n_blocks, tail_wait, 0)

    half = D_MODEL // 2
    hi = lax.bitcast_convert_type(h_ref[:, :half].astype(BF16).astype(F32), jnp.uint32)
    lo = lax.bitcast_convert_type(h_ref[:, half:].astype(BF16).astype(F32), jnp.uint32)
    pbuf[...] = (hi & jnp.uint32(0xFFFF0000)) | lax.shift_right_logical(lo, jnp.uint32(16))

    def row_copy(r, choice):
        slot = dest_ref[choice * n_tok + i * DISPATCH_ROWS + r]
        return pltpu.make_async_copy(pbuf.at[pl.ds(r, 1)], xb_ref.at[pl.ds(slot, 1)], ssem)

    def body(r, carry):
        row_copy(r, 0).start()
        row_copy(r, 1).start()
        return carry

    lax.fori_loop(0, DISPATCH_ROWS, body, 0, unroll=8)
    for _ in range(2):
        pltpu.make_async_copy(pbuf, xb_ref.at[pl.ds(0, DISPATCH_ROWS)], ssem).wait()


def _dispatch(dest_flat, last_block, n_valid, h2, n_blocks):
    n_tok = h2.shape[0]
    kern = functools.partial(_dispatch_kernel, n_tok=n_tok, n_blocks=n_blocks)
    return pl.pallas_call(
        kern,
        grid_spec=pltpu.PrefetchScalarGridSpec(
            num_scalar_prefetch=3,
            grid=(n_tok // DISPATCH_ROWS,),
            in_specs=[pl.BlockSpec((DISPATCH_ROWS, D_MODEL), lambda i, *_: (i, 0))],
            out_specs=pl.BlockSpec(memory_space=pl.ANY),
            scratch_shapes=[pltpu.VMEM((MOE_ROWS, D_MODEL // 2), jnp.uint32),
                            pltpu.VMEM((DISPATCH_ROWS, D_MODEL // 2), jnp.uint32),
                            pltpu.SemaphoreType.DMA(()),
                            pltpu.SemaphoreType.DMA(())]),
        out_shape=jax.ShapeDtypeStruct((n_blocks * MOE_ROWS, D_MODEL // 2), jnp.uint32),
        compiler_params=_params("arbitrary"),
        name="dispatch",
    )(dest_flat, last_block, n_valid, h2)


def _expert_kernel(be_ref, nv_ref, first_ref, next_ref, wslot_ref,
                   x_ref, wg_ref, wu_ref, wd_ref, y_ref,
                   wg_st, wu_st, wd_st, wg_s, wu_s, wd_s, wsem):
    i = pl.program_id(0)
    nv = nv_ref[0]

    def weight_copies(e, p):
        return [pltpu.make_async_copy(src.at[e], dst.at[p], wsem.at[p])
                for src, dst in ((wg_ref, wg_st), (wu_ref, wu_st), (wd_ref, wd_st))]

    @pl.when(i == 0)
    def _():
        for c in weight_copies(be_ref[0], 0):
            c.start()

    @pl.when((i < nv) & (first_ref[i] == 1))
    def _():
        p = wslot_ref[i]
        for c in weight_copies(be_ref[i], p):
            c.wait()
        wg_s[...] = wg_st[p].astype(BF16)
        wu_s[...] = wu_st[p].astype(BF16)
        wd_s[...] = wd_st[p].astype(BF16)

        @pl.when(next_ref[i] >= 0)
        def _():
            for c in weight_copies(next_ref[i], 1 - p):
                c.start()

    @pl.when(i < nv)
    def _():
        u = x_ref[...]
        hi = lax.bitcast_convert_type(u & jnp.uint32(0xFFFF0000), F32)
        lo = lax.bitcast_convert_type(lax.shift_left(u, jnp.uint32(16)), F32)
        x = jnp.concatenate([hi, lo], axis=1).astype(BF16)
        hg = jnp.dot(x, wg_s[...], preferred_element_type=F32)
        hu = jnp.dot(x, wu_s[...], preferred_element_type=F32)
        hid = (jax.nn.silu(hg) * hu).astype(BF16)
        y_ref[...] = jnp.dot(hid, wd_s[...], preferred_element_type=F32)

    @pl.when(i >= nv)
    def _():
        y_ref[...] = jnp.zeros_like(y_ref)


def _experts(block_e, n_valid, xb, w_gate, w_up, w_down):
    n_blocks = block_e.shape[0]
    idx = jnp.arange(n_blocks, dtype=jnp.int32)
    first = (idx < n_valid[0]) & ((idx == 0) | (block_e != jnp.roll(block_e, 1)))
    wslot = (jnp.cumsum(first.astype(jnp.int32)) - 1) % 2
    first_at = jnp.where(first, idx, n_blocks)
    next_first = jnp.concatenate([lax.cummin(first_at, reverse=True)[1:], jnp.full((1,), n_blocks, jnp.int32)])
    next_e = jnp.where(next_first < n_blocks, block_e[jnp.minimum(next_first, n_blocks - 1)], -1)
    any_spec = pl.BlockSpec(memory_space=pl.ANY)
    rows = pl.BlockSpec((MOE_ROWS, D_MODEL), lambda i, *_: (i, 0))
    return pl.pallas_call(
        _expert_kernel,
        grid_spec=pltpu.PrefetchScalarGridSpec(
            num_scalar_prefetch=5,
            grid=(n_blocks,),
            in_specs=[pl.BlockSpec((MOE_ROWS, D_MODEL // 2), lambda i, *_: (i, 0)), any_spec, any_spec, any_spec],
            out_specs=rows,
            scratch_shapes=[pltpu.VMEM((2, D_MODEL, D_EXPERT), F32),
                            pltpu.VMEM((2, D_MODEL, D_EXPERT), F32),
                            pltpu.VMEM((2, D_EXPERT, D_MODEL), F32),
                            pltpu.VMEM((D_MODEL, D_EXPERT), BF16),
                            pltpu.VMEM((D_MODEL, D_EXPERT), BF16),
                            pltpu.VMEM((D_EXPERT, D_MODEL), BF16),
                            pltpu.SemaphoreType.DMA((2,))]),
        out_shape=jax.ShapeDtypeStruct((n_blocks * MOE_ROWS, D_MODEL), F32),
        compiler_params=_params("arbitrary"),
        name="experts",
    )(block_e, n_valid, first.astype(jnp.int32), next_e.astype(jnp.int32), wslot.astype(jnp.int32),
      xb, w_gate, w_up, w_down)


def _final_kernel(dest_ref, x1_ref, route_ref, g2_ref, lg_ref, lb_ref, yb_ref, o_ref, ybuf, sem,
                  *, row0, n_tok):
    i = pl.program_id(0)
    tm = o_ref.shape[0]
    slot = i % 2

    def gather_start(tile, s):
        def body(r, carry):
            tok = row0 + tile * tm + r
            for choice in range(2):
                src = dest_ref[choice * n_tok + tok]
                pltpu.make_async_copy(yb_ref.at[pl.ds(src, 1)], ybuf.at[s, choice, pl.ds(r, 1)], sem.at[s]).start()
            return carry

        lax.fori_loop(0, tm, body, 0, unroll=8)

    @pl.when(i == 0)
    def _():
        gather_start(0, 0)

    @pl.when(i + 1 < pl.num_programs(0))
    def _():
        gather_start(i + 1, 1 - slot)

    for choice in range(2):
        pltpu.make_async_copy(yb_ref.at[pl.ds(0, tm)], ybuf.at[slot, choice], sem.at[slot]).wait()
    route = route_ref[...]
    ff = ybuf[slot, 0] * route[:, 2:3] + ybuf[slot, 1] * route[:, 3:4]
    o_ref[...] = _layer_norm_rows(ALPHA * x1_ref[...] + g2_ref[...] * ff, lg_ref[...], lb_ref[...])


def _final(dest_flat, x1, yb, route, mod, lg, lb, tm, rows_per_mod, row0, n_rows):
    n_tok = x1.shape[0]
    off = row0 // tm
    return pl.pallas_call(
        functools.partial(_final_kernel, row0=row0, n_tok=n_tok),
        grid_spec=pltpu.PrefetchScalarGridSpec(
            num_scalar_prefetch=1,
            grid=(n_rows // tm,),
            in_specs=[pl.BlockSpec((tm, D_MODEL), lambda i, *_: (off + i, 0)),
                      pl.BlockSpec((tm, LANES), lambda i, *_: (off + i, 0)),
                      _mod_spec(mod, 5, tm, rows_per_mod),
                      pl.BlockSpec((1, D_MODEL), lambda i, *_: (0, 0)),
                      pl.BlockSpec((1, D_MODEL), lambda i, *_: (0, 0)),
                      pl.BlockSpec(memory_space=pl.ANY)],
            out_specs=pl.BlockSpec((tm, D_MODEL), lambda i, *_: (i, 0)),
            scratch_shapes=[pltpu.VMEM((2, 2, tm, D_MODEL), F32),
                            pltpu.SemaphoreType.DMA((2,))]),
        out_shape=jax.ShapeDtypeStruct((n_rows, D_MODEL), F32),
        compiler_params=_params("arbitrary"),
        name="final",
    )(dest_flat, x1, route, mod, lg, lb, yb)


def _lane_row(vec, offset):
    return jnp.zeros((1, LANES), F32).at[0, offset:offset + vec.shape[0]].set(vec.astype(F32))


def kernel(x_prompt, x_sample, state_conv, state_ssm, c_prompt, c_sample, w_ada, b_ada, w_in, a_ws, a_bs, a_norm_g, a_norm_b, b_conv_w, b_a_log, b_dt_bias, b_onorm_g, w_out, ln1_g, ln1_b, w_router_g, b_router_g, w_router_e, b_router_e, w_gate, w_up, w_down, ln2_g, ln2_b):
    batch, seq, d = x_prompt.shape
    nb, steps, _ = x_sample.shape
    n_p = batch * seq
    n_s = nb * steps
    n_tok = n_p + n_s
    l = 0

    c_rows = batch + nb
    c_pad = (-c_rows) % 8
    c_all = jnp.concatenate([c_prompt, c_sample, jnp.zeros((c_pad, d), F32)], axis=0)
    mod = _ada(c_all, w_ada[l], b_ada[l].reshape(1, -1))
    mod_p = mod[:batch].reshape(batch, 1, 6 * d)
    mod_s = mod[batch:batch + nb]

    w_in_t = jnp.swapaxes(w_in[l], 0, 1)
    w_bd = jnp.pad(w_in_t[MAIN_WIDTH:], ((0, LANES - 2 * B_V_HEADS), (0, 0))).astype(BF16)
    xp = x_prompt.reshape(n_p, d)
    xs = jnp.swapaxes(x_sample, 0, 1).reshape(n_s, d)
    proj_p, bd_p, proj_s, bd_s = _inproj(xp, xs, mod_p, mod_s, w_in_t, w_bd, 512, seq)

    ng = a_norm_g[l].reshape(1, -1)
    nbias = a_norm_b[l].reshape(1, -1)
    bias_tile = jnp.repeat(a_bs[l].T, HEAD_DIM, axis=1)
    a_out_p = _mixa_prompt(proj_p, a_ws[l], bias_tile, ng, nbias, 256)
    coef = jnp.repeat(jnp.transpose(a_ws[l][:, :steps, :steps], (1, 2, 0)).reshape(steps * steps, A_HEADS),
                      HEAD_DIM, axis=1)
    a_out_s, chunkv = _mixa_sample(proj_s, coef, bias_tile[:steps], ng, nbias, steps, nb)

    nega = _lane_row(-jnp.exp(b_a_log[l].astype(F32)), B_V_HEADS)
    dtb = _lane_row(b_dt_bias[l], B_V_HEADS)
    og = b_onorm_g[l].reshape(1, -1)
    b_out_p, ssm_p = _mixb_prompt(proj_p, bd_p, b_conv_w[l], nega, dtb, og, batch, seq)
    qkv0 = 2 * A_WIDTH
    conv_p = proj_p.reshape(batch, seq, MAIN_WIDTH)[:, seq - (B_CONV - 1):, qkv0:qkv0 + B_CONV_CH]

    buf_s = jnp.swapaxes(state_conv[l], 0, 1).reshape((B_CONV - 1) * nb, B_CONV_CH)
    act_s, beta_s, g_s = _mixb_sample_pre(proj_s, buf_s, bd_s, b_conv_w[l], nega, dtb, steps, nb)
    rep = B_V_HEADS // B_QK_HEADS
    act4 = act_s.reshape(steps, nb, B_CONV_CH)
    q8 = jnp.repeat(act4[..., :B_KEY_WIDTH].reshape(steps, nb, B_QK_HEADS, HEAD_DIM), rep, axis=2)
    k8 = jnp.repeat(act4[..., B_KEY_WIDTH:2 * B_KEY_WIDTH].reshape(steps, nb, B_QK_HEADS, HEAD_DIM), rep, axis=2)

    def to_tiles(a):
        a = jnp.transpose(a, (1, 2, 0, 3))
        return jnp.pad(a, ((0, 0), (0, 0), (0, 8 - a.shape[2]), (0, 0)))

    kq_t = jnp.concatenate([to_tiles(k8)[:, :, :4], to_tiles(q8)[:, :, :4]], axis=2)
    v_t = to_tiles(act4[..., 2 * B_KEY_WIDTH:].reshape(steps, nb, B_V_HEADS, HEAD_DIM))
    z_t = to_tiles(proj_s[:, qkv0 + B_CONV_CH:].reshape(steps, nb, B_V_HEADS, HEAD_DIM))

    def gate_tiles(a, off):
        a = a[:, off:off + B_V_HEADS].reshape(steps, nb, B_V_HEADS, 1)
        return to_tiles(jnp.broadcast_to(a, (steps, nb, B_V_HEADS, HEAD_DIM)))

    o_t, ssm_s = _mixb_sample_rec(kq_t, v_t, z_t, gate_tiles(beta_s, 0), gate_tiles(g_s, B_V_HEADS),
                                  state_ssm[l], og, steps, 8)
    b_out_s = jnp.transpose(o_t[:, :, :steps], (2, 0, 1, 3)).reshape(n_s, B_VAL_WIDTH)
    conv_s = jnp.swapaxes(proj_s.reshape(steps, nb, MAIN_WIDTH)[steps - (B_CONV - 1):, :, qkv0:qkv0 + B_CONV_CH], 0, 1)

    w_out_b = w_out[l].astype(BF16)
    lg1 = ln1_g[l].reshape(1, -1)
    lb1 = ln1_b[l].reshape(1, -1)
    w_router = jnp.concatenate([w_router_g[l], w_router_e[l],
                                jnp.zeros((d, LANES - N_GROUPS - N_EXPERTS), F32)], axis=1)
    out_tm = 256
    mod_s_tile = jnp.tile(mod_s, (out_tm // nb, 1))
    r_bias = jnp.concatenate([b_router_g[l], b_router_e[l],
                              jnp.zeros((LANES - N_GROUPS - N_EXPERTS,), F32)]).reshape(1, LANES)
    x1, h2, route, counts_row = _outproj((a_out_p, b_out_p, xp), (a_out_s, b_out_s, xs), mod_p, mod_s_tile,
                                         w_out_b, lg1, lb1, w_router, r_bias, out_tm, seq)

    counts = counts_row[0, N_GROUPS:N_GROUPS + N_EXPERTS].astype(jnp.int32)
    route_t = route[:, :8].T
    eid = route_t[0:2].astype(jnp.int32)
    rank = route_t[4:6].astype(jnp.int32)
    padded = (counts + MOE_ROWS - 1) // MOE_ROWS * MOE_ROWS
    pend = jnp.cumsum(padded)
    pstart = pend - padded
    dest = rank + jnp.sum(jnp.where(eid[None] == jnp.arange(N_EXPERTS, dtype=jnp.int32)[:, None, None],
                                    pstart[:, None, None], 0), axis=0)
    n_blocks = -(-(2 * n_tok) // MOE_ROWS) + N_EXPERTS
    dest_flat = dest.reshape(-1)
    last_block = jnp.where(padded > 0, pend // MOE_ROWS - 1, -1).astype(jnp.int32)
    blk0 = jnp.arange(n_blocks, dtype=jnp.int32) * MOE_ROWS
    block_e = jnp.minimum(jnp.sum(pend[None, :] <= blk0[:, None], axis=1), N_EXPERTS - 1).astype(jnp.int32)
    n_valid = (pend[-1:] // MOE_ROWS).astype(jnp.int32)

    xb = _dispatch(dest_flat, last_block, n_valid, h2, n_blocks)
    yb = _experts(block_e, n_valid, xb, w_gate[l], w_up[l], w_down[l])

    lg2 = ln2_g[l].reshape(1, -1)
    lb2 = ln2_b[l].reshape(1, -1)
    y_p = _final(dest_flat, x1, yb, route, mod_p, lg2, lb2, 512, seq, 0, n_p)
    y_s = _final(dest_flat, x1, yb, route, mod_s, lg2, lb2, nb, 0, n_p, n_s)

    y_prompt = y_p.reshape(batch, seq, d)
    y_sample = jnp.swapaxes(y_s.reshape(steps, nb, d), 0, 1)
    chunkv_s = jnp.swapaxes(chunkv.reshape(steps, nb, A_HEADS, HEAD_DIM), 0, 1)
    return (y_prompt, y_sample, conv_p[None], ssm_p[None], conv_s[None], ssm_s[None], chunkv_s[None])
```

```python
import functools

import jax
import jax.numpy as jnp
from jax import lax
from jax.experimental import pallas as pl
from jax.experimental.pallas import tpu as pltpu

F32 = jnp.float32
BF16 = jnp.bfloat16

D_MODEL = 2048
DEPTH = 1
A_HEADS = 8
HEAD_DIM = 128
A_WIDTH = 1024
A_CHUNK = 128
B_QK_HEADS = 4
B_V_HEADS = 8
B_KEY_WIDTH = 512
B_VAL_WIDTH = 1024
B_CONV = 4
B_CONV_CH = 2048
DN_CHUNK = 64
MAIN_WIDTH = 2 * A_WIDTH + B_CONV_CH + B_VAL_WIDTH
N_GROUPS = 4
EXPERTS_PER_GROUP = 8
N_EXPERTS = 32
D_EXPERT = 512
ALPHA = (2 * DEPTH) ** 0.25
LN_EPS = 1e-5
RMS_EPS = 1e-6
L2_EPS = 1e-6

LANES = 128
VMEM_LIMIT = 56 * 1024 * 1024
MOE_ROWS = 256
MIXB_GROUP = 4


def _params(*sem):
    return pltpu.CompilerParams(dimension_semantics=sem, vmem_limit_bytes=VMEM_LIMIT)


def _mm(a, b):
    return jnp.dot(a.astype(BF16), b.astype(BF16), preferred_element_type=F32)


def _mm_nt(a, b):
    return lax.dot_general(a.astype(BF16), b.astype(BF16), (((1,), (1,)), ((), ())),
                           preferred_element_type=F32)


def _mm_tn(a, b):
    return lax.dot_general(a.astype(BF16), b.astype(BF16), (((0,), (0,)), ((), ())),
                           preferred_element_type=F32)


def _split(x):
    hi = x.astype(BF16)
    lo = (x - hi.astype(F32)).astype(BF16)
    return hi, lo


def _mm_exact_lhs(a_bf16, b):
    hi, lo = _split(b)
    return (jnp.dot(a_bf16, hi, preferred_element_type=F32)
            + jnp.dot(a_bf16, lo, preferred_element_type=F32))


def _mm3(a, b):
    ah, al = _split(a)
    bh, bl = _split(b)
    return (jnp.dot(ah, bh, preferred_element_type=F32) + jnp.dot(ah, bl, preferred_element_type=F32)
            + jnp.dot(al, bh, preferred_element_type=F32))


def _softplus(x):
    return jnp.maximum(x, 0.0) + jnp.log1p(jnp.exp(-jnp.abs(x)))


def _layer_norm_rows(x, g, b):
    mu = jnp.mean(x, -1, keepdims=True)
    xc = x - mu
    var = jnp.mean(xc * xc, -1, keepdims=True)
    return xc * lax.rsqrt(var + LN_EPS) * g + b


def _ada_kernel(c_ref, w_ref, b_ref, o_ref):
    a = jax.nn.silu(c_ref[...]).astype(BF16)
    o_ref[...] = jnp.dot(a, w_ref[...].astype(BF16), preferred_element_type=F32) + b_ref[...]


def _ada(c_all, w_ada, b_ada):
    rows = c_all.shape[0]
    tn = 1024
    return pl.pallas_call(
        _ada_kernel,
        grid=(6 * D_MODEL // tn,),
        in_specs=[pl.BlockSpec((rows, D_MODEL), lambda j: (0, 0)),
                  pl.BlockSpec((D_MODEL, tn), lambda j: (0, j)),
                  pl.BlockSpec((1, tn), lambda j: (0, j))],
        out_specs=pl.BlockSpec((rows, tn), lambda j: (0, j)),
        out_shape=jax.ShapeDtypeStruct((rows, 6 * D_MODEL), F32),
        compiler_params=_params("arbitrary"),
        name="ada",
    )(c_all, w_ada, b_ada)


def _inproj_kernel(xp_ref, scp_ref, shp_ref, xs_ref, scs_ref, shs_ref, w_ref, wbd_ref,
                   op_ref, bdp_ref, os_ref, bds_ref, wb_ref, *, prompt_tiles):
    j = pl.program_id(0)
    i = pl.program_id(1)

    @pl.when(i == 0)
    def _():
        wb_ref[...] = w_ref[...].astype(BF16)

    def project(h, o_ref, bd_ref):
        o_ref[...] = _mm_nt(h, wb_ref[...])

        @pl.when(j == 0)
        def _():
            bd_ref[0] = _mm_nt(h, wbd_ref[...])

        @pl.when(j != 0)
        def _():
            bd_ref[0] = jnp.zeros(bd_ref.shape[1:], F32)

    @pl.when(i < prompt_tiles)
    def _():
        project((xp_ref[...] * (1.0 + scp_ref[...]) + shp_ref[...]).astype(BF16), op_ref, bdp_ref)

    @pl.when(i >= prompt_tiles)
    def _():
        n_s, nb = xs_ref.shape[0], scs_ref.shape[0]
        x = xs_ref[...].reshape(n_s // nb, nb, D_MODEL)
        h = x * (1.0 + scs_ref[...])[None] + shs_ref[...][None]
        project(h.reshape(n_s, D_MODEL).astype(BF16), os_ref, bds_ref)


def _mod_spec(mod, col, tm, rows_per_mod, axis=0):
    if rows_per_mod:
        tiles = rows_per_mod // tm
        return pl.BlockSpec((None, 1, D_MODEL), lambda *g: (g[axis] // tiles, 0, col))
    return pl.BlockSpec((tm, D_MODEL), lambda *g: (0, col))


def _inproj(xp, xs, mod_p, mod_s, w_in_t, w_bd, tm, seq):
    n_p, n_s = xp.shape[0], xs.shape[0]
    nb = mod_s.shape[0]
    tn = 1024
    nj = MAIN_WIDTH // tn
    pt = n_p // tm
    tiles_per_seq = seq // tm
    pi = lambda i: jnp.minimum(i, pt - 1)
    p_mod = lambda col: pl.BlockSpec((None, 1, D_MODEL), lambda j, i: (pi(i) // tiles_per_seq, 0, col))
    s_mod = lambda col: pl.BlockSpec((nb, D_MODEL), lambda j, i: (0, col))
    proj_p, bd_p, proj_s, bd_s = pl.pallas_call(
        functools.partial(_inproj_kernel, prompt_tiles=pt),
        grid=(nj, pt + 1),
        in_specs=[pl.BlockSpec((tm, D_MODEL), lambda j, i: (pi(i), 0)), p_mod(1), p_mod(0),
                  pl.BlockSpec((n_s, D_MODEL), lambda j, i: (0, 0)), s_mod(1), s_mod(0),
                  pl.BlockSpec((tn, D_MODEL), lambda j, i: (j, 0)),
                  pl.BlockSpec((LANES, D_MODEL), lambda j, i: (0, 0))],
        out_specs=[pl.BlockSpec((tm, tn), lambda j, i: (pi(i), j)),
                   pl.BlockSpec((1, tm, LANES), lambda j, i: (j, pi(i), 0)),
                   pl.BlockSpec((n_s, tn), lambda j, i: (0, j)),
                   pl.BlockSpec((1, n_s, LANES), lambda j, i: (j, 0, 0))],
        out_shape=[jax.ShapeDtypeStruct((n_p, MAIN_WIDTH), F32),
                   jax.ShapeDtypeStruct((nj, n_p, LANES), F32),
                   jax.ShapeDtypeStruct((n_s, MAIN_WIDTH), F32),
                   jax.ShapeDtypeStruct((nj, n_s, LANES), F32)],
        scratch_shapes=[pltpu.VMEM((tn, D_MODEL), BF16)],
        compiler_params=_params("arbitrary", "arbitrary"),
        name="inproj",
    )(xp, mod_p, mod_p, xs, mod_s, mod_s, w_in_t, w_bd)
    return proj_p, bd_p[0], proj_s, bd_s[0]


def _mixa_prompt_kernel(p_ref, ws_ref, bias_ref, ng_ref, nb_ref, o_ref):
    rows = p_ref.shape[0]
    u = jax.nn.gelu(p_ref[:, :A_WIDTH])
    v = _layer_norm_rows(jax.nn.gelu(p_ref[:, A_WIDTH:]), ng_ref[...], nb_ref[...])
    ri = lax.broadcasted_iota(jnp.int32, (A_CHUNK, A_CHUNK), 0)
    ci = lax.broadcasted_iota(jnp.int32, (A_CHUNK, A_CHUNK), 1)
    for h in range(A_HEADS):
        w = jnp.where(ri >= ci, ws_ref[h], 0.0).astype(BF16)
        cols = slice(h * HEAD_DIM, (h + 1) * HEAD_DIM)
        for c in range(rows // A_CHUNK):
            rs = slice(c * A_CHUNK, (c + 1) * A_CHUNK)
            s = jnp.dot(w, v[rs, cols].astype(BF16), preferred_element_type=F32) + bias_ref[:, cols]
            o_ref[rs, cols] = u[rs, cols] * s


def _mixa_prompt(proj, a_ws, bias_tile, ng, nb, tm):
    m = proj.shape[0]
    return pl.pallas_call(
        _mixa_prompt_kernel,
        grid=(m // tm,),
        in_specs=[pl.BlockSpec((tm, 2 * A_WIDTH), lambda i: (i, 0)),
                  pl.BlockSpec((A_HEADS, A_CHUNK, A_CHUNK), lambda i: (0, 0, 0)),
                  pl.BlockSpec((A_CHUNK, A_WIDTH), lambda i: (0, 0)),
                  pl.BlockSpec((1, A_WIDTH), lambda i: (0, 0)),
                  pl.BlockSpec((1, A_WIDTH), lambda i: (0, 0))],
        out_specs=pl.BlockSpec((tm, A_WIDTH), lambda i: (i, 0)),
        out_shape=jax.ShapeDtypeStruct((m, A_WIDTH), F32),
        compiler_params=_params("arbitrary"),
        name="mixa_prompt",
    )(proj, a_ws, bias_tile, ng, nb)


def _mixa_sample_kernel(p_ref, coef_ref, bias_ref, ng_ref, nb_ref, o_ref, v_ref, *, steps, nb_rows):
    u = jax.nn.gelu(p_ref[:, :A_WIDTH])
    v = _layer_norm_rows(jax.nn.gelu(p_ref[:, A_WIDTH:]), ng_ref[...], nb_ref[...])
    v_ref[...] = v
    for t in range(steps):
        s = bias_ref[t:t + 1, :]
        for j in range(t + 1):
            s = s + coef_ref[t * steps + j:t * steps + j + 1, :] * v[j * nb_rows:(j + 1) * nb_rows, :]
        rs = slice(t * nb_rows, (t + 1) * nb_rows)
        o_ref[rs, :] = u[rs, :] * s


def _mixa_sample(proj, coef, bias, ng, nb, steps, nb_rows):
    m = proj.shape[0]
    kern = functools.partial(_mixa_sample_kernel, steps=steps, nb_rows=nb_rows)
    return pl.pallas_call(
        kern,
        grid=(1,),
        in_specs=[pl.BlockSpec((m, 2 * A_WIDTH), lambda i: (0, 0)),
                  pl.BlockSpec(coef.shape, lambda i: (0, 0)),
                  pl.BlockSpec(bias.shape, lambda i: (0, 0)),
                  pl.BlockSpec((1, A_WIDTH), lambda i: (0, 0)),
                  pl.BlockSpec((1, A_WIDTH), lambda i: (0, 0))],
        out_specs=[pl.BlockSpec((m, A_WIDTH), lambda i: (0, 0)),
                   pl.BlockSpec((m, A_WIDTH), lambda i: (0, 0))],
        out_shape=[jax.ShapeDtypeStruct((m, A_WIDTH), F32),
                   jax.ShapeDtypeStruct((m, A_WIDTH), F32)],
        compiler_params=_params("arbitrary"),
        name="mixa_sample",
    )(proj, coef, bias, ng, nb)


def _unit_lower_inverse_many(a_list, ri, ci, block):
    eye = (ri == ci).astype(F32)
    pair = (lax.shift_right_logical(ri, 1) == lax.shift_right_logical(ci, 1)) & ((ri & 1) == 1) & ((ci & 1) == 0)
    ts = [eye - jnp.where(pair, a, 0.0) for a in a_list]
    n = 2
    while n < block:
        sh = n.bit_length()
        m = ((lax.shift_right_logical(ri, sh) == lax.shift_right_logical(ci, sh))
             & ((ri & n) != 0) & ((ci & n) == 0))
        xs = [_mm(t, jnp.where(m, a, 0.0)) for t, a in zip(ts, a_list)]
        ts = [t - _mm(x, t) for t, x in zip(ts, xs)]
        n *= 2
    return ts


def _mixb_prompt_kernel(qkv_ref, z_ref, bd_ref, cw_ref, nega_ref, dtb_ref, og_ref,
                        o_ref, sfin_ref, s_ref, cbuf_ref):
    c = pl.program_id(0)
    C = DN_CHUNK
    nseq = qkv_ref.shape[0]

    @pl.when(c == 0)
    def _init():
        s_ref[...] = jnp.zeros_like(s_ref)
        cbuf_ref[:, 0:8, :] = jnp.zeros((nseq, 8, B_CONV_CH), F32)

    cw = cw_ref[...]
    og = og_ref[...]
    ri64 = lax.broadcasted_iota(jnp.int32, (C, C), 0)
    ci64 = lax.broadcasted_iota(jnp.int32, (C, C), 1)
    cum_lhs = (ri64 >= ci64).astype(BF16)

    R = MIXB_GROUP * C
    ri = lax.broadcasted_iota(jnp.int32, (R, R), 0)
    ci = lax.broadcasted_iota(jnp.int32, (R, R), 1)
    same = lax.shift_right_logical(ri, C.bit_length() - 1) == lax.shift_right_logical(ci, C.bit_length() - 1)
    tril = same & (ri >= ci)
    strict = same & (ri > ci)
    rep = B_V_HEADS // B_QK_HEADS

    def lane_col(a, lane):
        return jnp.broadcast_to(a[:, lane:lane + 1], (a.shape[0], HEAD_DIM))

    groups = B_V_HEADS // MIXB_GROUP
    units = [(b, grp) for b in range(nseq) for grp in range(groups)]
    heads_of = lambda grp: list(range(grp * MIXB_GROUP, (grp + 1) * MIXB_GROUP))

    acts, gcs, betas = [], [], []
    for b in range(nseq):
        x = qkv_ref[b]
        cbuf_ref[b, 8:8 + C, :] = x
        y = cbuf_ref[b, 5:5 + C, :] * cw[0:1]
        y = y + cbuf_ref[b, 6:6 + C, :] * cw[1:2]
        y = y + cbuf_ref[b, 7:7 + C, :] * cw[2:3]
        y = y + x * cw[3:4]
        cbuf_ref[b, 0:8, :] = x[C - 8:C, :]
        acts.append(jax.nn.silu(y))
        bd = bd_ref[b]
        betas.append(jax.nn.sigmoid(bd))
        g_all = nega_ref[...] * _softplus(bd + dtb_ref[...])
        gcs.append(_mm_exact_lhs(cum_lhs, g_all))

    kst, qst, gcol, glcol, bcol, decay, a_mat, rhs = {}, {}, {}, {}, {}, {}, {}, {}
    for u in units:
        b, grp = u
        act, gc_all, beta_all = acts[b], gcs[b], betas[b]

        def stack(fn):
            return jnp.concatenate([fn(h) for h in heads_of(grp)], axis=0)

        def l2n(cols0, h):
            s = act[:, cols0 + (h // rep) * HEAD_DIM:cols0 + (h // rep + 1) * HEAD_DIM]
            return s * lax.rsqrt(jnp.sum(s * s, -1, keepdims=True) + L2_EPS)

        kst[u] = stack(lambda h: l2n(B_KEY_WIDTH, h))
        qst[u] = stack(lambda h: l2n(0, h) * (HEAD_DIM ** -0.5))
        vst = stack(lambda h: act[:, 2 * B_KEY_WIDTH + h * HEAD_DIM:2 * B_KEY_WIDTH + (h + 1) * HEAD_DIM])
        gcol[u] = stack(lambda h: lane_col(gc_all, B_V_HEADS + h))
        glcol[u] = stack(lambda h: jnp.broadcast_to(gc_all[C - 1:C, B_V_HEADS + h:B_V_HEADS + h + 1], (C, HEAD_DIM)))
        bcol[u] = stack(lambda h: lane_col(beta_all, h))
        grow = gcol[u].T[0:1, :]
        diff = jnp.concatenate([gcol[u], gcol[u]], axis=1) - grow
        decay[u] = jnp.where(tril, jnp.exp(jnp.where(tril, diff, 0.0)), 0.0)
        rhs[u] = jnp.concatenate([vst * bcol[u], kst[u] * bcol[u] * jnp.exp(gcol[u])], axis=1)
    for u in units:
        a_mat[u] = jnp.where(strict, jnp.concatenate([bcol[u], bcol[u]], axis=1) * _mm_nt(kst[u], kst[u]) * decay[u], 0.0)

    t_inv = _unit_lower_inverse_many([a_mat[u] for u in units], ri, ci, C)
    sol = [_mm(t, rhs[u]) for t, u in zip(t_inv, units)]
    qk = [_mm_nt(qst[u], kst[u]) * decay[u] for u in units]
    ws = []
    for sl, u in zip(sol, units):
        b, grp = u
        q_dec = qst[u] * jnp.exp(gcol[u])
        ws.append([_mm(jnp.concatenate([sl[i * C:(i + 1) * C, HEAD_DIM:], q_dec[i * C:(i + 1) * C]], axis=0),
                       s_ref[b * B_V_HEADS + h]) for i, h in enumerate(heads_of(grp))])
    v_new = [sl[:, :HEAD_DIM] - jnp.concatenate([w[:C] for w in wl], axis=0) for sl, wl in zip(sol, ws)]
    outs = [jnp.concatenate([w[C:] for w in wl], axis=0) + _mm(q, v) for wl, q, v in zip(ws, qk, v_new)]
    for u, v, o in zip(units, v_new, outs):
        b, grp = u
        k_dec = kst[u] * jnp.exp(glcol[u] - gcol[u])
        on = o * lax.rsqrt(jnp.mean(o * o, -1, keepdims=True) + RMS_EPS) * og
        for i, h in enumerate(heads_of(grp)):
            rs = slice(i * C, (i + 1) * C)
            si = b * B_V_HEADS + h
            s_ref[si] = s_ref[si] * jnp.exp(glcol[u][i * C:i * C + 1, :]) + _mm_tn(k_dec[rs], v[rs])
            cols = slice(h * HEAD_DIM, (h + 1) * HEAD_DIM)
            o_ref[b, :, cols] = on[rs] * jax.nn.silu(z_ref[b, :, cols])

    @pl.when(c == pl.num_programs(0) - 1)
    def _fin():
        sfin_ref[...] = s_ref[...]


def _mixb_prompt(proj, bd, conv_w, nega, dtb, og, batch, seq):
    nc = seq // DN_CHUNK
    qkv_blk = 2 * A_WIDTH // B_CONV_CH
    z_blk = (2 * A_WIDTH + B_CONV_CH) // B_VAL_WIDTH
    proj3 = proj.reshape(batch, seq, proj.shape[-1])
    bd3 = bd.reshape(batch, seq, LANES)
    o, s_fin = pl.pallas_call(
        _mixb_prompt_kernel,
        grid=(nc,),
        in_specs=[pl.BlockSpec((batch, DN_CHUNK, B_CONV_CH), lambda c: (0, c, qkv_blk)),
                  pl.BlockSpec((batch, DN_CHUNK, B_VAL_WIDTH), lambda c: (0, c, z_blk)),
                  pl.BlockSpec((batch, DN_CHUNK, LANES), lambda c: (0, c, 0)),
                  pl.BlockSpec((B_CONV, B_CONV_CH), lambda c: (0, 0)),
                  pl.BlockSpec((1, LANES), lambda c: (0, 0)),
                  pl.BlockSpec((1, LANES), lambda c: (0, 0)),
                  pl.BlockSpec((1, HEAD_DIM), lambda c: (0, 0))],
        out_specs=[pl.BlockSpec((batch, DN_CHUNK, B_VAL_WIDTH), lambda c: (0, c, 0)),
                   pl.BlockSpec((batch * B_V_HEADS, HEAD_DIM, HEAD_DIM), lambda c: (0, 0, 0))],
        out_shape=[jax.ShapeDtypeStruct((batch, seq, B_VAL_WIDTH), F32),
                   jax.ShapeDtypeStruct((batch * B_V_HEADS, HEAD_DIM, HEAD_DIM), F32)],
        scratch_shapes=[pltpu.VMEM((batch * B_V_HEADS, HEAD_DIM, HEAD_DIM), F32),
                        pltpu.VMEM((batch, DN_CHUNK + 8, B_CONV_CH), F32)],
        compiler_params=_params("arbitrary"),
        name="mixb_prompt",
    )(proj3, proj3, bd3, conv_w, nega, dtb, og)
    return (o.reshape(batch * seq, B_VAL_WIDTH),
            s_fin.reshape(batch, B_V_HEADS, HEAD_DIM, HEAD_DIM))


def _mixb_sample_pre_kernel(qkv_ref, buf_ref, bd_ref, cw_ref, nega_ref, dtb_ref,
                            act_ref, beta_ref, g_ref, *, steps, nb_rows):
    cw = cw_ref[...]

    def slab(j):
        if j < B_CONV - 1:
            return buf_ref[j * nb_rows:(j + 1) * nb_rows, :]
        jj = j - (B_CONV - 1)
        return qkv_ref[jj * nb_rows:(jj + 1) * nb_rows, :]

    for t in range(steps):
        y = slab(t) * cw[0:1]
        for i in range(1, B_CONV):
            y = y + slab(t + i) * cw[i:i + 1]
        act = jax.nn.silu(y)
        rs = slice(t * nb_rows, (t + 1) * nb_rows)
        for qh in range(B_QK_HEADS):
            cq = slice(qh * HEAD_DIM, (qh + 1) * HEAD_DIM)
            ck = slice(B_KEY_WIDTH + qh * HEAD_DIM, B_KEY_WIDTH + (qh + 1) * HEAD_DIM)
            qs = act[:, cq]
            ks = act[:, ck]
            act_ref[rs, cq] = qs * lax.rsqrt(jnp.sum(qs * qs, -1, keepdims=True) + L2_EPS) * (HEAD_DIM ** -0.5)
            act_ref[rs, ck] = ks * lax.rsqrt(jnp.sum(ks * ks, -1, keepdims=True) + L2_EPS)
        act_ref[rs, 2 * B_KEY_WIDTH:] = act[:, 2 * B_KEY_WIDTH:]
    bd = bd_ref[...]
    beta_ref[...] = jax.nn.sigmoid(bd)
    g_ref[...] = nega_ref[...] * _softplus(bd + dtb_ref[...])


def _mixb_sample_pre(proj, buf, bd, conv_w, nega, dtb, steps, nb_rows):
    m = proj.shape[0]
    qkv_blk = 2 * A_WIDTH // B_CONV_CH
    kern = functools.partial(_mixb_sample_pre_kernel, steps=steps, nb_rows=nb_rows)
    return pl.pallas_call(
        kern,
        grid=(1,),
        in_specs=[pl.BlockSpec((m, B_CONV_CH), lambda i: (0, qkv_blk)),
                  pl.BlockSpec(buf.shape, lambda i: (0, 0)),
                  pl.BlockSpec((m, LANES), lambda i: (0, 0)),
                  pl.BlockSpec((B_CONV, B_CONV_CH), lambda i: (0, 0)),
                  pl.BlockSpec((1, LANES), lambda i: (0, 0)),
                  pl.BlockSpec((1, LANES), lambda i: (0, 0))],
        out_specs=[pl.BlockSpec((m, B_CONV_CH), lambda i: (0, 0)),
                   pl.BlockSpec((m, LANES), lambda i: (0, 0)),
                   pl.BlockSpec((m, LANES), lambda i: (0, 0))],
        out_shape=[jax.ShapeDtypeStruct((m, B_CONV_CH), F32),
                   jax.ShapeDtypeStruct((m, LANES), F32),
                   jax.ShapeDtypeStruct((m, LANES), F32)],
        compiler_params=_params("arbitrary"),
        name="mixb_sample_pre",
    )(proj, buf, bd, conv_w, nega, dtb)


def _mixb_sample_rec_kernel(kq_ref, v_ref, z_ref, beta_ref, g_ref, s0_ref, og_ref,
                            o_ref, s_out_ref, *, steps, pairs):
    og = og_ref[...]
    zpad = jnp.zeros((HEAD_DIM - 8, HEAD_DIM), F32)
    zrows = jnp.zeros((8 - steps, HEAD_DIM), F32)
    heads = range(B_V_HEADS)

    def body(bi, carry):
        kqs = [_mm(kq_ref[bi, h], s0_ref[bi, h]) for h in heads]
        pending = []
        for h in heads:
            kq = kq_ref[bi, h]
            g = g_ref[bi, h]
            beta = beta_ref[bi, h]
            v = v_ref[bi, h]
            gc = [g[0:1]]
            for t in range(1, steps):
                gc.append(gc[-1] + g[t:t + 1])
            k = [kq[t:t + 1] for t in range(steps)]
            q = [kq[4 + t:5 + t] for t in range(steps)]
            d = []
            for t in range(steps):
                acc = v[t:t + 1] - jnp.exp(gc[t]) * kqs[h][t:t + 1]
                for j in range(t):
                    kk = jnp.sum(k[j] * k[t], -1, keepdims=True)
                    acc = acc - jnp.exp(gc[t] - gc[j]) * kk * d[j]
                d.append(beta[t:t + 1] * acc)
            outs = []
            for t in range(steps):
                o = jnp.exp(gc[t]) * kqs[h][4 + t:5 + t]
                for j in range(t + 1):
                    qk = jnp.sum(k[j] * q[t], -1, keepdims=True)
                    o = o + jnp.exp(gc[t] - gc[j]) * qk * d[j]
                outs.append(o * lax.rsqrt(jnp.mean(o * o, -1, keepdims=True) + RMS_EPS) * og)
            o_ref[bi, h] = jnp.concatenate(outs + [zrows], axis=0) * jax.nn.silu(z_ref[bi, h])
            k_dec = jnp.concatenate([jnp.exp(gc[-1] - gc[j]) * k[j] for j in range(steps)] + [zrows], axis=0)
            k_pad = jnp.concatenate([k_dec, zpad], axis=0)
            d_pad = jnp.concatenate(d + [zrows, zpad], axis=0)
            pending.append((k_pad.T, d_pad, jnp.exp(gc[-1])))
        for h, (k_t, d_pad, decay_last) in zip(heads, pending):
            s_out_ref[bi, h] = s0_ref[bi, h] * decay_last + _mm(k_t, d_pad)
        return carry

    lax.fori_loop(0, pairs // B_V_HEADS, body, 0)


def _mixb_sample_rec(kq, v, z, beta, g, s0, og, steps, bb):
    nb = kq.shape[0]
    tile = pl.BlockSpec((bb, B_V_HEADS, 8, HEAD_DIM), lambda i: (i, 0, 0, 0))
    st = pl.BlockSpec((bb, B_V_HEADS, HEAD_DIM, HEAD_DIM), lambda i: (i, 0, 0, 0))
    kern = functools.partial(_mixb_sample_rec_kernel, steps=steps, pairs=bb * B_V_HEADS)
    return pl.pallas_call(
        kern,
        grid=(nb // bb,),
        in_specs=[tile, tile, tile, tile, tile, st, pl.BlockSpec((1, HEAD_DIM), lambda i: (0, 0))],
        out_specs=[tile, st],
        out_shape=[jax.ShapeDtypeStruct((nb, B_V_HEADS, 8, HEAD_DIM), F32),
                   jax.ShapeDtypeStruct((nb, B_V_HEADS, HEAD_DIM, HEAD_DIM), F32)],
        compiler_params=_params("arbitrary"),
        name="mixb_sample_rec",
    )(kq, v, z, beta, g, s0, og)


def _outproj_kernel(ap_ref, bp_ref, xp_ref, g1p_ref, sc2p_ref, sh2p_ref,
                    as_ref, bs_ref, xs_ref, g1s_ref, sc2s_ref, sh2s_ref,
                    w_ref, lg_ref, lb_ref, wr_ref, rb_ref, x1_ref, h2_ref, route_ref, count_ref,
                    base_ref, mix_ref, *, prompt_tiles):
    s = pl.program_id(0)

    @pl.when(s == 0)
    def _():
        base_ref[...] = jnp.zeros_like(base_ref)
        mix_ref[...] = jnp.zeros_like(mix_ref)

    cur_prompt = s < prompt_tiles
    a = jnp.where(cur_prompt, ap_ref[...], as_ref[...]).astype(BF16)
    mix_a = jnp.dot(a, w_ref[:A_WIDTH, :], preferred_element_type=F32)

    def project():
        b = jnp.where(cur_prompt, bp_ref[...], bs_ref[...]).astype(BF16)
        return mix_a + jnp.dot(b, w_ref[A_WIDTH:, :], preferred_element_type=F32)

    prev_prompt = s - 1 < prompt_tiles
    pick = lambda p_ref, s_ref: jnp.where(prev_prompt, p_ref[...], s_ref[...])
    mix = mix_ref[(s + 1) % 2]
    x1 = _layer_norm_rows(ALPHA * pick(xp_ref, xs_ref) + pick(g1p_ref, g1s_ref) * mix, lg_ref[...], lb_ref[...])
    x1_ref[...] = x1
    h2 = x1 * (1.0 + pick(sc2p_ref, sc2s_ref)) + pick(sh2p_ref, sh2s_ref)
    h2_ref[...] = h2
    route_ref[...], mix_new = _route_tile(_mm3(h2, wr_ref[...]), rb_ref[...], base_ref, s >= 1, project)
    count_ref[...] = base_ref[...]
    mix_ref[s % 2] = mix_new


def _outproj(prompt, sample, mod_p, mod_s, w_out, lg, lb, w_router, r_bias, tm, seq):
    n_p = prompt[2].shape[0]
    n_s = sample[2].shape[0]
    pt = n_p // tm
    st = n_s // tm
    nt = pt + st
    m = n_p + n_s
    tiles_per_seq = seq // tm
    cur = lambda s: jnp.minimum(s, nt - 1)
    prev = lambda s: jnp.maximum(s - 1, 0)
    p_idx = lambda t: jnp.minimum(t, pt - 1)
    s_idx = lambda t: jnp.clip(t - pt, 0, st - 1)
    p_cur = lambda w: pl.BlockSpec((tm, w), lambda s: (p_idx(cur(s)), 0))
    s_cur = lambda w: pl.BlockSpec((tm, w), lambda s: (s_idx(cur(s)), 0))
    p_prev = lambda w: pl.BlockSpec((tm, w), lambda s: (p_idx(prev(s)), 0))
    s_prev = lambda w: pl.BlockSpec((tm, w), lambda s: (s_idx(prev(s)), 0))
    p_mod = lambda col: pl.BlockSpec((None, 1, D_MODEL), lambda s: (p_idx(prev(s)) // tiles_per_seq, 0, col))
    s_mod = lambda col: pl.BlockSpec((tm, D_MODEL), lambda s: (0, col))
    row = lambda w: pl.BlockSpec((tm, w), lambda s: (prev(s), 0))
    full = lambda shape: pl.BlockSpec(shape, lambda s: (0, 0))
    return pl.pallas_call(
        functools.partial(_outproj_kernel, prompt_tiles=pt),
        grid=(nt + 1,),
        in_specs=[p_cur(A_WIDTH), p_cur(B_VAL_WIDTH), p_prev(D_MODEL), p_mod(2), p_mod(4), p_mod(3),
                  s_cur(A_WIDTH), s_cur(B_VAL_WIDTH), s_prev(D_MODEL), s_mod(2), s_mod(4), s_mod(3),
                  full((D_MODEL, D_MODEL)), full((1, D_MODEL)), full((1, D_MODEL)),
                  full((D_MODEL, LANES)), full((1, LANES))],
        out_specs=[row(D_MODEL), row(D_MODEL), row(LANES),
                   full((1, LANES))],
        out_shape=[jax.ShapeDtypeStruct((m, D_MODEL), F32),
                   jax.ShapeDtypeStruct((m, D_MODEL), F32),
                   jax.ShapeDtypeStruct((m, LANES), F32),
                   jax.ShapeDtypeStruct((1, LANES), F32)],
        scratch_shapes=[pltpu.VMEM((1, LANES), F32),
                        pltpu.VMEM((2, tm, D_MODEL), F32)],
        compiler_params=_params("arbitrary"),
        name="outproj",
    )(*prompt, mod_p, mod_p, mod_p, *sample, mod_s, mod_s, mod_s, w_out, lg, lb, w_router, r_bias)


def _route_tile(lg, bias, base_ref, valid, between):
    tm = lg.shape[0]
    lane = lax.broadcasted_iota(jnp.int32, lg.shape, 1)
    neg = -jnp.inf

    def first_argmax(score):
        mx = jnp.max(score, -1, keepdims=True)
        return jnp.min(jnp.where(score == mx, lane, LANES), -1, keepdims=True)

    def pick(vals, idx):
        return jnp.sum(jnp.where(lane == idx, vals, 0.0), -1, keepdims=True)

    gmask = lane < N_GROUPS
    mg = jnp.max(jnp.where(gmask, lg, neg), -1, keepdims=True)
    eg = jnp.where(gmask, jnp.exp(jnp.where(gmask, lg - mg, 0.0)), 0.0)
    pg = eg / jnp.sum(eg, -1, keepdims=True)
    sel_g = first_argmax(jnp.where(gmask, lg + bias, neg))
    p_sel = pick(pg, sel_g)

    lo = N_GROUPS + sel_g * EXPERTS_PER_GROUP
    emask = (lane >= lo) & (lane < lo + EXPERTS_PER_GROUP)
    me = jnp.max(jnp.where(emask, lg, neg), -1, keepdims=True)
    ee = jnp.where(emask, jnp.exp(jnp.where(emask, lg - me, 0.0)), 0.0)
    pe = ee / jnp.sum(ee, -1, keepdims=True)
    score = jnp.where(emask, pe + bias, neg)
    i1 = first_argmax(score)
    i2 = first_argmax(jnp.where(lane == i1, neg, score))
    w1 = pick(pe, i1)
    w2 = pick(pe, i2)
    wsum = w1 + w2
    gate1 = w1 / wsum * p_sel
    gate2 = w2 / wsum * p_sel

    hot = ((lane == i1) | (lane == i2)).astype(BF16)
    extra = between()
    ri = lax.broadcasted_iota(jnp.int32, (tm, tm), 0)
    ci = lax.broadcasted_iota(jnp.int32, (tm, tm), 1)
    before = jnp.dot((ri > ci).astype(BF16), hot, preferred_element_type=F32) + base_ref[...]
    rank1 = pick(before, i1)
    rank2 = pick(before, i2)
    base_ref[...] = jnp.where(valid, base_ref[...] + jnp.sum(hot.astype(F32), 0, keepdims=True), base_ref[...])

    out = jnp.where(lane == 0, (i1 - N_GROUPS).astype(F32), 0.0)
    out = jnp.where(lane == 1, (i2 - N_GROUPS).astype(F32), out)
    out = jnp.where(lane == 2, gate1, out)
    out = jnp.where(lane == 3, gate2, out)
    out = jnp.where(lane == 4, rank1, out)
    out = jnp.where(lane == 5, rank2, out)
    return out, extra


DISPATCH_ROWS = 512


def _dispatch_kernel(dest_ref, last_ref, nv_ref, h_ref, xb_ref, zbuf, pbuf, zsem, ssem, *, n_tok, n_blocks):
    i = pl.program_id(0)
    nv = nv_ref[0]

    def zero_block(blk):
        return pltpu.make_async_copy(zbuf, xb_ref.at[pl.ds(blk * MOE_ROWS, MOE_ROWS)], zsem)

    @pl.when(i == 0)
    def _():
        zbuf[...] = jnp.zeros_like(zbuf)
        for e in range(N_EXPERTS):
            @pl.when(last_ref[e] >= 0)
            def _():
                zero_block(last_ref[e]).start()

        def tail_start(b, carry):
            zero_block(b).start()
            return carry

        lax.fori_loop(nv, n_blocks, tail_start, 0)
        for e in range(N_EXPERTS):
            @pl.when(last_ref[e] >= 0)
            def _():
                zero_block(0).wait()

        def tail_wait(b, carry):
            zero_block(0).wait()
            return carry

        lax.fori_loop(nv, n_blocks, tail_wait, 0)

    half = D_MODEL // 2
    hi = lax.bitcast_convert_type(h_ref[:, :half].astype(BF16).astype(F32), jnp.uint32)
    lo = lax.bitcast_convert_type(h_ref[:, half:].astype(BF16).astype(F32), jnp.uint32)
    pbuf[...] = (hi & jnp.uint32(0xFFFF0000)) | lax.shift_right_logical(lo, jnp.uint32(16))

    def row_copy(r, choice):
        slot = dest_ref[choice * n_tok + i * DISPATCH_ROWS + r]
        return pltpu.make_async_copy(pbuf.at[pl.ds(r, 1)], xb_ref.at[pl.ds(slot, 1)], ssem)

    def body(r, carry):
        row_copy(r, 0).start()
        row_copy(r, 1).start()
        return carry

    lax.fori_loop(0, DISPATCH_ROWS, body, 0, unroll=8)
    for _ in range(2):
        pltpu.make_async_copy(pbuf, xb_ref.at[pl.ds(0, DISPATCH_ROWS)], ssem).wait()


def _dispatch(dest_flat, last_block, n_valid, h2, n_blocks):
    n_tok = h2.shape[0]
    kern = functools.partial(_dispatch_kernel, n_tok=n_tok, n_blocks=n_blocks)
    return pl.pallas_call(
        kern,
        grid_spec=pltpu.PrefetchScalarGridSpec(
            num_scalar_prefetch=3,
            grid=(n_tok // DISPATCH_ROWS,),
            in_specs=[pl.BlockSpec((DISPATCH_ROWS, D_MODEL), lambda i, *_: (i, 0))],
            out_specs=pl.BlockSpec(memory_space=pl.ANY),
            scratch_shapes=[pltpu.VMEM((MOE_ROWS, D_MODEL // 2), jnp.uint32),
                            pltpu.VMEM((DISPATCH_ROWS, D_MODEL // 2), jnp.uint32),
                            pltpu.SemaphoreType.DMA(()),
                            pltpu.SemaphoreType.DMA(())]),
        out_shape=jax.ShapeDtypeStruct((n_blocks * MOE_ROWS, D_MODEL // 2), jnp.uint32),
        compiler_params=_params("arbitrary"),
        name="dispatch",
    )(dest_flat, last_block, n_valid, h2)


def _expert_kernel(be_ref, nv_ref, first_ref, next_ref, wslot_ref,
                   x_ref, wg_ref, wu_ref, wd_ref, y_ref,
                   wg_st, wu_st, wd_st, wg_s, wu_s, wd_s, wsem):
    i = pl.program_id(0)
    nv = nv_ref[0]

    def weight_copies(e, p):
        return [pltpu.make_async_copy(src.at[e], dst.at[p], wsem.at[p])
                for src, dst in ((wg_ref, wg_st), (wu_ref, wu_st), (wd_ref, wd_st))]

    @pl.when(i == 0)
    def _():
        for c in weight_copies(be_ref[0], 0):
            c.start()

    @pl.when((i < nv) & (first_ref[i] == 1))
    def _():
        p = wslot_ref[i]
        for c in weight_copies(be_ref[i], p):
            c.wait()
        wg_s[...] = wg_st[p].astype(BF16)
        wu_s[...] = wu_st[p].astype(BF16)
        wd_s[...] = wd_st[p].astype(BF16)

        @pl.when(next_ref[i] >= 0)
        def _():
            for c in weight_copies(next_ref[i], 1 - p):
                c.start()

    @pl.when(i < nv)
    def _():
        u = x_ref[...]
        hi = lax.bitcast_convert_type(u & jnp.uint32(0xFFFF0000), F32)
        lo = lax.bitcast_convert_type(lax.shift_left(u, jnp.uint32(16)), F32)
        x = jnp.concatenate([hi, lo], axis=1).astype(BF16)
        hg = jnp.dot(x, wg_s[...], preferred_element_type=F32)
        hu = jnp.dot(x, wu_s[...], preferred_element_type=F32)
        hid = (jax.nn.silu(hg) * hu).astype(BF16)
        y_ref[...] = jnp.dot(hid, wd_s[...], preferred_element_type=F32)

    @pl.when(i >= nv)
    def _():
        y_ref[...] = jnp.zeros_like(y_ref)


def _experts(block_e, n_valid, xb, w_gate, w_up, w_down):
    n_blocks = block_e.shape[0]
    idx = jnp.arange(n_blocks, dtype=jnp.int32)
    first = (idx < n_valid[0]) & ((idx == 0) | (block_e != jnp.roll(block_e, 1)))
    wslot = (jnp.cumsum(first.astype(jnp.int32)) - 1) % 2
    first_at = jnp.where(first, idx, n_blocks)
    next_first = jnp.concatenate([lax.cummin(first_at, reverse=True)[1:], jnp.full((1,), n_blocks, jnp.int32)])
    next_e = jnp.where(next_first < n_blocks, block_e[jnp.minimum(next_first, n_blocks - 1)], -1)
    any_spec = pl.BlockSpec(memory_space=pl.ANY)
    rows = pl.BlockSpec((MOE_ROWS, D_MODEL), lambda i, *_: (i, 0))
    return pl.pallas_call(
        _expert_kernel,
        grid_spec=pltpu.PrefetchScalarGridSpec(
            num_scalar_prefetch=5,
            grid=(n_blocks,),
            in_specs=[pl.BlockSpec((MOE_ROWS, D_MODEL // 2), lambda i, *_: (i, 0)), any_spec, any_spec, any_spec],
            out_specs=rows,
            scratch_shapes=[pltpu.VMEM((2, D_MODEL, D_EXPERT), F32),
                            pltpu.VMEM((2, D_MODEL, D_EXPERT), F32),
                            pltpu.VMEM((2, D_EXPERT, D_MODEL), F32),
                            pltpu.VMEM((D_MODEL, D_EXPERT), BF16),
                            pltpu.VMEM((D_MODEL, D_EXPERT), BF16),
                            pltpu.VMEM((D_EXPERT, D_MODEL), BF16),
                            pltpu.SemaphoreType.DMA((2,))]),
        out_shape=jax.ShapeDtypeStruct((n_blocks * MOE_ROWS, D_MODEL), F32),
        compiler_params=_params("arbitrary"),
        name="experts",
    )(block_e, n_valid, first.astype(jnp.int32), next_e.astype(jnp.int32), wslot.astype(jnp.int32),
      xb, w_gate, w_up, w_down)


def _final_kernel(dest_ref, x1_ref, route_ref, g2_ref, lg_ref, lb_ref, yb_ref, o_ref, ybuf, sem,
                  *, row0, n_tok):
    i = pl.program_id(0)
    tm = o_ref.shape[0]
    slot = i % 2

    def gather_start(tile, s):
        def body(r, carry):
            tok = row0 + tile * tm + r
            for choice in range(2):
                src = dest_ref[choice * n_tok + tok]
                pltpu.make_async_copy(yb_ref.at[pl.ds(src, 1)], ybuf.at[s, choice, pl.ds(r, 1)], sem.at[s]).start()
            return carry

        lax.fori_loop(0, tm, body, 0, unroll=8)

    @pl.when(i == 0)
    def _():
        gather_start(0, 0)

    @pl.when(i + 1 < pl.num_programs(0))
    def _():
        gather_start(i + 1, 1 - slot)

    for choice in range(2):
        pltpu.make_async_copy(yb_ref.at[pl.ds(0, tm)], ybuf.at[slot, choice], sem.at[slot]).wait()
    route = route_ref[...]
    ff = ybuf[slot, 0] * route[:, 2:3] + ybuf[slot, 1] * route[:, 3:4]
    o_ref[...] = _layer_norm_rows(ALPHA * x1_ref[...] + g2_ref[...] * ff, lg_ref[...], lb_ref[...])


def _final(dest_flat, x1, yb, route, mod, lg, lb, tm, rows_per_mod, row0, n_rows):
    n_tok = x1.shape[0]
    off = row0 // tm
    return pl.pallas_call(
        functools.partial(_final_kernel, row0=row0, n_tok=n_tok),
        grid_spec=pltpu.PrefetchScalarGridSpec(
            num_scalar_prefetch=1,
            grid=(n_rows // tm,),
            in_specs=[pl.BlockSpec((tm, D_MODEL), lambda i, *_: (off + i, 0)),
                      pl.BlockSpec((tm, LANES), lambda i, *_: (off + i, 0)),
                      _mod_spec(mod, 5, tm, rows_per_mod),
                      pl.BlockSpec((1, D_MODEL), lambda i, *_: (0, 0)),
                      pl.BlockSpec((1, D_MODEL), lambda i, *_: (0, 0)),
                      pl.BlockSpec(memory_space=pl.ANY)],
            out_specs=pl.BlockSpec((tm, D_MODEL), lambda i, *_: (i, 0)),
            scratch_shapes=[pltpu.VMEM((2, 2, tm, D_MODEL), F32),
                            pltpu.SemaphoreType.DMA((2,))]),
        out_shape=jax.ShapeDtypeStruct((n_rows, D_MODEL), F32),
        compiler_params=_params("arbitrary"),
        name="final",
    )(dest_flat, x1, route, mod, lg, lb, yb)


def _lane_row(vec, offset):
    return jnp.zeros((1, LANES), F32).at[0, offset:offset + vec.shape[0]].set(vec.astype(F32))


def kernel(x_prompt, x_sample, state_conv, state_ssm, c_prompt, c_sample, w_ada, b_ada, w_in, a_ws, a_bs, a_norm_g, a_norm_b, b_conv_w, b_a_log, b_dt_bias, b_onorm_g, w_out, ln1_g, ln1_b, w_router_g, b_router_g, w_router_e, b_router_e, w_gate, w_up, w_down, ln2_g, ln2_b):
    batch, seq, d = x_prompt.shape
    nb, steps, _ = x_sample.shape
    n_p = batch * seq
    n_s = nb * steps
    n_tok = n_p + n_s
    l = 0

    c_rows = batch + nb
    c_pad = (-c_rows) % 8
    c_all = jnp.concatenate([c_prompt, c_sample, jnp.zeros((c_pad, d), F32)], axis=0)
    mod = _ada(c_all, w_ada[l], b_ada[l].reshape(1, -1))
    mod_p = mod[:batch].reshape(batch, 1, 6 * d)
    mod_s = mod[batch:batch + nb]

    w_in_t = jnp.swapaxes(w_in[l], 0, 1)
    w_bd = jnp.pad(w_in_t[MAIN_WIDTH:], ((0, LANES - 2 * B_V_HEADS), (0, 0))).astype(BF16)
    xp = x_prompt.reshape(n_p, d)
    xs = jnp.swapaxes(x_sample, 0, 1).reshape(n_s, d)
    proj_p, bd_p, proj_s, bd_s = _inproj(xp, xs, mod_p, mod_s, w_in_t, w_bd, 512, seq)

    ng = a_norm_g[l].reshape(1, -1)
    nbias = a_norm_b[l].reshape(1, -1)
    bias_tile = jnp.repeat(a_bs[l].T, HEAD_DIM, axis=1)
    a_out_p = _mixa_prompt(proj_p, a_ws[l], bias_tile, ng, nbias, 256)
    coef = jnp.repeat(jnp.transpose(a_ws[l][:, :steps, :steps], (1, 2, 0)).reshape(steps * steps, A_HEADS),
                      HEAD_DIM, axis=1)
    a_out_s, chunkv = _mixa_sample(proj_s, coef, bias_tile[:steps], ng, nbias, steps, nb)

    nega = _lane_row(-jnp.exp(b_a_log[l].astype(F32)), B_V_HEADS)
    dtb = _lane_row(b_dt_bias[l], B_V_HEADS)
    og = b_onorm_g[l].reshape(1, -1)
    b_out_p, ssm_p = _mixb_prompt(proj_p, bd_p, b_conv_w[l], nega, dtb, og, batch, seq)
    qkv0 = 2 * A_WIDTH
    conv_p = proj_p.reshape(batch, seq, MAIN_WIDTH)[:, seq - (B_CONV - 1):, qkv0:qkv0 + B_CONV_CH]

    buf_s = jnp.swapaxes(state_conv[l], 0, 1).reshape((B_CONV - 1) * nb, B_CONV_CH)
    act_s, beta_s, g_s = _mixb_sample_pre(proj_s, buf_s, bd_s, b_conv_w[l], nega, dtb, steps, nb)
    rep = B_V_HEADS // B_QK_HEADS
    act4 = act_s.reshape(steps, nb, B_CONV_CH)
    q8 = jnp.repeat(act4[..., :B_KEY_WIDTH].reshape(steps, nb, B_QK_HEADS, HEAD_DIM), rep, axis=2)
    k8 = jnp.repeat(act4[..., B_KEY_WIDTH:2 * B_KEY_WIDTH].reshape(steps, nb, B_QK_HEADS, HEAD_DIM), rep, axis=2)

    def to_tiles(a):
        a = jnp.transpose(a, (1, 2, 0, 3))
        return jnp.pad(a, ((0, 0), (0, 0), (0, 8 - a.shape[2]), (0, 0)))

    kq_t = jnp.concatenate([to_tiles(k8)[:, :, :4], to_tiles(q8)[:, :, :4]], axis=2)
    v_t = to_tiles(act4[..., 2 * B_KEY_WIDTH:].reshape(steps, nb, B_V_HEADS, HEAD_DIM))
    z_t = to_tiles(proj_s[:, qkv0 + B_CONV_CH:].reshape(steps, nb, B_V_HEADS, HEAD_DIM))

    def gate_tiles(a, off):
        a = a[:, off:off + B_V_HEADS].reshape(steps, nb, B_V_HEADS, 1)
        return to_tiles(jnp.broadcast_to(a, (steps, nb, B_V_HEADS, HEAD_DIM)))

    o_t, ssm_s = _mixb_sample_rec(kq_t, v_t, z_t, gate_tiles(beta_s, 0), gate_tiles(g_s, B_V_HEADS),
                                  state_ssm[l], og, steps, 8)
    b_out_s = jnp.transpose(o_t[:, :, :steps], (2, 0, 1, 3)).reshape(n_s, B_VAL_WIDTH)
    conv_s = jnp.swapaxes(proj_s.reshape(steps, nb, MAIN_WIDTH)[steps - (B_CONV - 1):, :, qkv0:qkv0 + B_CONV_CH], 0, 1)

    w_out_b = w_out[l].astype(BF16)
    lg1 = ln1_g[l].reshape(1, -1)
    lb1 = ln1_b[l].reshape(1, -1)
    w_router = jnp.concatenate([w_router_g[l], w_router_e[l],
                                jnp.zeros((d, LANES - N_GROUPS - N_EXPERTS), F32)], axis=1)
    out_tm = 256
    mod_s_tile = jnp.tile(mod_s, (out_tm // nb, 1))
    r_bias = jnp.concatenate([b_router_g[l], b_router_e[l],
                              jnp.zeros((LANES - N_GROUPS - N_EXPERTS,), F32)]).reshape(1, LANES)
    x1, h2, route, counts_row = _outproj((a_out_p, b_out_p, xp), (a_out_s, b_out_s, xs), mod_p, mod_s_tile,
                                         w_out_b, lg1, lb1, w_router, r_bias, out_tm, seq)

    counts = counts_row[0, N_GROUPS:N_GROUPS + N_EXPERTS].astype(jnp.int32)
    route_t = route[:, :8].T
    eid = route_t[0:2].astype(jnp.int32)
    rank = route_t[4:6].astype(jnp.int32)
    padded = (counts + MOE_ROWS - 1) // MOE_ROWS * MOE_ROWS
    pend = jnp.cumsum(padded)
    pstart = pend - padded
    dest = rank + jnp.sum(jnp.where(eid[None] == jnp.arange(N_EXPERTS, dtype=jnp.int32)[:, None, None],
                                    pstart[:, None, None], 0), axis=0)
    n_blocks = -(-(2 * n_tok) // MOE_ROWS) + N_EXPERTS
    dest_flat = dest.reshape(-1)
    last_block = jnp.where(padded > 0, pend // MOE_ROWS - 1, -1).astype(jnp.int32)
    blk0 = jnp.arange(n_blocks, dtype=jnp.int32) * MOE_ROWS
    block_e = jnp.minimum(jnp.sum(pend[None, :] <= blk0[:, None], axis=1), N_EXPERTS - 1).astype(jnp.int32)
    n_valid = (pend[-1:] // MOE_ROWS).astype(jnp.int32)

    xb = _dispatch(dest_flat, last_block, n_valid, h2, n_blocks)
    yb = _experts(block_e, n_valid, xb, w_gate[l], w_up[l], w_down[l])

    lg2 = ln2_g[l].reshape(1, -1)
    lb2 = ln2_b[l].reshape(1, -1)
    y_p = _final(dest_flat, x1, yb, route, mod_p, lg2, lb2, 512, seq, 0, n_p)
    y_s = _final(dest_flat, x1, yb, route, mod_s, lg2, lb2, nb, 0, n_p, n_s)

    y_prompt = y_p.reshape(batch, seq, d)
    y_sample = jnp.swapaxes(y_s.reshape(steps, nb, d), 0, 1)
    chunkv_s = jnp.swapaxes(chunkv.reshape(steps, nb, A_HEADS, HEAD_DIM), 0, 1)
    return (y_prompt, y_sample, conv_p[None], ssm_p[None], conv_s[None], ssm_s[None], chunkv_s[None])
```

```python
import functools

import jax
import jax.numpy as jnp
from jax import lax
from jax.experimental import pallas as pl
from jax.experimental.pallas import tpu as pltpu

F32 = jnp.float32
BF16 = jnp.bfloat16

D_MODEL = 2048
DEPTH = 1
A_HEADS = 8
HEAD_DIM = 128
A_WIDTH = 1024
A_CHUNK = 128
B_QK_HEADS = 4
B_V_HEADS = 8
B_KEY_WIDTH = 512
B_VAL_WIDTH = 1024
B_CONV = 4
B_CONV_CH = 2048
DN_CHUNK = 64
MAIN_WIDTH = 2 * A_WIDTH + B_CONV_CH + B_VAL_WIDTH
N_GROUPS = 4
EXPERTS_PER_GROUP = 8
N_EXPERTS = 32
D_EXPERT = 512
ALPHA = (2 * DEPTH) ** 0.25
LN_EPS = 1e-5
RMS_EPS = 1e-6
L2_EPS = 1e-6

LANES = 128
VMEM_LIMIT = 56 * 1024 * 1024
MOE_ROWS = 256
MIXB_GROUP = 4


def _params(*sem):
    return pltpu.CompilerParams(dimension_semantics=sem, vmem_limit_bytes=VMEM_LIMIT)


def _mm(a, b):
    return jnp.dot(a.astype(BF16), b.astype(BF16), preferred_element_type=F32)


def _mm_nt(a, b):
    return lax.dot_general(a.astype(BF16), b.astype(BF16), (((1,), (1,)), ((), ())),
                           preferred_element_type=F32)


def _mm_tn(a, b):
    return lax.dot_general(a.astype(BF16), b.astype(BF16), (((0,), (0,)), ((), ())),
                           preferred_element_type=F32)


def _split(x):
    hi = x.astype(BF16)
    lo = (x - hi.astype(F32)).astype(BF16)
    return hi, lo


def _mm_exact_lhs(a_bf16, b):
    hi, lo = _split(b)
    return (jnp.dot(a_bf16, hi, preferred_element_type=F32)
            + jnp.dot(a_bf16, lo, preferred_element_type=F32))


def _mm3(a, b):
    ah, al = _split(a)
    bh, bl = _split(b)
    return (jnp.dot(ah, bh, preferred_element_type=F32) + jnp.dot(ah, bl, preferred_element_type=F32)
            + jnp.dot(al, bh, preferred_element_type=F32))


def _softplus(x):
    return jnp.maximum(x, 0.0) + jnp.log1p(jnp.exp(-jnp.abs(x)))


def _layer_norm_rows(x, g, b):
    mu = jnp.mean(x, -1, keepdims=True)
    xc = x - mu
    var = jnp.mean(xc * xc, -1, keepdims=True)
    return xc * lax.rsqrt(var + LN_EPS) * g + b


def _ada_kernel(c_ref, w_ref, b_ref, o_ref):
    a = jax.nn.silu(c_ref[...]).astype(BF16)
    o_ref[...] = jnp.dot(a, w_ref[...].astype(BF16), preferred_element_type=F32) + b_ref[...]


def _ada(c_all, w_ada, b_ada):
    rows = c_all.shape[0]
    tn = 1024
    return pl.pallas_call(
        _ada_kernel,
        grid=(6 * D_MODEL // tn,),
        in_specs=[pl.BlockSpec((rows, D_MODEL), lambda j: (0, 0)),
                  pl.BlockSpec((D_MODEL, tn), lambda j: (0, j)),
                  pl.BlockSpec((1, tn), lambda j: (0, j))],
        out_specs=pl.BlockSpec((rows, tn), lambda j: (0, j)),
        out_shape=jax.ShapeDtypeStruct((rows, 6 * D_MODEL), F32),
        compiler_params=_params("arbitrary"),
        name="ada",
    )(c_all, w_ada, b_ada)


def _inproj_kernel(xp_ref, scp_ref, shp_ref, xs_ref, scs_ref, shs_ref, w_ref, wbd_ref,
                   op_ref, bdp_ref, os_ref, bds_ref, wb_ref, *, prompt_tiles):
    j = pl.program_id(0)
    i = pl.program_id(1)

    @pl.when(i == 0)
    def _():
        wb_ref[...] = w_ref[...].astype(BF16)

    def project(h, o_ref, bd_ref):
        o_ref[...] = _mm_nt(h, wb_ref[...])

        @pl.when(j == 0)
        def _():
            bd_ref[0] = _mm_nt(h, wbd_ref[...])

        @pl.when(j != 0)
        def _():
            bd_ref[0] = jnp.zeros(bd_ref.shape[1:], F32)

    @pl.when(i < prompt_tiles)
    def _():
        project((xp_ref[...] * (1.0 + scp_ref[...]) + shp_ref[...]).astype(BF16), op_ref, bdp_ref)

    @pl.when(i >= prompt_tiles)
    def _():
        n_s, nb = xs_ref.shape[0], scs_ref.shape[0]
        x = xs_ref[...].reshape(n_s // nb, nb, D_MODEL)
        h = x * (1.0 + scs_ref[...])[None] + shs_ref[...][None]
        project(h.reshape(n_s, D_MODEL).astype(BF16), os_ref, bds_ref)


def _mod_spec(mod, col, tm, rows_per_mod, axis=0):
    if rows_per_mod:
        tiles = rows_per_mod // tm
        return pl.BlockSpec((None, 1, D_MODEL), lambda *g: (g[axis] // tiles, 0, col))
    return pl.BlockSpec((tm, D_MODEL), lambda *g: (0, col))


def _inproj(xp, xs, mod_p, mod_s, w_in_t, w_bd, tm, seq):
    n_p, n_s = xp.shape[0], xs.shape[0]
    nb = mod_s.shape[0]
    tn = 1024
    nj = MAIN_WIDTH // tn
    pt = n_p // tm
    tiles_per_seq = seq // tm
    pi = lambda i: jnp.minimum(i, pt - 1)
    p_mod = lambda col: pl.BlockSpec((None, 1, D_MODEL), lambda j, i: (pi(i) // tiles_per_seq, 0, col))
    s_mod = lambda col: pl.BlockSpec((nb, D_MODEL), lambda j, i: (0, col))
    proj_p, bd_p, proj_s, bd_s = pl.pallas_call(
        functools.partial(_inproj_kernel, prompt_tiles=pt),
        grid=(nj, pt + 1),
        in_specs=[pl.BlockSpec((tm, D_MODEL), lambda j, i: (pi(i), 0)), p_mod(1), p_mod(0),
                  pl.BlockSpec((n_s, D_MODEL), lambda j, i: (0, 0)), s_mod(1), s_mod(0),
                  pl.BlockSpec((tn, D_MODEL), lambda j, i: (j, 0)),
                  pl.BlockSpec((LANES, D_MODEL), lambda j, i: (0, 0))],
        out_specs=[pl.BlockSpec((tm, tn), lambda j, i: (pi(i), j)),
                   pl.BlockSpec((1, tm, LANES), lambda j, i: (j, pi(i), 0)),
                   pl.BlockSpec((n_s, tn), lambda j, i: (0, j)),
                   pl.BlockSpec((1, n_s, LANES), lambda j, i: (j, 0, 0))],
        out_shape=[jax.ShapeDtypeStruct((n_p, MAIN_WIDTH), F32),
                   jax.ShapeDtypeStruct((nj, n_p, LANES), F32),
                   jax.ShapeDtypeStruct((n_s, MAIN_WIDTH), F32),
                   jax.ShapeDtypeStruct((nj, n_s, LANES), F32)],
        scratch_shapes=[pltpu.VMEM((tn, D_MODEL), BF16)],
        compiler_params=_params("arbitrary", "arbitrary"),
        name="inproj",
    )(xp, mod_p, mod_p, xs, mod_s, mod_s, w_in_t, w_bd)
    return proj_p, bd_p[0], proj_s, bd_s[0]


def _mixa_prompt_kernel(p_ref, ws_ref, bias_ref, ng_ref, nb_ref, o_ref):
    rows = p_ref.shape[0]
    u = jax.nn.gelu(p_ref[:, :A_WIDTH])
    v = _layer_norm_rows(jax.nn.gelu(p_ref[:, A_WIDTH:]), ng_ref[...], nb_ref[...])
    ri = lax.broadcasted_iota(jnp.int32, (A_CHUNK, A_CHUNK), 0)
    ci = lax.broadcasted_iota(jnp.int32, (A_CHUNK, A_CHUNK), 1)
    for h in range(A_HEADS):
        w = jnp.where(ri >= ci, ws_ref[h], 0.0).astype(BF16)
        cols = slice(h * HEAD_DIM, (h + 1) * HEAD_DIM)
        for c in range(rows // A_CHUNK):
            rs = slice(c * A_CHUNK, (c + 1) * A_CHUNK)
            s = jnp.dot(w, v[rs, cols].astype(BF16), preferred_element_type=F32) + bias_ref[:, cols]
            o_ref[rs, cols] = u[rs, cols] * s


def _mixa_prompt(proj, a_ws, bias_tile, ng, nb, tm):
    m = proj.shape[0]
    return pl.pallas_call(
        _mixa_prompt_kernel,
        grid=(m // tm,),
        in_specs=[pl.BlockSpec((tm, 2 * A_WIDTH), lambda i: (i, 0)),
                  pl.BlockSpec((A_HEADS, A_CHUNK, A_CHUNK), lambda i: (0, 0, 0)),
                  pl.BlockSpec((A_CHUNK, A_WIDTH), lambda i: (0, 0)),
                  pl.BlockSpec((1, A_WIDTH), lambda i: (0, 0)),
                  pl.BlockSpec((1, A_WIDTH), lambda i: (0, 0))],
        out_specs=pl.BlockSpec((tm, A_WIDTH), lambda i: (i, 0)),
        out_shape=jax.ShapeDtypeStruct((m, A_WIDTH), F32),
        compiler_params=_params("arbitrary"),
        name="mixa_prompt",
    )(proj, a_ws, bias_tile, ng, nb)


def _mixa_sample_kernel(p_ref, coef_ref, bias_ref, ng_ref, nb_ref, o_ref, v_ref, *, steps, nb_rows):
    u = jax.nn.gelu(p_ref[:, :A_WIDTH])
    v = _layer_norm_rows(jax.nn.gelu(p_ref[:, A_WIDTH:]), ng_ref[...], nb_ref[...])
    v_ref[...] = v
    for t in range(steps):
        s = bias_ref[t:t + 1, :]
        for j in range(t + 1):
            s = s + coef_ref[t * steps + j:t * steps + j + 1, :] * v[j * nb_rows:(j + 1) * nb_rows, :]
        rs = slice(t * nb_rows, (t + 1) * nb_rows)
        o_ref[rs, :] = u[rs, :] * s


def _mixa_sample(proj, coef, bias, ng, nb, steps, nb_rows):
    m = proj.shape[0]
    kern = functools.partial(_mixa_sample_kernel, steps=steps, nb_rows=nb_rows)
    return pl.pallas_call(
        kern,
        grid=(1,),
        in_specs=[pl.BlockSpec((m, 2 * A_WIDTH), lambda i: (0, 0)),
                  pl.BlockSpec(coef.shape, lambda i: (0, 0)),
                  pl.BlockSpec(bias.shape, lambda i: (0, 0)),
                  pl.BlockSpec((1, A_WIDTH), lambda i: (0, 0)),
                  pl.BlockSpec((1, A_WIDTH), lambda i: (0, 0))],
        out_specs=[pl.BlockSpec((m, A_WIDTH), lambda i: (0, 0)),
                   pl.BlockSpec((m, A_WIDTH), lambda i: (0, 0))],
        out_shape=[jax.ShapeDtypeStruct((m, A_WIDTH), F32),
                   jax.ShapeDtypeStruct((m, A_WIDTH), F32)],
        compiler_params=_params("arbitrary"),
        name="mixa_sample",
    )(proj, coef, bias, ng, nb)


def _unit_lower_inverse_many(a_list, ri, ci, block):
    eye = (ri == ci).astype(F32)
    pair = (lax.shift_right_logical(ri, 1) == lax.shift_right_logical(ci, 1)) & ((ri & 1) == 1) & ((ci & 1) == 0)
    ts = [eye - jnp.where(pair, a, 0.0) for a in a_list]
    n = 2
    while n < block:
        sh = n.bit_length()
        m = ((lax.shift_right_logical(ri, sh) == lax.shift_right_logical(ci, sh))
             & ((ri & n) != 0) & ((ci & n) == 0))
        xs = [_mm(t, jnp.where(m, a, 0.0)) for t, a in zip(ts, a_list)]
        ts = [t - _mm(x, t) for t, x in zip(ts, xs)]
        n *= 2
    return ts


def _mixb_prompt_kernel(qkv_ref, z_ref, bd_ref, cw_ref, nega_ref, dtb_ref, og_ref,
                        o_ref, sfin_ref, s_ref, cbuf_ref):
    c = pl.program_id(0)
    C = DN_CHUNK
    nseq = qkv_ref.shape[0]

    @pl.when(c == 0)
    def _init():
        s_ref[...] = jnp.zeros_like(s_ref)
        cbuf_ref[:, 0:8, :] = jnp.zeros((nseq, 8, B_CONV_CH), F32)

    cw = cw_ref[...]
    og = og_ref[...]
    ri64 = lax.broadcasted_iota(jnp.int32, (C, C), 0)
    ci64 = lax.broadcasted_iota(jnp.int32, (C, C), 1)
    cum_lhs = (ri64 >= ci64).astype(BF16)

    R = MIXB_GROUP * C
    ri = lax.broadcasted_iota(jnp.int32, (R, R), 0)
    ci = lax.broadcasted_iota(jnp.int32, (R, R), 1)
    same = lax.shift_right_logical(ri, C.bit_length() - 1) == lax.shift_right_logical(ci, C.bit_length() - 1)
    tril = same & (ri >= ci)
    strict = same & (ri > ci)
    rep = B_V_HEADS // B_QK_HEADS

    def lane_col(a, lane):
        return jnp.broadcast_to(a[:, lane:lane + 1], (a.shape[0], HEAD_DIM))

    groups = B_V_HEADS // MIXB_GROUP
    units = [(b, grp) for b in range(nseq) for grp in range(groups)]
    heads_of = lambda grp: list(range(grp * MIXB_GROUP, (grp + 1) * MIXB_GROUP))

    acts, gcs, betas = [], [], []
    for b in range(nseq):
        x = qkv_ref[b]
        cbuf_ref[b, 8:8 + C, :] = x
        y = cbuf_ref[b, 5:5 + C, :] * cw[0:1]
        y = y + cbuf_ref[b, 6:6 + C, :] * cw[1:2]
        y = y + cbuf_ref[b, 7:7 + C, :] * cw[2:3]
        y = y + x * cw[3:4]
        cbuf_ref[b, 0:8, :] = x[C - 8:C, :]
        acts.append(jax.nn.silu(y))
        bd = bd_ref[b]
        betas.append(jax.nn.sigmoid(bd))
        g_all = nega_ref[...] * _softplus(bd + dtb_ref[...])
        gcs.append(_mm_exact_lhs(cum_lhs, g_all))

    kst, qst, gcol, glcol, bcol, decay, a_mat, rhs = {}, {}, {}, {}, {}, {}, {}, {}
    for u in units:
        b, grp = u
        act, gc_all, beta_all = acts[b], gcs[b], betas[b]

        def stack(fn):
            return jnp.concatenate([fn(h) for h in heads_of(grp)], axis=0)

        def l2n(cols0, h):
            s = act[:, cols0 + (h // rep) * HEAD_DIM:cols0 + (h // rep + 1) * HEAD_DIM]
            return s * lax.rsqrt(jnp.sum(s * s, -1, keepdims=True) + L2_EPS)

        kst[u] = stack(lambda h: l2n(B_KEY_WIDTH, h))
        qst[u] = stack(lambda h: l2n(0, h) * (HEAD_DIM ** -0.5))
        vst = stack(lambda h: act[:, 2 * B_KEY_WIDTH + h * HEAD_DIM:2 * B_KEY_WIDTH + (h + 1) * HEAD_DIM])
        gcol[u] = stack(lambda h: lane_col(gc_all, B_V_HEADS + h))
        glcol[u] = stack(lambda h: jnp.broadcast_to(gc_all[C - 1:C, B_V_HEADS + h:B_V_HEADS + h + 1], (C, HEAD_DIM)))
        bcol[u] = stack(lambda h: lane_col(beta_all, h))
        grow = gcol[u].T[0:1, :]
        diff = jnp.concatenate([gcol[u], gcol[u]], axis=1) - grow
        decay[u] = jnp.where(tril, jnp.exp(jnp.where(tril, diff, 0.0)), 0.0)
        rhs[u] = jnp.concatenate([vst * bcol[u], kst[u] * bcol[u] * jnp.exp(gcol[u])], axis=1)
    for u in units:
        a_mat[u] = jnp.where(strict, jnp.concatenate([bcol[u], bcol[u]], axis=1) * _mm_nt(kst[u], kst[u]) * decay[u], 0.0)

    t_inv = _unit_lower_inverse_many([a_mat[u] for u in units], ri, ci, C)
    sol = [_mm(t, rhs[u]) for t, u in zip(t_inv, units)]
    qk = [_mm_nt(qst[u], kst[u]) * decay[u] for u in units]
    ws = []
    for sl, u in zip(sol, units):
        b, grp = u
        q_dec = qst[u] * jnp.exp(gcol[u])
        ws.append([_mm(jnp.concatenate([sl[i * C:(i + 1) * C, HEAD_DIM:], q_dec[i * C:(i + 1) * C]], axis=0),
                       s_ref[b * B_V_HEADS + h]) for i, h in enumerate(heads_of(grp))])
    v_new = [sl[:, :HEAD_DIM] - jnp.concatenate([w[:C] for w in wl], axis=0) for sl, wl in zip(sol, ws)]
    outs = [jnp.concatenate([w[C:] for w in wl], axis=0) + _mm(q, v) for wl, q, v in zip(ws, qk, v_new)]
    for u, v, o in zip(units, v_new, outs):
        b, grp = u
        k_dec = kst[u] * jnp.exp(glcol[u] - gcol[u])
        on = o * lax.rsqrt(jnp.mean(o * o, -1, keepdims=True) + RMS_EPS) * og
        for i, h in enumerate(heads_of(grp)):
            rs = slice(i * C, (i + 1) * C)
            si = b * B_V_HEADS + h
            s_ref[si] = s_ref[si] * jnp.exp(glcol[u][i * C:i * C + 1, :]) + _mm_tn(k_dec[rs], v[rs])
            cols = slice(h * HEAD_DIM, (h + 1) * HEAD_DIM)
            o_ref[b, :, cols] = on[rs] * jax.nn.silu(z_ref[b, :, cols])

    @pl.when(c == pl.num_programs(0) - 1)
    def _fin():
        sfin_ref[...] = s_ref[...]


def _mixb_prompt(proj, bd, conv_w, nega, dtb, og, batch, seq):
    nc = seq // DN_CHUNK
    qkv_blk = 2 * A_WIDTH // B_CONV_CH
    z_blk = (2 * A_WIDTH + B_CONV_CH) // B_VAL_WIDTH
    proj3 = proj.reshape(batch, seq, proj.shape[-1])
    bd3 = bd.reshape(batch, seq, LANES)
    o, s_fin = pl.pallas_call(
        _mixb_prompt_kernel,
        grid=(nc,),
        in_specs=[pl.BlockSpec((batch, DN_CHUNK, B_CONV_CH), lambda c: (0, c, qkv_blk)),
                  pl.BlockSpec((batch, DN_CHUNK, B_VAL_WIDTH), lambda c: (0, c, z_blk)),
                  pl.BlockSpec((batch, DN_CHUNK, LANES), lambda c: (0, c, 0)),
                  pl.BlockSpec((B_CONV, B_CONV_CH), lambda c: (0, 0)),
                  pl.BlockSpec((1, LANES), lambda c: (0, 0)),
                  pl.BlockSpec((1, LANES), lambda c: (0, 0)),
                  pl.BlockSpec((1, HEAD_DIM), lambda c: (0, 0))],
        out_specs=[pl.BlockSpec((batch, DN_CHUNK, B_VAL_WIDTH), lambda c: (0, c, 0)),
                   pl.BlockSpec((batch * B_V_HEADS, HEAD_DIM, HEAD_DIM), lambda c: (0, 0, 0))],
        out_shape=[jax.ShapeDtypeStruct((batch, seq, B_VAL_WIDTH), F32),
                   jax.ShapeDtypeStruct((batch * B_V_HEADS, HEAD_DIM, HEAD_DIM), F32)],
        scratch_shapes=[pltpu.VMEM((batch * B_V_HEADS, HEAD_DIM, HEAD_DIM), F32),
                        pltpu.VMEM((batch, DN_CHUNK + 8, B_CONV_CH), F32)],
        compiler_params=_params("arbitrary"),
        name="mixb_prompt",
    )(proj3, proj3, bd3, conv_w, nega, dtb, og)
    return (o.reshape(batch * seq, B_VAL_WIDTH),
            s_fin.reshape(batch, B_V_HEADS, HEAD_DIM, HEAD_DIM))


def _mixb_sample_pre_kernel(qkv_ref, buf_ref, bd_ref, cw_ref, nega_ref, dtb_ref,
                            act_ref, beta_ref, g_ref, *, steps, nb_rows):
    cw = cw_ref[...]

    def slab(j):
        if j < B_CONV - 1:
            return buf_ref[j * nb_rows:(j + 1) * nb_rows, :]
        jj = j - (B_CONV - 1)
        return qkv_ref[jj * nb_rows:(jj + 1) * nb_rows, :]

    for t in range(steps):
        y = slab(t) * cw[0:1]
        for i in range(1, B_CONV):
            y = y + slab(t + i) * cw[i:i + 1]
        act = jax.nn.silu(y)
        rs = slice(t * nb_rows, (t + 1) * nb_rows)
        for qh in range(B_QK_HEADS):
            cq = slice(qh * HEAD_DIM, (qh + 1) * HEAD_DIM)
            ck = slice(B_KEY_WIDTH + qh * HEAD_DIM, B_KEY_WIDTH + (qh + 1) * HEAD_DIM)
            qs = act[:, cq]
            ks = act[:, ck]
            act_ref[rs, cq] = qs * lax.rsqrt(jnp.sum(qs * qs, -1, keepdims=True) + L2_EPS) * (HEAD_DIM ** -0.5)
            act_ref[rs, ck] = ks * lax.rsqrt(jnp.sum(ks * ks, -1, keepdims=True) + L2_EPS)
        act_ref[rs, 2 * B_KEY_WIDTH:] = act[:, 2 * B_KEY_WIDTH:]
    bd = bd_ref[...]
    beta_ref[...] = jax.nn.sigmoid(bd)
    g_ref[...] = nega_ref[...] * _softplus(bd + dtb_ref[...])


def _mixb_sample_pre(proj, buf, bd, conv_w, nega, dtb, steps, nb_rows):
    m = proj.shape[0]
    qkv_blk = 2 * A_WIDTH // B_CONV_CH
    kern = functools.partial(_mixb_sample_pre_kernel, steps=steps, nb_rows=nb_rows)
    return pl.pallas_call(
        kern,
        grid=(1,),
        in_specs=[pl.BlockSpec((m, B_CONV_CH), lambda i: (0, qkv_blk)),
                  pl.BlockSpec(buf.shape, lambda i: (0, 0)),
                  pl.BlockSpec((m, LANES), lambda i: (0, 0)),
                  pl.BlockSpec((B_CONV, B_CONV_CH), lambda i: (0, 0)),
                  pl.BlockSpec((1, LANES), lambda i: (0, 0)),
                  pl.BlockSpec((1, LANES), lambda i: (0, 0))],
        out_specs=[pl.BlockSpec((m, B_CONV_CH), lambda i: (0, 0)),
                   pl.BlockSpec((m, LANES), lambda i: (0, 0)),
                   pl.BlockSpec((m, LANES), lambda i: (0, 0))],
        out_shape=[jax.ShapeDtypeStruct((m, B_CONV_CH), F32),
                   jax.ShapeDtypeStruct((m, LANES), F32),
                   jax.ShapeDtypeStruct((m, LANES), F32)],
        compiler_params=_params("arbitrary"),
        name="mixb_sample_pre",
    )(proj, buf, bd, conv_w, nega, dtb)


def _mixb_sample_rec_kernel(kq_ref, v_ref, z_ref, beta_ref, g_ref, s0_ref, og_ref,
                            o_ref, s_out_ref, *, steps, pairs):
    og = og_ref[...]
    zpad = jnp.zeros((HEAD_DIM - 8, HEAD_DIM), F32)
    zrows = jnp.zeros((8 - steps, HEAD_DIM), F32)
    heads = range(B_V_HEADS)

    def body(bi, carry):
        kqs = [_mm(kq_ref[bi, h], s0_ref[bi, h]) for h in heads]
        pending = []
        for h in heads:
            kq = kq_ref[bi, h]
            g = g_ref[bi, h]
            beta = beta_ref[bi, h]
            v = v_ref[bi, h]
            gc = [g[0:1]]
            for t in range(1, steps):
                gc.append(gc[-1] + g[t:t + 1])
            k = [kq[t:t + 1] for t in range(steps)]
            q = [kq[4 + t:5 + t] for t in range(steps)]
            d = []
            for t in range(steps):
                acc = v[t:t + 1] - jnp.exp(gc[t]) * kqs[h][t:t + 1]
                for j in range(t):
                    kk = jnp.sum(k[j] * k[t], -1, keepdims=True)
                    acc = acc - jnp.exp(gc[t] - gc[j]) * kk * d[j]
                d.append(beta[t:t + 1] * acc)
            outs = []
            for t in range(steps):
                o = jnp.exp(gc[t]) * kqs[h][4 + t:5 + t]
                for j in range(t + 1):
                    qk = jnp.sum(k[j] * q[t], -1, keepdims=True)
                    o = o + jnp.exp(gc[t] - gc[j]) * qk * d[j]
                outs.append(o * lax.rsqrt(jnp.mean(o * o, -1, keepdims=True) + RMS_EPS) * og)
            o_ref[bi, h] = jnp.concatenate(outs + [zrows], axis=0) * jax.nn.silu(z_ref[bi, h])
            k_dec = jnp.concatenate([jnp.exp(gc[-1] - gc[j]) * k[j] for j in range(steps)] + [zrows], axis=0)
            k_pad = jnp.concatenate([k_dec, zpad], axis=0)
            d_pad = jnp.concatenate(d + [zrows, zpad], axis=0)
            pending.append((k_pad.T, d_pad, jnp.exp(gc[-1])))
        for h, (k_t, d_pad, decay_last) in zip(heads, pending):
            s_out_ref[bi, h] = s0_ref[bi, h] * decay_last + _mm(k_t, d_pad)
        return carry

    lax.fori_loop(0, pairs // B_V_HEADS, body, 0)


def _mixb_sample_rec(kq, v, z, beta, g, s0, og, steps, bb):
    nb = kq.shape[0]
    tile = pl.BlockSpec((bb, B_V_HEADS, 8, HEAD_DIM), lambda i: (i, 0, 0, 0))
    st = pl.BlockSpec((bb, B_V_HEADS, HEAD_DIM, HEAD_DIM), lambda i: (i, 0, 0, 0))
    kern = functools.partial(_mixb_sample_rec_kernel, steps=steps, pairs=bb * B_V_HEADS)
    return pl.pallas_call(
        kern,
        grid=(nb // bb,),
        in_specs=[tile, tile, tile, tile, tile, st, pl.BlockSpec((1, HEAD_DIM), lambda i: (0, 0))],
        out_specs=[tile, st],
        out_shape=[jax.ShapeDtypeStruct((nb, B_V_HEADS, 8, HEAD_DIM), F32),
                   jax.ShapeDtypeStruct((nb, B_V_HEADS, HEAD_DIM, HEAD_DIM), F32)],
        compiler_params=_params("arbitrary"),
        name="mixb_sample_rec",
    )(kq, v, z, beta, g, s0, og)


def _outproj_kernel(ap_ref, bp_ref, xp_ref, g1p_ref, sc2p_ref, sh2p_ref,
                    as_ref, bs_ref, xs_ref, g1s_ref, sc2s_ref, sh2s_ref,
                    w_ref, lg_ref, lb_ref, wr_ref, rb_ref, x1_ref, h2_ref, route_ref, count_ref,
                    base_ref, mix_ref, *, prompt_tiles):
    s = pl.program_id(0)

    @pl.when(s == 0)
    def _():
        base_ref[...] = jnp.zeros_like(base_ref)
        mix_ref[...] = jnp.zeros_like(mix_ref)

    cur_prompt = s < prompt_tiles
    a = jnp.where(cur_prompt, ap_ref[...], as_ref[...]).astype(BF16)
    mix_a = jnp.dot(a, w_ref[:A_WIDTH, :], preferred_element_type=F32)

    def project():
        b = jnp.where(cur_prompt, bp_ref[...], bs_ref[...]).astype(BF16)
        return mix_a + jnp.dot(b, w_ref[A_WIDTH:, :], preferred_element_type=F32)

    prev_prompt = s - 1 < prompt_tiles
    pick = lambda p_ref, s_ref: jnp.where(prev_prompt, p_ref[...], s_ref[...])
    mix = mix_ref[(s + 1) % 2]
    x1 = _layer_norm_rows(ALPHA * pick(xp_ref, xs_ref) + pick(g1p_ref, g1s_ref) * mix, lg_ref[...], lb_ref[...])
    x1_ref[...] = x1
    h2 = x1 * (1.0 + pick(sc2p_ref, sc2s_ref)) + pick(sh2p_ref, sh2s_ref)
    h2_ref[...] = h2
    route_ref[...], mix_new = _route_tile(_mm3(h2, wr_ref[...]), rb_ref[...], base_ref, s >= 1, project)
    count_ref[...] = base_ref[...]
    mix_ref[s % 2] = mix_new


def _outproj(prompt, sample, mod_p, mod_s, w_out, lg, lb, w_router, r_bias, tm, seq):
    n_p = prompt[2].shape[0]
    n_s = sample[2].shape[0]
    pt = n_p // tm
    st = n_s // tm
    nt = pt + st
    m = n_p + n_s
    tiles_per_seq = seq // tm
    cur = lambda s: jnp.minimum(s, nt - 1)
    prev = lambda s: jnp.maximum(s - 1, 0)
    p_idx = lambda t: jnp.minimum(t, pt - 1)
    s_idx = lambda t: jnp.clip(t - pt, 0, st - 1)
    p_cur = lambda w: pl.BlockSpec((tm, w), lambda s: (p_idx(cur(s)), 0))
    s_cur = lambda w: pl.BlockSpec((tm, w), lambda s: (s_idx(cur(s)), 0))
    p_prev = lambda w: pl.BlockSpec((tm, w), lambda s: (p_idx(prev(s)), 0))
    s_prev = lambda w: pl.BlockSpec((tm, w), lambda s: (s_idx(prev(s)), 0))
    p_mod = lambda col: pl.BlockSpec((None, 1, D_MODEL), lambda s: (p_idx(prev(s)) // tiles_per_seq, 0, col))
    s_mod = lambda col: pl.BlockSpec((tm, D_MODEL), lambda s: (0, col))
    row = lambda w: pl.BlockSpec((tm, w), lambda s: (prev(s), 0))
    full = lambda shape: pl.BlockSpec(shape, lambda s: (0, 0))
    return pl.pallas_call(
        functools.partial(_outproj_kernel, prompt_tiles=pt),
        grid=(nt + 1,),
        in_specs=[p_cur(A_WIDTH), p_cur(B_VAL_WIDTH), p_prev(D_MODEL), p_mod(2), p_mod(4), p_mod(3),
                  s_cur(A_WIDTH), s_cur(B_VAL_WIDTH), s_prev(D_MODEL), s_mod(2), s_mod(4), s_mod(3),
                  full((D_MODEL, D_MODEL)), full((1, D_MODEL)), full((1, D_MODEL)),
                  full((D_MODEL, LANES)), full((1, LANES))],
        out_specs=[row(D_MODEL), row(D_MODEL), row(LANES),
                   full((1, LANES))],
        out_shape=[jax.ShapeDtypeStruct((m, D_MODEL), F32),
                   jax.ShapeDtypeStruct((m, D_MODEL), F32),
                   jax.ShapeDtypeStruct((m, LANES), F32),
                   jax.ShapeDtypeStruct((1, LANES), F32)],
        scratch_shapes=[pltpu.VMEM((1, LANES), F32),
                        pltpu.VMEM((2, tm, D_MODEL), F32)],
        compiler_params=_params("arbitrary"),
        name="outproj",
    )(*prompt, mod_p, mod_p, mod_p, *sample, mod_s, mod_s, mod_s, w_out, lg, lb, w_router, r_bias)


def _route_tile(lg, bias, base_ref, valid, between):
    tm = lg.shape[0]
    lane = lax.broadcasted_iota(jnp.int32, lg.shape, 1)
    neg = -jnp.inf

    def first_argmax(score):
        mx = jnp.max(score, -1, keepdims=True)
        return jnp.min(jnp.where(score == mx, lane, LANES), -1, keepdims=True)

    def pick(vals, idx):
        return jnp.sum(jnp.where(lane == idx, vals, 0.0), -1, keepdims=True)

    gmask = lane < N_GROUPS
    mg = jnp.max(jnp.where(gmask, lg, neg), -1, keepdims=True)
    eg = jnp.where(gmask, jnp.exp(jnp.where(gmask, lg - mg, 0.0)), 0.0)
    pg = eg / jnp.sum(eg, -1, keepdims=True)
    sel_g = first_argmax(jnp.where(gmask, lg + bias, neg))
    p_sel = pick(pg, sel_g)

    lo = N_GROUPS + sel_g * EXPERTS_PER_GROUP
    emask = (lane >= lo) & (lane < lo + EXPERTS_PER_GROUP)
    me = jnp.max(jnp.where(emask, lg, neg), -1, keepdims=True)
    ee = jnp.where(emask, jnp.exp(jnp.where(emask, lg - me, 0.0)), 0.0)
    pe = ee / jnp.sum(ee, -1, keepdims=True)
    score = jnp.where(emask, pe + bias, neg)
    i1 = first_argmax(score)
    i2 = first_argmax(jnp.where(lane == i1, neg, score))
    w1 = pick(pe, i1)
    w2 = pick(pe, i2)
    wsum = w1 + w2
    gate1 = w1 / wsum * p_sel
    gate2 = w2 / wsum * p_sel

    hot = ((lane == i1) | (lane == i2)).astype(BF16)
    extra = between()
    ri = lax.broadcasted_iota(jnp.int32, (tm, tm), 0)
    ci = lax.broadcasted_iota(jnp.int32, (tm, tm), 1)
    before = jnp.dot((ri > ci).astype(BF16), hot, preferred_element_type=F32) + base_ref[...]
    rank1 = pick(before, i1)
    rank2 = pick(before, i2)
    base_ref[...] = jnp.where(valid, base_ref[...] + jnp.sum(hot.astype(F32), 0, keepdims=True), base_ref[...])

    out = jnp.where(lane == 0, (i1 - N_GROUPS).astype(F32), 0.0)
    out = jnp.where(lane == 1, (i2 - N_GROUPS).astype(F32), out)
    out = jnp.where(lane == 2, gate1, out)
    out = jnp.where(lane == 3, gate2, out)
    out = jnp.where(lane == 4, rank1, out)
    out = jnp.where(lane == 5, rank2, out)
    return out, extra


DISPATCH_ROWS = 512


def _dispatch_kernel(dest_ref, last_ref, nv_ref, h_ref, xb_ref, zbuf, pbuf, zsem, ssem, *, n_tok, n_blocks):
    i = pl.program_id(0)
    nv = nv_ref[0]

    def zero_block(blk):
        return pltpu.make_async_copy(zbuf, xb_ref.at[pl.ds(blk * MOE_ROWS, MOE_ROWS)], zsem)

    @pl.when(i == 0)
    def _():
        zbuf[...] = jnp.zeros_like(zbuf)
        for e in range(N_EXPERTS):
            @pl.when(last_ref[e] >= 0)
            def _():
                zero_block(last_ref[e]).start()

        def tail_start(b, carry):
            zero_block(b).start()
            return carry

        lax.fori_loop(nv, n_blocks, tail_start, 0)
        for e in range(N_EXPERTS):
            @pl.when(last_ref[e] >= 0)
            def _():
                zero_block(0).wait()

        def tail_wait(b, carry):
            zero_block(0).wait()
            return carry

        lax.fori_loop(nv, n_blocks, tail_wait, 0)

    half = D_MODEL // 2
    hi = lax.bitcast_convert_type(h_ref[:, :half].astype(BF16).astype(F32), jnp.uint32)
    lo = lax.bitcast_convert_type(h_ref[:, half:].astype(BF16).astype(F32), jnp.uint32)
    pbuf[...] = (hi & jnp.uint32(0xFFFF0000)) | lax.shift_right_logical(lo, jnp.uint32(16))

    def row_copy(r, choice):
        slot = dest_ref[choice * n_tok + i * DISPATCH_ROWS + r]
        return pltpu.make_async_copy(pbuf.at[pl.ds(r, 1)], xb_ref.at[pl.ds(slot, 1)], ssem)

    def body(r, carry):
        row_copy(r, 0).start()
        row_copy(r, 1).start()
        return carry

    lax.fori_loop(0, DISPATCH_ROWS, body, 0, unroll=8)
    for _ in range(2):
        pltpu.make_async_copy(pbuf, xb_ref.at[pl.ds(0, DISPATCH_ROWS)], ssem).wait()


def _dispatch(dest_flat, last_block, n_valid, h2, n_blocks):
    n_tok = h2.shape[0]
    kern = functools.partial(_dispatch_kernel, n_tok=n_tok, n_blocks=n_blocks)
    return pl.pallas_call(
        kern,
        grid_spec=pltpu.PrefetchScalarGridSpec(
            num_scalar_prefetch=3,
            grid=(n_tok // DISPATCH_ROWS,),
            in_specs=[pl.BlockSpec((DISPATCH_ROWS, D_MODEL), lambda i, *_: (i, 0))],
            out_specs=pl.BlockSpec(memory_space=pl.ANY),
            scratch_shapes=[pltpu.VMEM((MOE_ROWS, D_MODEL // 2), jnp.uint32),
                            pltpu.VMEM((DISPATCH_ROWS, D_MODEL // 2), jnp.uint32),
                            pltpu.SemaphoreType.DMA(()),
                            pltpu.SemaphoreType.DMA(())]),
        out_shape=jax.ShapeDtypeStruct((n_blocks * MOE_ROWS, D_MODEL // 2), jnp.uint32),
        compiler_params=_params("arbitrary"),
        name="dispatch",
    )(dest_flat, last_block, n_valid, h2)


def _expert_kernel(be_ref, nv_ref, first_ref, next_ref, wslot_ref,
                   x_ref, wg_ref, wu_ref, wd_ref, y_ref,
                   wg_st, wu_st, wd_st, wg_s, wu_s, wd_s, wsem):
    i = pl.program_id(0)
    nv = nv_ref[0]

    def weight_copies(e, p):
        return [pltpu.make_async_copy(src.at[e], dst.at[p], wsem.at[p])
                for src, dst in ((wg_ref, wg_st), (wu_ref, wu_st), (wd_ref, wd_st))]

    @pl.when(i == 0)
    def _():
        for c in weight_copies(be_ref[0], 0):
            c.start()

    @pl.when((i < nv) & (first_ref[i] == 1))
    def _():
        p = wslot_ref[i]
        for c in weight_copies(be_ref[i], p):
            c.wait()
        wg_s[...] = wg_st[p].astype(BF16)
        wu_s[...] = wu_st[p].astype(BF16)
        wd_s[...] = wd_st[p].astype(BF16)

        @pl.when(next_ref[i] >= 0)
        def _():
            for c in weight_copies(next_ref[i], 1 - p):
                c.start()

    @pl.when(i < nv)
    def _():
        u = x_ref[...]
        hi = lax.bitcast_convert_type(u & jnp.uint32(0xFFFF0000), F32)
        lo = lax.bitcast_convert_type(lax.shift_left(u, jnp.uint32(16)), F32)
        x = jnp.concatenate([hi, lo], axis=1).astype(BF16)
        hg = jnp.dot(x, wg_s[...], preferred_element_type=F32)
        hu = jnp.dot(x, wu_s[...], preferred_element_type=F32)
        hid = (jax.nn.silu(hg) * hu).astype(BF16)
        y_ref[...] = jnp.dot(hid, wd_s[...], preferred_element_type=F32)

    @pl.when(i >= nv)
    def _():
        y_ref[...] = jnp.zeros_like(y_ref)


def _experts(block_e, n_valid, xb, w_gate, w_up, w_down):
    n_blocks = block_e.shape[0]
    idx = jnp.arange(n_blocks, dtype=jnp.int32)
    first = (idx < n_valid[0]) & ((idx == 0) | (block_e != jnp.roll(block_e, 1)))
    wslot = (jnp.cumsum(first.astype(jnp.int32)) - 1) % 2
    first_at = jnp.where(first, idx, n_blocks)
    next_first = jnp.concatenate([lax.cummin(first_at, reverse=True)[1:], jnp.full((1,), n_blocks, jnp.int32)])
    next_e = jnp.where(next_first < n_blocks, block_e[jnp.minimum(next_first, n_blocks - 1)], -1)
    any_spec = pl.BlockSpec(memory_space=pl.ANY)
    rows = pl.BlockSpec((MOE_ROWS, D_MODEL), lambda i, *_: (i, 0))
    return pl.pallas_call(
        _expert_kernel,
        grid_spec=pltpu.PrefetchScalarGridSpec(
            num_scalar_prefetch=5,
            grid=(n_blocks,),
            in_specs=[pl.BlockSpec((MOE_ROWS, D_MODEL // 2), lambda i, *_: (i, 0)), any_spec, any_spec, any_spec],
            out_specs=rows,
            scratch_shapes=[pltpu.VMEM((2, D_MODEL, D_EXPERT), F32),
                            pltpu.VMEM((2, D_MODEL, D_EXPERT), F32),
                            pltpu.VMEM((2, D_EXPERT, D_MODEL), F32),
                            pltpu.VMEM((D_MODEL, D_EXPERT), BF16),
                            pltpu.VMEM((D_MODEL, D_EXPERT), BF16),
                            pltpu.VMEM((D_EXPERT, D_MODEL), BF16),
                            pltpu.SemaphoreType.DMA((2,))]),
        out_shape=jax.ShapeDtypeStruct((n_blocks * MOE_ROWS, D_MODEL), F32),
        compiler_params=_params("arbitrary"),
        name="experts",
    )(block_e, n_valid, first.astype(jnp.int32), next_e.astype(jnp.int32), wslot.astype(jnp.int32),
      xb, w_gate, w_up, w_down)


def _final_kernel(dest_ref, x1_ref, route_ref, g2_ref, lg_ref, lb_ref, yb_ref, o_ref, ybuf, sem,
                  *, row0, n_tok):
    i = pl.program_id(0)
    last = pl.num_programs(0) - 1
    tm = o_ref.shape[0]
    slot = i % 2

    def row_copy(tile, r, choice, s):
        src = dest_ref[choice * n_tok + row0 + tile * tm + r]
        return pltpu.make_async_copy(yb_ref.at[pl.ds(src, 1)], ybuf.at[s, choice, pl.ds(r, 1)], sem.at[s])

    def tile_wait(s):
        for choice in range(2):
            pltpu.make_async_copy(yb_ref.at[pl.ds(0, tm)], ybuf.at[s, choice], sem.at[s]).wait()

    @pl.when(i == 0)
    def _():
        def body(r, carry):
            row_copy(0, r, 0, 0).start()
            row_copy(0, r, 1, 0).start()
            return carry

        lax.fori_loop(0, tm, body, 0, unroll=8)

    tile_wait(slot)
    nxt = jnp.minimum(i + 1, last)
    for r in range(tm):
        row_copy(nxt, r, 0, 1 - slot).start()
        row_copy(nxt, r, 1, 1 - slot).start()
    route = route_ref[...]
    ff = ybuf[slot, 0] * route[:, 2:3] + ybuf[slot, 1] * route[:, 3:4]
    o_ref[...] = _layer_norm_rows(ALPHA * x1_ref[...] + g2_ref[...] * ff, lg_ref[...], lb_ref[...])

    @pl.when(i == last)
    def _():
        tile_wait(1 - slot)


def _final(dest_flat, x1, yb, route, mod, lg, lb, tm, rows_per_mod, row0, n_rows):
    n_tok = x1.shape[0]
    off = row0 // tm
    return pl.pallas_call(
        functools.partial(_final_kernel, row0=row0, n_tok=n_tok),
        grid_spec=pltpu.PrefetchScalarGridSpec(
            num_scalar_prefetch=1,
            grid=(n_rows // tm,),
            in_specs=[pl.BlockSpec((tm, D_MODEL), lambda i, *_: (off + i, 0)),
                      pl.BlockSpec((tm, LANES), lambda i, *_: (off + i, 0)),
                      _mod_spec(mod, 5, tm, rows_per_mod),
                      pl.BlockSpec((1, D_MODEL), lambda i, *_: (0, 0)),
                      pl.BlockSpec((1, D_MODEL), lambda i, *_: (0, 0)),
                      pl.BlockSpec(memory_space=pl.ANY)],
            out_specs=pl.BlockSpec((tm, D_MODEL), lambda i, *_: (i, 0)),
            scratch_shapes=[pltpu.VMEM((2, 2, tm, D_MODEL), F32),
                            pltpu.SemaphoreType.DMA((2,))]),
        out_shape=jax.ShapeDtypeStruct((n_rows, D_MODEL), F32),
        compiler_params=_params("arbitrary"),
        name="final",
    )(dest_flat, x1, route, mod, lg, lb, yb)


def _lane_row(vec, offset):
    return jnp.zeros((1, LANES), F32).at[0, offset:offset + vec.shape[0]].set(vec.astype(F32))


def kernel(x_prompt, x_sample, state_conv, state_ssm, c_prompt, c_sample, w_ada, b_ada, w_in, a_ws, a_bs, a_norm_g, a_norm_b, b_conv_w, b_a_log, b_dt_bias, b_onorm_g, w_out, ln1_g, ln1_b, w_router_g, b_router_g, w_router_e, b_router_e, w_gate, w_up, w_down, ln2_g, ln2_b):
    batch, seq, d = x_prompt.shape
    nb, steps, _ = x_sample.shape
    n_p = batch * seq
    n_s = nb * steps
    n_tok = n_p + n_s
    l = 0

    c_rows = batch + nb
    c_pad = (-c_rows) % 8
    c_all = jnp.concatenate([c_prompt, c_sample, jnp.zeros((c_pad, d), F32)], axis=0)
    mod = _ada(c_all, w_ada[l], b_ada[l].reshape(1, -1))
    mod_p = mod[:batch].reshape(batch, 1, 6 * d)
    mod_s = mod[batch:batch + nb]

    w_in_t = jnp.swapaxes(w_in[l], 0, 1)
    w_bd = jnp.pad(w_in_t[MAIN_WIDTH:], ((0, LANES - 2 * B_V_HEADS), (0, 0))).astype(BF16)
    xp = x_prompt.reshape(n_p, d)
    xs = jnp.swapaxes(x_sample, 0, 1).reshape(n_s, d)
    proj_p, bd_p, proj_s, bd_s = _inproj(xp, xs, mod_p, mod_s, w_in_t, w_bd, 512, seq)

    ng = a_norm_g[l].reshape(1, -1)
    nbias = a_norm_b[l].reshape(1, -1)
    bias_tile = jnp.repeat(a_bs[l].T, HEAD_DIM, axis=1)
    a_out_p = _mixa_prompt(proj_p, a_ws[l], bias_tile, ng, nbias, 256)
    coef = jnp.repeat(jnp.transpose(a_ws[l][:, :steps, :steps], (1, 2, 0)).reshape(steps * steps, A_HEADS),
                      HEAD_DIM, axis=1)
    a_out_s, chunkv = _mixa_sample(proj_s, coef, bias_tile[:steps], ng, nbias, steps, nb)

    nega = _lane_row(-jnp.exp(b_a_log[l].astype(F32)), B_V_HEADS)
    dtb = _lane_row(b_dt_bias[l], B_V_HEADS)
    og = b_onorm_g[l].reshape(1, -1)
    b_out_p, ssm_p = _mixb_prompt(proj_p, bd_p, b_conv_w[l], nega, dtb, og, batch, seq)
    qkv0 = 2 * A_WIDTH
    conv_p = proj_p.reshape(batch, seq, MAIN_WIDTH)[:, seq - (B_CONV - 1):, qkv0:qkv0 + B_CONV_CH]

    buf_s = jnp.swapaxes(state_conv[l], 0, 1).reshape((B_CONV - 1) * nb, B_CONV_CH)
    act_s, beta_s, g_s = _mixb_sample_pre(proj_s, buf_s, bd_s, b_conv_w[l], nega, dtb, steps, nb)
    rep = B_V_HEADS // B_QK_HEADS
    act4 = act_s.reshape(steps, nb, B_CONV_CH)
    q8 = jnp.repeat(act4[..., :B_KEY_WIDTH].reshape(steps, nb, B_QK_HEADS, HEAD_DIM), rep, axis=2)
    k8 = jnp.repeat(act4[..., B_KEY_WIDTH:2 * B_KEY_WIDTH].reshape(steps, nb, B_QK_HEADS, HEAD_DIM), rep, axis=2)

    def to_tiles(a):
        a = jnp.transpose(a, (1, 2, 0, 3))
        return jnp.pad(a, ((0, 0), (0, 0), (0, 8 - a.shape[2]), (0, 0)))

    kq_t = jnp.concatenate([to_tiles(k8)[:, :, :4], to_tiles(q8)[:, :, :4]], axis=2)
    v_t = to_tiles(act4[..., 2 * B_KEY_WIDTH:].reshape(steps, nb, B_V_HEADS, HEAD_DIM))
    z_t = to_tiles(proj_s[:, qkv0 + B_CONV_CH:].reshape(steps, nb, B_V_HEADS, HEAD_DIM))

    def gate_tiles(a, off):
        a = a[:, off:off + B_V_HEADS].reshape(steps, nb, B_V_HEADS, 1)
        return to_tiles(jnp.broadcast_to(a, (steps, nb, B_V_HEADS, HEAD_DIM)))

    o_t, ssm_s = _mixb_sample_rec(kq_t, v_t, z_t, gate_tiles(beta_s, 0), gate_tiles(g_s, B_V_HEADS),
                                  state_ssm[l], og, steps, 8)
    b_out_s = jnp.transpose(o_t[:, :, :steps], (2, 0, 1, 3)).reshape(n_s, B_VAL_WIDTH)
    conv_s = jnp.swapaxes(proj_s.reshape(steps, nb, MAIN_WIDTH)[steps - (B_CONV - 1):, :, qkv0:qkv0 + B_CONV_CH], 0, 1)

    w_out_b = w_out[l].astype(BF16)
    lg1 = ln1_g[l].reshape(1, -1)
    lb1 = ln1_b[l].reshape(1, -1)
    w_router = jnp.concatenate([w_router_g[l], w_router_e[l],
                                jnp.zeros((d, LANES - N_GROUPS - N_EXPERTS), F32)], axis=1)
    out_tm = 256
    mod_s_tile = jnp.tile(mod_s, (out_tm // nb, 1))
    r_bias = jnp.concatenate([b_router_g[l], b_router_e[l],
                              jnp.zeros((LANES - N_GROUPS - N_EXPERTS,), F32)]).reshape(1, LANES)
    x1, h2, route, counts_row = _outproj((a_out_p, b_out_p, xp), (a_out_s, b_out_s, xs), mod_p, mod_s_tile,
                                         w_out_b, lg1, lb1, w_router, r_bias, out_tm, seq)

    counts = counts_row[0, N_GROUPS:N_GROUPS + N_EXPERTS].astype(jnp.int32)
    route_t = route[:, :8].T
    eid = route_t[0:2].astype(jnp.int32)
    rank = route_t[4:6].astype(jnp.int32)
    padded = (counts + MOE_ROWS - 1) // MOE_ROWS * MOE_ROWS
    pend = jnp.cumsum(padded)
    pstart = pend - padded
    dest = rank + jnp.sum(jnp.where(eid[None] == jnp.arange(N_EXPERTS, dtype=jnp.int32)[:, None, None],
                                    pstart[:, None, None], 0), axis=0)
    n_blocks = -(-(2 * n_tok) // MOE_ROWS) + N_EXPERTS
    dest_flat = dest.reshape(-1)
    last_block = jnp.where(padded > 0, pend // MOE_ROWS - 1, -1).astype(jnp.int32)
    blk0 = jnp.arange(n_blocks, dtype=jnp.int32) * MOE_ROWS
    block_e = jnp.minimum(jnp.sum(pend[None, :] <= blk0[:, None], axis=1), N_EXPERTS - 1).astype(jnp.int32)
    n_valid = (pend[-1:] // MOE_ROWS).astype(jnp.int32)

    xb = _dispatch(dest_flat, last_block, n_valid, h2, n_blocks)
    yb = _experts(block_e, n_valid, xb, w_gate[l], w_up[l], w_down[l])

    lg2 = ln2_g[l].reshape(1, -1)
    lb2 = ln2_b[l].reshape(1, -1)
    y_p = _final(dest_flat, x1, yb, route, mod_p, lg2, lb2, 512, seq, 0, n_p)
    y_s = _final(dest_flat, x1, yb, route, mod_s, lg2, lb2, nb, 0, n_p, n_s)

    y_prompt = y_p.reshape(batch, seq, d)
    y_sample = jnp.swapaxes(y_s.reshape(steps, nb, d), 0, 1)
    chunkv_s = jnp.swapaxes(chunkv.reshape(steps, nb, A_HEADS, HEAD_DIM), 0, 1)
    return (y_prompt, y_sample, conv_p[None], ssm_p[None], conv_s[None], ssm_s[None], chunkv_s[None])
```

```python
import functools

import jax
import jax.numpy as jnp
from jax import lax
from jax.experimental import pallas as pl
from jax.experimental.pallas import tpu as pltpu

F32 = jnp.float32
BF16 = jnp.bfloat16

D_MODEL = 2048
DEPTH = 1
A_HEADS = 8
HEAD_DIM = 128
A_WIDTH = 1024
A_CHUNK = 128
B_QK_HEADS = 4
B_V_HEADS = 8
B_KEY_WIDTH = 512
B_VAL_WIDTH = 1024
B_CONV = 4
B_CONV_CH = 2048
DN_CHUNK = 64
MAIN_WIDTH = 2 * A_WIDTH + B_CONV_CH + B_VAL_WIDTH
N_GROUPS = 4
EXPERTS_PER_GROUP = 8
N_EXPERTS = 32
D_EXPERT = 512
ALPHA = (2 * DEPTH) ** 0.25
LN_EPS = 1e-5
RMS_EPS = 1e-6
L2_EPS = 1e-6

LANES = 128
VMEM_LIMIT = 56 * 1024 * 1024
MOE_ROWS = 256
MIXB_GROUP = 4


def _params(*sem):
    return pltpu.CompilerParams(dimension_semantics=sem, vmem_limit_bytes=VMEM_LIMIT)


def _mm(a, b):
    return jnp.dot(a.astype(BF16), b.astype(BF16), preferred_element_type=F32)


def _mm_nt(a, b):
    return lax.dot_general(a.astype(BF16), b.astype(BF16), (((1,), (1,)), ((), ())),
                           preferred_element_type=F32)


def _mm_tn(a, b):
    return lax.dot_general(a.astype(BF16), b.astype(BF16), (((0,), (0,)), ((), ())),
                           preferred_element_type=F32)


def _split(x):
    hi = x.astype(BF16)
    lo = (x - hi.astype(F32)).astype(BF16)
    return hi, lo


def _mm_exact_lhs(a_bf16, b):
    hi, lo = _split(b)
    return (jnp.dot(a_bf16, hi, preferred_element_type=F32)
            + jnp.dot(a_bf16, lo, preferred_element_type=F32))


def _mm3(a, b):
    ah, al = _split(a)
    bh, bl = _split(b)
    return (jnp.dot(ah, bh, preferred_element_type=F32) + jnp.dot(ah, bl, preferred_element_type=F32)
            + jnp.dot(al, bh, preferred_element_type=F32))


def _softplus(x):
    return jnp.maximum(x, 0.0) + jnp.log1p(jnp.exp(-jnp.abs(x)))


def _layer_norm_rows(x, g, b):
    mu = jnp.mean(x, -1, keepdims=True)
    xc = x - mu
    var = jnp.mean(xc * xc, -1, keepdims=True)
    return xc * lax.rsqrt(var + LN_EPS) * g + b


def _ada_kernel(c_ref, w_ref, b_ref, o_ref):
    a = jax.nn.silu(c_ref[...]).astype(BF16)
    o_ref[...] = jnp.dot(a, w_ref[...].astype(BF16), preferred_element_type=F32) + b_ref[...]


def _ada(c_all, w_ada, b_ada):
    rows = c_all.shape[0]
    tn = 1024
    return pl.pallas_call(
        _ada_kernel,
        grid=(6 * D_MODEL // tn,),
        in_specs=[pl.BlockSpec((rows, D_MODEL), lambda j: (0, 0)),
                  pl.BlockSpec((D_MODEL, tn), lambda j: (0, j)),
                  pl.BlockSpec((1, tn), lambda j: (0, j))],
        out_specs=pl.BlockSpec((rows, tn), lambda j: (0, j)),
        out_shape=jax.ShapeDtypeStruct((rows, 6 * D_MODEL), F32),
        compiler_params=_params("arbitrary"),
        name="ada",
    )(c_all, w_ada, b_ada)


def _inproj_kernel(xp_ref, scp_ref, shp_ref, xs_ref, scs_ref, shs_ref, w_ref, wbd_ref,
                   op_ref, bdp_ref, os_ref, bds_ref, wb_ref, *, prompt_tiles):
    j = pl.program_id(0)
    i = pl.program_id(1)

    @pl.when(i == 0)
    def _():
        wb_ref[...] = w_ref[...].astype(BF16)

    def project(h, o_ref, bd_ref):
        o_ref[...] = _mm_nt(h, wb_ref[...])

        @pl.when(j == 0)
        def _():
            bd_ref[0] = _mm_nt(h, wbd_ref[...])

        @pl.when(j != 0)
        def _():
            bd_ref[0] = jnp.zeros(bd_ref.shape[1:], F32)

    @pl.when(i < prompt_tiles)
    def _():
        project((xp_ref[...] * (1.0 + scp_ref[...]) + shp_ref[...]).astype(BF16), op_ref, bdp_ref)

    @pl.when(i >= prompt_tiles)
    def _():
        n_s, nb = xs_ref.shape[0], scs_ref.shape[0]
        x = xs_ref[...].reshape(n_s // nb, nb, D_MODEL)
        h = x * (1.0 + scs_ref[...])[None] + shs_ref[...][None]
        project(h.reshape(n_s, D_MODEL).astype(BF16), os_ref, bds_ref)


def _mod_spec(mod, col, tm, rows_per_mod, axis=0):
    if rows_per_mod:
        tiles = rows_per_mod // tm
        return pl.BlockSpec((None, 1, D_MODEL), lambda *g: (g[axis] // tiles, 0, col))
    return pl.BlockSpec((tm, D_MODEL), lambda *g: (0, col))


def _inproj(xp, xs, mod_p, mod_s, w_in_t, w_bd, tm, seq):
    n_p, n_s = xp.shape[0], xs.shape[0]
    nb = mod_s.shape[0]
    tn = 1024
    nj = MAIN_WIDTH // tn
    pt = n_p // tm
    tiles_per_seq = seq // tm
    pi = lambda i: jnp.minimum(i, pt - 1)
    p_mod = lambda col: pl.BlockSpec((None, 1, D_MODEL), lambda j, i: (pi(i) // tiles_per_seq, 0, col))
    s_mod = lambda col: pl.BlockSpec((nb, D_MODEL), lambda j, i: (0, col))
    proj_p, bd_p, proj_s, bd_s = pl.pallas_call(
        functools.partial(_inproj_kernel, prompt_tiles=pt),
        grid=(nj, pt + 1),
        in_specs=[pl.BlockSpec((tm, D_MODEL), lambda j, i: (pi(i), 0)), p_mod(1), p_mod(0),
                  pl.BlockSpec((n_s, D_MODEL), lambda j, i: (0, 0)), s_mod(1), s_mod(0),
                  pl.BlockSpec((tn, D_MODEL), lambda j, i: (j, 0)),
                  pl.BlockSpec((LANES, D_MODEL), lambda j, i: (0, 0))],
        out_specs=[pl.BlockSpec((tm, tn), lambda j, i: (pi(i), j)),
                   pl.BlockSpec((1, tm, LANES), lambda j, i: (j, pi(i), 0)),
                   pl.BlockSpec((n_s, tn), lambda j, i: (0, j)),
                   pl.BlockSpec((1, n_s, LANES), lambda j, i: (j, 0, 0))],
        out_shape=[jax.ShapeDtypeStruct((n_p, MAIN_WIDTH), F32),
                   jax.ShapeDtypeStruct((nj, n_p, LANES), F32),
                   jax.ShapeDtypeStruct((n_s, MAIN_WIDTH), F32),
                   jax.ShapeDtypeStruct((nj, n_s, LANES), F32)],
        scratch_shapes=[pltpu.VMEM((tn, D_MODEL), BF16)],
        compiler_params=_params("arbitrary", "arbitrary"),
        name="inproj",
    )(xp, mod_p, mod_p, xs, mod_s, mod_s, w_in_t, w_bd)
    return proj_p, bd_p[0], proj_s, bd_s[0]


def _mixa_prompt_kernel(p_ref, ws_ref, bias_ref, ng_ref, nb_ref, o_ref):
    rows = p_ref.shape[0]
    u = jax.nn.gelu(p_ref[:, :A_WIDTH])
    v = _layer_norm_rows(jax.nn.gelu(p_ref[:, A_WIDTH:]), ng_ref[...], nb_ref[...])
    ri = lax.broadcasted_iota(jnp.int32, (A_CHUNK, A_CHUNK), 0)
    ci = lax.broadcasted_iota(jnp.int32, (A_CHUNK, A_CHUNK), 1)
    for h in range(A_HEADS):
        w = jnp.where(ri >= ci, ws_ref[h], 0.0).astype(BF16)
        cols = slice(h * HEAD_DIM, (h + 1) * HEAD_DIM)
        for c in range(rows // A_CHUNK):
            rs = slice(c * A_CHUNK, (c + 1) * A_CHUNK)
            s = jnp.dot(w, v[rs, cols].astype(BF16), preferred_element_type=F32) + bias_ref[:, cols]
            o_ref[rs, cols] = u[rs, cols] * s


def _mixa_prompt(proj, a_ws, bias_tile, ng, nb, tm):
    m = proj.shape[0]
    return pl.pallas_call(
        _mixa_prompt_kernel,
        grid=(m // tm,),
        in_specs=[pl.BlockSpec((tm, 2 * A_WIDTH), lambda i: (i, 0)),
                  pl.BlockSpec((A_HEADS, A_CHUNK, A_CHUNK), lambda i: (0, 0, 0)),
                  pl.BlockSpec((A_CHUNK, A_WIDTH), lambda i: (0, 0)),
                  pl.BlockSpec((1, A_WIDTH), lambda i: (0, 0)),
                  pl.BlockSpec((1, A_WIDTH), lambda i: (0, 0))],
        out_specs=pl.BlockSpec((tm, A_WIDTH), lambda i: (i, 0)),
        out_shape=jax.ShapeDtypeStruct((m, A_WIDTH), F32),
        compiler_params=_params("arbitrary"),
        name="mixa_prompt",
    )(proj, a_ws, bias_tile, ng, nb)


def _mixa_sample_kernel(p_ref, coef_ref, bias_ref, ng_ref, nb_ref, o_ref, v_ref, *, steps, nb_rows):
    u = jax.nn.gelu(p_ref[:, :A_WIDTH])
    v = _layer_norm_rows(jax.nn.gelu(p_ref[:, A_WIDTH:]), ng_ref[...], nb_ref[...])
    v_ref[...] = v
    for t in range(steps):
        s = bias_ref[t:t + 1, :]
        for j in range(t + 1):
            s = s + coef_ref[t * steps + j:t * steps + j + 1, :] * v[j * nb_rows:(j + 1) * nb_rows, :]
        rs = slice(t * nb_rows, (t + 1) * nb_rows)
        o_ref[rs, :] = u[rs, :] * s


def _mixa_sample(proj, coef, bias, ng, nb, steps, nb_rows):
    m = proj.shape[0]
    kern = functools.partial(_mixa_sample_kernel, steps=steps, nb_rows=nb_rows)
    return pl.pallas_call(
        kern,
        grid=(1,),
        in_specs=[pl.BlockSpec((m, 2 * A_WIDTH), lambda i: (0, 0)),
                  pl.BlockSpec(coef.shape, lambda i: (0, 0)),
                  pl.BlockSpec(bias.shape, lambda i: (0, 0)),
                  pl.BlockSpec((1, A_WIDTH), lambda i: (0, 0)),
                  pl.BlockSpec((1, A_WIDTH), lambda i: (0, 0))],
        out_specs=[pl.BlockSpec((m, A_WIDTH), lambda i: (0, 0)),
                   pl.BlockSpec((m, A_WIDTH), lambda i: (0, 0))],
        out_shape=[jax.ShapeDtypeStruct((m, A_WIDTH), F32),
                   jax.ShapeDtypeStruct((m, A_WIDTH), F32)],
        compiler_params=_params("arbitrary"),
        name="mixa_sample",
    )(proj, coef, bias, ng, nb)


def _unit_lower_inverse_many(a_list, ri, ci, block):
    rows = a_list[0].shape[0]
    eye = (ri == ci).astype(F32)
    pair = (lax.shift_right_logical(ri, 1) == lax.shift_right_logical(ci, 1)) & ((ri & 1) == 1) & ((ci & 1) == 0)
    ts = [eye - jnp.where(pair, a, 0.0) for a in a_list]
    n = 2
    while n < block:
        sh = n.bit_length()
        m = ((lax.shift_right_logical(ri, sh) == lax.shift_right_logical(ci, sh))
             & ((ri & n) != 0) & ((ci & n) == 0))
        ans = [jnp.where(m, a, 0.0) for a in a_list]
        if n % 8:
            xs = [_mm(t, an) for t, an in zip(ts, ans)]
            ts = [t - _mm(x, t) for t, x in zip(ts, xs)]
        else:
            lower = [slice(r0 + n, r0 + 2 * n) for r0 in range(0, rows, 2 * n)]
            pick = lambda t: jnp.concatenate([t[sl] for sl in lower], axis=0)
            xs = [_mm(pick(t), an) for t, an in zip(ts, ans)]
            upd = [pick(t) - _mm(x, t) for t, x in zip(ts, xs)]
            ts = [jnp.concatenate([piece for k, r0 in enumerate(range(0, rows, 2 * n))
                                   for piece in (t[r0:r0 + n], u[k * n:(k + 1) * n])], axis=0)
                  for t, u in zip(ts, upd)]
        n *= 2
    return ts


def _mixb_prompt_kernel(qkv_ref, z_ref, bd_ref, cw_ref, nega_ref, dtb_ref, og_ref,
                        o_ref, sfin_ref, s_ref, cbuf_ref):
    c = pl.program_id(0)
    C = DN_CHUNK
    nseq = qkv_ref.shape[0]

    @pl.when(c == 0)
    def _init():
        s_ref[...] = jnp.zeros_like(s_ref)
        cbuf_ref[:, 0:8, :] = jnp.zeros((nseq, 8, B_CONV_CH), F32)

    cw = cw_ref[...]
    og = og_ref[...]
    ri64 = lax.broadcasted_iota(jnp.int32, (C, C), 0)
    ci64 = lax.broadcasted_iota(jnp.int32, (C, C), 1)
    cum_lhs = (ri64 >= ci64).astype(BF16)

    R = MIXB_GROUP * C
    ri = lax.broadcasted_iota(jnp.int32, (R, R), 0)
    ci = lax.broadcasted_iota(jnp.int32, (R, R), 1)
    same = lax.shift_right_logical(ri, C.bit_length() - 1) == lax.shift_right_logical(ci, C.bit_length() - 1)
    tril = same & (ri >= ci)
    strict = same & (ri > ci)
    rep = B_V_HEADS // B_QK_HEADS

    def lane_col(a, lane):
        return jnp.broadcast_to(a[:, lane:lane + 1], (a.shape[0], HEAD_DIM))

    groups = B_V_HEADS // MIXB_GROUP
    units = [(b, grp) for b in range(nseq) for grp in range(groups)]
    heads_of = lambda grp: list(range(grp * MIXB_GROUP, (grp + 1) * MIXB_GROUP))

    acts, gcs, betas = [], [], []
    for b in range(nseq):
        x = qkv_ref[b]
        cbuf_ref[b, 8:8 + C, :] = x
        y = cbuf_ref[b, 5:5 + C, :] * cw[0:1]
        y = y + cbuf_ref[b, 6:6 + C, :] * cw[1:2]
        y = y + cbuf_ref[b, 7:7 + C, :] * cw[2:3]
        y = y + x * cw[3:4]
        cbuf_ref[b, 0:8, :] = x[C - 8:C, :]
        acts.append(jax.nn.silu(y))
        bd = bd_ref[b]
        betas.append(jax.nn.sigmoid(bd))
        g_all = nega_ref[...] * _softplus(bd + dtb_ref[...])
        gcs.append(_mm_exact_lhs(cum_lhs, g_all))

    kst, qst, gcol, glcol, bcol, decay, a_mat, rhs = {}, {}, {}, {}, {}, {}, {}, {}
    for u in units:
        b, grp = u
        act, gc_all, beta_all = acts[b], gcs[b], betas[b]

        def stack(fn):
            return jnp.concatenate([fn(h) for h in heads_of(grp)], axis=0)

        def l2n(cols0, h):
            s = act[:, cols0 + (h // rep) * HEAD_DIM:cols0 + (h // rep + 1) * HEAD_DIM]
            return s * lax.rsqrt(jnp.sum(s * s, -1, keepdims=True) + L2_EPS)

        kst[u] = stack(lambda h: l2n(B_KEY_WIDTH, h))
        qst[u] = stack(lambda h: l2n(0, h) * (HEAD_DIM ** -0.5))
        vst = stack(lambda h: act[:, 2 * B_KEY_WIDTH + h * HEAD_DIM:2 * B_KEY_WIDTH + (h + 1) * HEAD_DIM])
        gcol[u] = stack(lambda h: lane_col(gc_all, B_V_HEADS + h))
        glcol[u] = stack(lambda h: jnp.broadcast_to(gc_all[C - 1:C, B_V_HEADS + h:B_V_HEADS + h + 1], (C, HEAD_DIM)))
        bcol[u] = stack(lambda h: lane_col(beta_all, h))
        grow = gcol[u].T[0:1, :]
        diff = jnp.concatenate([gcol[u], gcol[u]], axis=1) - grow
        decay[u] = jnp.where(tril, jnp.exp(jnp.where(tril, diff, 0.0)), 0.0)
        rhs[u] = jnp.concatenate([vst * bcol[u], kst[u] * bcol[u] * jnp.exp(gcol[u])], axis=1)
    for u in units:
        a_mat[u] = jnp.where(strict, jnp.concatenate([bcol[u], bcol[u]], axis=1) * _mm_nt(kst[u], kst[u]) * decay[u], 0.0)

    t_inv = _unit_lower_inverse_many([a_mat[u] for u in units], ri, ci, C)
    sol = [_mm(t, rhs[u]) for t, u in zip(t_inv, units)]
    qk = [_mm_nt(qst[u], kst[u]) * decay[u] for u in units]
    ws = []
    for sl, u in zip(sol, units):
        b, grp = u
        q_dec = qst[u] * jnp.exp(gcol[u])
        ws.append([_mm(jnp.concatenate([sl[i * C:(i + 1) * C, HEAD_DIM:], q_dec[i * C:(i + 1) * C]], axis=0),
                       s_ref[b * B_V_HEADS + h]) for i, h in enumerate(heads_of(grp))])
    v_new = [sl[:, :HEAD_DIM] - jnp.concatenate([w[:C] for w in wl], axis=0) for sl, wl in zip(sol, ws)]
    outs = [jnp.concatenate([w[C:] for w in wl], axis=0) + _mm(q, v) for wl, q, v in zip(ws, qk, v_new)]
    for u, v, o in zip(units, v_new, outs):
        b, grp = u
        k_dec = kst[u] * jnp.exp(glcol[u] - gcol[u])
        on = o * lax.rsqrt(jnp.mean(o * o, -1, keepdims=True) + RMS_EPS) * og
        for i, h in enumerate(heads_of(grp)):
            rs = slice(i * C, (i + 1) * C)
            si = b * B_V_HEADS + h
            s_ref[si] = s_ref[si] * jnp.exp(glcol[u][i * C:i * C + 1, :]) + _mm_tn(k_dec[rs], v[rs])
            cols = slice(h * HEAD_DIM, (h + 1) * HEAD_DIM)
            o_ref[b, :, cols] = on[rs] * jax.nn.silu(z_ref[b, :, cols])

    @pl.when(c == pl.num_programs(0) - 1)
    def _fin():
        sfin_ref[...] = s_ref[...]


def _mixb_prompt(proj, bd, conv_w, nega, dtb, og, batch, seq):
    nc = seq // DN_CHUNK
    qkv_blk = 2 * A_WIDTH // B_CONV_CH
    z_blk = (2 * A_WIDTH + B_CONV_CH) // B_VAL_WIDTH
    proj3 = proj.reshape(batch, seq, proj.shape[-1])
    bd3 = bd.reshape(batch, seq, LANES)
    o, s_fin = pl.pallas_call(
        _mixb_prompt_kernel,
        grid=(nc,),
        in_specs=[pl.BlockSpec((batch, DN_CHUNK, B_CONV_CH), lambda c: (0, c, qkv_blk)),
                  pl.BlockSpec((batch, DN_CHUNK, B_VAL_WIDTH), lambda c: (0, c, z_blk)),
                  pl.BlockSpec((batch, DN_CHUNK, LANES), lambda c: (0, c, 0)),
                  pl.BlockSpec((B_CONV, B_CONV_CH), lambda c: (0, 0)),
                  pl.BlockSpec((1, LANES), lambda c: (0, 0)),
                  pl.BlockSpec((1, LANES), lambda c: (0, 0)),
                  pl.BlockSpec((1, HEAD_DIM), lambda c: (0, 0))],
        out_specs=[pl.BlockSpec((batch, DN_CHUNK, B_VAL_WIDTH), lambda c: (0, c, 0)),
                   pl.BlockSpec((batch * B_V_HEADS, HEAD_DIM, HEAD_DIM), lambda c: (0, 0, 0))],
        out_shape=[jax.ShapeDtypeStruct((batch, seq, B_VAL_WIDTH), F32),
                   jax.ShapeDtypeStruct((batch * B_V_HEADS, HEAD_DIM, HEAD_DIM), F32)],
        scratch_shapes=[pltpu.VMEM((batch * B_V_HEADS, HEAD_DIM, HEAD_DIM), F32),
                        pltpu.VMEM((batch, DN_CHUNK + 8, B_CONV_CH), F32)],
        compiler_params=_params("arbitrary"),
        name="mixb_prompt",
    )(proj3, proj3, bd3, conv_w, nega, dtb, og)
    return (o.reshape(batch * seq, B_VAL_WIDTH),
            s_fin.reshape(batch, B_V_HEADS, HEAD_DIM, HEAD_DIM))


def _mixb_sample_pre_kernel(qkv_ref, buf_ref, bd_ref, cw_ref, nega_ref, dtb_ref,
                            act_ref, beta_ref, g_ref, *, steps, nb_rows):
    cw = cw_ref[...]

    def slab(j):
        if j < B_CONV - 1:
            return buf_ref[j * nb_rows:(j + 1) * nb_rows, :]
        jj = j - (B_CONV - 1)
        return qkv_ref[jj * nb_rows:(jj + 1) * nb_rows, :]

    for t in range(steps):
        y = slab(t) * cw[0:1]
        for i in range(1, B_CONV):
            y = y + slab(t + i) * cw[i:i + 1]
        act = jax.nn.silu(y)
        rs = slice(t * nb_rows, (t + 1) * nb_rows)
        for qh in range(B_QK_HEADS):
            cq = slice(qh * HEAD_DIM, (qh + 1) * HEAD_DIM)
            ck = slice(B_KEY_WIDTH + qh * HEAD_DIM, B_KEY_WIDTH + (qh + 1) * HEAD_DIM)
            qs = act[:, cq]
            ks = act[:, ck]
            act_ref[rs, cq] = qs * lax.rsqrt(jnp.sum(qs * qs, -1, keepdims=True) + L2_EPS) * (HEAD_DIM ** -0.5)
            act_ref[rs, ck] = ks * lax.rsqrt(jnp.sum(ks * ks, -1, keepdims=True) + L2_EPS)
        act_ref[rs, 2 * B_KEY_WIDTH:] = act[:, 2 * B_KEY_WIDTH:]
    bd = bd_ref[...]
    beta_ref[...] = jax.nn.sigmoid(bd)
    g_ref[...] = nega_ref[...] * _softplus(bd + dtb_ref[...])


def _mixb_sample_pre(proj, buf, bd, conv_w, nega, dtb, steps, nb_rows):
    m = proj.shape[0]
    qkv_blk = 2 * A_WIDTH // B_CONV_CH
    kern = functools.partial(_mixb_sample_pre_kernel, steps=steps, nb_rows=nb_rows)
    return pl.pallas_call(
        kern,
        grid=(1,),
        in_specs=[pl.BlockSpec((m, B_CONV_CH), lambda i: (0, qkv_blk)),
                  pl.BlockSpec(buf.shape, lambda i: (0, 0)),
                  pl.BlockSpec((m, LANES), lambda i: (0, 0)),
                  pl.BlockSpec((B_CONV, B_CONV_CH), lambda i: (0, 0)),
                  pl.BlockSpec((1, LANES), lambda i: (0, 0)),
                  pl.BlockSpec((1, LANES), lambda i: (0, 0))],
        out_specs=[pl.BlockSpec((m, B_CONV_CH), lambda i: (0, 0)),
                   pl.BlockSpec((m, LANES), lambda i: (0, 0)),
                   pl.BlockSpec((m, LANES), lambda i: (0, 0))],
        out_shape=[jax.ShapeDtypeStruct((m, B_CONV_CH), F32),
                   jax.ShapeDtypeStruct((m, LANES), F32),
                   jax.ShapeDtypeStruct((m, LANES), F32)],
        compiler_params=_params("arbitrary"),
        name="mixb_sample_pre",
    )(proj, buf, bd, conv_w, nega, dtb)


def _mixb_sample_rec_kernel(kq_ref, v_ref, z_ref, beta_ref, g_ref, s0_ref, og_ref,
                            o_ref, s_out_ref, *, steps, pairs):
    og = og_ref[...]
    zpad = jnp.zeros((HEAD_DIM - 8, HEAD_DIM), F32)
    zrows = jnp.zeros((8 - steps, HEAD_DIM), F32)
    heads = range(B_V_HEADS)

    def body(bi, carry):
        kqs = [_mm(kq_ref[bi, h], s0_ref[bi, h]) for h in heads]
        pending = []
        for h in heads:
            kq = kq_ref[bi, h]
            g = g_ref[bi, h]
            beta = beta_ref[bi, h]
            v = v_ref[bi, h]
            gc = [g[0:1]]
            for t in range(1, steps):
                gc.append(gc[-1] + g[t:t + 1])
            k = [kq[t:t + 1] for t in range(steps)]
            q = [kq[4 + t:5 + t] for t in range(steps)]
            d = []
            for t in range(steps):
                acc = v[t:t + 1] - jnp.exp(gc[t]) * kqs[h][t:t + 1]
                for j in range(t):
                    kk = jnp.sum(k[j] * k[t], -1, keepdims=True)
                    acc = acc - jnp.exp(gc[t] - gc[j]) * kk * d[j]
                d.append(beta[t:t + 1] * acc)
            outs = []
            for t in range(steps):
                o = jnp.exp(gc[t]) * kqs[h][4 + t:5 + t]
                for j in range(t + 1):
                    qk = jnp.sum(k[j] * q[t], -1, keepdims=True)
                    o = o + jnp.exp(gc[t] - gc[j]) * qk * d[j]
                outs.append(o * lax.rsqrt(jnp.mean(o * o, -1, keepdims=True) + RMS_EPS) * og)
            o_ref[bi, h] = jnp.concatenate(outs + [zrows], axis=0) * jax.nn.silu(z_ref[bi, h])
            k_dec = jnp.concatenate([jnp.exp(gc[-1] - gc[j]) * k[j] for j in range(steps)] + [zrows], axis=0)
            k_pad = jnp.concatenate([k_dec, zpad], axis=0)
            d_pad = jnp.concatenate(d + [zrows, zpad], axis=0)
            pending.append((k_pad.T, d_pad, jnp.exp(gc[-1])))
        for h, (k_t, d_pad, decay_last) in zip(heads, pending):
            s_out_ref[bi, h] = s0_ref[bi, h] * decay_last + _mm(k_t, d_pad)
        return carry

    lax.fori_loop(0, pairs // B_V_HEADS, body, 0)


def _mixb_sample_rec(kq, v, z, beta, g, s0, og, steps, bb):
    nb = kq.shape[0]
    tile = pl.BlockSpec((bb, B_V_HEADS, 8, HEAD_DIM), lambda i: (i, 0, 0, 0))
    st = pl.BlockSpec((bb, B_V_HEADS, HEAD_DIM, HEAD_DIM), lambda i: (i, 0, 0, 0))
    kern = functools.partial(_mixb_sample_rec_kernel, steps=steps, pairs=bb * B_V_HEADS)
    return pl.pallas_call(
        kern,
        grid=(nb // bb,),
        in_specs=[tile, tile, tile, tile, tile, st, pl.BlockSpec((1, HEAD_DIM), lambda i: (0, 0))],
        out_specs=[tile, st],
        out_shape=[jax.ShapeDtypeStruct((nb, B_V_HEADS, 8, HEAD_DIM), F32),
                   jax.ShapeDtypeStruct((nb, B_V_HEADS, HEAD_DIM, HEAD_DIM), F32)],
        compiler_params=_params("arbitrary"),
        name="mixb_sample_rec",
    )(kq, v, z, beta, g, s0, og)


def _outproj_kernel(ap_ref, bp_ref, xp_ref, g1p_ref, sc2p_ref, sh2p_ref,
                    as_ref, bs_ref, xs_ref, g1s_ref, sc2s_ref, sh2s_ref,
                    w_ref, lg_ref, lb_ref, wr_ref, rb_ref, x1_ref, h2_ref, route_ref, count_ref,
                    base_ref, mix_ref, *, prompt_tiles):
    s = pl.program_id(0)

    @pl.when(s == 0)
    def _():
        base_ref[...] = jnp.zeros_like(base_ref)
        mix_ref[...] = jnp.zeros_like(mix_ref)

    cur_prompt = s < prompt_tiles
    a = jnp.where(cur_prompt, ap_ref[...], as_ref[...]).astype(BF16)
    mix_a = jnp.dot(a, w_ref[:A_WIDTH, :], preferred_element_type=F32)

    def project():
        b = jnp.where(cur_prompt, bp_ref[...], bs_ref[...]).astype(BF16)
        return mix_a + jnp.dot(b, w_ref[A_WIDTH:, :], preferred_element_type=F32)

    prev_prompt = s - 1 < prompt_tiles
    pick = lambda p_ref, s_ref: jnp.where(prev_prompt, p_ref[...], s_ref[...])
    mix = mix_ref[(s + 1) % 2]
    x1 = _layer_norm_rows(ALPHA * pick(xp_ref, xs_ref) + pick(g1p_ref, g1s_ref) * mix, lg_ref[...], lb_ref[...])
    x1_ref[...] = x1
    h2 = x1 * (1.0 + pick(sc2p_ref, sc2s_ref)) + pick(sh2p_ref, sh2s_ref)
    h2_ref[...] = h2
    route_ref[...], mix_new = _route_tile(_mm3(h2, wr_ref[...]), rb_ref[...], base_ref, s >= 1, project)
    count_ref[...] = base_ref[...]
    mix_ref[s % 2] = mix_new


def _outproj(prompt, sample, mod_p, mod_s, w_out, lg, lb, w_router, r_bias, tm, seq):
    n_p = prompt[2].shape[0]
    n_s = sample[2].shape[0]
    pt = n_p // tm
    st = n_s // tm
    nt = pt + st
    m = n_p + n_s
    tiles_per_seq = seq // tm
    cur = lambda s: jnp.minimum(s, nt - 1)
    prev = lambda s: jnp.maximum(s - 1, 0)
    p_idx = lambda t: jnp.minimum(t, pt - 1)
    s_idx = lambda t: jnp.clip(t - pt, 0, st - 1)
    p_cur = lambda w: pl.BlockSpec((tm, w), lambda s: (p_idx(cur(s)), 0))
    s_cur = lambda w: pl.BlockSpec((tm, w), lambda s: (s_idx(cur(s)), 0))
    p_prev = lambda w: pl.BlockSpec((tm, w), lambda s: (p_idx(prev(s)), 0))
    s_prev = lambda w: pl.BlockSpec((tm, w), lambda s: (s_idx(prev(s)), 0))
    p_mod = lambda col: pl.BlockSpec((None, 1, D_MODEL), lambda s: (p_idx(prev(s)) // tiles_per_seq, 0, col))
    s_mod = lambda col: pl.BlockSpec((tm, D_MODEL), lambda s: (0, col))
    row = lambda w: pl.BlockSpec((tm, w), lambda s: (prev(s), 0))
    full = lambda shape: pl.BlockSpec(shape, lambda s: (0, 0))
    return pl.pallas_call(
        functools.partial(_outproj_kernel, prompt_tiles=pt),
        grid=(nt + 1,),
        in_specs=[p_cur(A_WIDTH), p_cur(B_VAL_WIDTH), p_prev(D_MODEL), p_mod(2), p_mod(4), p_mod(3),
                  s_cur(A_WIDTH), s_cur(B_VAL_WIDTH), s_prev(D_MODEL), s_mod(2), s_mod(4), s_mod(3),
                  full((D_MODEL, D_MODEL)), full((1, D_MODEL)), full((1, D_MODEL)),
                  full((D_MODEL, LANES)), full((1, LANES))],
        out_specs=[row(D_MODEL), row(D_MODEL), row(LANES),
                   full((1, LANES))],
        out_shape=[jax.ShapeDtypeStruct((m, D_MODEL), F32),
                   jax.ShapeDtypeStruct((m, D_MODEL), F32),
                   jax.ShapeDtypeStruct((m, LANES), F32),
                   jax.ShapeDtypeStruct((1, LANES), F32)],
        scratch_shapes=[pltpu.VMEM((1, LANES), F32),
                        pltpu.VMEM((2, tm, D_MODEL), F32)],
        compiler_params=_params("arbitrary"),
        name="outproj",
    )(*prompt, mod_p, mod_p, mod_p, *sample, mod_s, mod_s, mod_s, w_out, lg, lb, w_router, r_bias)


def _route_tile(lg, bias, base_ref, valid, between):
    tm = lg.shape[0]
    lane = lax.broadcasted_iota(jnp.int32, lg.shape, 1)
    neg = -jnp.inf

    def first_argmax(score):
        mx = jnp.max(score, -1, keepdims=True)
        return jnp.min(jnp.where(score == mx, lane, LANES), -1, keepdims=True)

    def pick(vals, idx):
        return jnp.sum(jnp.where(lane == idx, vals, 0.0), -1, keepdims=True)

    gmask = lane < N_GROUPS
    mg = jnp.max(jnp.where(gmask, lg, neg), -1, keepdims=True)
    eg = jnp.where(gmask, jnp.exp(jnp.where(gmask, lg - mg, 0.0)), 0.0)
    pg = eg / jnp.sum(eg, -1, keepdims=True)
    sel_g = first_argmax(jnp.where(gmask, lg + bias, neg))
    p_sel = pick(pg, sel_g)

    lo = N_GROUPS + sel_g * EXPERTS_PER_GROUP
    emask = (lane >= lo) & (lane < lo + EXPERTS_PER_GROUP)
    me = jnp.max(jnp.where(emask, lg, neg), -1, keepdims=True)
    ee = jnp.where(emask, jnp.exp(jnp.where(emask, lg - me, 0.0)), 0.0)
    pe = ee / jnp.sum(ee, -1, keepdims=True)
    score = jnp.where(emask, pe + bias, neg)
    i1 = first_argmax(score)
    i2 = first_argmax(jnp.where(lane == i1, neg, score))
    w1 = pick(pe, i1)
    w2 = pick(pe, i2)
    wsum = w1 + w2
    gate1 = w1 / wsum * p_sel
    gate2 = w2 / wsum * p_sel

    hot = ((lane == i1) | (lane == i2)).astype(BF16)
    extra = between()
    ri = lax.broadcasted_iota(jnp.int32, (tm, tm), 0)
    ci = lax.broadcasted_iota(jnp.int32, (tm, tm), 1)
    before = jnp.dot((ri > ci).astype(BF16), hot, preferred_element_type=F32) + base_ref[...]
    rank1 = pick(before, i1)
    rank2 = pick(before, i2)
    base_ref[...] = jnp.where(valid, base_ref[...] + jnp.sum(hot.astype(F32), 0, keepdims=True), base_ref[...])

    out = jnp.where(lane == 0, (i1 - N_GROUPS).astype(F32), 0.0)
    out = jnp.where(lane == 1, (i2 - N_GROUPS).astype(F32), out)
    out = jnp.where(lane == 2, gate1, out)
    out = jnp.where(lane == 3, gate2, out)
    out = jnp.where(lane == 4, rank1, out)
    out = jnp.where(lane == 5, rank2, out)
    return out, extra


DISPATCH_ROWS = 512


def _dispatch_kernel(dest_ref, last_ref, nv_ref, h_ref, xb_ref, zbuf, pbuf, zsem, ssem, *, n_tok, n_blocks):
    i = pl.program_id(0)
    nv = nv_ref[0]

    def zero_block(blk):
        return pltpu.make_async_copy(zbuf, xb_ref.at[pl.ds(blk * MOE_ROWS, MOE_ROWS)], zsem)

    @pl.when(i == 0)
    def _():
        zbuf[...] = jnp.zeros_like(zbuf)
        for e in range(N_EXPERTS):
            @pl.when(last_ref[e] >= 0)
            def _():
                zero_block(last_ref[e]).start()

        def tail_start(b, carry):
            zero_block(b).start()
            return carry

        lax.fori_loop(nv, n_blocks, tail_start, 0)
        for e in range(N_EXPERTS):
            @pl.when(last_ref[e] >= 0)
            def _():
                zero_block(0).wait()

        def tail_wait(b, carry):
            zero_block(0).wait()
            return carry

        lax.fori_loop(nv, n_blocks, tail_wait, 0)

    half = D_MODEL // 2
    hi = lax.bitcast_convert_type(h_ref[:, :half].astype(BF16).astype(F32), jnp.uint32)
    lo = lax.bitcast_convert_type(h_ref[:, half:].astype(BF16).astype(F32), jnp.uint32)
    pbuf[...] = (hi & jnp.uint32(0xFFFF0000)) | lax.shift_right_logical(lo, jnp.uint32(16))

    def row_copy(r, choice):
        slot = dest_ref[choice * n_tok + i * DISPATCH_ROWS + r]
        return pltpu.make_async_copy(pbuf.at[pl.ds(r, 1)], xb_ref.at[pl.ds(slot, 1)], ssem)

    def body(r, carry):
        row_copy(r, 0).start()
        row_copy(r, 1).start()
        return carry

    lax.fori_loop(0, DISPATCH_ROWS, body, 0, unroll=8)
    for _ in range(2):
        pltpu.make_async_copy(pbuf, xb_ref.at[pl.ds(0, DISPATCH_ROWS)], ssem).wait()


def _dispatch(dest_flat, last_block, n_valid, h2, n_blocks):
    n_tok = h2.shape[0]
    kern = functools.partial(_dispatch_kernel, n_tok=n_tok, n_blocks=n_blocks)
    return pl.pallas_call(
        kern,
        grid_spec=pltpu.PrefetchScalarGridSpec(
            num_scalar_prefetch=3,
            grid=(n_tok // DISPATCH_ROWS,),
            in_specs=[pl.BlockSpec((DISPATCH_ROWS, D_MODEL), lambda i, *_: (i, 0))],
            out_specs=pl.BlockSpec(memory_space=pl.ANY),
            scratch_shapes=[pltpu.VMEM((MOE_ROWS, D_MODEL // 2), jnp.uint32),
                            pltpu.VMEM((DISPATCH_ROWS, D_MODEL // 2), jnp.uint32),
                            pltpu.SemaphoreType.DMA(()),
                            pltpu.SemaphoreType.DMA(())]),
        out_shape=jax.ShapeDtypeStruct((n_blocks * MOE_ROWS, D_MODEL // 2), jnp.uint32),
        compiler_params=_params("arbitrary"),
        name="dispatch",
    )(dest_flat, last_block, n_valid, h2)


def _expert_kernel(be_ref, nv_ref, first_ref, next_ref, wslot_ref,
                   x_ref, wg_ref, wu_ref, wd_ref, y_ref,
                   wg_st, wu_st, wd_st, wg_s, wu_s, wd_s, wsem):
    i = pl.program_id(0)
    nv = nv_ref[0]

    def weight_copies(e, p):
        return [pltpu.make_async_copy(src.at[e], dst.at[p], wsem.at[p])
                for src, dst in ((wg_ref, wg_st), (wu_ref, wu_st), (wd_ref, wd_st))]

    @pl.when(i == 0)
    def _():
        for c in weight_copies(be_ref[0], 0):
            c.start()

    @pl.when((i < nv) & (first_ref[i] == 1))
    def _():
        p = wslot_ref[i]
        for c in weight_copies(be_ref[i], p):
            c.wait()
        wg_s[...] = wg_st[p].astype(BF16)
        wu_s[...] = wu_st[p].astype(BF16)
        wd_s[...] = wd_st[p].astype(BF16)

        @pl.when(next_ref[i] >= 0)
        def _():
            for c in weight_copies(next_ref[i], 1 - p):
                c.start()

    @pl.when(i < nv)
    def _():
        u = x_ref[...]
        hi = lax.bitcast_convert_type(u & jnp.uint32(0xFFFF0000), F32)
        lo = lax.bitcast_convert_type(lax.shift_left(u, jnp.uint32(16)), F32)
        x = jnp.concatenate([hi, lo], axis=1).astype(BF16)
        hg = jnp.dot(x, wg_s[...], preferred_element_type=F32)
        hu = jnp.dot(x, wu_s[...], preferred_element_type=F32)
        hid = (jax.nn.silu(hg) * hu).astype(BF16)
        y_ref[...] = jnp.dot(hid, wd_s[...], preferred_element_type=F32)

    @pl.when(i >= nv)
    def _():
        y_ref[...] = jnp.zeros_like(y_ref)


def _experts(block_e, n_valid, xb, w_gate, w_up, w_down):
    n_blocks = block_e.shape[0]
    idx = jnp.arange(n_blocks, dtype=jnp.int32)
    first = (idx < n_valid[0]) & ((idx == 0) | (block_e != jnp.roll(block_e, 1)))
    wslot = (jnp.cumsum(first.astype(jnp.int32)) - 1) % 2
    first_at = jnp.where(first, idx, n_blocks)
    next_first = jnp.concatenate([lax.cummin(first_at, reverse=True)[1:], jnp.full((1,), n_blocks, jnp.int32)])
    next_e = jnp.where(next_first < n_blocks, block_e[jnp.minimum(next_first, n_blocks - 1)], -1)
    any_spec = pl.BlockSpec(memory_space=pl.ANY)
    rows = pl.BlockSpec((MOE_ROWS, D_MODEL), lambda i, *_: (i, 0))
    return pl.pallas_call(
        _expert_kernel,
        grid_spec=pltpu.PrefetchScalarGridSpec(
            num_scalar_prefetch=5,
            grid=(n_blocks,),
            in_specs=[pl.BlockSpec((MOE_ROWS, D_MODEL // 2), lambda i, *_: (i, 0)), any_spec, any_spec, any_spec],
            out_specs=rows,
            scratch_shapes=[pltpu.VMEM((2, D_MODEL, D_EXPERT), F32),
                            pltpu.VMEM((2, D_MODEL, D_EXPERT), F32),
                            pltpu.VMEM((2, D_EXPERT, D_MODEL), F32),
                            pltpu.VMEM((D_MODEL, D_EXPERT), BF16),
                            pltpu.VMEM((D_MODEL, D_EXPERT), BF16),
                            pltpu.VMEM((D_EXPERT, D_MODEL), BF16),
                            pltpu.SemaphoreType.DMA((2,))]),
        out_shape=jax.ShapeDtypeStruct((n_blocks * MOE_ROWS, D_MODEL), F32),
        compiler_params=_params("arbitrary"),
        name="experts",
    )(block_e, n_valid, first.astype(jnp.int32), next_e.astype(jnp.int32), wslot.astype(jnp.int32),
      xb, w_gate, w_up, w_down)


def _final_kernel(dest_ref, x1_ref, route_ref, g2_ref, lg_ref, lb_ref, yb_ref, o_ref, ybuf, sem,
                  *, row0, n_tok):
    i = pl.program_id(0)
    last = pl.num_programs(0) - 1
    tm = o_ref.shape[0]
    slot = i % 2

    def row_copy(tile, r, choice, s):
        src = dest_ref[choice * n_tok + row0 + tile * tm + r]
        return pltpu.make_async_copy(yb_ref.at[pl.ds(src, 1)], ybuf.at[s, choice, pl.ds(r, 1)], sem.at[s])

    def tile_wait(s):
        for choice in range(2):
            pltpu.make_async_copy(yb_ref.at[pl.ds(0, tm)], ybuf.at[s, choice], sem.at[s]).wait()

    @pl.when(i == 0)
    def _():
        def body(r, carry):
            row_copy(0, r, 0, 0).start()
            row_copy(0, r, 1, 0).start()
            return carry

        lax.fori_loop(0, tm, body, 0, unroll=8)

    tile_wait(slot)
    nxt = jnp.minimum(i + 1, last)
    for r in range(tm):
        row_copy(nxt, r, 0, 1 - slot).start()
        row_copy(nxt, r, 1, 1 - slot).start()
    route = route_ref[...]
    ff = ybuf[slot, 0] * route[:, 2:3] + ybuf[slot, 1] * route[:, 3:4]
    o_ref[...] = _layer_norm_rows(ALPHA * x1_ref[...] + g2_ref[...] * ff, lg_ref[...], lb_ref[...])

    @pl.when(i == last)
    def _():
        tile_wait(1 - slot)


def _final(dest_flat, x1, yb, route, mod, lg, lb, tm, rows_per_mod, row0, n_rows):
    n_tok = x1.shape[0]
    off = row0 // tm
    return pl.pallas_call(
        functools.partial(_final_kernel, row0=row0, n_tok=n_tok),
        grid_spec=pltpu.PrefetchScalarGridSpec(
            num_scalar_prefetch=1,
            grid=(n_rows // tm,),
            in_specs=[pl.BlockSpec((tm, D_MODEL), lambda i, *_: (off + i, 0)),
                      pl.BlockSpec((tm, LANES), lambda i, *_: (off + i, 0)),
                      _mod_spec(mod, 5, tm, rows_per_mod),
                      pl.BlockSpec((1, D_MODEL), lambda i, *_: (0, 0)),
                      pl.BlockSpec((1, D_MODEL), lambda i, *_: (0, 0)),
                      pl.BlockSpec(memory_space=pl.ANY)],
            out_specs=pl.BlockSpec((tm, D_MODEL), lambda i, *_: (i, 0)),
            scratch_shapes=[pltpu.VMEM((2, 2, tm, D_MODEL), F32),
                            pltpu.SemaphoreType.DMA((2,))]),
        out_shape=jax.ShapeDtypeStruct((n_rows, D_MODEL), F32),
        compiler_params=_params("arbitrary"),
        name="final",
    )(dest_flat, x1, route, mod, lg, lb, yb)


def _lane_row(vec, offset):
    return jnp.zeros((1, LANES), F32).at[0, offset:offset + vec.shape[0]].set(vec.astype(F32))


def kernel(x_prompt, x_sample, state_conv, state_ssm, c_prompt, c_sample, w_ada, b_ada, w_in, a_ws, a_bs, a_norm_g, a_norm_b, b_conv_w, b_a_log, b_dt_bias, b_onorm_g, w_out, ln1_g, ln1_b, w_router_g, b_router_g, w_router_e, b_router_e, w_gate, w_up, w_down, ln2_g, ln2_b):
    batch, seq, d = x_prompt.shape
    nb, steps, _ = x_sample.shape
    n_p = batch * seq
    n_s = nb * steps
    n_tok = n_p + n_s
    l = 0

    c_rows = batch + nb
    c_pad = (-c_rows) % 8
    c_all = jnp.concatenate([c_prompt, c_sample, jnp.zeros((c_pad, d), F32)], axis=0)
    mod = _ada(c_all, w_ada[l], b_ada[l].reshape(1, -1))
    mod_p = mod[:batch].reshape(batch, 1, 6 * d)
    mod_s = mod[batch:batch + nb]

    w_in_t = jnp.swapaxes(w_in[l], 0, 1)
    w_bd = jnp.pad(w_in_t[MAIN_WIDTH:], ((0, LANES - 2 * B_V_HEADS), (0, 0))).astype(BF16)
    xp = x_prompt.reshape(n_p, d)
    xs = jnp.swapaxes(x_sample, 0, 1).reshape(n_s, d)
    proj_p, bd_p, proj_s, bd_s = _inproj(xp, xs, mod_p, mod_s, w_in_t, w_bd, 512, seq)

    ng = a_norm_g[l].reshape(1, -1)
    nbias = a_norm_b[l].reshape(1, -1)
    bias_tile = jnp.repeat(a_bs[l].T, HEAD_DIM, axis=1)
    a_out_p = _mixa_prompt(proj_p, a_ws[l], bias_tile, ng, nbias, 256)
    coef = jnp.repeat(jnp.transpose(a_ws[l][:, :steps, :steps], (1, 2, 0)).reshape(steps * steps, A_HEADS),
                      HEAD_DIM, axis=1)
    a_out_s, chunkv = _mixa_sample(proj_s, coef, bias_tile[:steps], ng, nbias, steps, nb)

    nega = _lane_row(-jnp.exp(b_a_log[l].astype(F32)), B_V_HEADS)
    dtb = _lane_row(b_dt_bias[l], B_V_HEADS)
    og = b_onorm_g[l].reshape(1, -1)
    b_out_p, ssm_p = _mixb_prompt(proj_p, bd_p, b_conv_w[l], nega, dtb, og, batch, seq)
    qkv0 = 2 * A_WIDTH
    conv_p = proj_p.reshape(batch, seq, MAIN_WIDTH)[:, seq - (B_CONV - 1):, qkv0:qkv0 + B_CONV_CH]

    buf_s = jnp.swapaxes(state_conv[l], 0, 1).reshape((B_CONV - 1) * nb, B_CONV_CH)
    act_s, beta_s, g_s = _mixb_sample_pre(proj_s, buf_s, bd_s, b_conv_w[l], nega, dtb, steps, nb)
    rep = B_V_HEADS // B_QK_HEADS
    act4 = act_s.reshape(steps, nb, B_CONV_CH)
    q8 = jnp.repeat(act4[..., :B_KEY_WIDTH].reshape(steps, nb, B_QK_HEADS, HEAD_DIM), rep, axis=2)
    k8 = jnp.repeat(act4[..., B_KEY_WIDTH:2 * B_KEY_WIDTH].reshape(steps, nb, B_QK_HEADS, HEAD_DIM), rep, axis=2)

    def to_tiles(a):
        a = jnp.transpose(a, (1, 2, 0, 3))
        return jnp.pad(a, ((0, 0), (0, 0), (0, 8 - a.shape[2]), (0, 0)))

    kq_t = jnp.concatenate([to_tiles(k8)[:, :, :4], to_tiles(q8)[:, :, :4]], axis=2)
    v_t = to_tiles(act4[..., 2 * B_KEY_WIDTH:].reshape(steps, nb, B_V_HEADS, HEAD_DIM))
    z_t = to_tiles(proj_s[:, qkv0 + B_CONV_CH:].reshape(steps, nb, B_V_HEADS, HEAD_DIM))

    def gate_tiles(a, off):
        a = a[:, off:off + B_V_HEADS].reshape(steps, nb, B_V_HEADS, 1)
        return to_tiles(jnp.broadcast_to(a, (steps, nb, B_V_HEADS, HEAD_DIM)))

    o_t, ssm_s = _mixb_sample_rec(kq_t, v_t, z_t, gate_tiles(beta_s, 0), gate_tiles(g_s, B_V_HEADS),
                                  state_ssm[l], og, steps, 8)
    b_out_s = jnp.transpose(o_t[:, :, :steps], (2, 0, 1, 3)).reshape(n_s, B_VAL_WIDTH)
    conv_s = jnp.swapaxes(proj_s.reshape(steps, nb, MAIN_WIDTH)[steps - (B_CONV - 1):, :, qkv0:qkv0 + B_CONV_CH], 0, 1)

    w_out_b = w_out[l].astype(BF16)
    lg1 = ln1_g[l].reshape(1, -1)
    lb1 = ln1_b[l].reshape(1, -1)
    w_router = jnp.concatenate([w_router_g[l], w_router_e[l],
                                jnp.zeros((d, LANES - N_GROUPS - N_EXPERTS), F32)], axis=1)
    out_tm = 256
    mod_s_tile = jnp.tile(mod_s, (out_tm // nb, 1))
    r_bias = jnp.concatenate([b_router_g[l], b_router_e[l],
                              jnp.zeros((LANES - N_GROUPS - N_EXPERTS,), F32)]).reshape(1, LANES)
    x1, h2, route, counts_row = _outproj((a_out_p, b_out_p, xp), (a_out_s, b_out_s, xs), mod_p, mod_s_tile,
                                         w_out_b, lg1, lb1, w_router, r_bias, out_tm, seq)

    counts = counts_row[0, N_GROUPS:N_GROUPS + N_EXPERTS].astype(jnp.int32)
    route_t = route[:, :8].T
    eid = route_t[0:2].astype(jnp.int32)
    rank = route_t[4:6].astype(jnp.int32)
    padded = (counts + MOE_ROWS - 1) // MOE_ROWS * MOE_ROWS
    pend = jnp.cumsum(padded)
    pstart = pend - padded
    dest = rank + jnp.sum(jnp.where(eid[None] == jnp.arange(N_EXPERTS, dtype=jnp.int32)[:, None, None],
                                    pstart[:, None, None], 0), axis=0)
    n_blocks = -(-(2 * n_tok) // MOE_ROWS) + N_EXPERTS
    dest_flat = dest.reshape(-1)
    last_block = jnp.where(padded > 0, pend // MOE_ROWS - 1, -1).astype(jnp.int32)
    blk0 = jnp.arange(n_blocks, dtype=jnp.int32) * MOE_ROWS
    block_e = jnp.minimum(jnp.sum(pend[None, :] <= blk0[:, None], axis=1), N_EXPERTS - 1).astype(jnp.int32)
    n_valid = (pend[-1:] // MOE_ROWS).astype(jnp.int32)

    xb = _dispatch(dest_flat, last_block, n_valid, h2, n_blocks)
    yb = _experts(block_e, n_valid, xb, w_gate[l], w_up[l], w_down[l])

    lg2 = ln2_g[l].reshape(1, -1)
    lb2 = ln2_b[l].reshape(1, -1)
    y_p = _final(dest_flat, x1, yb, route, mod_p, lg2, lb2, 512, seq, 0, n_p)
    y_s = _final(dest_flat, x1, yb, route, mod_s, lg2, lb2, nb, 0, n_p, n_s)

    y_prompt = y_p.reshape(batch, seq, d)
    y_sample = jnp.swapaxes(y_s.reshape(steps, nb, d), 0, 1)
    chunkv_s = jnp.swapaxes(chunkv.reshape(steps, nb, A_HEADS, HEAD_DIM), 0, 1)
    return (y_prompt, y_sample, conv_p[None], ssm_p[None], conv_s[None], ssm_s[None], chunkv_s[None])
```

```python
import functools

import jax
import jax.numpy as jnp
from jax import lax
from jax.experimental import pallas as pl
from jax.experimental.pallas import tpu as pltpu

F32 = jnp.float32
BF16 = jnp.bfloat16

D_MODEL = 2048
DEPTH = 1
A_HEADS = 8
HEAD_DIM = 128
A_WIDTH = 1024
A_CHUNK = 128
B_QK_HEADS = 4
B_V_HEADS = 8
B_KEY_WIDTH = 512
B_VAL_WIDTH = 1024
B_CONV = 4
B_CONV_CH = 2048
DN_CHUNK = 64
MAIN_WIDTH = 2 * A_WIDTH + B_CONV_CH + B_VAL_WIDTH
N_GROUPS = 4
EXPERTS_PER_GROUP = 8
N_EXPERTS = 32
D_EXPERT = 512
ALPHA = (2 * DEPTH) ** 0.25
LN_EPS = 1e-5
RMS_EPS = 1e-6
L2_EPS = 1e-6

LANES = 128
VMEM_LIMIT = 56 * 1024 * 1024
MOE_ROWS = 256
MIXB_GROUP = 4


def _params(*sem):
    return pltpu.CompilerParams(dimension_semantics=sem, vmem_limit_bytes=VMEM_LIMIT)


def _mm(a, b):
    return jnp.dot(a.astype(BF16), b.astype(BF16), preferred_element_type=F32)


def _mm_nt(a, b):
    return lax.dot_general(a.astype(BF16), b.astype(BF16), (((1,), (1,)), ((), ())),
                           preferred_element_type=F32)


def _mm_tn(a, b):
    return lax.dot_general(a.astype(BF16), b.astype(BF16), (((0,), (0,)), ((), ())),
                           preferred_element_type=F32)


def _split(x):
    hi = x.astype(BF16)
    lo = (x - hi.astype(F32)).astype(BF16)
    return hi, lo


def _mm_exact_lhs(a_bf16, b):
    hi, lo = _split(b)
    return (jnp.dot(a_bf16, hi, preferred_element_type=F32)
            + jnp.dot(a_bf16, lo, preferred_element_type=F32))


def _mm3(a, b):
    ah, al = _split(a)
    bh, bl = _split(b)
    return (jnp.dot(ah, bh, preferred_element_type=F32) + jnp.dot(ah, bl, preferred_element_type=F32)
            + jnp.dot(al, bh, preferred_element_type=F32))


def _pack_bf16_pairs(x):
    half = x.shape[1] // 2
    hi = lax.bitcast_convert_type(x[:, :half].astype(BF16).astype(F32), jnp.uint32)
    lo = lax.bitcast_convert_type(x[:, half:].astype(BF16).astype(F32), jnp.uint32)
    return (hi & jnp.uint32(0xFFFF0000)) | lax.shift_right_logical(lo, jnp.uint32(16))


def _unpack_bf16_pairs(u):
    hi = lax.bitcast_convert_type(u & jnp.uint32(0xFFFF0000), F32)
    lo = lax.bitcast_convert_type(lax.shift_left(u, jnp.uint32(16)), F32)
    return jnp.concatenate([hi, lo], axis=1).astype(BF16)


def _softplus(x):
    return jnp.maximum(x, 0.0) + jnp.log1p(jnp.exp(-jnp.abs(x)))


def _layer_norm_rows(x, g, b):
    mu = jnp.mean(x, -1, keepdims=True)
    xc = x - mu
    var = jnp.mean(xc * xc, -1, keepdims=True)
    return xc * lax.rsqrt(var + LN_EPS) * g + b


def _ada_kernel(c_ref, w_ref, b_ref, o_ref):
    a = jax.nn.silu(c_ref[...]).astype(BF16)
    o_ref[...] = jnp.dot(a, w_ref[...].astype(BF16), preferred_element_type=F32) + b_ref[...]


def _ada(c_all, w_ada, b_ada):
    rows = c_all.shape[0]
    tn = 1024
    return pl.pallas_call(
        _ada_kernel,
        grid=(6 * D_MODEL // tn,),
        in_specs=[pl.BlockSpec((rows, D_MODEL), lambda j: (0, 0)),
                  pl.BlockSpec((D_MODEL, tn), lambda j: (0, j)),
                  pl.BlockSpec((1, tn), lambda j: (0, j))],
        out_specs=pl.BlockSpec((rows, tn), lambda j: (0, j)),
        out_shape=jax.ShapeDtypeStruct((rows, 6 * D_MODEL), F32),
        compiler_params=_params("arbitrary"),
        name="ada",
    )(c_all, w_ada, b_ada)


def _inproj_kernel(xp_ref, scp_ref, shp_ref, xs_ref, scs_ref, shs_ref, w_ref, wbd_ref,
                   op_ref, bdp_ref, os_ref, bds_ref, wb_ref, *, prompt_tiles):
    j = pl.program_id(0)
    i = pl.program_id(1)

    @pl.when(i == 0)
    def _():
        wb_ref[...] = w_ref[...].astype(BF16)

    def project(h, o_ref, bd_ref):
        o_ref[...] = _mm_nt(h, wb_ref[...])

        @pl.when(j == 0)
        def _():
            bd_ref[0] = _mm_nt(h, wbd_ref[...])

        @pl.when(j != 0)
        def _():
            bd_ref[0] = jnp.zeros(bd_ref.shape[1:], F32)

    @pl.when(i < prompt_tiles)
    def _():
        project((xp_ref[...] * (1.0 + scp_ref[...]) + shp_ref[...]).astype(BF16), op_ref, bdp_ref)

    @pl.when(i >= prompt_tiles)
    def _():
        n_s, nb = xs_ref.shape[0], scs_ref.shape[0]
        x = xs_ref[...].reshape(n_s // nb, nb, D_MODEL)
        h = x * (1.0 + scs_ref[...])[None] + shs_ref[...][None]
        project(h.reshape(n_s, D_MODEL).astype(BF16), os_ref, bds_ref)


def _mod_spec(mod, col, tm, rows_per_mod, axis=0):
    if rows_per_mod:
        tiles = rows_per_mod // tm
        return pl.BlockSpec((None, 1, D_MODEL), lambda *g: (g[axis] // tiles, 0, col))
    return pl.BlockSpec((tm, D_MODEL), lambda *g: (0, col))


def _inproj(xp, xs, mod_p, mod_s, w_in_t, w_bd, tm, seq):
    n_p, n_s = xp.shape[0], xs.shape[0]
    nb = mod_s.shape[0]
    tn = 1024
    nj = MAIN_WIDTH // tn
    pt = n_p // tm
    tiles_per_seq = seq // tm
    pi = lambda i: jnp.minimum(i, pt - 1)
    p_mod = lambda col: pl.BlockSpec((None, 1, D_MODEL), lambda j, i: (pi(i) // tiles_per_seq, 0, col))
    s_mod = lambda col: pl.BlockSpec((nb, D_MODEL), lambda j, i: (0, col))
    proj_p, bd_p, proj_s, bd_s = pl.pallas_call(
        functools.partial(_inproj_kernel, prompt_tiles=pt),
        grid=(nj, pt + 1),
        in_specs=[pl.BlockSpec((tm, D_MODEL), lambda j, i: (pi(i), 0)), p_mod(1), p_mod(0),
                  pl.BlockSpec((n_s, D_MODEL), lambda j, i: (0, 0)), s_mod(1), s_mod(0),
                  pl.BlockSpec((tn, D_MODEL), lambda j, i: (j, 0)),
                  pl.BlockSpec((LANES, D_MODEL), lambda j, i: (0, 0))],
        out_specs=[pl.BlockSpec((tm, tn), lambda j, i: (pi(i), j)),
                   pl.BlockSpec((1, tm, LANES), lambda j, i: (j, pi(i), 0)),
                   pl.BlockSpec((n_s, tn), lambda j, i: (0, j)),
                   pl.BlockSpec((1, n_s, LANES), lambda j, i: (j, 0, 0))],
        out_shape=[jax.ShapeDtypeStruct((n_p, MAIN_WIDTH), F32),
                   jax.ShapeDtypeStruct((nj, n_p, LANES), F32),
                   jax.ShapeDtypeStruct((n_s, MAIN_WIDTH), F32),
                   jax.ShapeDtypeStruct((nj, n_s, LANES), F32)],
        scratch_shapes=[pltpu.VMEM((tn, D_MODEL), BF16)],
        compiler_params=_params("arbitrary", "arbitrary"),
        name="inproj",
    )(xp, mod_p, mod_p, xs, mod_s, mod_s, w_in_t, w_bd)
    return proj_p, bd_p[0], proj_s, bd_s[0]


def _mixa_prompt_kernel(p_ref, ws_ref, bias_ref, ng_ref, nb_ref, o_ref):
    rows = p_ref.shape[0]
    u = jax.nn.gelu(p_ref[:, :A_WIDTH])
    v = _layer_norm_rows(jax.nn.gelu(p_ref[:, A_WIDTH:]), ng_ref[...], nb_ref[...])
    ri = lax.broadcasted_iota(jnp.int32, (A_CHUNK, A_CHUNK), 0)
    ci = lax.broadcasted_iota(jnp.int32, (A_CHUNK, A_CHUNK), 1)
    for h in range(A_HEADS):
        w = jnp.where(ri >= ci, ws_ref[h], 0.0).astype(BF16)
        cols = slice(h * HEAD_DIM, (h + 1) * HEAD_DIM)
        for c in range(rows // A_CHUNK):
            rs = slice(c * A_CHUNK, (c + 1) * A_CHUNK)
            s = jnp.dot(w, v[rs, cols].astype(BF16), preferred_element_type=F32) + bias_ref[:, cols]
            o_ref[rs, cols] = u[rs, cols] * s


def _mixa_prompt(proj, a_ws, bias_tile, ng, nb, tm):
    m = proj.shape[0]
    return pl.pallas_call(
        _mixa_prompt_kernel,
        grid=(m // tm,),
        in_specs=[pl.BlockSpec((tm, 2 * A_WIDTH), lambda i: (i, 0)),
                  pl.BlockSpec((A_HEADS, A_CHUNK, A_CHUNK), lambda i: (0, 0, 0)),
                  pl.BlockSpec((A_CHUNK, A_WIDTH), lambda i: (0, 0)),
                  pl.BlockSpec((1, A_WIDTH), lambda i: (0, 0)),
                  pl.BlockSpec((1, A_WIDTH), lambda i: (0, 0))],
        out_specs=pl.BlockSpec((tm, A_WIDTH), lambda i: (i, 0)),
        out_shape=jax.ShapeDtypeStruct((m, A_WIDTH), F32),
        compiler_params=_params("arbitrary"),
        name="mixa_prompt",
    )(proj, a_ws, bias_tile, ng, nb)


def _mixa_sample_kernel(p_ref, coef_ref, bias_ref, ng_ref, nb_ref, o_ref, v_ref, *, steps, nb_rows):
    u = jax.nn.gelu(p_ref[:, :A_WIDTH])
    v = _layer_norm_rows(jax.nn.gelu(p_ref[:, A_WIDTH:]), ng_ref[...], nb_ref[...])
    v_ref[...] = v
    for t in range(steps):
        s = bias_ref[t:t + 1, :]
        for j in range(t + 1):
            s = s + coef_ref[t * steps + j:t * steps + j + 1, :] * v[j * nb_rows:(j + 1) * nb_rows, :]
        rs = slice(t * nb_rows, (t + 1) * nb_rows)
        o_ref[rs, :] = u[rs, :] * s


def _mixa_sample(proj, coef, bias, ng, nb, steps, nb_rows):
    m = proj.shape[0]
    kern = functools.partial(_mixa_sample_kernel, steps=steps, nb_rows=nb_rows)
    return pl.pallas_call(
        kern,
        grid=(1,),
        in_specs=[pl.BlockSpec((m, 2 * A_WIDTH), lambda i: (0, 0)),
                  pl.BlockSpec(coef.shape, lambda i: (0, 0)),
                  pl.BlockSpec(bias.shape, lambda i: (0, 0)),
                  pl.BlockSpec((1, A_WIDTH), lambda i: (0, 0)),
                  pl.BlockSpec((1, A_WIDTH), lambda i: (0, 0))],
        out_specs=[pl.BlockSpec((m, A_WIDTH), lambda i: (0, 0)),
                   pl.BlockSpec((m, A_WIDTH), lambda i: (0, 0))],
        out_shape=[jax.ShapeDtypeStruct((m, A_WIDTH), F32),
                   jax.ShapeDtypeStruct((m, A_WIDTH), F32)],
        compiler_params=_params("arbitrary"),
        name="mixa_sample",
    )(proj, coef, bias, ng, nb)


def _unit_lower_inverse_many(a_list, ri, ci, block):
    rows = a_list[0].shape[0]
    eye = (ri == ci).astype(F32)
    pair = (lax.shift_right_logical(ri, 1) == lax.shift_right_logical(ci, 1)) & ((ri & 1) == 1) & ((ci & 1) == 0)
    ts = [eye - jnp.where(pair, a, 0.0) for a in a_list]
    n = 2
    while n < block:
        sh = n.bit_length()
        m = ((lax.shift_right_logical(ri, sh) == lax.shift_right_logical(ci, sh))
             & ((ri & n) != 0) & ((ci & n) == 0))
        ans = [jnp.where(m, a, 0.0) for a in a_list]
        if n % 8:
            xs = [_mm(t, an) for t, an in zip(ts, ans)]
            ts = [t - _mm(x, t) for t, x in zip(ts, xs)]
        else:
            lower = [slice(r0 + n, r0 + 2 * n) for r0 in range(0, rows, 2 * n)]
            pick = lambda t: jnp.concatenate([t[sl] for sl in lower], axis=0)
            xs = [_mm(pick(t), an) for t, an in zip(ts, ans)]
            upd = [pick(t) - _mm(x, t) for t, x in zip(ts, xs)]
            ts = [jnp.concatenate([piece for k, r0 in enumerate(range(0, rows, 2 * n))
                                   for piece in (t[r0:r0 + n], u[k * n:(k + 1) * n])], axis=0)
                  for t, u in zip(ts, upd)]
        n *= 2
    return ts


def _mixb_prompt_kernel(qkv_ref, z_ref, bd_ref, cw_ref, nega_ref, dtb_ref, og_ref,
                        o_ref, sfin_ref, s_ref, cbuf_ref):
    c = pl.program_id(0)
    C = DN_CHUNK
    nseq = qkv_ref.shape[0]

    @pl.when(c == 0)
    def _init():
        s_ref[...] = jnp.zeros_like(s_ref)
        cbuf_ref[:, 0:8, :] = jnp.zeros((nseq, 8, B_CONV_CH), F32)

    cw = cw_ref[...]
    og = og_ref[...]
    ri64 = lax.broadcasted_iota(jnp.int32, (C, C), 0)
    ci64 = lax.broadcasted_iota(jnp.int32, (C, C), 1)
    cum_lhs = (ri64 >= ci64).astype(BF16)

    R = MIXB_GROUP * C
    ri = lax.broadcasted_iota(jnp.int32, (R, R), 0)
    ci = lax.broadcasted_iota(jnp.int32, (R, R), 1)
    same = lax.shift_right_logical(ri, C.bit_length() - 1) == lax.shift_right_logical(ci, C.bit_length() - 1)
    tril = same & (ri >= ci)
    strict = same & (ri > ci)
    rep = B_V_HEADS // B_QK_HEADS

    def lane_col(a, lane):
        return jnp.broadcast_to(a[:, lane:lane + 1], (a.shape[0], HEAD_DIM))

    groups = B_V_HEADS // MIXB_GROUP
    units = [(b, grp) for b in range(nseq) for grp in range(groups)]
    heads_of = lambda grp: list(range(grp * MIXB_GROUP, (grp + 1) * MIXB_GROUP))

    acts, gcs, betas = [], [], []
    for b in range(nseq):
        x = qkv_ref[b]
        cbuf_ref[b, 8:8 + C, :] = x
        y = cbuf_ref[b, 5:5 + C, :] * cw[0:1]
        y = y + cbuf_ref[b, 6:6 + C, :] * cw[1:2]
        y = y + cbuf_ref[b, 7:7 + C, :] * cw[2:3]
        y = y + x * cw[3:4]
        cbuf_ref[b, 0:8, :] = x[C - 8:C, :]
        acts.append(jax.nn.silu(y))
        bd = bd_ref[b]
        betas.append(jax.nn.sigmoid(bd))
        g_all = nega_ref[...] * _softplus(bd + dtb_ref[...])
        gcs.append(_mm_exact_lhs(cum_lhs, g_all))

    kst, qst, gcol, glcol, bcol, decay, a_mat, rhs = {}, {}, {}, {}, {}, {}, {}, {}
    for u in units:
        b, grp = u
        act, gc_all, beta_all = acts[b], gcs[b], betas[b]

        def stack(fn):
            return jnp.concatenate([fn(h) for h in heads_of(grp)], axis=0)

        def l2n(cols0, h):
            s = act[:, cols0 + (h // rep) * HEAD_DIM:cols0 + (h // rep + 1) * HEAD_DIM]
            return s * lax.rsqrt(jnp.sum(s * s, -1, keepdims=True) + L2_EPS)

        kst[u] = stack(lambda h: l2n(B_KEY_WIDTH, h))
        qst[u] = stack(lambda h: l2n(0, h) * (HEAD_DIM ** -0.5))
        vst = stack(lambda h: act[:, 2 * B_KEY_WIDTH + h * HEAD_DIM:2 * B_KEY_WIDTH + (h + 1) * HEAD_DIM])
        gcol[u] = stack(lambda h: lane_col(gc_all, B_V_HEADS + h))
        glcol[u] = stack(lambda h: jnp.broadcast_to(gc_all[C - 1:C, B_V_HEADS + h:B_V_HEADS + h + 1], (C, HEAD_DIM)))
        bcol[u] = stack(lambda h: lane_col(beta_all, h))
        grow = gcol[u].T[0:1, :]
        diff = jnp.concatenate([gcol[u], gcol[u]], axis=1) - grow
        decay[u] = jnp.where(tril, jnp.exp(jnp.where(tril, diff, 0.0)), 0.0)
        rhs[u] = jnp.concatenate([vst * bcol[u], kst[u] * bcol[u] * jnp.exp(gcol[u])], axis=1)
    for u in units:
        a_mat[u] = jnp.where(strict, jnp.concatenate([bcol[u], bcol[u]], axis=1) * _mm_nt(kst[u], kst[u]) * decay[u], 0.0)

    t_inv = _unit_lower_inverse_many([a_mat[u] for u in units], ri, ci, C)
    sol = [_mm(t, rhs[u]) for t, u in zip(t_inv, units)]
    qk = [_mm_nt(qst[u], kst[u]) * decay[u] for u in units]
    ws = []
    for sl, u in zip(sol, units):
        b, grp = u
        q_dec = qst[u] * jnp.exp(gcol[u])
        ws.append([_mm(jnp.concatenate([sl[i * C:(i + 1) * C, HEAD_DIM:], q_dec[i * C:(i + 1) * C]], axis=0),
                       s_ref[b * B_V_HEADS + h]) for i, h in enumerate(heads_of(grp))])
    v_new = [sl[:, :HEAD_DIM] - jnp.concatenate([w[:C] for w in wl], axis=0) for sl, wl in zip(sol, ws)]
    outs = [jnp.concatenate([w[C:] for w in wl], axis=0) + _mm(q, v) for wl, q, v in zip(ws, qk, v_new)]
    for u, v, o in zip(units, v_new, outs):
        b, grp = u
        k_dec = kst[u] * jnp.exp(glcol[u] - gcol[u])
        on = o * lax.rsqrt(jnp.mean(o * o, -1, keepdims=True) + RMS_EPS) * og
        for i, h in enumerate(heads_of(grp)):
            rs = slice(i * C, (i + 1) * C)
            si = b * B_V_HEADS + h
            s_ref[si] = s_ref[si] * jnp.exp(glcol[u][i * C:i * C + 1, :]) + _mm_tn(k_dec[rs], v[rs])
            cols = slice(h * HEAD_DIM, (h + 1) * HEAD_DIM)
            o_ref[b, :, cols] = on[rs] * jax.nn.silu(z_ref[b, :, cols])

    @pl.when(c == pl.num_programs(0) - 1)
    def _fin():
        sfin_ref[...] = s_ref[...]


def _mixb_prompt(proj, bd, conv_w, nega, dtb, og, batch, seq):
    nc = seq // DN_CHUNK
    qkv_blk = 2 * A_WIDTH // B_CONV_CH
    z_blk = (2 * A_WIDTH + B_CONV_CH) // B_VAL_WIDTH
    proj3 = proj.reshape(batch, seq, proj.shape[-1])
    bd3 = bd.reshape(batch, seq, LANES)
    o, s_fin = pl.pallas_call(
        _mixb_prompt_kernel,
        grid=(nc,),
        in_specs=[pl.BlockSpec((batch, DN_CHUNK, B_CONV_CH), lambda c: (0, c, qkv_blk)),
                  pl.BlockSpec((batch, DN_CHUNK, B_VAL_WIDTH), lambda c: (0, c, z_blk)),
                  pl.BlockSpec((batch, DN_CHUNK, LANES), lambda c: (0, c, 0)),
                  pl.BlockSpec((B_CONV, B_CONV_CH), lambda c: (0, 0)),
                  pl.BlockSpec((1, LANES), lambda c: (0, 0)),
                  pl.BlockSpec((1, LANES), lambda c: (0, 0)),
                  pl.BlockSpec((1, HEAD_DIM), lambda c: (0, 0))],
        out_specs=[pl.BlockSpec((batch, DN_CHUNK, B_VAL_WIDTH), lambda c: (0, c, 0)),
                   pl.BlockSpec((batch * B_V_HEADS, HEAD_DIM, HEAD_DIM), lambda c: (0, 0, 0))],
        out_shape=[jax.ShapeDtypeStruct((batch, seq, B_VAL_WIDTH), F32),
                   jax.ShapeDtypeStruct((batch * B_V_HEADS, HEAD_DIM, HEAD_DIM), F32)],
        scratch_shapes=[pltpu.VMEM((batch * B_V_HEADS, HEAD_DIM, HEAD_DIM), F32),
                        pltpu.VMEM((batch, DN_CHUNK + 8, B_CONV_CH), F32)],
        compiler_params=_params("arbitrary"),
        name="mixb_prompt",
    )(proj3, proj3, bd3, conv_w, nega, dtb, og)
    return (o.reshape(batch * seq, B_VAL_WIDTH),
            s_fin.reshape(batch, B_V_HEADS, HEAD_DIM, HEAD_DIM))


def _mixb_sample_pre_kernel(qkv_ref, buf_ref, bd_ref, cw_ref, nega_ref, dtb_ref,
                            act_ref, beta_ref, g_ref, *, steps, nb_rows):
    cw = cw_ref[...]

    def slab(j):
        if j < B_CONV - 1:
            return buf_ref[j * nb_rows:(j + 1) * nb_rows, :]
        jj = j - (B_CONV - 1)
        return qkv_ref[jj * nb_rows:(jj + 1) * nb_rows, :]

    for t in range(steps):
        y = slab(t) * cw[0:1]
        for i in range(1, B_CONV):
            y = y + slab(t + i) * cw[i:i + 1]
        act = jax.nn.silu(y)
        rs = slice(t * nb_rows, (t + 1) * nb_rows)
        for qh in range(B_QK_HEADS):
            cq = slice(qh * HEAD_DIM, (qh + 1) * HEAD_DIM)
            ck = slice(B_KEY_WIDTH + qh * HEAD_DIM, B_KEY_WIDTH + (qh + 1) * HEAD_DIM)
            qs = act[:, cq]
            ks = act[:, ck]
            act_ref[rs, cq] = qs * lax.rsqrt(jnp.sum(qs * qs, -1, keepdims=True) + L2_EPS) * (HEAD_DIM ** -0.5)
            act_ref[rs, ck] = ks * lax.rsqrt(jnp.sum(ks * ks, -1, keepdims=True) + L2_EPS)
        act_ref[rs, 2 * B_KEY_WIDTH:] = act[:, 2 * B_KEY_WIDTH:]
    bd = bd_ref[...]
    beta_ref[...] = jax.nn.sigmoid(bd)
    g_ref[...] = nega_ref[...] * _softplus(bd + dtb_ref[...])


def _mixb_sample_pre(proj, buf, bd, conv_w, nega, dtb, steps, nb_rows):
    m = proj.shape[0]
    qkv_blk = 2 * A_WIDTH // B_CONV_CH
    kern = functools.partial(_mixb_sample_pre_kernel, steps=steps, nb_rows=nb_rows)
    return pl.pallas_call(
        kern,
        grid=(1,),
        in_specs=[pl.BlockSpec((m, B_CONV_CH), lambda i: (0, qkv_blk)),
                  pl.BlockSpec(buf.shape, lambda i: (0, 0)),
                  pl.BlockSpec((m, LANES), lambda i: (0, 0)),
                  pl.BlockSpec((B_CONV, B_CONV_CH), lambda i: (0, 0)),
                  pl.BlockSpec((1, LANES), lambda i: (0, 0)),
                  pl.BlockSpec((1, LANES), lambda i: (0, 0))],
        out_specs=[pl.BlockSpec((m, B_CONV_CH), lambda i: (0, 0)),
                   pl.BlockSpec((m, LANES), lambda i: (0, 0)),
                   pl.BlockSpec((m, LANES), lambda i: (0, 0))],
        out_shape=[jax.ShapeDtypeStruct((m, B_CONV_CH), F32),
                   jax.ShapeDtypeStruct((m, LANES), F32),
                   jax.ShapeDtypeStruct((m, LANES), F32)],
        compiler_params=_params("arbitrary"),
        name="mixb_sample_pre",
    )(proj, buf, bd, conv_w, nega, dtb)


def _mixb_sample_rec_kernel(kq_ref, v_ref, z_ref, beta_ref, g_ref, s0_ref, og_ref,
                            o_ref, s_out_ref, *, steps, pairs):
    og = og_ref[...]
    zpad = jnp.zeros((HEAD_DIM - 8, HEAD_DIM), F32)
    zrows = jnp.zeros((8 - steps, HEAD_DIM), F32)
    heads = range(B_V_HEADS)

    def body(bi, carry):
        kqs = [_mm(kq_ref[bi, h], s0_ref[bi, h]) for h in heads]
        pending = []
        for h in heads:
            kq = kq_ref[bi, h]
            g = g_ref[bi, h]
            beta = beta_ref[bi, h]
            v = v_ref[bi, h]
            gc = [g[0:1]]
            for t in range(1, steps):
                gc.append(gc[-1] + g[t:t + 1])
            k = [kq[t:t + 1] for t in range(steps)]
            q = [kq[4 + t:5 + t] for t in range(steps)]
            d = []
            for t in range(steps):
                acc = v[t:t + 1] - jnp.exp(gc[t]) * kqs[h][t:t + 1]
                for j in range(t):
                    kk = jnp.sum(k[j] * k[t], -1, keepdims=True)
                    acc = acc - jnp.exp(gc[t] - gc[j]) * kk * d[j]
                d.append(beta[t:t + 1] * acc)
            outs = []
            for t in range(steps):
                o = jnp.exp(gc[t]) * kqs[h][4 + t:5 + t]
                for j in range(t + 1):
                    qk = jnp.sum(k[j] * q[t], -1, keepdims=True)
                    o = o + jnp.exp(gc[t] - gc[j]) * qk * d[j]
                outs.append(o * lax.rsqrt(jnp.mean(o * o, -1, keepdims=True) + RMS_EPS) * og)
            o_ref[bi, h] = jnp.concatenate(outs + [zrows], axis=0) * jax.nn.silu(z_ref[bi, h])
            k_dec = jnp.concatenate([jnp.exp(gc[-1] - gc[j]) * k[j] for j in range(steps)] + [zrows], axis=0)
            k_pad = jnp.concatenate([k_dec, zpad], axis=0)
            d_pad = jnp.concatenate(d + [zrows, zpad], axis=0)
            pending.append((k_pad.T, d_pad, jnp.exp(gc[-1])))
        for h, (k_t, d_pad, decay_last) in zip(heads, pending):
            s_out_ref[bi, h] = s0_ref[bi, h] * decay_last + _mm(k_t, d_pad)
        return carry

    lax.fori_loop(0, pairs // B_V_HEADS, body, 0)


def _mixb_sample_rec(kq, v, z, beta, g, s0, og, steps, bb):
    nb = kq.shape[0]
    tile = pl.BlockSpec((bb, B_V_HEADS, 8, HEAD_DIM), lambda i: (i, 0, 0, 0))
    st = pl.BlockSpec((bb, B_V_HEADS, HEAD_DIM, HEAD_DIM), lambda i: (i, 0, 0, 0))
    kern = functools.partial(_mixb_sample_rec_kernel, steps=steps, pairs=bb * B_V_HEADS)
    return pl.pallas_call(
        kern,
        grid=(nb // bb,),
        in_specs=[tile, tile, tile, tile, tile, st, pl.BlockSpec((1, HEAD_DIM), lambda i: (0, 0))],
        out_specs=[tile, st],
        out_shape=[jax.ShapeDtypeStruct((nb, B_V_HEADS, 8, HEAD_DIM), F32),
                   jax.ShapeDtypeStruct((nb, B_V_HEADS, HEAD_DIM, HEAD_DIM), F32)],
        compiler_params=_params("arbitrary"),
        name="mixb_sample_rec",
    )(kq, v, z, beta, g, s0, og)


def _outproj_kernel(ap_ref, bp_ref, xp_ref, g1p_ref, sc2p_ref, sh2p_ref,
                    as_ref, bs_ref, xs_ref, g1s_ref, sc2s_ref, sh2s_ref,
                    w_ref, lg_ref, lb_ref, wr_ref, rb_ref, x1_ref, h2_ref, route_ref, count_ref,
                    base_ref, mix_ref, *, prompt_tiles):
    s = pl.program_id(0)

    @pl.when(s == 0)
    def _():
        base_ref[...] = jnp.zeros_like(base_ref)
        mix_ref[...] = jnp.zeros_like(mix_ref)

    cur_prompt = s < prompt_tiles
    a = jnp.where(cur_prompt, ap_ref[...], as_ref[...]).astype(BF16)
    mix_a = jnp.dot(a, w_ref[:A_WIDTH, :], preferred_element_type=F32)

    def project():
        b = jnp.where(cur_prompt, bp_ref[...], bs_ref[...]).astype(BF16)
        return mix_a + jnp.dot(b, w_ref[A_WIDTH:, :], preferred_element_type=F32)

    prev_prompt = s - 1 < prompt_tiles
    pick = lambda p_ref, s_ref: jnp.where(prev_prompt, p_ref[...], s_ref[...])
    mix = mix_ref[(s + 1) % 2]
    x1 = _layer_norm_rows(ALPHA * pick(xp_ref, xs_ref) + pick(g1p_ref, g1s_ref) * mix, lg_ref[...], lb_ref[...])
    x1_ref[...] = x1
    h2 = x1 * (1.0 + pick(sc2p_ref, sc2s_ref)) + pick(sh2p_ref, sh2s_ref)
    h2_ref[...] = _pack_bf16_pairs(h2)
    route_ref[...], mix_new = _route_tile(_mm3(h2, wr_ref[...]), rb_ref[...], base_ref, s >= 1, project)
    count_ref[...] = base_ref[...]
    mix_ref[s % 2] = mix_new


def _outproj(prompt, sample, mod_p, mod_s, w_out, lg, lb, w_router, r_bias, tm, seq):
    n_p = prompt[2].shape[0]
    n_s = sample[2].shape[0]
    pt = n_p // tm
    st = n_s // tm
    nt = pt + st
    m = n_p + n_s
    tiles_per_seq = seq // tm
    cur = lambda s: jnp.minimum(s, nt - 1)
    prev = lambda s: jnp.maximum(s - 1, 0)
    p_idx = lambda t: jnp.minimum(t, pt - 1)
    s_idx = lambda t: jnp.clip(t - pt, 0, st - 1)
    p_cur = lambda w: pl.BlockSpec((tm, w), lambda s: (p_idx(cur(s)), 0))
    s_cur = lambda w: pl.BlockSpec((tm, w), lambda s: (s_idx(cur(s)), 0))
    p_prev = lambda w: pl.BlockSpec((tm, w), lambda s: (p_idx(prev(s)), 0))
    s_prev = lambda w: pl.BlockSpec((tm, w), lambda s: (s_idx(prev(s)), 0))
    p_mod = lambda col: pl.BlockSpec((None, 1, D_MODEL), lambda s: (p_idx(prev(s)) // tiles_per_seq, 0, col))
    s_mod = lambda col: pl.BlockSpec((tm, D_MODEL), lambda s: (0, col))
    row = lambda w: pl.BlockSpec((tm, w), lambda s: (prev(s), 0))
    full = lambda shape: pl.BlockSpec(shape, lambda s: (0, 0))
    return pl.pallas_call(
        functools.partial(_outproj_kernel, prompt_tiles=pt),
        grid=(nt + 1,),
        in_specs=[p_cur(A_WIDTH), p_cur(B_VAL_WIDTH), p_prev(D_MODEL), p_mod(2), p_mod(4), p_mod(3),
                  s_cur(A_WIDTH), s_cur(B_VAL_WIDTH), s_prev(D_MODEL), s_mod(2), s_mod(4), s_mod(3),
                  full((D_MODEL, D_MODEL)), full((1, D_MODEL)), full((1, D_MODEL)),
                  full((D_MODEL, LANES)), full((1, LANES))],
        out_specs=[row(D_MODEL), row(D_MODEL // 2), row(LANES),
                   full((1, LANES))],
        out_shape=[jax.ShapeDtypeStruct((m, D_MODEL), F32),
                   jax.ShapeDtypeStruct((m, D_MODEL // 2), jnp.uint32),
                   jax.ShapeDtypeStruct((m, LANES), F32),
                   jax.ShapeDtypeStruct((1, LANES), F32)],
        scratch_shapes=[pltpu.VMEM((1, LANES), F32),
                        pltpu.VMEM((2, tm, D_MODEL), F32)],
        compiler_params=_params("arbitrary"),
        name="outproj",
    )(*prompt, mod_p, mod_p, mod_p, *sample, mod_s, mod_s, mod_s, w_out, lg, lb, w_router, r_bias)


def _route_tile(lg, bias, base_ref, valid, between):
    tm = lg.shape[0]
    lane = lax.broadcasted_iota(jnp.int32, lg.shape, 1)
    neg = -jnp.inf

    def first_argmax(score):
        mx = jnp.max(score, -1, keepdims=True)
        return jnp.min(jnp.where(score == mx, lane, LANES), -1, keepdims=True)

    def pick(vals, idx):
        return jnp.sum(jnp.where(lane == idx, vals, 0.0), -1, keepdims=True)

    gmask = lane < N_GROUPS
    mg = jnp.max(jnp.where(gmask, lg, neg), -1, keepdims=True)
    eg = jnp.where(gmask, jnp.exp(jnp.where(gmask, lg - mg, 0.0)), 0.0)
    pg = eg / jnp.sum(eg, -1, keepdims=True)
    sel_g = first_argmax(jnp.where(gmask, lg + bias, neg))
    p_sel = pick(pg, sel_g)

    lo = N_GROUPS + sel_g * EXPERTS_PER_GROUP
    emask = (lane >= lo) & (lane < lo + EXPERTS_PER_GROUP)
    me = jnp.max(jnp.where(emask, lg, neg), -1, keepdims=True)
    ee = jnp.where(emask, jnp.exp(jnp.where(emask, lg - me, 0.0)), 0.0)
    pe = ee / jnp.sum(ee, -1, keepdims=True)
    score = jnp.where(emask, pe + bias, neg)
    i1 = first_argmax(score)
    i2 = first_argmax(jnp.where(lane == i1, neg, score))
    w1 = pick(pe, i1)
    w2 = pick(pe, i2)
    wsum = w1 + w2
    gate1 = w1 / wsum * p_sel
    gate2 = w2 / wsum * p_sel

    hot = ((lane == i1) | (lane == i2)).astype(BF16)
    extra = between()
    ri = lax.broadcasted_iota(jnp.int32, (tm, tm), 0)
    ci = lax.broadcasted_iota(jnp.int32, (tm, tm), 1)
    before = jnp.dot((ri > ci).astype(BF16), hot, preferred_element_type=F32) + base_ref[...]
    rank1 = pick(before, i1)
    rank2 = pick(before, i2)
    base_ref[...] = jnp.where(valid, base_ref[...] + jnp.sum(hot.astype(F32), 0, keepdims=True), base_ref[...])

    out = jnp.where(lane == 0, (i1 - N_GROUPS).astype(F32), 0.0)
    out = jnp.where(lane == 1, (i2 - N_GROUPS).astype(F32), out)
    out = jnp.where(lane == 2, gate1, out)
    out = jnp.where(lane == 3, gate2, out)
    out = jnp.where(lane == 4, rank1, out)
    out = jnp.where(lane == 5, rank2, out)
    return out, extra


DISPATCH_ROWS = 512


def _dispatch_kernel(dest_ref, last_ref, nv_ref, h_ref, xb_ref, zbuf, zsem, ssem, *, n_tok, n_blocks):
    i = pl.program_id(0)
    nv = nv_ref[0]

    def zero_block(blk):
        return pltpu.make_async_copy(zbuf, xb_ref.at[pl.ds(blk * MOE_ROWS, MOE_ROWS)], zsem)

    @pl.when(i == 0)
    def _():
        zbuf[...] = jnp.zeros_like(zbuf)
        for e in range(N_EXPERTS):
            @pl.when(last_ref[e] >= 0)
            def _():
                zero_block(last_ref[e]).start()

        def tail_start(b, carry):
            zero_block(b).start()
            return carry

        lax.fori_loop(nv, n_blocks, tail_start, 0)
        for e in range(N_EXPERTS):
            @pl.when(last_ref[e] >= 0)
            def _():
                zero_block(0).wait()

        def tail_wait(b, carry):
            zero_block(0).wait()
            return carry

        lax.fori_loop(nv, n_blocks, tail_wait, 0)

    def row_copy(r, choice):
        slot = dest_ref[choice * n_tok + i * DISPATCH_ROWS + r]
        return pltpu.make_async_copy(h_ref.at[pl.ds(r, 1)], xb_ref.at[pl.ds(slot, 1)], ssem)

    def body(r, carry):
        row_copy(r, 0).start()
        row_copy(r, 1).start()
        return carry

    lax.fori_loop(0, DISPATCH_ROWS, body, 0, unroll=8)
    for _ in range(2):
        pltpu.make_async_copy(h_ref, xb_ref.at[pl.ds(0, DISPATCH_ROWS)], ssem).wait()


def _dispatch(dest_flat, last_block, n_valid, h2, n_blocks):
    n_tok = h2.shape[0]
    kern = functools.partial(_dispatch_kernel, n_tok=n_tok, n_blocks=n_blocks)
    return pl.pallas_call(
        kern,
        grid_spec=pltpu.PrefetchScalarGridSpec(
            num_scalar_prefetch=3,
            grid=(n_tok // DISPATCH_ROWS,),
            in_specs=[pl.BlockSpec((DISPATCH_ROWS, D_MODEL // 2), lambda i, *_: (i, 0))],
            out_specs=pl.BlockSpec(memory_space=pl.ANY),
            scratch_shapes=[pltpu.VMEM((MOE_ROWS, D_MODEL // 2), jnp.uint32),
                            pltpu.SemaphoreType.DMA(()),
                            pltpu.SemaphoreType.DMA(())]),
        out_shape=jax.ShapeDtypeStruct((n_blocks * MOE_ROWS, D_MODEL // 2), jnp.uint32),
        compiler_params=_params("arbitrary"),
        name="dispatch",
    )(dest_flat, last_block, n_valid, h2)


def _expert_kernel(be_ref, nv_ref, first_ref, next_ref, wslot_ref,
                   x_ref, wg_ref, wu_ref, wd_ref, y_ref,
                   wg_st, wu_st, wd_st, wg_s, wu_s, wd_s, wsem):
    i = pl.program_id(0)
    nv = nv_ref[0]

    def weight_copies(e, p):
        return [pltpu.make_async_copy(src.at[e], dst.at[p], wsem.at[p])
                for src, dst in ((wg_ref, wg_st), (wu_ref, wu_st), (wd_ref, wd_st))]

    @pl.when(i == 0)
    def _():
        for c in weight_copies(be_ref[0], 0):
            c.start()

    @pl.when((i < nv) & (first_ref[i] == 1))
    def _():
        p = wslot_ref[i]
        for c in weight_copies(be_ref[i], p):
            c.wait()
        wg_s[...] = wg_st[p].astype(BF16)
        wu_s[...] = wu_st[p].astype(BF16)
        wd_s[...] = wd_st[p].astype(BF16)

        @pl.when(next_ref[i] >= 0)
        def _():
            for c in weight_copies(next_ref[i], 1 - p):
                c.start()

    @pl.when(i < nv)
    def _():
        x = _unpack_bf16_pairs(x_ref[...])
        hg = jnp.dot(x, wg_s[...], preferred_element_type=F32)
        hu = jnp.dot(x, wu_s[...], preferred_element_type=F32)
        hid = (jax.nn.silu(hg) * hu).astype(BF16)
        y_ref[...] = jnp.dot(hid, wd_s[...], preferred_element_type=F32)

    @pl.when(i >= nv)
    def _():
        y_ref[...] = jnp.zeros_like(y_ref)


def _experts(block_e, n_valid, xb, w_gate, w_up, w_down):
    n_blocks = block_e.shape[0]
    idx = jnp.arange(n_blocks, dtype=jnp.int32)
    first = (idx < n_valid[0]) & ((idx == 0) | (block_e != jnp.roll(block_e, 1)))
    wslot = (jnp.cumsum(first.astype(jnp.int32)) - 1) % 2
    first_at = jnp.where(first, idx, n_blocks)
    next_first = jnp.concatenate([lax.cummin(first_at, reverse=True)[1:], jnp.full((1,), n_blocks, jnp.int32)])
    next_e = jnp.where(next_first < n_blocks, block_e[jnp.minimum(next_first, n_blocks - 1)], -1)
    any_spec = pl.BlockSpec(memory_space=pl.ANY)
    rows = pl.BlockSpec((MOE_ROWS, D_MODEL), lambda i, *_: (i, 0))
    return pl.pallas_call(
        _expert_kernel,
        grid_spec=pltpu.PrefetchScalarGridSpec(
            num_scalar_prefetch=5,
            grid=(n_blocks,),
            in_specs=[pl.BlockSpec((MOE_ROWS, D_MODEL // 2), lambda i, *_: (i, 0)), any_spec, any_spec, any_spec],
            out_specs=rows,
            scratch_shapes=[pltpu.VMEM((2, D_MODEL, D_EXPERT), F32),
                            pltpu.VMEM((2, D_MODEL, D_EXPERT), F32),
                            pltpu.VMEM((2, D_EXPERT, D_MODEL), F32),
                            pltpu.VMEM((D_MODEL, D_EXPERT), BF16),
                            pltpu.VMEM((D_MODEL, D_EXPERT), BF16),
                            pltpu.VMEM((D_EXPERT, D_MODEL), BF16),
                            pltpu.SemaphoreType.DMA((2,))]),
        out_shape=jax.ShapeDtypeStruct((n_blocks * MOE_ROWS, D_MODEL), F32),
        compiler_params=_params("arbitrary"),
        name="experts",
    )(block_e, n_valid, first.astype(jnp.int32), next_e.astype(jnp.int32), wslot.astype(jnp.int32),
      xb, w_gate, w_up, w_down)


def _final_kernel(dest_ref, x1_ref, route_ref, g2_ref, lg_ref, lb_ref, yb_ref, o_ref, ybuf, sem,
                  *, row0, n_tok):
    i = pl.program_id(0)
    last = pl.num_programs(0) - 1
    tm = o_ref.shape[0]
    slot = i % 2

    def row_copy(tile, r, choice, s):
        src = dest_ref[choice * n_tok + row0 + tile * tm + r]
        return pltpu.make_async_copy(yb_ref.at[pl.ds(src, 1)], ybuf.at[s, choice, pl.ds(r, 1)], sem.at[s])

    def tile_wait(s):
        for choice in range(2):
            pltpu.make_async_copy(yb_ref.at[pl.ds(0, tm)], ybuf.at[s, choice], sem.at[s]).wait()

    @pl.when(i == 0)
    def _():
        def body(r, carry):
            row_copy(0, r, 0, 0).start()
            row_copy(0, r, 1, 0).start()
            return carry

        lax.fori_loop(0, tm, body, 0, unroll=8)

    tile_wait(slot)
    nxt = jnp.minimum(i + 1, last)
    for r in range(tm):
        row_copy(nxt, r, 0, 1 - slot).start()
        row_copy(nxt, r, 1, 1 - slot).start()
    route = route_ref[...]
    ff = ybuf[slot, 0] * route[:, 2:3] + ybuf[slot, 1] * route[:, 3:4]
    o_ref[...] = _layer_norm_rows(ALPHA * x1_ref[...] + g2_ref[...] * ff, lg_ref[...], lb_ref[...])

    @pl.when(i == last)
    def _():
        tile_wait(1 - slot)


def _final(dest_flat, x1, yb, route, mod, lg, lb, tm, rows_per_mod, row0, n_rows):
    n_tok = x1.shape[0]
    off = row0 // tm
    return pl.pallas_call(
        functools.partial(_final_kernel, row0=row0, n_tok=n_tok),
        grid_spec=pltpu.PrefetchScalarGridSpec(
            num_scalar_prefetch=1,
            grid=(n_rows // tm,),
            in_specs=[pl.BlockSpec((tm, D_MODEL), lambda i, *_: (off + i, 0)),
                      pl.BlockSpec((tm, LANES), lambda i, *_: (off + i, 0)),
                      _mod_spec(mod, 5, tm, rows_per_mod),
                      pl.BlockSpec((1, D_MODEL), lambda i, *_: (0, 0)),
                      pl.BlockSpec((1, D_MODEL), lambda i, *_: (0, 0)),
                      pl.BlockSpec(memory_space=pl.ANY)],
            out_specs=pl.BlockSpec((tm, D_MODEL), lambda i, *_: (i, 0)),
            scratch_shapes=[pltpu.VMEM((2, 2, tm, D_MODEL), F32),
                            pltpu.SemaphoreType.DMA((2,))]),
        out_shape=jax.ShapeDtypeStruct((n_rows, D_MODEL), F32),
        compiler_params=_params("arbitrary"),
        name="final",
    )(dest_flat, x1, route, mod, lg, lb, yb)


def _lane_row(vec, offset):
    return jnp.zeros((1, LANES), F32).at[0, offset:offset + vec.shape[0]].set(vec.astype(F32))


def kernel(x_prompt, x_sample, state_conv, state_ssm, c_prompt, c_sample, w_ada, b_ada, w_in, a_ws, a_bs, a_norm_g, a_norm_b, b_conv_w, b_a_log, b_dt_bias, b_onorm_g, w_out, ln1_g, ln1_b, w_router_g, b_router_g, w_router_e, b_router_e, w_gate, w_up, w_down, ln2_g, ln2_b):
    batch, seq, d = x_prompt.shape
    nb, steps, _ = x_sample.shape
    n_p = batch * seq
    n_s = nb * steps
    n_tok = n_p + n_s
    l = 0

    c_rows = batch + nb
    c_pad = (-c_rows) % 8
    c_all = jnp.concatenate([c_prompt, c_sample, jnp.zeros((c_pad, d), F32)], axis=0)
    mod = _ada(c_all, w_ada[l], b_ada[l].reshape(1, -1))
    mod_p = mod[:batch].reshape(batch, 1, 6 * d)
    mod_s = mod[batch:batch + nb]

    w_in_t = jnp.swapaxes(w_in[l], 0, 1)
    w_bd = jnp.pad(w_in_t[MAIN_WIDTH:], ((0, LANES - 2 * B_V_HEADS), (0, 0))).astype(BF16)
    xp = x_prompt.reshape(n_p, d)
    xs = jnp.swapaxes(x_sample, 0, 1).reshape(n_s, d)
    proj_p, bd_p, proj_s, bd_s = _inproj(xp, xs, mod_p, mod_s, w_in_t, w_bd, 512, seq)

    ng = a_norm_g[l].reshape(1, -1)
    nbias = a_norm_b[l].reshape(1, -1)
    bias_tile = jnp.repeat(a_bs[l].T, HEAD_DIM, axis=1)
    a_out_p = _mixa_prompt(proj_p, a_ws[l], bias_tile, ng, nbias, 256)
    coef = jnp.repeat(jnp.transpose(a_ws[l][:, :steps, :steps], (1, 2, 0)).reshape(steps * steps, A_HEADS),
                      HEAD_DIM, axis=1)
    a_out_s, chunkv = _mixa_sample(proj_s, coef, bias_tile[:steps], ng, nbias, steps, nb)

    nega = _lane_row(-jnp.exp(b_a_log[l].astype(F32)), B_V_HEADS)
    dtb = _lane_row(b_dt_bias[l], B_V_HEADS)
    og = b_onorm_g[l].reshape(1, -1)
    b_out_p, ssm_p = _mixb_prompt(proj_p, bd_p, b_conv_w[l], nega, dtb, og, batch, seq)
    qkv0 = 2 * A_WIDTH
    conv_p = proj_p.reshape(batch, seq, MAIN_WIDTH)[:, seq - (B_CONV - 1):, qkv0:qkv0 + B_CONV_CH]

    buf_s = jnp.swapaxes(state_conv[l], 0, 1).reshape((B_CONV - 1) * nb, B_CONV_CH)
    act_s, beta_s, g_s = _mixb_sample_pre(proj_s, buf_s, bd_s, b_conv_w[l], nega, dtb, steps, nb)
    rep = B_V_HEADS // B_QK_HEADS
    act4 = act_s.reshape(steps, nb, B_CONV_CH)
    q8 = jnp.repeat(act4[..., :B_KEY_WIDTH].reshape(steps, nb, B_QK_HEADS, HEAD_DIM), rep, axis=2)
    k8 = jnp.repeat(act4[..., B_KEY_WIDTH:2 * B_KEY_WIDTH].reshape(steps, nb, B_QK_HEADS, HEAD_DIM), rep, axis=2)

    def to_tiles(a):
        a = jnp.transpose(a, (1, 2, 0, 3))
        return jnp.pad(a, ((0, 0), (0, 0), (0, 8 - a.shape[2]), (0, 0)))

    kq_t = jnp.concatenate([to_tiles(k8)[:, :, :4], to_tiles(q8)[:, :, :4]], axis=2)
    v_t = to_tiles(act4[..., 2 * B_KEY_WIDTH:].reshape(steps, nb, B_V_HEADS, HEAD_DIM))
    z_t = to_tiles(proj_s[:, qkv0 + B_CONV_CH:].reshape(steps, nb, B_V_HEADS, HEAD_DIM))

    def gate_tiles(a, off):
        a = a[:, off:off + B_V_HEADS].reshape(steps, nb, B_V_HEADS, 1)
        return to_tiles(jnp.broadcast_to(a, (steps, nb, B_V_HEADS, HEAD_DIM)))

    o_t, ssm_s = _mixb_sample_rec(kq_t, v_t, z_t, gate_tiles(beta_s, 0), gate_tiles(g_s, B_V_HEADS),
                                  state_ssm[l], og, steps, 8)
    b_out_s = jnp.transpose(o_t[:, :, :steps], (2, 0, 1, 3)).reshape(n_s, B_VAL_WIDTH)
    conv_s = jnp.swapaxes(proj_s.reshape(steps, nb, MAIN_WIDTH)[steps - (B_CONV - 1):, :, qkv0:qkv0 + B_CONV_CH], 0, 1)

    w_out_b = w_out[l].astype(BF16)
    lg1 = ln1_g[l].reshape(1, -1)
    lb1 = ln1_b[l].reshape(1, -1)
    w_router = jnp.concatenate([w_router_g[l], w_router_e[l],
                                jnp.zeros((d, LANES - N_GROUPS - N_EXPERTS), F32)], axis=1)
    out_tm = 256
    mod_s_tile = jnp.tile(mod_s, (out_tm // nb, 1))
    r_bias = jnp.concatenate([b_router_g[l], b_router_e[l],
                              jnp.zeros((LANES - N_GROUPS - N_EXPERTS,), F32)]).reshape(1, LANES)
    x1, h2, route, counts_row = _outproj((a_out_p, b_out_p, xp), (a_out_s, b_out_s, xs), mod_p, mod_s_tile,
                                         w_out_b, lg1, lb1, w_router, r_bias, out_tm, seq)

    counts = counts_row[0, N_GROUPS:N_GROUPS + N_EXPERTS].astype(jnp.int32)
    route_t = route[:, :8].T
    eid = route_t[0:2].astype(jnp.int32)
    rank = route_t[4:6].astype(jnp.int32)
    padded = (counts + MOE_ROWS - 1) // MOE_ROWS * MOE_ROWS
    pend = jnp.cumsum(padded)
    pstart = pend - padded
    dest = rank + jnp.sum(jnp.where(eid[None] == jnp.arange(N_EXPERTS, dtype=jnp.int32)[:, None, None],
                                    pstart[:, None, None], 0), axis=0)
    n_blocks = -(-(2 * n_tok) // MOE_ROWS) + N_EXPERTS
    dest_flat = dest.reshape(-1)
    last_block = jnp.where(padded > 0, pend // MOE_ROWS - 1, -1).astype(jnp.int32)
    blk0 = jnp.arange(n_blocks, dtype=jnp.int32) * MOE_ROWS
    block_e = jnp.minimum(jnp.sum(pend[None, :] <= blk0[:, None], axis=1), N_EXPERTS - 1).astype(jnp.int32)
    n_valid = (pend[-1:] // MOE_ROWS).astype(jnp.int32)

    xb = _dispatch(dest_flat, last_block, n_valid, h2, n_blocks)
    yb = _experts(block_e, n_valid, xb, w_gate[l], w_up[l], w_down[l])

    lg2 = ln2_g[l].reshape(1, -1)
    lb2 = ln2_b[l].reshape(1, -1)
    y_p = _final(dest_flat, x1, yb, route, mod_p, lg2, lb2, 512, seq, 0, n_p)
    y_s = _final(dest_flat, x1, yb, route, mod_s, lg2, lb2, nb, 0, n_p, n_s)

    y_prompt = y_p.reshape(batch, seq, d)
    y_sample = jnp.swapaxes(y_s.reshape(steps, nb, d), 0, 1)
    chunkv_s = jnp.swapaxes(chunkv.reshape(steps, nb, A_HEADS, HEAD_DIM), 0, 1)
    return (y_prompt, y_sample, conv_p[None], ssm_p[None], conv_s[None], ssm_s[None], chunkv_s[None])
```

```python
import functools

import jax
import jax.numpy as jnp
from jax import lax
from jax.experimental import pallas as pl
from jax.experimental.pallas import tpu as pltpu

F32 = jnp.float32
BF16 = jnp.bfloat16

D_MODEL = 2048
DEPTH = 1
A_HEADS = 8
HEAD_DIM = 128
A_WIDTH = 1024
A_CHUNK = 128
B_QK_HEADS = 4
B_V_HEADS = 8
B_KEY_WIDTH = 512
B_VAL_WIDTH = 1024
B_CONV = 4
B_CONV_CH = 2048
DN_CHUNK = 64
MAIN_WIDTH = 2 * A_WIDTH + B_CONV_CH + B_VAL_WIDTH
N_GROUPS = 4
EXPERTS_PER_GROUP = 8
N_EXPERTS = 32
D_EXPERT = 512
ALPHA = (2 * DEPTH) ** 0.25
LN_EPS = 1e-5
RMS_EPS = 1e-6
L2_EPS = 1e-6

LANES = 128
VMEM_LIMIT = 56 * 1024 * 1024
MOE_ROWS = 256
MIXB_GROUP = 4


def _params(*sem):
    return pltpu.CompilerParams(dimension_semantics=sem, vmem_limit_bytes=VMEM_LIMIT)


def _mm(a, b):
    return jnp.dot(a.astype(BF16), b.astype(BF16), preferred_element_type=F32)


def _mm_nt(a, b):
    return lax.dot_general(a.astype(BF16), b.astype(BF16), (((1,), (1,)), ((), ())),
                           preferred_element_type=F32)


def _mm_tn(a, b):
    return lax.dot_general(a.astype(BF16), b.astype(BF16), (((0,), (0,)), ((), ())),
                           preferred_element_type=F32)


def _split(x):
    hi = x.astype(BF16)
    lo = (x - hi.astype(F32)).astype(BF16)
    return hi, lo


def _mm_exact_lhs(a_bf16, b):
    hi, lo = _split(b)
    return (jnp.dot(a_bf16, hi, preferred_element_type=F32)
            + jnp.dot(a_bf16, lo, preferred_element_type=F32))


def _mm3(a, b):
    ah, al = _split(a)
    bh, bl = _split(b)
    return (jnp.dot(ah, bh, preferred_element_type=F32) + jnp.dot(ah, bl, preferred_element_type=F32)
            + jnp.dot(al, bh, preferred_element_type=F32))


def _pack_bf16_pairs(x):
    half = x.shape[1] // 2
    hi = lax.bitcast_convert_type(x[:, :half].astype(BF16).astype(F32), jnp.uint32)
    lo = lax.bitcast_convert_type(x[:, half:].astype(BF16).astype(F32), jnp.uint32)
    return (hi & jnp.uint32(0xFFFF0000)) | lax.shift_right_logical(lo, jnp.uint32(16))


def _unpack_bf16_pairs(u):
    hi = lax.bitcast_convert_type(u & jnp.uint32(0xFFFF0000), F32)
    lo = lax.bitcast_convert_type(lax.shift_left(u, jnp.uint32(16)), F32)
    return jnp.concatenate([hi, lo], axis=1).astype(BF16)


def _softplus(x):
    return jnp.maximum(x, 0.0) + jnp.log1p(jnp.exp(-jnp.abs(x)))


def _layer_norm_rows(x, g, b):
    mu = jnp.mean(x, -1, keepdims=True)
    xc = x - mu
    var = jnp.mean(xc * xc, -1, keepdims=True)
    return xc * lax.rsqrt(var + LN_EPS) * g + b


def _ada_kernel(c_ref, w_ref, b_ref, o_ref):
    a = jax.nn.silu(c_ref[...]).astype(BF16)
    o_ref[...] = jnp.dot(a, w_ref[...].astype(BF16), preferred_element_type=F32) + b_ref[...]


def _ada(c_all, w_ada, b_ada):
    rows = c_all.shape[0]
    tn = 1024
    return pl.pallas_call(
        _ada_kernel,
        grid=(6 * D_MODEL // tn,),
        in_specs=[pl.BlockSpec((rows, D_MODEL), lambda j: (0, 0)),
                  pl.BlockSpec((D_MODEL, tn), lambda j: (0, j)),
                  pl.BlockSpec((1, tn), lambda j: (0, j))],
        out_specs=pl.BlockSpec((rows, tn), lambda j: (0, j)),
        out_shape=jax.ShapeDtypeStruct((rows, 6 * D_MODEL), F32),
        compiler_params=_params("arbitrary"),
        name="ada",
    )(c_all, w_ada, b_ada)


def _inproj_kernel(xp_ref, scp_ref, shp_ref, xs_ref, scs_ref, shs_ref, w_ref, wbd_ref,
                   op_ref, bdp_ref, os_ref, bds_ref, wb_ref, *, prompt_tiles):
    j = pl.program_id(0)
    i = pl.program_id(1)

    @pl.when(i == 0)
    def _():
        wb_ref[...] = w_ref[...].astype(BF16)

    def project(h, o_ref, bd_ref):
        o_ref[...] = _mm_nt(h, wb_ref[...])

        @pl.when(j == 0)
        def _():
            bd_ref[0] = _mm_nt(h, wbd_ref[...])

        @pl.when(j != 0)
        def _():
            bd_ref[0] = jnp.zeros(bd_ref.shape[1:], F32)

    @pl.when(i < prompt_tiles)
    def _():
        project((xp_ref[...] * (1.0 + scp_ref[...]) + shp_ref[...]).astype(BF16), op_ref, bdp_ref)

    @pl.when(i >= prompt_tiles)
    def _():
        n_s, nb = xs_ref.shape[0], scs_ref.shape[0]
        x = xs_ref[...].reshape(n_s // nb, nb, D_MODEL)
        h = x * (1.0 + scs_ref[...])[None] + shs_ref[...][None]
        project(h.reshape(n_s, D_MODEL).astype(BF16), os_ref, bds_ref)


def _mod_spec(mod, col, tm, rows_per_mod, axis=0):
    if rows_per_mod:
        tiles = rows_per_mod // tm
        return pl.BlockSpec((None, 1, D_MODEL), lambda *g: (g[axis] // tiles, 0, col))
    return pl.BlockSpec((tm, D_MODEL), lambda *g: (0, col))


def _inproj(xp, xs, mod_p, mod_s, w_in_t, w_bd, tm, seq):
    n_p, n_s = xp.shape[0], xs.shape[0]
    nb = mod_s.shape[0]
    tn = 1024
    nj = MAIN_WIDTH // tn
    pt = n_p // tm
    tiles_per_seq = seq // tm
    pi = lambda i: jnp.minimum(i, pt - 1)
    p_mod = lambda col: pl.BlockSpec((None, 1, D_MODEL), lambda j, i: (pi(i) // tiles_per_seq, 0, col))
    s_mod = lambda col: pl.BlockSpec((nb, D_MODEL), lambda j, i: (0, col))
    proj_p, bd_p, proj_s, bd_s = pl.pallas_call(
        functools.partial(_inproj_kernel, prompt_tiles=pt),
        grid=(nj, pt + 1),
        in_specs=[pl.BlockSpec((tm, D_MODEL), lambda j, i: (pi(i), 0)), p_mod(1), p_mod(0),
                  pl.BlockSpec((n_s, D_MODEL), lambda j, i: (0, 0)), s_mod(1), s_mod(0),
                  pl.BlockSpec((tn, D_MODEL), lambda j, i: (j, 0)),
                  pl.BlockSpec((LANES, D_MODEL), lambda j, i: (0, 0))],
        out_specs=[pl.BlockSpec((tm, tn), lambda j, i: (pi(i), j)),
                   pl.BlockSpec((1, tm, LANES), lambda j, i: (j, pi(i), 0)),
                   pl.BlockSpec((n_s, tn), lambda j, i: (0, j)),
                   pl.BlockSpec((1, n_s, LANES), lambda j, i: (j, 0, 0))],
        out_shape=[jax.ShapeDtypeStruct((n_p, MAIN_WIDTH), F32),
                   jax.ShapeDtypeStruct((nj, n_p, LANES), F32),
                   jax.ShapeDtypeStruct((n_s, MAIN_WIDTH), F32),
                   jax.ShapeDtypeStruct((nj, n_s, LANES), F32)],
        scratch_shapes=[pltpu.VMEM((tn, D_MODEL), BF16)],
        compiler_params=_params("arbitrary", "arbitrary"),
        name="inproj",
    )(xp, mod_p, mod_p, xs, mod_s, mod_s, w_in_t, w_bd)
    return proj_p, bd_p[0], proj_s, bd_s[0]


def _mixa_prompt_kernel(p_ref, ws_ref, bias_ref, ng_ref, nb_ref, o_ref):
    rows = p_ref.shape[0]
    u = jax.nn.gelu(p_ref[:, :A_WIDTH])
    v = _layer_norm_rows(jax.nn.gelu(p_ref[:, A_WIDTH:]), ng_ref[...], nb_ref[...])
    ri = lax.broadcasted_iota(jnp.int32, (A_CHUNK, A_CHUNK), 0)
    ci = lax.broadcasted_iota(jnp.int32, (A_CHUNK, A_CHUNK), 1)
    for h in range(A_HEADS):
        w = jnp.where(ri >= ci, ws_ref[h], 0.0).astype(BF16)
        cols = slice(h * HEAD_DIM, (h + 1) * HEAD_DIM)
        for c in range(rows // A_CHUNK):
            rs = slice(c * A_CHUNK, (c + 1) * A_CHUNK)
            s = jnp.dot(w, v[rs, cols].astype(BF16), preferred_element_type=F32) + bias_ref[:, cols]
            o_ref[rs, cols] = u[rs, cols] * s


def _mixa_prompt(proj, a_ws, bias_tile, ng, nb, tm):
    m = proj.shape[0]
    return pl.pallas_call(
        _mixa_prompt_kernel,
        grid=(m // tm,),
        in_specs=[pl.BlockSpec((tm, 2 * A_WIDTH), lambda i: (i, 0)),
                  pl.BlockSpec((A_HEADS, A_CHUNK, A_CHUNK), lambda i: (0, 0, 0)),
                  pl.BlockSpec((A_CHUNK, A_WIDTH), lambda i: (0, 0)),
                  pl.BlockSpec((1, A_WIDTH), lambda i: (0, 0)),
                  pl.BlockSpec((1, A_WIDTH), lambda i: (0, 0))],
        out_specs=pl.BlockSpec((tm, A_WIDTH), lambda i: (i, 0)),
        out_shape=jax.ShapeDtypeStruct((m, A_WIDTH), F32),
        compiler_params=_params("arbitrary"),
        name="mixa_prompt",
    )(proj, a_ws, bias_tile, ng, nb)


def _mixa_sample_kernel(p_ref, coef_ref, bias_ref, ng_ref, nb_ref, o_ref, v_ref, *, steps, nb_rows):
    u = jax.nn.gelu(p_ref[:, :A_WIDTH])
    v = _layer_norm_rows(jax.nn.gelu(p_ref[:, A_WIDTH:]), ng_ref[...], nb_ref[...])
    v_ref[...] = v
    for t in range(steps):
        s = bias_ref[t:t + 1, :]
        for j in range(t + 1):
            s = s + coef_ref[t * steps + j:t * steps + j + 1, :] * v[j * nb_rows:(j + 1) * nb_rows, :]
        rs = slice(t * nb_rows, (t + 1) * nb_rows)
        o_ref[rs, :] = u[rs, :] * s


def _mixa_sample(proj, coef, bias, ng, nb, steps, nb_rows):
    m = proj.shape[0]
    kern = functools.partial(_mixa_sample_kernel, steps=steps, nb_rows=nb_rows)
    return pl.pallas_call(
        kern,
        grid=(1,),
        in_specs=[pl.BlockSpec((m, 2 * A_WIDTH), lambda i: (0, 0)),
                  pl.BlockSpec(coef.shape, lambda i: (0, 0)),
                  pl.BlockSpec(bias.shape, lambda i: (0, 0)),
                  pl.BlockSpec((1, A_WIDTH), lambda i: (0, 0)),
                  pl.BlockSpec((1, A_WIDTH), lambda i: (0, 0))],
        out_specs=[pl.BlockSpec((m, A_WIDTH), lambda i: (0, 0)),
                   pl.BlockSpec((m, A_WIDTH), lambda i: (0, 0))],
        out_shape=[jax.ShapeDtypeStruct((m, A_WIDTH), F32),
                   jax.ShapeDtypeStruct((m, A_WIDTH), F32)],
        compiler_params=_params("arbitrary"),
        name="mixa_sample",
    )(proj, coef, bias, ng, nb)


def _unit_lower_inverse_many(a_list, ri, ci, block):
    rows = a_list[0].shape[0]
    eye = (ri == ci).astype(F32)
    pair = (lax.shift_right_logical(ri, 1) == lax.shift_right_logical(ci, 1)) & ((ri & 1) == 1) & ((ci & 1) == 0)
    ts = [eye - jnp.where(pair, a, 0.0) for a in a_list]
    n = 2
    while n < block:
        sh = n.bit_length()
        m = ((lax.shift_right_logical(ri, sh) == lax.shift_right_logical(ci, sh))
             & ((ri & n) != 0) & ((ci & n) == 0))
        ans = [jnp.where(m, a, 0.0) for a in a_list]
        if n % 8:
            xs = [_mm(t, an) for t, an in zip(ts, ans)]
            ts = [t - _mm(x, t) for t, x in zip(ts, xs)]
        else:
            lower = [slice(r0 + n, r0 + 2 * n) for r0 in range(0, rows, 2 * n)]
            pick = lambda t: jnp.concatenate([t[sl] for sl in lower], axis=0)
            xs = [_mm(pick(t), an) for t, an in zip(ts, ans)]
            upd = [pick(t) - _mm(x, t) for t, x in zip(ts, xs)]
            ts = [jnp.concatenate([piece for k, r0 in enumerate(range(0, rows, 2 * n))
                                   for piece in (t[r0:r0 + n], u[k * n:(k + 1) * n])], axis=0)
                  for t, u in zip(ts, upd)]
        n *= 2
    return ts


def _mixb_prompt_kernel(qkv_ref, z_ref, bd_ref, cw_ref, nega_ref, dtb_ref, og_ref,
                        o_ref, sfin_ref, s_ref, cbuf_ref):
    c = pl.program_id(0)
    C = DN_CHUNK
    nseq = qkv_ref.shape[0]

    @pl.when(c == 0)
    def _init():
        s_ref[...] = jnp.zeros_like(s_ref)
        cbuf_ref[:, 0:8, :] = jnp.zeros((nseq, 8, B_CONV_CH), F32)

    cw = cw_ref[...]
    og = og_ref[...]
    ri64 = lax.broadcasted_iota(jnp.int32, (C, C), 0)
    ci64 = lax.broadcasted_iota(jnp.int32, (C, C), 1)
    cum_lhs = (ri64 >= ci64).astype(BF16)

    R = MIXB_GROUP * C
    ri = lax.broadcasted_iota(jnp.int32, (R, R), 0)
    ci = lax.broadcasted_iota(jnp.int32, (R, R), 1)
    same = lax.shift_right_logical(ri, C.bit_length() - 1) == lax.shift_right_logical(ci, C.bit_length() - 1)
    tril = same & (ri >= ci)
    strict = same & (ri > ci)
    rep = B_V_HEADS // B_QK_HEADS

    def lane_col(a, lane):
        return jnp.broadcast_to(a[:, lane:lane + 1], (a.shape[0], HEAD_DIM))

    groups = B_V_HEADS // MIXB_GROUP
    units = [(b, grp) for b in range(nseq) for grp in range(groups)]
    heads_of = lambda grp: list(range(grp * MIXB_GROUP, (grp + 1) * MIXB_GROUP))

    acts, gcs, betas = [], [], []
    for b in range(nseq):
        x = qkv_ref[b]
        cbuf_ref[b, 8:8 + C, :] = x
        y = cbuf_ref[b, 5:5 + C, :] * cw[0:1]
        y = y + cbuf_ref[b, 6:6 + C, :] * cw[1:2]
        y = y + cbuf_ref[b, 7:7 + C, :] * cw[2:3]
        y = y + x * cw[3:4]
        cbuf_ref[b, 0:8, :] = x[C - 8:C, :]
        acts.append(jax.nn.silu(y))
        bd = bd_ref[b]
        betas.append(jax.nn.sigmoid(bd))
        g_all = nega_ref[...] * _softplus(bd + dtb_ref[...])
        gcs.append(_mm_exact_lhs(cum_lhs, g_all))

    kst, qst, gcol, glcol, bcol, decay, a_mat, rhs = {}, {}, {}, {}, {}, {}, {}, {}
    for u in units:
        b, grp = u
        act, gc_all, beta_all = acts[b], gcs[b], betas[b]

        def stack(fn):
            return jnp.concatenate([fn(h) for h in heads_of(grp)], axis=0)

        def l2n(cols0, h):
            s = act[:, cols0 + (h // rep) * HEAD_DIM:cols0 + (h // rep + 1) * HEAD_DIM]
            return s * lax.rsqrt(jnp.sum(s * s, -1, keepdims=True) + L2_EPS)

        kst[u] = stack(lambda h: l2n(B_KEY_WIDTH, h))
        qst[u] = stack(lambda h: l2n(0, h) * (HEAD_DIM ** -0.5))
        vst = stack(lambda h: act[:, 2 * B_KEY_WIDTH + h * HEAD_DIM:2 * B_KEY_WIDTH + (h + 1) * HEAD_DIM])
        gcol[u] = stack(lambda h: lane_col(gc_all, B_V_HEADS + h))
        glcol[u] = stack(lambda h: jnp.broadcast_to(gc_all[C - 1:C, B_V_HEADS + h:B_V_HEADS + h + 1], (C, HEAD_DIM)))
        bcol[u] = stack(lambda h: lane_col(beta_all, h))
        grow = gcol[u].T[0:1, :]
        diff = jnp.concatenate([gcol[u], gcol[u]], axis=1) - grow
        decay[u] = jnp.where(tril, jnp.exp(jnp.where(tril, diff, 0.0)), 0.0)
        rhs[u] = jnp.concatenate([vst * bcol[u], kst[u] * bcol[u] * jnp.exp(gcol[u])], axis=1)
    for u in units:
        a_mat[u] = jnp.where(strict, jnp.concatenate([bcol[u], bcol[u]], axis=1) * _mm_nt(kst[u], kst[u]) * decay[u], 0.0)

    t_inv = _unit_lower_inverse_many([a_mat[u] for u in units], ri, ci, C)
    sol = [_mm(t, rhs[u]) for t, u in zip(t_inv, units)]
    qk = [_mm_nt(qst[u], kst[u]) * decay[u] for u in units]
    ws = []
    for sl, u in zip(sol, units):
        b, grp = u
        q_dec = qst[u] * jnp.exp(gcol[u])
        ws.append([_mm(jnp.concatenate([sl[i * C:(i + 1) * C, HEAD_DIM:], q_dec[i * C:(i + 1) * C]], axis=0),
                       s_ref[b * B_V_HEADS + h]) for i, h in enumerate(heads_of(grp))])
    v_new = [sl[:, :HEAD_DIM] - jnp.concatenate([w[:C] for w in wl], axis=0) for sl, wl in zip(sol, ws)]
    outs = [jnp.concatenate([w[C:] for w in wl], axis=0) + _mm(q, v) for wl, q, v in zip(ws, qk, v_new)]
    for u, v, o in zip(units, v_new, outs):
        b, grp = u
        k_dec = kst[u] * jnp.exp(glcol[u] - gcol[u])
        on = o * lax.rsqrt(jnp.mean(o * o, -1, keepdims=True) + RMS_EPS) * og
        for i, h in enumerate(heads_of(grp)):
            rs = slice(i * C, (i + 1) * C)
            si = b * B_V_HEADS + h
            s_ref[si] = s_ref[si] * jnp.exp(glcol[u][i * C:i * C + 1, :]) + _mm_tn(k_dec[rs], v[rs])
            cols = slice(h * HEAD_DIM, (h + 1) * HEAD_DIM)
            o_ref[b, :, cols] = on[rs] * jax.nn.silu(z_ref[b, :, cols])

    @pl.when(c == pl.num_programs(0) - 1)
    def _fin():
        sfin_ref[...] = s_ref[...]


def _mixb_prompt(proj, bd, conv_w, nega, dtb, og, batch, seq):
    nc = seq // DN_CHUNK
    qkv_blk = 2 * A_WIDTH // B_CONV_CH
    z_blk = (2 * A_WIDTH + B_CONV_CH) // B_VAL_WIDTH
    proj3 = proj.reshape(batch, seq, proj.shape[-1])
    bd3 = bd.reshape(batch, seq, LANES)
    o, s_fin = pl.pallas_call(
        _mixb_prompt_kernel,
        grid=(nc,),
        in_specs=[pl.BlockSpec((batch, DN_CHUNK, B_CONV_CH), lambda c: (0, c, qkv_blk)),
                  pl.BlockSpec((batch, DN_CHUNK, B_VAL_WIDTH), lambda c: (0, c, z_blk)),
                  pl.BlockSpec((batch, DN_CHUNK, LANES), lambda c: (0, c, 0)),
                  pl.BlockSpec((B_CONV, B_CONV_CH), lambda c: (0, 0)),
                  pl.BlockSpec((1, LANES), lambda c: (0, 0)),
                  pl.BlockSpec((1, LANES), lambda c: (0, 0)),
                  pl.BlockSpec((1, HEAD_DIM), lambda c: (0, 0))],
        out_specs=[pl.BlockSpec((batch, DN_CHUNK, B_VAL_WIDTH), lambda c: (0, c, 0)),
                   pl.BlockSpec((batch * B_V_HEADS, HEAD_DIM, HEAD_DIM), lambda c: (0, 0, 0))],
        out_shape=[jax.ShapeDtypeStruct((batch, seq, B_VAL_WIDTH), F32),
                   jax.ShapeDtypeStruct((batch * B_V_HEADS, HEAD_DIM, HEAD_DIM), F32)],
        scratch_shapes=[pltpu.VMEM((batch * B_V_HEADS, HEAD_DIM, HEAD_DIM), F32),
                        pltpu.VMEM((batch, DN_CHUNK + 8, B_CONV_CH), F32)],
        compiler_params=_params("arbitrary"),
        name="mixb_prompt",
    )(proj3, proj3, bd3, conv_w, nega, dtb, og)
    return (o.reshape(batch * seq, B_VAL_WIDTH),
            s_fin.reshape(batch, B_V_HEADS, HEAD_DIM, HEAD_DIM))


def _mixb_sample_pre_kernel(qkv_ref, buf_ref, bd_ref, cw_ref, nega_ref, dtb_ref,
                            act_ref, beta_ref, g_ref, *, steps, nb_rows):
    cw = cw_ref[...]

    def slab(j):
        if j < B_CONV - 1:
            return buf_ref[j * nb_rows:(j + 1) * nb_rows, :]
        jj = j - (B_CONV - 1)
        return qkv_ref[jj * nb_rows:(jj + 1) * nb_rows, :]

    for t in range(steps):
        y = slab(t) * cw[0:1]
        for i in range(1, B_CONV):
            y = y + slab(t + i) * cw[i:i + 1]
        act = jax.nn.silu(y)
        rs = slice(t * nb_rows, (t + 1) * nb_rows)
        for qh in range(B_QK_HEADS):
            cq = slice(qh * HEAD_DIM, (qh + 1) * HEAD_DIM)
            ck = slice(B_KEY_WIDTH + qh * HEAD_DIM, B_KEY_WIDTH + (qh + 1) * HEAD_DIM)
            qs = act[:, cq]
            ks = act[:, ck]
            act_ref[rs, cq] = qs * lax.rsqrt(jnp.sum(qs * qs, -1, keepdims=True) + L2_EPS) * (HEAD_DIM ** -0.5)
            act_ref[rs, ck] = ks * lax.rsqrt(jnp.sum(ks * ks, -1, keepdims=True) + L2_EPS)
        act_ref[rs, 2 * B_KEY_WIDTH:] = act[:, 2 * B_KEY_WIDTH:]
    bd = bd_ref[...]
    beta_ref[...] = jax.nn.sigmoid(bd)
    g_ref[...] = nega_ref[...] * _softplus(bd + dtb_ref[...])


def _mixb_sample_pre(proj, buf, bd, conv_w, nega, dtb, steps, nb_rows):
    m = proj.shape[0]
    qkv_blk = 2 * A_WIDTH // B_CONV_CH
    kern = functools.partial(_mixb_sample_pre_kernel, steps=steps, nb_rows=nb_rows)
    return pl.pallas_call(
        kern,
        grid=(1,),
        in_specs=[pl.BlockSpec((m, B_CONV_CH), lambda i: (0, qkv_blk)),
                  pl.BlockSpec(buf.shape, lambda i: (0, 0)),
                  pl.BlockSpec((m, LANES), lambda i: (0, 0)),
                  pl.BlockSpec((B_CONV, B_CONV_CH), lambda i: (0, 0)),
                  pl.BlockSpec((1, LANES), lambda i: (0, 0)),
                  pl.BlockSpec((1, LANES), lambda i: (0, 0))],
        out_specs=[pl.BlockSpec((m, B_CONV_CH), lambda i: (0, 0)),
                   pl.BlockSpec((m, LANES), lambda i: (0, 0)),
                   pl.BlockSpec((m, LANES), lambda i: (0, 0))],
        out_shape=[jax.ShapeDtypeStruct((m, B_CONV_CH), F32),
                   jax.ShapeDtypeStruct((m, LANES), F32),
                   jax.ShapeDtypeStruct((m, LANES), F32)],
        compiler_params=_params("arbitrary"),
        name="mixb_sample_pre",
    )(proj, buf, bd, conv_w, nega, dtb)


def _mixb_sample_rec_kernel(kq_ref, v_ref, z_ref, beta_ref, g_ref, s0_ref, og_ref,
                            o_ref, s_out_ref, *, steps, pairs):
    og = og_ref[...]
    zpad = jnp.zeros((HEAD_DIM - 8, HEAD_DIM), F32)
    zrows = jnp.zeros((8 - steps, HEAD_DIM), F32)
    heads = range(B_V_HEADS)

    def body(bi, carry):
        kqs = [_mm(kq_ref[bi, h], s0_ref[bi, h]) for h in heads]
        pending = []
        for h in heads:
            kq = kq_ref[bi, h]
            g = g_ref[bi, h]
            beta = beta_ref[bi, h]
            v = v_ref[bi, h]
            gc = [g[0:1]]
            for t in range(1, steps):
                gc.append(gc[-1] + g[t:t + 1])
            k = [kq[t:t + 1] for t in range(steps)]
            q = [kq[4 + t:5 + t] for t in range(steps)]
            d = []
            for t in range(steps):
                acc = v[t:t + 1] - jnp.exp(gc[t]) * kqs[h][t:t + 1]
                for j in range(t):
                    kk = jnp.sum(k[j] * k[t], -1, keepdims=True)
                    acc = acc - jnp.exp(gc[t] - gc[j]) * kk * d[j]
                d.append(beta[t:t + 1] * acc)
            outs = []
            for t in range(steps):
                o = jnp.exp(gc[t]) * kqs[h][4 + t:5 + t]
                for j in range(t + 1):
                    qk = jnp.sum(k[j] * q[t], -1, keepdims=True)
                    o = o + jnp.exp(gc[t] - gc[j]) * qk * d[j]
                outs.append(o * lax.rsqrt(jnp.mean(o * o, -1, keepdims=True) + RMS_EPS) * og)
            o_ref[bi, h] = jnp.concatenate(outs + [zrows], axis=0) * jax.nn.silu(z_ref[bi, h])
            k_dec = jnp.concatenate([jnp.exp(gc[-1] - gc[j]) * k[j] for j in range(steps)] + [zrows], axis=0)
            k_pad = jnp.concatenate([k_dec, zpad], axis=0)
            d_pad = jnp.concatenate(d + [zrows, zpad], axis=0)
            pending.append((k_pad.T, d_pad, jnp.exp(gc[-1])))
        for h, (k_t, d_pad, decay_last) in zip(heads, pending):
            s_out_ref[bi, h] = s0_ref[bi, h] * decay_last + _mm(k_t, d_pad)
        return carry

    lax.fori_loop(0, pairs // B_V_HEADS, body, 0)


def _mixb_sample_rec(kq, v, z, beta, g, s0, og, steps, bb):
    nb = kq.shape[0]
    tile = pl.BlockSpec((bb, B_V_HEADS, 8, HEAD_DIM), lambda i: (i, 0, 0, 0))
    st = pl.BlockSpec((bb, B_V_HEADS, HEAD_DIM, HEAD_DIM), lambda i: (i, 0, 0, 0))
    kern = functools.partial(_mixb_sample_rec_kernel, steps=steps, pairs=bb * B_V_HEADS)
    return pl.pallas_call(
        kern,
        grid=(nb // bb,),
        in_specs=[tile, tile, tile, tile, tile, st, pl.BlockSpec((1, HEAD_DIM), lambda i: (0, 0))],
        out_specs=[tile, st],
        out_shape=[jax.ShapeDtypeStruct((nb, B_V_HEADS, 8, HEAD_DIM), F32),
                   jax.ShapeDtypeStruct((nb, B_V_HEADS, HEAD_DIM, HEAD_DIM), F32)],
        compiler_params=_params("arbitrary"),
        name="mixb_sample_rec",
    )(kq, v, z, beta, g, s0, og)


def _outproj_kernel(ap_ref, bp_ref, xp_ref, g1p_ref, sc2p_ref, sh2p_ref,
                    as_ref, bs_ref, xs_ref, g1s_ref, sc2s_ref, sh2s_ref,
                    w_ref, lg_ref, lb_ref, wr_ref, rb_ref, x1_ref, h2_ref, route_ref, count_ref,
                    base_ref, mix_ref, *, prompt_tiles):
    s = pl.program_id(0)

    @pl.when(s == 0)
    def _():
        base_ref[...] = jnp.zeros_like(base_ref)
        mix_ref[...] = jnp.zeros_like(mix_ref)

    cur_prompt = s < prompt_tiles
    a = jnp.where(cur_prompt, ap_ref[...], as_ref[...]).astype(BF16)
    mix_a = jnp.dot(a, w_ref[:A_WIDTH, :], preferred_element_type=F32)

    def project():
        b = jnp.where(cur_prompt, bp_ref[...], bs_ref[...]).astype(BF16)
        return mix_a + jnp.dot(b, w_ref[A_WIDTH:, :], preferred_element_type=F32)

    prev_prompt = s - 1 < prompt_tiles
    pick = lambda p_ref, s_ref: jnp.where(prev_prompt, p_ref[...], s_ref[...])
    mix = mix_ref[(s + 1) % 2]
    x1 = _layer_norm_rows(ALPHA * pick(xp_ref, xs_ref) + pick(g1p_ref, g1s_ref) * mix, lg_ref[...], lb_ref[...])
    x1_ref[...] = x1
    h2 = x1 * (1.0 + pick(sc2p_ref, sc2s_ref)) + pick(sh2p_ref, sh2s_ref)
    h2_ref[...] = _pack_bf16_pairs(h2)
    route_ref[...], mix_new = _route_tile(_mm3(h2, wr_ref[...]), rb_ref[...], base_ref, s >= 1, project)
    count_ref[...] = base_ref[...]
    mix_ref[s % 2] = mix_new


def _outproj(prompt, sample, mod_p, mod_s, w_out, lg, lb, w_router, r_bias, tm, seq):
    n_p = prompt[2].shape[0]
    n_s = sample[2].shape[0]
    pt = n_p // tm
    st = n_s // tm
    nt = pt + st
    m = n_p + n_s
    tiles_per_seq = seq // tm
    cur = lambda s: jnp.minimum(s, nt - 1)
    prev = lambda s: jnp.maximum(s - 1, 0)
    p_idx = lambda t: jnp.minimum(t, pt - 1)
    s_idx = lambda t: jnp.clip(t - pt, 0, st - 1)
    p_cur = lambda w: pl.BlockSpec((tm, w), lambda s: (p_idx(cur(s)), 0))
    s_cur = lambda w: pl.BlockSpec((tm, w), lambda s: (s_idx(cur(s)), 0))
    p_prev = lambda w: pl.BlockSpec((tm, w), lambda s: (p_idx(prev(s)), 0))
    s_prev = lambda w: pl.BlockSpec((tm, w), lambda s: (s_idx(prev(s)), 0))
    p_mod = lambda col: pl.BlockSpec((None, 1, D_MODEL), lambda s: (p_idx(prev(s)) // tiles_per_seq, 0, col))
    s_mod = lambda col: pl.BlockSpec((tm, D_MODEL), lambda s: (0, col))
    row = lambda w: pl.BlockSpec((tm, w), lambda s: (prev(s), 0))
    full = lambda shape: pl.BlockSpec(shape, lambda s: (0, 0))
    return pl.pallas_call(
        functools.partial(_outproj_kernel, prompt_tiles=pt),
        grid=(nt + 1,),
        in_specs=[p_cur(A_WIDTH), p_cur(B_VAL_WIDTH), p_prev(D_MODEL), p_mod(2), p_mod(4), p_mod(3),
                  s_cur(A_WIDTH), s_cur(B_VAL_WIDTH), s_prev(D_MODEL), s_mod(2), s_mod(4), s_mod(3),
                  full((D_MODEL, D_MODEL)), full((1, D_MODEL)), full((1, D_MODEL)),
                  full((D_MODEL, LANES)), full((1, LANES))],
        out_specs=[row(D_MODEL), row(D_MODEL // 2), row(LANES),
                   full((1, LANES))],
        out_shape=[jax.ShapeDtypeStruct((m, D_MODEL), F32),
                   jax.ShapeDtypeStruct((m, D_MODEL // 2), jnp.uint32),
                   jax.ShapeDtypeStruct((m, LANES), F32),
                   jax.ShapeDtypeStruct((1, LANES), F32)],
        scratch_shapes=[pltpu.VMEM((1, LANES), F32),
                        pltpu.VMEM((2, tm, D_MODEL), F32)],
        compiler_params=_params("arbitrary"),
        name="outproj",
    )(*prompt, mod_p, mod_p, mod_p, *sample, mod_s, mod_s, mod_s, w_out, lg, lb, w_router, r_bias)


def _route_tile(lg, bias, base_ref, valid, between):
    tm = lg.shape[0]
    lane = lax.broadcasted_iota(jnp.int32, lg.shape, 1)
    neg = -jnp.inf

    def first_argmax(score):
        mx = jnp.max(score, -1, keepdims=True)
        return jnp.min(jnp.where(score == mx, lane, LANES), -1, keepdims=True)

    def pick(vals, idx):
        return jnp.sum(jnp.where(lane == idx, vals, 0.0), -1, keepdims=True)

    gmask = lane < N_GROUPS
    mg = jnp.max(jnp.where(gmask, lg, neg), -1, keepdims=True)
    eg = jnp.where(gmask, jnp.exp(jnp.where(gmask, lg - mg, 0.0)), 0.0)
    pg = eg / jnp.sum(eg, -1, keepdims=True)
    sel_g = first_argmax(jnp.where(gmask, lg + bias, neg))
    p_sel = pick(pg, sel_g)

    lo = N_GROUPS + sel_g * EXPERTS_PER_GROUP
    emask = (lane >= lo) & (lane < lo + EXPERTS_PER_GROUP)
    me = jnp.max(jnp.where(emask, lg, neg), -1, keepdims=True)
    ee = jnp.where(emask, jnp.exp(jnp.where(emask, lg - me, 0.0)), 0.0)
    pe = ee / jnp.sum(ee, -1, keepdims=True)
    score = jnp.where(emask, pe + bias, neg)
    i1 = first_argmax(score)
    i2 = first_argmax(jnp.where(lane == i1, neg, score))
    w1 = pick(pe, i1)
    w2 = pick(pe, i2)
    wsum = w1 + w2
    gate1 = w1 / wsum * p_sel
    gate2 = w2 / wsum * p_sel

    hot = ((lane == i1) | (lane == i2)).astype(BF16)
    extra = between()
    ri = lax.broadcasted_iota(jnp.int32, (tm, tm), 0)
    ci = lax.broadcasted_iota(jnp.int32, (tm, tm), 1)
    before = jnp.dot((ri > ci).astype(BF16), hot, preferred_element_type=F32) + base_ref[...]
    rank1 = pick(before, i1)
    rank2 = pick(before, i2)
    base_ref[...] = jnp.where(valid, base_ref[...] + jnp.sum(hot.astype(F32), 0, keepdims=True), base_ref[...])

    out = jnp.where(lane == 0, (i1 - N_GROUPS).astype(F32), 0.0)
    out = jnp.where(lane == 1, (i2 - N_GROUPS).astype(F32), out)
    out = jnp.where(lane == 2, gate1, out)
    out = jnp.where(lane == 3, gate2, out)
    out = jnp.where(lane == 4, rank1, out)
    out = jnp.where(lane == 5, rank2, out)
    return out, extra


DISPATCH_ROWS = 512


def _dispatch_kernel(dest_ref, last_ref, nv_ref, h_ref, xb_ref, zbuf, zsem, ssem, *, n_tok, n_blocks):
    i = pl.program_id(0)
    nv = nv_ref[0]

    def zero_block(blk):
        return pltpu.make_async_copy(zbuf, xb_ref.at[pl.ds(blk * MOE_ROWS, MOE_ROWS)], zsem)

    @pl.when(i == 0)
    def _():
        zbuf[...] = jnp.zeros_like(zbuf)
        for e in range(N_EXPERTS):
            @pl.when(last_ref[e] >= 0)
            def _():
                zero_block(last_ref[e]).start()

        def tail_start(b, carry):
            zero_block(b).start()
            return carry

        lax.fori_loop(nv, n_blocks, tail_start, 0)
        for e in range(N_EXPERTS):
            @pl.when(last_ref[e] >= 0)
            def _():
                zero_block(0).wait()

        def tail_wait(b, carry):
            zero_block(0).wait()
            return carry

        lax.fori_loop(nv, n_blocks, tail_wait, 0)

    def row_copy(r, choice):
        slot = dest_ref[choice * n_tok + i * DISPATCH_ROWS + r]
        return pltpu.make_async_copy(h_ref.at[pl.ds(r, 1)], xb_ref.at[pl.ds(slot, 1)], ssem)

    def body(r, carry):
        row_copy(r, 0).start()
        row_copy(r, 1).start()
        return carry

    lax.fori_loop(0, DISPATCH_ROWS, body, 0, unroll=8)
    for _ in range(2):
        pltpu.make_async_copy(h_ref, xb_ref.at[pl.ds(0, DISPATCH_ROWS)], ssem).wait()


def _dispatch(dest_flat, last_block, n_valid, h2, n_blocks):
    n_tok = h2.shape[0]
    kern = functools.partial(_dispatch_kernel, n_tok=n_tok, n_blocks=n_blocks)
    return pl.pallas_call(
        kern,
        grid_spec=pltpu.PrefetchScalarGridSpec(
            num_scalar_prefetch=3,
            grid=(n_tok // DISPATCH_ROWS,),
            in_specs=[pl.BlockSpec((DISPATCH_ROWS, D_MODEL // 2), lambda i, *_: (i, 0))],
            out_specs=pl.BlockSpec(memory_space=pl.ANY),
            scratch_shapes=[pltpu.VMEM((MOE_ROWS, D_MODEL // 2), jnp.uint32),
                            pltpu.SemaphoreType.DMA(()),
                            pltpu.SemaphoreType.DMA(())]),
        out_shape=jax.ShapeDtypeStruct((n_blocks * MOE_ROWS, D_MODEL // 2), jnp.uint32),
        compiler_params=_params("arbitrary"),
        name="dispatch",
    )(dest_flat, last_block, n_valid, h2)


def _expert_kernel(be_ref, nv_ref, first_ref, next_ref, wslot_ref,
                   x_ref, wg_ref, wu_ref, wd_ref, y_ref,
                   wg_st, wu_st, wd_st, wsem):
    i = pl.program_id(0)
    nv = nv_ref[0]

    def weight_copies(e, p):
        return [pltpu.make_async_copy(src.at[e], dst.at[p], wsem.at[p])
                for src, dst in ((wg_ref, wg_st), (wu_ref, wu_st), (wd_ref, wd_st))]

    @pl.when(i == 0)
    def _():
        for c in weight_copies(be_ref[0], 0):
            c.start()

    @pl.when((i < nv) & (first_ref[i] == 1))
    def _():
        p = wslot_ref[i]
        for c in weight_copies(be_ref[i], p):
            c.wait()

        @pl.when(next_ref[i] >= 0)
        def _():
            for c in weight_copies(next_ref[i], 1 - p):
                c.start()

    @pl.when(i < nv)
    def _():
        p = wslot_ref[i]
        x = _unpack_bf16_pairs(x_ref[...])
        hg = jnp.dot(x, wg_st[p].astype(BF16), preferred_element_type=F32)
        hu = jnp.dot(x, wu_st[p].astype(BF16), preferred_element_type=F32)
        hid = (jax.nn.silu(hg) * hu).astype(BF16)
        y_ref[...] = jnp.dot(hid, wd_st[p].astype(BF16), preferred_element_type=F32)

    @pl.when(i >= nv)
    def _():
        y_ref[...] = jnp.zeros_like(y_ref)


def _experts(block_e, n_valid, xb, w_gate, w_up, w_down):
    n_blocks = block_e.shape[0]
    idx = jnp.arange(n_blocks, dtype=jnp.int32)
    first = (idx < n_valid[0]) & ((idx == 0) | (block_e != jnp.roll(block_e, 1)))
    wslot = (jnp.cumsum(first.astype(jnp.int32)) - 1) % 2
    first_at = jnp.where(first, idx, n_blocks)
    next_first = jnp.concatenate([lax.cummin(first_at, reverse=True)[1:], jnp.full((1,), n_blocks, jnp.int32)])
    next_e = jnp.where(next_first < n_blocks, block_e[jnp.minimum(next_first, n_blocks - 1)], -1)
    any_spec = pl.BlockSpec(memory_space=pl.ANY)
    rows = pl.BlockSpec((MOE_ROWS, D_MODEL), lambda i, *_: (i, 0))
    return pl.pallas_call(
        _expert_kernel,
        grid_spec=pltpu.PrefetchScalarGridSpec(
            num_scalar_prefetch=5,
            grid=(n_blocks,),
            in_specs=[pl.BlockSpec((MOE_ROWS, D_MODEL // 2), lambda i, *_: (i, 0)), any_spec, any_spec, any_spec],
            out_specs=rows,
            scratch_shapes=[pltpu.VMEM((2, D_MODEL, D_EXPERT), F32),
                            pltpu.VMEM((2, D_MODEL, D_EXPERT), F32),
                            pltpu.VMEM((2, D_EXPERT, D_MODEL), F32),
                            pltpu.SemaphoreType.DMA((2,))]),
        out_shape=jax.ShapeDtypeStruct((n_blocks * MOE_ROWS, D_MODEL), F32),
        compiler_params=_params("arbitrary"),
        name="experts",
    )(block_e, n_valid, first.astype(jnp.int32), next_e.astype(jnp.int32), wslot.astype(jnp.int32),
      xb, w_gate, w_up, w_down)


def _final_kernel(dest_ref, x1_ref, route_ref, g2_ref, lg_ref, lb_ref, yb_ref, o_ref, ybuf, sem,
                  *, row0, n_tok):
    i = pl.program_id(0)
    last = pl.num_programs(0) - 1
    tm = o_ref.shape[0]
    slot = i % 2

    def row_copy(tile, r, choice, s):
        src = dest_ref[choice * n_tok + row0 + tile * tm + r]
        return pltpu.make_async_copy(yb_ref.at[pl.ds(src, 1)], ybuf.at[s, choice, pl.ds(r, 1)], sem.at[s])

    def tile_wait(s):
        for choice in range(2):
            pltpu.make_async_copy(yb_ref.at[pl.ds(0, tm)], ybuf.at[s, choice], sem.at[s]).wait()

    @pl.when(i == 0)
    def _():
        def body(r, carry):
            row_copy(0, r, 0, 0).start()
            row_copy(0, r, 1, 0).start()
            return carry

        lax.fori_loop(0, tm, body, 0, unroll=8)

    tile_wait(slot)
    nxt = jnp.minimum(i + 1, last)
    for r in range(tm):
        row_copy(nxt, r, 0, 1 - slot).start()
        row_copy(nxt, r, 1, 1 - slot).start()
    route = route_ref[...]
    ff = ybuf[slot, 0] * route[:, 2:3] + ybuf[slot, 1] * route[:, 3:4]
    o_ref[...] = _layer_norm_rows(ALPHA * x1_ref[...] + g2_ref[...] * ff, lg_ref[...], lb_ref[...])

    @pl.when(i == last)
    def _():
        tile_wait(1 - slot)


def _final(dest_flat, x1, yb, route, mod, lg, lb, tm, rows_per_mod, row0, n_rows):
    n_tok = x1.shape[0]
    off = row0 // tm
    return pl.pallas_call(
        functools.partial(_final_kernel, row0=row0, n_tok=n_tok),
        grid_spec=pltpu.PrefetchScalarGridSpec(
            num_scalar_prefetch=1,
            grid=(n_rows // tm,),
            in_specs=[pl.BlockSpec((tm, D_MODEL), lambda i, *_: (off + i, 0)),
                      pl.BlockSpec((tm, LANES), lambda i, *_: (off + i, 0)),
                      _mod_spec(mod, 5, tm, rows_per_mod),
                      pl.BlockSpec((1, D_MODEL), lambda i, *_: (0, 0)),
                      pl.BlockSpec((1, D_MODEL), lambda i, *_: (0, 0)),
                      pl.BlockSpec(memory_space=pl.ANY)],
            out_specs=pl.BlockSpec((tm, D_MODEL), lambda i, *_: (i, 0)),
            scratch_shapes=[pltpu.VMEM((2, 2, tm, D_MODEL), F32),
                            pltpu.SemaphoreType.DMA((2,))]),
        out_shape=jax.ShapeDtypeStruct((n_rows, D_MODEL), F32),
        compiler_params=_params("arbitrary"),
        name="final",
    )(dest_flat, x1, route, mod, lg, lb, yb)


def _lane_row(vec, offset):
    return jnp.zeros((1, LANES), F32).at[0, offset:offset + vec.shape[0]].set(vec.astype(F32))


def kernel(x_prompt, x_sample, state_conv, state_ssm, c_prompt, c_sample, w_ada, b_ada, w_in, a_ws, a_bs, a_norm_g, a_norm_b, b_conv_w, b_a_log, b_dt_bias, b_onorm_g, w_out, ln1_g, ln1_b, w_router_g, b_router_g, w_router_e, b_router_e, w_gate, w_up, w_down, ln2_g, ln2_b):
    batch, seq, d = x_prompt.shape
    nb, steps, _ = x_sample.shape
    n_p = batch * seq
    n_s = nb * steps
    n_tok = n_p + n_s
    l = 0

    c_rows = batch + nb
    c_pad = (-c_rows) % 8
    c_all = jnp.concatenate([c_prompt, c_sample, jnp.zeros((c_pad, d), F32)], axis=0)
    mod = _ada(c_all, w_ada[l], b_ada[l].reshape(1, -1))
    mod_p = mod[:batch].reshape(batch, 1, 6 * d)
    mod_s = mod[batch:batch + nb]

    w_in_t = jnp.swapaxes(w_in[l], 0, 1)
    w_bd = jnp.pad(w_in_t[MAIN_WIDTH:], ((0, LANES - 2 * B_V_HEADS), (0, 0))).astype(BF16)
    xp = x_prompt.reshape(n_p, d)
    xs = jnp.swapaxes(x_sample, 0, 1).reshape(n_s, d)
    proj_p, bd_p, proj_s, bd_s = _inproj(xp, xs, mod_p, mod_s, w_in_t, w_bd, 512, seq)

    ng = a_norm_g[l].reshape(1, -1)
    nbias = a_norm_b[l].reshape(1, -1)
    bias_tile = jnp.repeat(a_bs[l].T, HEAD_DIM, axis=1)
    a_out_p = _mixa_prompt(proj_p, a_ws[l], bias_tile, ng, nbias, 512)
    coef = jnp.repeat(jnp.transpose(a_ws[l][:, :steps, :steps], (1, 2, 0)).reshape(steps * steps, A_HEADS),
                      HEAD_DIM, axis=1)
    a_out_s, chunkv = _mixa_sample(proj_s, coef, bias_tile[:steps], ng, nbias, steps, nb)

    nega = _lane_row(-jnp.exp(b_a_log[l].astype(F32)), B_V_HEADS)
    dtb = _lane_row(b_dt_bias[l], B_V_HEADS)
    og = b_onorm_g[l].reshape(1, -1)
    b_out_p, ssm_p = _mixb_prompt(proj_p, bd_p, b_conv_w[l], nega, dtb, og, batch, seq)
    qkv0 = 2 * A_WIDTH
    conv_p = proj_p.reshape(batch, seq, MAIN_WIDTH)[:, seq - (B_CONV - 1):, qkv0:qkv0 + B_CONV_CH]

    buf_s = jnp.swapaxes(state_conv[l], 0, 1).reshape((B_CONV - 1) * nb, B_CONV_CH)
    act_s, beta_s, g_s = _mixb_sample_pre(proj_s, buf_s, bd_s, b_conv_w[l], nega, dtb, steps, nb)
    rep = B_V_HEADS // B_QK_HEADS
    act4 = act_s.reshape(steps, nb, B_CONV_CH)
    q8 = jnp.repeat(act4[..., :B_KEY_WIDTH].reshape(steps, nb, B_QK_HEADS, HEAD_DIM), rep, axis=2)
    k8 = jnp.repeat(act4[..., B_KEY_WIDTH:2 * B_KEY_WIDTH].reshape(steps, nb, B_QK_HEADS, HEAD_DIM), rep, axis=2)

    def to_tiles(a):
        a = jnp.transpose(a, (1, 2, 0, 3))
        return jnp.pad(a, ((0, 0), (0, 0), (0, 8 - a.shape[2]), (0, 0)))

    kq_t = jnp.concatenate([to_tiles(k8)[:, :, :4], to_tiles(q8)[:, :, :4]], axis=2)
    v_t = to_tiles(act4[..., 2 * B_KEY_WIDTH:].reshape(steps, nb, B_V_HEADS, HEAD_DIM))
    z_t = to_tiles(proj_s[:, qkv0 + B_CONV_CH:].reshape(steps, nb, B_V_HEADS, HEAD_DIM))

    def gate_tiles(a, off):
        a = a[:, off:off + B_V_HEADS].reshape(steps, nb, B_V_HEADS, 1)
        return to_tiles(jnp.broadcast_to(a, (steps, nb, B_V_HEADS, HEAD_DIM)))

    o_t, ssm_s = _mixb_sample_rec(kq_t, v_t, z_t, gate_tiles(beta_s, 0), gate_tiles(g_s, B_V_HEADS),
                                  state_ssm[l], og, steps, 8)
    b_out_s = jnp.transpose(o_t[:, :, :steps], (2, 0, 1, 3)).reshape(n_s, B_VAL_WIDTH)
    conv_s = jnp.swapaxes(proj_s.reshape(steps, nb, MAIN_WIDTH)[steps - (B_CONV - 1):, :, qkv0:qkv0 + B_CONV_CH], 0, 1)

    w_out_b = w_out[l].astype(BF16)
    lg1 = ln1_g[l].reshape(1, -1)
    lb1 = ln1_b[l].reshape(1, -1)
    w_router = jnp.concatenate([w_router_g[l], w_router_e[l],
                                jnp.zeros((d, LANES - N_GROUPS - N_EXPERTS), F32)], axis=1)
    out_tm = 256
    mod_s_tile = jnp.tile(mod_s, (out_tm // nb, 1))
    r_bias = jnp.concatenate([b_router_g[l], b_router_e[l],
                              jnp.zeros((LANES - N_GROUPS - N_EXPERTS,), F32)]).reshape(1, LANES)
    x1, h2, route, counts_row = _outproj((a_out_p, b_out_p, xp), (a_out_s, b_out_s, xs), mod_p, mod_s_tile,
                                         w_out_b, lg1, lb1, w_router, r_bias, out_tm, seq)

    counts = counts_row[0, N_GROUPS:N_GROUPS + N_EXPERTS].astype(jnp.int32)
    route_t = route[:, :8].T
    eid = route_t[0:2].astype(jnp.int32)
    rank = route_t[4:6].astype(jnp.int32)
    padded = (counts + MOE_ROWS - 1) // MOE_ROWS * MOE_ROWS
    pend = jnp.cumsum(padded)
    pstart = pend - padded
    dest = rank + jnp.sum(jnp.where(eid[None] == jnp.arange(N_EXPERTS, dtype=jnp.int32)[:, None, None],
                                    pstart[:, None, None], 0), axis=0)
    n_blocks = -(-(2 * n_tok) // MOE_ROWS) + N_EXPERTS
    dest_flat = dest.reshape(-1)
    last_block = jnp.where(padded > 0, pend // MOE_ROWS - 1, -1).astype(jnp.int32)
    blk0 = jnp.arange(n_blocks, dtype=jnp.int32) * MOE_ROWS
    block_e = jnp.minimum(jnp.sum(pend[None, :] <= blk0[:, None], axis=1), N_EXPERTS - 1).astype(jnp.int32)
    n_valid = (pend[-1:] // MOE_ROWS).astype(jnp.int32)

    xb = _dispatch(dest_flat, last_block, n_valid, h2, n_blocks)
    yb = _experts(block_e, n_valid, xb, w_gate[l], w_up[l], w_down[l])

    lg2 = ln2_g[l].reshape(1, -1)
    lb2 = ln2_b[l].reshape(1, -1)
    y_p = _final(dest_flat, x1, yb, route, mod_p, lg2, lb2, 512, seq, 0, n_p)
    y_s = _final(dest_flat, x1, yb, route, mod_s, lg2, lb2, nb, 0, n_p, n_s)

    y_prompt = y_p.reshape(batch, seq, d)
    y_sample = jnp.swapaxes(y_s.reshape(steps, nb, d), 0, 1)
    chunkv_s = jnp.swapaxes(chunkv.reshape(steps, nb, A_HEADS, HEAD_DIM), 0, 1)
    return (y_prompt, y_sample, conv_p[None], ssm_p[None], conv_s[None], ssm_s[None], chunkv_s[None])
```

```python
import functools

import jax
import jax.numpy as jnp
from jax import lax
from jax.experimental import pallas as pl
from jax.experimental.pallas import tpu as pltpu

F32 = jnp.float32
BF16 = jnp.bfloat16

D_MODEL = 2048
DEPTH = 1
A_HEADS = 8
HEAD_DIM = 128
A_WIDTH = 1024
A_CHUNK = 128
B_QK_HEADS = 4
B_V_HEADS = 8
B_KEY_WIDTH = 512
B_VAL_WIDTH = 1024
B_CONV = 4
B_CONV_CH = 2048
DN_CHUNK = 64
MAIN_WIDTH = 2 * A_WIDTH + B_CONV_CH + B_VAL_WIDTH
N_GROUPS = 4
EXPERTS_PER_GROUP = 8
N_EXPERTS = 32
D_EXPERT = 512
ALPHA = (2 * DEPTH) ** 0.25
LN_EPS = 1e-5
RMS_EPS = 1e-6
L2_EPS = 1e-6

LANES = 128
VMEM_LIMIT = 56 * 1024 * 1024
MOE_ROWS = 256
MIXB_GROUP = 4


def _params(*sem):
    return pltpu.CompilerParams(dimension_semantics=sem, vmem_limit_bytes=VMEM_LIMIT)


def _mm(a, b):
    return jnp.dot(a.astype(BF16), b.astype(BF16), preferred_element_type=F32)


def _mm_nt(a, b):
    return lax.dot_general(a.astype(BF16), b.astype(BF16), (((1,), (1,)), ((), ())),
                           preferred_element_type=F32)


def _mm_tn(a, b):
    return lax.dot_general(a.astype(BF16), b.astype(BF16), (((0,), (0,)), ((), ())),
                           preferred_element_type=F32)


def _split(x):
    hi = x.astype(BF16)
    lo = (x - hi.astype(F32)).astype(BF16)
    return hi, lo


def _mm_exact_lhs(a_bf16, b):
    hi, lo = _split(b)
    return (jnp.dot(a_bf16, hi, preferred_element_type=F32)
            + jnp.dot(a_bf16, lo, preferred_element_type=F32))


def _mm3(a, b):
    ah, al = _split(a)
    bh, bl = _split(b)
    return (jnp.dot(ah, bh, preferred_element_type=F32) + jnp.dot(ah, bl, preferred_element_type=F32)
            + jnp.dot(al, bh, preferred_element_type=F32))


def _pack_bf16_pairs(x):
    half = x.shape[1] // 2
    hi = lax.bitcast_convert_type(x[:, :half].astype(BF16).astype(F32), jnp.uint32)
    lo = lax.bitcast_convert_type(x[:, half:].astype(BF16).astype(F32), jnp.uint32)
    return (hi & jnp.uint32(0xFFFF0000)) | lax.shift_right_logical(lo, jnp.uint32(16))


def _unpack_bf16_pairs(u):
    hi = lax.bitcast_convert_type(u & jnp.uint32(0xFFFF0000), F32)
    lo = lax.bitcast_convert_type(lax.shift_left(u, jnp.uint32(16)), F32)
    return jnp.concatenate([hi, lo], axis=1).astype(BF16)


def _softplus(x):
    return jnp.maximum(x, 0.0) + jnp.log1p(jnp.exp(-jnp.abs(x)))


def _layer_norm_rows(x, g, b):
    mu = jnp.mean(x, -1, keepdims=True)
    xc = x - mu
    var = jnp.mean(xc * xc, -1, keepdims=True)
    return xc * lax.rsqrt(var + LN_EPS) * g + b


def _ada_kernel(c_ref, w_ref, b_ref, o_ref):
    a = jax.nn.silu(c_ref[...]).astype(BF16)
    o_ref[...] = jnp.dot(a, w_ref[...].astype(BF16), preferred_element_type=F32) + b_ref[...]


def _ada(c_all, w_ada, b_ada):
    rows = c_all.shape[0]
    tn = 1024
    return pl.pallas_call(
        _ada_kernel,
        grid=(6 * D_MODEL // tn,),
        in_specs=[pl.BlockSpec((rows, D_MODEL), lambda j: (0, 0)),
                  pl.BlockSpec((D_MODEL, tn), lambda j: (0, j)),
                  pl.BlockSpec((1, tn), lambda j: (0, j))],
        out_specs=pl.BlockSpec((rows, tn), lambda j: (0, j)),
        out_shape=jax.ShapeDtypeStruct((rows, 6 * D_MODEL), F32),
        compiler_params=_params("arbitrary"),
        name="ada",
    )(c_all, w_ada, b_ada)


def _inproj_kernel(xp_ref, scp_ref, shp_ref, xs_ref, scs_ref, shs_ref, w_ref, wbd_ref,
                   op_ref, bdp_ref, os_ref, bds_ref, wb_ref, *, prompt_tiles):
    j = pl.program_id(0)
    i = pl.program_id(1)

    @pl.when(i == 0)
    def _():
        wb_ref[...] = w_ref[...].astype(BF16)

    def project(h, o_ref, bd_ref):
        o_ref[...] = _mm_nt(h, wb_ref[...])

        @pl.when(j == 0)
        def _():
            bd_ref[0] = _mm_nt(h, wbd_ref[...])

        @pl.when(j != 0)
        def _():
            bd_ref[0] = jnp.zeros(bd_ref.shape[1:], F32)

    @pl.when(i < prompt_tiles)
    def _():
        project((xp_ref[...] * (1.0 + scp_ref[...]) + shp_ref[...]).astype(BF16), op_ref, bdp_ref)

    @pl.when(i >= prompt_tiles)
    def _():
        n_s, nb = xs_ref.shape[0], scs_ref.shape[0]
        x = xs_ref[...].reshape(n_s // nb, nb, D_MODEL)
        h = x * (1.0 + scs_ref[...])[None] + shs_ref[...][None]
        project(h.reshape(n_s, D_MODEL).astype(BF16), os_ref, bds_ref)


def _mod_spec(mod, col, tm, rows_per_mod, axis=0):
    if rows_per_mod:
        tiles = rows_per_mod // tm
        return pl.BlockSpec((None, 1, D_MODEL), lambda *g: (g[axis] // tiles, 0, col))
    return pl.BlockSpec((tm, D_MODEL), lambda *g: (0, col))


def _inproj(xp, xs, mod_p, mod_s, w_in_t, w_bd, tm, seq):
    n_p, n_s = xp.shape[0], xs.shape[0]
    nb = mod_s.shape[0]
    tn = 1024
    nj = MAIN_WIDTH // tn
    pt = n_p // tm
    tiles_per_seq = seq // tm
    pi = lambda i: jnp.minimum(i, pt - 1)
    p_mod = lambda col: pl.BlockSpec((None, 1, D_MODEL), lambda j, i: (pi(i) // tiles_per_seq, 0, col))
    s_mod = lambda col: pl.BlockSpec((nb, D_MODEL), lambda j, i: (0, col))
    proj_p, bd_p, proj_s, bd_s = pl.pallas_call(
        functools.partial(_inproj_kernel, prompt_tiles=pt),
        grid=(nj, pt + 1),
        in_specs=[pl.BlockSpec((tm, D_MODEL), lambda j, i: (pi(i), 0)), p_mod(1), p_mod(0),
                  pl.BlockSpec((n_s, D_MODEL), lambda j, i: (0, 0)), s_mod(1), s_mod(0),
                  pl.BlockSpec((tn, D_MODEL), lambda j, i: (j, 0)),
                  pl.BlockSpec((LANES, D_MODEL), lambda j, i: (0, 0))],
        out_specs=[pl.BlockSpec((tm, tn), lambda j, i: (pi(i), j)),
                   pl.BlockSpec((1, tm, LANES), lambda j, i: (j, pi(i), 0)),
                   pl.BlockSpec((n_s, tn), lambda j, i: (0, j)),
                   pl.BlockSpec((1, n_s, LANES), lambda j, i: (j, 0, 0))],
        out_shape=[jax.ShapeDtypeStruct((n_p, MAIN_WIDTH), F32),
                   jax.ShapeDtypeStruct((nj, n_p, LANES), F32),
                   jax.ShapeDtypeStruct((n_s, MAIN_WIDTH), F32),
                   jax.ShapeDtypeStruct((nj, n_s, LANES), F32)],
        scratch_shapes=[pltpu.VMEM((tn, D_MODEL), BF16)],
        compiler_params=_params("arbitrary", "arbitrary"),
        name="inproj",
    )(xp, mod_p, mod_p, xs, mod_s, mod_s, w_in_t, w_bd)
    return proj_p, bd_p[0], proj_s, bd_s[0]


def _mixa_prompt_kernel(p_ref, ws_ref, bias_ref, ng_ref, nb_ref, o_ref):
    rows = p_ref.shape[0]
    u = jax.nn.gelu(p_ref[:, :A_WIDTH])
    v = _layer_norm_rows(jax.nn.gelu(p_ref[:, A_WIDTH:]), ng_ref[...], nb_ref[...])
    ri = lax.broadcasted_iota(jnp.int32, (A_CHUNK, A_CHUNK), 0)
    ci = lax.broadcasted_iota(jnp.int32, (A_CHUNK, A_CHUNK), 1)
    for h in range(A_HEADS):
        w = jnp.where(ri >= ci, ws_ref[h], 0.0).astype(BF16)
        cols = slice(h * HEAD_DIM, (h + 1) * HEAD_DIM)
        for c in range(rows // A_CHUNK):
            rs = slice(c * A_CHUNK, (c + 1) * A_CHUNK)
            s = jnp.dot(w, v[rs, cols].astype(BF16), preferred_element_type=F32) + bias_ref[:, cols]
            o_ref[rs, cols] = u[rs, cols] * s


def _mixa_prompt(proj, a_ws, bias_tile, ng, nb, tm):
    m = proj.shape[0]
    return pl.pallas_call(
        _mixa_prompt_kernel,
        grid=(m // tm,),
        in_specs=[pl.BlockSpec((tm, 2 * A_WIDTH), lambda i: (i, 0)),
                  pl.BlockSpec((A_HEADS, A_CHUNK, A_CHUNK), lambda i: (0, 0, 0)),
                  pl.BlockSpec((A_CHUNK, A_WIDTH), lambda i: (0, 0)),
                  pl.BlockSpec((1, A_WIDTH), lambda i: (0, 0)),
                  pl.BlockSpec((1, A_WIDTH), lambda i: (0, 0))],
        out_specs=pl.BlockSpec((tm, A_WIDTH), lambda i: (i, 0)),
        out_shape=jax.ShapeDtypeStruct((m, A_WIDTH), F32),
        compiler_params=_params("arbitrary"),
        name="mixa_prompt",
    )(proj, a_ws, bias_tile, ng, nb)


def _mixa_sample_kernel(p_ref, coef_ref, bias_ref, ng_ref, nb_ref, o_ref, v_ref, *, steps, nb_rows):
    u = jax.nn.gelu(p_ref[:, :A_WIDTH])
    v = _layer_norm_rows(jax.nn.gelu(p_ref[:, A_WIDTH:]), ng_ref[...], nb_ref[...])
    v_ref[...] = v
    for t in range(steps):
        s = bias_ref[t:t + 1, :]
        for j in range(t + 1):
            s = s + coef_ref[t * steps + j:t * steps + j + 1, :] * v[j * nb_rows:(j + 1) * nb_rows, :]
        rs = slice(t * nb_rows, (t + 1) * nb_rows)
        o_ref[rs, :] = u[rs, :] * s


def _mixa_sample(proj, coef, bias, ng, nb, steps, nb_rows):
    m = proj.shape[0]
    kern = functools.partial(_mixa_sample_kernel, steps=steps, nb_rows=nb_rows)
    return pl.pallas_call(
        kern,
        grid=(1,),
        in_specs=[pl.BlockSpec((m, 2 * A_WIDTH), lambda i: (0, 0)),
                  pl.BlockSpec(coef.shape, lambda i: (0, 0)),
                  pl.BlockSpec(bias.shape, lambda i: (0, 0)),
                  pl.BlockSpec((1, A_WIDTH), lambda i: (0, 0)),
                  pl.BlockSpec((1, A_WIDTH), lambda i: (0, 0))],
        out_specs=[pl.BlockSpec((m, A_WIDTH), lambda i: (0, 0)),
                   pl.BlockSpec((m, A_WIDTH), lambda i: (0, 0))],
        out_shape=[jax.ShapeDtypeStruct((m, A_WIDTH), F32),
                   jax.ShapeDtypeStruct((m, A_WIDTH), F32)],
        compiler_params=_params("arbitrary"),
        name="mixa_sample",
    )(proj, coef, bias, ng, nb)


def _unit_lower_inverse_many(a_list, ri, ci, block):
    rows = a_list[0].shape[0]
    eye = (ri == ci).astype(F32)
    pair = (lax.shift_right_logical(ri, 1) == lax.shift_right_logical(ci, 1)) & ((ri & 1) == 1) & ((ci & 1) == 0)
    ts = [eye - jnp.where(pair, a, 0.0) for a in a_list]
    n = 2
    while n < block:
        sh = n.bit_length()
        m = ((lax.shift_right_logical(ri, sh) == lax.shift_right_logical(ci, sh))
             & ((ri & n) != 0) & ((ci & n) == 0))
        ans = [jnp.where(m, a, 0.0) for a in a_list]
        if n % 8:
            xs = [_mm(t, an) for t, an in zip(ts, ans)]
            ts = [t - _mm(x, t) for t, x in zip(ts, xs)]
        else:
            lower = [slice(r0 + n, r0 + 2 * n) for r0 in range(0, rows, 2 * n)]
            pick = lambda t: jnp.concatenate([t[sl] for sl in lower], axis=0)
            xs = [_mm(pick(t), an) for t, an in zip(ts, ans)]
            upd = [pick(t) - _mm(x, t) for t, x in zip(ts, xs)]
            ts = [jnp.concatenate([piece for k, r0 in enumerate(range(0, rows, 2 * n))
                                   for piece in (t[r0:r0 + n], u[k * n:(k + 1) * n])], axis=0)
                  for t, u in zip(ts, upd)]
        n *= 2
    return ts


def _mixb_prompt_kernel(qkv_ref, z_ref, bd_ref, cw_ref, nega_ref, dtb_ref, og_ref,
                        o_ref, sfin_ref, s_ref, cbuf_ref):
    c = pl.program_id(0)
    C = DN_CHUNK
    nseq = qkv_ref.shape[0]

    @pl.when(c == 0)
    def _init():
        s_ref[...] = jnp.zeros_like(s_ref)
        cbuf_ref[:, 0:8, :] = jnp.zeros((nseq, 8, B_CONV_CH), F32)

    cw = cw_ref[...]
    og = og_ref[...]
    ri64 = lax.broadcasted_iota(jnp.int32, (C, C), 0)
    ci64 = lax.broadcasted_iota(jnp.int32, (C, C), 1)
    cum_lhs = (ri64 >= ci64).astype(BF16)

    R = MIXB_GROUP * C
    ri = lax.broadcasted_iota(jnp.int32, (R, R), 0)
    ci = lax.broadcasted_iota(jnp.int32, (R, R), 1)
    same = lax.shift_right_logical(ri, C.bit_length() - 1) == lax.shift_right_logical(ci, C.bit_length() - 1)
    tril = same & (ri >= ci)
    strict = same & (ri > ci)
    rep = B_V_HEADS // B_QK_HEADS

    def lane_col(a, lane):
        return jnp.broadcast_to(a[:, lane:lane + 1], (a.shape[0], HEAD_DIM))

    groups = B_V_HEADS // MIXB_GROUP
    all_units = [(b, grp) for b in range(nseq) for grp in range(groups)]
    heads_of = lambda grp: list(range(grp * MIXB_GROUP, (grp + 1) * MIXB_GROUP))

    acts, gcs, betas = [], [], []
    for b in range(nseq):
        x = qkv_ref[b]
        cbuf_ref[b, 8:8 + C, :] = x
        y = cbuf_ref[b, 5:5 + C, :] * cw[0:1]
        y = y + cbuf_ref[b, 6:6 + C, :] * cw[1:2]
        y = y + cbuf_ref[b, 7:7 + C, :] * cw[2:3]
        y = y + x * cw[3:4]
        cbuf_ref[b, 0:8, :] = x[C - 8:C, :]
        acts.append(jax.nn.silu(y))
        bd = bd_ref[b]
        betas.append(jax.nn.sigmoid(bd))
        g_all = nega_ref[...] * _softplus(bd + dtb_ref[...])
        gcs.append(_mm_exact_lhs(cum_lhs, g_all))

    for units in (all_units[:len(all_units) // 2], all_units[len(all_units) // 2:]):
        kst, qst, gcol, glcol, bcol, decay, a_mat, rhs = {}, {}, {}, {}, {}, {}, {}, {}
        for u in units:
            b, grp = u
            act, gc_all, beta_all = acts[b], gcs[b], betas[b]

            def stack(fn):
                return jnp.concatenate([fn(h) for h in heads_of(grp)], axis=0)

            def l2n(cols0, h):
                s = act[:, cols0 + (h // rep) * HEAD_DIM:cols0 + (h // rep + 1) * HEAD_DIM]
                return s * lax.rsqrt(jnp.sum(s * s, -1, keepdims=True) + L2_EPS)

            kst[u] = stack(lambda h: l2n(B_KEY_WIDTH, h))
            qst[u] = stack(lambda h: l2n(0, h) * (HEAD_DIM ** -0.5))
            vst = stack(lambda h: act[:, 2 * B_KEY_WIDTH + h * HEAD_DIM:2 * B_KEY_WIDTH + (h + 1) * HEAD_DIM])
            gcol[u] = stack(lambda h: lane_col(gc_all, B_V_HEADS + h))
            glcol[u] = stack(lambda h: jnp.broadcast_to(gc_all[C - 1:C, B_V_HEADS + h:B_V_HEADS + h + 1], (C, HEAD_DIM)))
            bcol[u] = stack(lambda h: lane_col(beta_all, h))
            grow = gcol[u].T[0:1, :]
            diff = jnp.concatenate([gcol[u], gcol[u]], axis=1) - grow
            decay[u] = jnp.where(tril, jnp.exp(jnp.where(tril, diff, 0.0)), 0.0)
            rhs[u] = jnp.concatenate([vst * bcol[u], kst[u] * bcol[u] * jnp.exp(gcol[u])], axis=1)
        for u in units:
            a_mat[u] = jnp.where(strict, jnp.concatenate([bcol[u], bcol[u]], axis=1) * _mm_nt(kst[u], kst[u]) * decay[u], 0.0)

        t_inv = _unit_lower_inverse_many([a_mat[u] for u in units], ri, ci, C)
        sol = [_mm(t, rhs[u]) for t, u in zip(t_inv, units)]
        qk = [_mm_nt(qst[u], kst[u]) * decay[u] for u in units]
        ws = []
        for sl, u in zip(sol, units):
            b, grp = u
            q_dec = qst[u] * jnp.exp(gcol[u])
            ws.append([_mm(jnp.concatenate([sl[i * C:(i + 1) * C, HEAD_DIM:], q_dec[i * C:(i + 1) * C]], axis=0),
                           s_ref[b * B_V_HEADS + h]) for i, h in enumerate(heads_of(grp))])
        v_new = [sl[:, :HEAD_DIM] - jnp.concatenate([w[:C] for w in wl], axis=0) for sl, wl in zip(sol, ws)]
        outs = [jnp.concatenate([w[C:] for w in wl], axis=0) + _mm(q, v) for wl, q, v in zip(ws, qk, v_new)]
        for u, v, o in zip(units, v_new, outs):
            b, grp = u
            k_dec = kst[u] * jnp.exp(glcol[u] - gcol[u])
            on = o * lax.rsqrt(jnp.mean(o * o, -1, keepdims=True) + RMS_EPS) * og
            for i, h in enumerate(heads_of(grp)):
                rs = slice(i * C, (i + 1) * C)
                si = b * B_V_HEADS + h
                s_ref[si] = s_ref[si] * jnp.exp(glcol[u][i * C:i * C + 1, :]) + _mm_tn(k_dec[rs], v[rs])
                cols = slice(h * HEAD_DIM, (h + 1) * HEAD_DIM)
                o_ref[b, :, cols] = on[rs] * jax.nn.silu(z_ref[b, :, cols])


    @pl.when(c == pl.num_programs(0) - 1)
    def _fin():
        sfin_ref[...] = s_ref[...]


def _mixb_prompt(proj, bd, conv_w, nega, dtb, og, batch, seq):
    nc = seq // DN_CHUNK
    qkv_blk = 2 * A_WIDTH // B_CONV_CH
    z_blk = (2 * A_WIDTH + B_CONV_CH) // B_VAL_WIDTH
    proj3 = proj.reshape(batch, seq, proj.shape[-1])
    bd3 = bd.reshape(batch, seq, LANES)
    o, s_fin = pl.pallas_call(
        _mixb_prompt_kernel,
        grid=(nc,),
        in_specs=[pl.BlockSpec((batch, DN_CHUNK, B_CONV_CH), lambda c: (0, c, qkv_blk)),
                  pl.BlockSpec((batch, DN_CHUNK, B_VAL_WIDTH), lambda c: (0, c, z_blk)),
                  pl.BlockSpec((batch, DN_CHUNK, LANES), lambda c: (0, c, 0)),
                  pl.BlockSpec((B_CONV, B_CONV_CH), lambda c: (0, 0)),
                  pl.BlockSpec((1, LANES), lambda c: (0, 0)),
                  pl.BlockSpec((1, LANES), lambda c: (0, 0)),
                  pl.BlockSpec((1, HEAD_DIM), lambda c: (0, 0))],
        out_specs=[pl.BlockSpec((batch, DN_CHUNK, B_VAL_WIDTH), lambda c: (0, c, 0)),
                   pl.BlockSpec((batch * B_V_HEADS, HEAD_DIM, HEAD_DIM), lambda c: (0, 0, 0))],
        out_shape=[jax.ShapeDtypeStruct((batch, seq, B_VAL_WIDTH), F32),
                   jax.ShapeDtypeStruct((batch * B_V_HEADS, HEAD_DIM, HEAD_DIM), F32)],
        scratch_shapes=[pltpu.VMEM((batch * B_V_HEADS, HEAD_DIM, HEAD_DIM), F32),
                        pltpu.VMEM((batch, DN_CHUNK + 8, B_CONV_CH), F32)],
        compiler_params=_params("arbitrary"),
        name="mixb_prompt",
    )(proj3, proj3, bd3, conv_w, nega, dtb, og)
    return (o.reshape(batch * seq, B_VAL_WIDTH),
            s_fin.reshape(batch, B_V_HEADS, HEAD_DIM, HEAD_DIM))


def _mixb_sample_pre_kernel(qkv_ref, buf_ref, bd_ref, cw_ref, nega_ref, dtb_ref,
                            act_ref, beta_ref, g_ref, *, steps, nb_rows):
    cw = cw_ref[...]

    def slab(j):
        if j < B_CONV - 1:
            return buf_ref[j * nb_rows:(j + 1) * nb_rows, :]
        jj = j - (B_CONV - 1)
        return qkv_ref[jj * nb_rows:(jj + 1) * nb_rows, :]

    for t in range(steps):
        y = slab(t) * cw[0:1]
        for i in range(1, B_CONV):
            y = y + slab(t + i) * cw[i:i + 1]
        act = jax.nn.silu(y)
        rs = slice(t * nb_rows, (t + 1) * nb_rows)
        for qh in range(B_QK_HEADS):
            cq = slice(qh * HEAD_DIM, (qh + 1) * HEAD_DIM)
            ck = slice(B_KEY_WIDTH + qh * HEAD_DIM, B_KEY_WIDTH + (qh + 1) * HEAD_DIM)
            qs = act[:, cq]
            ks = act[:, ck]
            act_ref[rs, cq] = qs * lax.rsqrt(jnp.sum(qs * qs, -1, keepdims=True) + L2_EPS) * (HEAD_DIM ** -0.5)
            act_ref[rs, ck] = ks * lax.rsqrt(jnp.sum(ks * ks, -1, keepdims=True) + L2_EPS)
        act_ref[rs, 2 * B_KEY_WIDTH:] = act[:, 2 * B_KEY_WIDTH:]
    bd = bd_ref[...]
    beta_ref[...] = jax.nn.sigmoid(bd)
    g_ref[...] = nega_ref[...] * _softplus(bd + dtb_ref[...])


def _mixb_sample_pre(proj, buf, bd, conv_w, nega, dtb, steps, nb_rows):
    m = proj.shape[0]
    qkv_blk = 2 * A_WIDTH // B_CONV_CH
    kern = functools.partial(_mixb_sample_pre_kernel, steps=steps, nb_rows=nb_rows)
    return pl.pallas_call(
        kern,
        grid=(1,),
        in_specs=[pl.BlockSpec((m, B_CONV_CH), lambda i: (0, qkv_blk)),
                  pl.BlockSpec(buf.shape, lambda i: (0, 0)),
                  pl.BlockSpec((m, LANES), lambda i: (0, 0)),
                  pl.BlockSpec((B_CONV, B_CONV_CH), lambda i: (0, 0)),
                  pl.BlockSpec((1, LANES), lambda i: (0, 0)),
                  pl.BlockSpec((1, LANES), lambda i: (0, 0))],
        out_specs=[pl.BlockSpec((m, B_CONV_CH), lambda i: (0, 0)),
                   pl.BlockSpec((m, LANES), lambda i: (0, 0)),
                   pl.BlockSpec((m, LANES), lambda i: (0, 0))],
        out_shape=[jax.ShapeDtypeStruct((m, B_CONV_CH), F32),
                   jax.ShapeDtypeStruct((m, LANES), F32),
                   jax.ShapeDtypeStruct((m, LANES), F32)],
        compiler_params=_params("arbitrary"),
        name="mixb_sample_pre",
    )(proj, buf, bd, conv_w, nega, dtb)


def _mixb_sample_rec_kernel(kq_ref, v_ref, z_ref, beta_ref, g_ref, s0_ref, og_ref,
                            o_ref, s_out_ref, *, steps, pairs):
    og = og_ref[...]
    zpad = jnp.zeros((HEAD_DIM - 8, HEAD_DIM), F32)
    zrows = jnp.zeros((8 - steps, HEAD_DIM), F32)
    heads = range(B_V_HEADS)

    def body(bi, carry):
        kqs = [_mm(kq_ref[bi, h], s0_ref[bi, h]) for h in heads]
        pending = []
        for h in heads:
            kq = kq_ref[bi, h]
            g = g_ref[bi, h]
            beta = beta_ref[bi, h]
            v = v_ref[bi, h]
            gc = [g[0:1]]
            for t in range(1, steps):
                gc.append(gc[-1] + g[t:t + 1])
            k = [kq[t:t + 1] for t in range(steps)]
            q = [kq[4 + t:5 + t] for t in range(steps)]
            d = []
            for t in range(steps):
                acc = v[t:t + 1] - jnp.exp(gc[t]) * kqs[h][t:t + 1]
                for j in range(t):
                    kk = jnp.sum(k[j] * k[t], -1, keepdims=True)
                    acc = acc - jnp.exp(gc[t] - gc[j]) * kk * d[j]
                d.append(beta[t:t + 1] * acc)
            outs = []
            for t in range(steps):
                o = jnp.exp(gc[t]) * kqs[h][4 + t:5 + t]
                for j in range(t + 1):
                    qk = jnp.sum(k[j] * q[t], -1, keepdims=True)
                    o = o + jnp.exp(gc[t] - gc[j]) * qk * d[j]
                outs.append(o * lax.rsqrt(jnp.mean(o * o, -1, keepdims=True) + RMS_EPS) * og)
            o_ref[bi, h] = jnp.concatenate(outs + [zrows], axis=0) * jax.nn.silu(z_ref[bi, h])
            k_dec = jnp.concatenate([jnp.exp(gc[-1] - gc[j]) * k[j] for j in range(steps)] + [zrows], axis=0)
            k_pad = jnp.concatenate([k_dec, zpad], axis=0)
            d_pad = jnp.concatenate(d + [zrows, zpad], axis=0)
            pending.append((k_pad.T, d_pad, jnp.exp(gc[-1])))
        for h, (k_t, d_pad, decay_last) in zip(heads, pending):
            s_out_ref[bi, h] = s0_ref[bi, h] * decay_last + _mm(k_t, d_pad)
        return carry

    lax.fori_loop(0, pairs // B_V_HEADS, body, 0)


def _mixb_sample_rec(kq, v, z, beta, g, s0, og, steps, bb):
    nb = kq.shape[0]
    tile = pl.BlockSpec((bb, B_V_HEADS, 8, HEAD_DIM), lambda i: (i, 0, 0, 0))
    st = pl.BlockSpec((bb, B_V_HEADS, HEAD_DIM, HEAD_DIM), lambda i: (i, 0, 0, 0))
    kern = functools.partial(_mixb_sample_rec_kernel, steps=steps, pairs=bb * B_V_HEADS)
    return pl.pallas_call(
        kern,
        grid=(nb // bb,),
        in_specs=[tile, tile, tile, tile, tile, st, pl.BlockSpec((1, HEAD_DIM), lambda i: (0, 0))],
        out_specs=[tile, st],
        out_shape=[jax.ShapeDtypeStruct((nb, B_V_HEADS, 8, HEAD_DIM), F32),
                   jax.ShapeDtypeStruct((nb, B_V_HEADS, HEAD_DIM, HEAD_DIM), F32)],
        compiler_params=_params("arbitrary"),
        name="mixb_sample_rec",
    )(kq, v, z, beta, g, s0, og)


def _outproj_kernel(ap_ref, bp_ref, xp_ref, g1p_ref, sc2p_ref, sh2p_ref,
                    as_ref, bs_ref, xs_ref, g1s_ref, sc2s_ref, sh2s_ref,
                    w_ref, lg_ref, lb_ref, wr_ref, rb_ref, x1_ref, h2_ref, route_ref, count_ref,
                    base_ref, mix_ref, *, prompt_tiles):
    s = pl.program_id(0)

    @pl.when(s == 0)
    def _():
        base_ref[...] = jnp.zeros_like(base_ref)
        mix_ref[...] = jnp.zeros_like(mix_ref)

    cur_prompt = s < prompt_tiles
    a = jnp.where(cur_prompt, ap_ref[...], as_ref[...]).astype(BF16)
    mix_a = jnp.dot(a, w_ref[:A_WIDTH, :], preferred_element_type=F32)

    def project():
        b = jnp.where(cur_prompt, bp_ref[...], bs_ref[...]).astype(BF16)
        return mix_a + jnp.dot(b, w_ref[A_WIDTH:, :], preferred_element_type=F32)

    prev_prompt = s - 1 < prompt_tiles
    pick = lambda p_ref, s_ref: jnp.where(prev_prompt, p_ref[...], s_ref[...])
    mix = mix_ref[(s + 1) % 2]
    x1 = _layer_norm_rows(ALPHA * pick(xp_ref, xs_ref) + pick(g1p_ref, g1s_ref) * mix, lg_ref[...], lb_ref[...])
    x1_ref[...] = x1
    h2 = x1 * (1.0 + pick(sc2p_ref, sc2s_ref)) + pick(sh2p_ref, sh2s_ref)
    h2_ref[...] = _pack_bf16_pairs(h2)
    route_ref[...], mix_new = _route_tile(_mm3(h2, wr_ref[...]), rb_ref[...], base_ref, s >= 1, project)
    count_ref[...] = base_ref[...]
    mix_ref[s % 2] = mix_new


def _outproj(prompt, sample, mod_p, mod_s, w_out, lg, lb, w_router, r_bias, tm, seq):
    n_p = prompt[2].shape[0]
    n_s = sample[2].shape[0]
    pt = n_p // tm
    st = n_s // tm
    nt = pt + st
    m = n_p + n_s
    tiles_per_seq = seq // tm
    cur = lambda s: jnp.minimum(s, nt - 1)
    prev = lambda s: jnp.maximum(s - 1, 0)
    p_idx = lambda t: jnp.minimum(t, pt - 1)
    s_idx = lambda t: jnp.clip(t - pt, 0, st - 1)
    p_cur = lambda w: pl.BlockSpec((tm, w), lambda s: (p_idx(cur(s)), 0))
    s_cur = lambda w: pl.BlockSpec((tm, w), lambda s: (s_idx(cur(s)), 0))
    p_prev = lambda w: pl.BlockSpec((tm, w), lambda s: (p_idx(prev(s)), 0))
    s_prev = lambda w: pl.BlockSpec((tm, w), lambda s: (s_idx(prev(s)), 0))
    p_mod = lambda col: pl.BlockSpec((None, 1, D_MODEL), lambda s: (p_idx(prev(s)) // tiles_per_seq, 0, col))
    s_mod = lambda col: pl.BlockSpec((tm, D_MODEL), lambda s: (0, col))
    row = lambda w: pl.BlockSpec((tm, w), lambda s: (prev(s), 0))
    full = lambda shape: pl.BlockSpec(shape, lambda s: (0, 0))
    return pl.pallas_call(
        functools.partial(_outproj_kernel, prompt_tiles=pt),
        grid=(nt + 1,),
        in_specs=[p_cur(A_WIDTH), p_cur(B_VAL_WIDTH), p_prev(D_MODEL), p_mod(2), p_mod(4), p_mod(3),
                  s_cur(A_WIDTH), s_cur(B_VAL_WIDTH), s_prev(D_MODEL), s_mod(2), s_mod(4), s_mod(3),
                  full((D_MODEL, D_MODEL)), full((1, D_MODEL)), full((1, D_MODEL)),
                  full((D_MODEL, LANES)), full((1, LANES))],
        out_specs=[row(D_MODEL), row(D_MODEL // 2), row(LANES),
                   full((1, LANES))],
        out_shape=[jax.ShapeDtypeStruct((m, D_MODEL), F32),
                   jax.ShapeDtypeStruct((m, D_MODEL // 2), jnp.uint32),
                   jax.ShapeDtypeStruct((m, LANES), F32),
                   jax.ShapeDtypeStruct((1, LANES), F32)],
        scratch_shapes=[pltpu.VMEM((1, LANES), F32),
                        pltpu.VMEM((2, tm, D_MODEL), F32)],
        compiler_params=_params("arbitrary"),
        name="outproj",
    )(*prompt, mod_p, mod_p, mod_p, *sample, mod_s, mod_s, mod_s, w_out, lg, lb, w_router, r_bias)


def _route_tile(lg, bias, base_ref, valid, between):
    tm = lg.shape[0]
    lane = lax.broadcasted_iota(jnp.int32, lg.shape, 1)
    neg = -jnp.inf

    def first_argmax(score):
        mx = jnp.max(score, -1, keepdims=True)
        return jnp.min(jnp.where(score == mx, lane, LANES), -1, keepdims=True)

    def pick(vals, idx):
        return jnp.sum(jnp.where(lane == idx, vals, 0.0), -1, keepdims=True)

    gmask = lane < N_GROUPS
    mg = jnp.max(jnp.where(gmask, lg, neg), -1, keepdims=True)
    eg = jnp.where(gmask, jnp.exp(jnp.where(gmask, lg - mg, 0.0)), 0.0)
    pg = eg / jnp.sum(eg, -1, keepdims=True)
    sel_g = first_argmax(jnp.where(gmask, lg + bias, neg))
    p_sel = pick(pg, sel_g)

    lo = N_GROUPS + sel_g * EXPERTS_PER_GROUP
    emask = (lane >= lo) & (lane < lo + EXPERTS_PER_GROUP)
    me = jnp.max(jnp.where(emask, lg, neg), -1, keepdims=True)
    ee = jnp.where(emask, jnp.exp(jnp.where(emask, lg - me, 0.0)), 0.0)
    pe = ee / jnp.sum(ee, -1, keepdims=True)
    score = jnp.where(emask, pe + bias, neg)
    i1 = first_argmax(score)
    i2 = first_argmax(jnp.where(lane == i1, neg, score))
    w1 = pick(pe, i1)
    w2 = pick(pe, i2)
    wsum = w1 + w2
    gate1 = w1 / wsum * p_sel
    gate2 = w2 / wsum * p_sel

    hot = ((lane == i1) | (lane == i2)).astype(BF16)
    extra = between()
    ri = lax.broadcasted_iota(jnp.int32, (tm, tm), 0)
    ci = lax.broadcasted_iota(jnp.int32, (tm, tm), 1)
    before = jnp.dot((ri > ci).astype(BF16), hot, preferred_element_type=F32) + base_ref[...]
    rank1 = pick(before, i1)
    rank2 = pick(before, i2)
    base_ref[...] = jnp.where(valid, base_ref[...] + jnp.sum(hot.astype(F32), 0, keepdims=True), base_ref[...])

    out = jnp.where(lane == 0, (i1 - N_GROUPS).astype(F32), 0.0)
    out = jnp.where(lane == 1, (i2 - N_GROUPS).astype(F32), out)
    out = jnp.where(lane == 2, gate1, out)
    out = jnp.where(lane == 3, gate2, out)
    out = jnp.where(lane == 4, rank1, out)
    out = jnp.where(lane == 5, rank2, out)
    return out, extra


DISPATCH_ROWS = 512


def _dispatch_kernel(dest_ref, last_ref, nv_ref, h_ref, xb_ref, zbuf, zsem, ssem, *, n_tok, n_blocks):
    i = pl.program_id(0)
    nv = nv_ref[0]

    def zero_block(blk):
        return pltpu.make_async_copy(zbuf, xb_ref.at[pl.ds(blk * MOE_ROWS, MOE_ROWS)], zsem)

    @pl.when(i == 0)
    def _():
        zbuf[...] = jnp.zeros_like(zbuf)
        for e in range(N_EXPERTS):
            @pl.when(last_ref[e] >= 0)
            def _():
                zero_block(last_ref[e]).start()

        def tail_start(b, carry):
            zero_block(b).start()
            return carry

        lax.fori_loop(nv, n_blocks, tail_start, 0)
        for e in range(N_EXPERTS):
            @pl.when(last_ref[e] >= 0)
            def _():
                zero_block(0).wait()

        def tail_wait(b, carry):
            zero_block(0).wait()
            return carry

        lax.fori_loop(nv, n_blocks, tail_wait, 0)

    def row_copy(r, choice):
        slot = dest_ref[choice * n_tok + i * DISPATCH_ROWS + r]
        return pltpu.make_async_copy(h_ref.at[pl.ds(r, 1)], xb_ref.at[pl.ds(slot, 1)], ssem)

    def body(r, carry):
        row_copy(r, 0).start()
        row_copy(r, 1).start()
        return carry

    lax.fori_loop(0, DISPATCH_ROWS, body, 0, unroll=8)
    for _ in range(2):
        pltpu.make_async_copy(h_ref, xb_ref.at[pl.ds(0, DISPATCH_ROWS)], ssem).wait()


def _dispatch(dest_flat, last_block, n_valid, h2, n_blocks):
    n_tok = h2.shape[0]
    kern = functools.partial(_dispatch_kernel, n_tok=n_tok, n_blocks=n_blocks)
    return pl.pallas_call(
        kern,
        grid_spec=pltpu.PrefetchScalarGridSpec(
            num_scalar_prefetch=3,
            grid=(n_tok // DISPATCH_ROWS,),
            in_specs=[pl.BlockSpec((DISPATCH_ROWS, D_MODEL // 2), lambda i, *_: (i, 0))],
            out_specs=pl.BlockSpec(memory_space=pl.ANY),
            scratch_shapes=[pltpu.VMEM((MOE_ROWS, D_MODEL // 2), jnp.uint32),
                            pltpu.SemaphoreType.DMA(()),
                            pltpu.SemaphoreType.DMA(())]),
        out_shape=jax.ShapeDtypeStruct((n_blocks * MOE_ROWS, D_MODEL // 2), jnp.uint32),
        compiler_params=_params("arbitrary"),
        name="dispatch",
    )(dest_flat, last_block, n_valid, h2)


def _expert_kernel(be_ref, nv_ref, first_ref, next_ref, wslot_ref,
                   x_ref, wg_ref, wu_ref, wd_ref, y_ref,
                   wg_st, wu_st, wd_st, wg_s, wu_s, wd_s, wsem):
    i = pl.program_id(0)
    nv = nv_ref[0]

    def weight_copies(e, p):
        return [pltpu.make_async_copy(src.at[e], dst.at[p], wsem.at[p])
                for src, dst in ((wg_ref, wg_st), (wu_ref, wu_st), (wd_ref, wd_st))]

    @pl.when(i == 0)
    def _():
        for c in weight_copies(be_ref[0], 0):
            c.start()

    @pl.when((i < nv) & (first_ref[i] == 1))
    def _():
        p = wslot_ref[i]
        for c in weight_copies(be_ref[i], p):
            c.wait()
        wg_s[...] = wg_st[p].astype(BF16)
        wu_s[...] = wu_st[p].astype(BF16)
        wd_s[...] = wd_st[p].astype(BF16)

        @pl.when(next_ref[i] >= 0)
        def _():
            for c in weight_copies(next_ref[i], 1 - p):
                c.start()

    @pl.when(i < nv)
    def _():
        x = _unpack_bf16_pairs(x_ref[...])
        hg = jnp.dot(x, wg_s[...], preferred_element_type=F32)
        hu = jnp.dot(x, wu_s[...], preferred_element_type=F32)
        hid = (jax.nn.silu(hg) * hu).astype(BF16)
        y_ref[...] = jnp.dot(hid, wd_s[...], preferred_element_type=F32)

    @pl.when(i >= nv)
    def _():
        y_ref[...] = jnp.zeros_like(y_ref)


def _experts(block_e, n_valid, xb, w_gate, w_up, w_down):
    n_blocks = block_e.shape[0]
    idx = jnp.arange(n_blocks, dtype=jnp.int32)
    first = (idx < n_valid[0]) & ((idx == 0) | (block_e != jnp.roll(block_e, 1)))
    wslot = (jnp.cumsum(first.astype(jnp.int32)) - 1) % 2
    first_at = jnp.where(first, idx, n_blocks)
    next_first = jnp.concatenate([lax.cummin(first_at, reverse=True)[1:], jnp.full((1,), n_blocks, jnp.int32)])
    next_e = jnp.where(next_first < n_blocks, block_e[jnp.minimum(next_first, n_blocks - 1)], -1)
    any_spec = pl.BlockSpec(memory_space=pl.ANY)
    rows = pl.BlockSpec((MOE_ROWS, D_MODEL), lambda i, *_: (i, 0))
    return pl.pallas_call(
        _expert_kernel,
        grid_spec=pltpu.PrefetchScalarGridSpec(
            num_scalar_prefetch=5,
            grid=(n_blocks,),
            in_specs=[pl.BlockSpec((MOE_ROWS, D_MODEL // 2), lambda i, *_: (i, 0)), any_spec, any_spec, any_spec],
            out_specs=rows,
            scratch_shapes=[pltpu.VMEM((2, D_MODEL, D_EXPERT), F32),
                            pltpu.VMEM((2, D_MODEL, D_EXPERT), F32),
                            pltpu.VMEM((2, D_EXPERT, D_MODEL), F32),
                            pltpu.VMEM((D_MODEL, D_EXPERT), BF16),
                            pltpu.VMEM((D_MODEL, D_EXPERT), BF16),
                            pltpu.VMEM((D_EXPERT, D_MODEL), BF16),
                            pltpu.SemaphoreType.DMA((2,))]),
        out_shape=jax.ShapeDtypeStruct((n_blocks * MOE_ROWS, D_MODEL), F32),
        compiler_params=_params("arbitrary"),
        name="experts",
    )(block_e, n_valid, first.astype(jnp.int32), next_e.astype(jnp.int32), wslot.astype(jnp.int32),
      xb, w_gate, w_up, w_down)


def _final_kernel(dest_ref, x1_ref, route_ref, g2_ref, lg_ref, lb_ref, yb_ref, o_ref, ybuf, sem,
                  *, row0, n_tok):
    i = pl.program_id(0)
    last = pl.num_programs(0) - 1
    tm = o_ref.shape[0]
    slot = i % 2

    def row_copy(tile, r, choice, s):
        src = dest_ref[choice * n_tok + row0 + tile * tm + r]
        return pltpu.make_async_copy(yb_ref.at[pl.ds(src, 1)], ybuf.at[s, choice, pl.ds(r, 1)], sem.at[s])

    def tile_wait(s):
        for choice in range(2):
            pltpu.make_async_copy(yb_ref.at[pl.ds(0, tm)], ybuf.at[s, choice], sem.at[s]).wait()

    @pl.when(i == 0)
    def _():
        def body(r, carry):
            row_copy(0, r, 0, 0).start()
            row_copy(0, r, 1, 0).start()
            return carry

        lax.fori_loop(0, tm, body, 0, unroll=8)

    tile_wait(slot)
    nxt = jnp.minimum(i + 1, last)
    for r in range(tm):
        row_copy(nxt, r, 0, 1 - slot).start()
        row_copy(nxt, r, 1, 1 - slot).start()
    route = route_ref[...]
    ff = ybuf[slot, 0] * route[:, 2:3] + ybuf[slot, 1] * route[:, 3:4]
    o_ref[...] = _layer_norm_rows(ALPHA * x1_ref[...] + g2_ref[...] * ff, lg_ref[...], lb_ref[...])

    @pl.when(i == last)
    def _():
        tile_wait(1 - slot)


def _final(dest_flat, x1, yb, route, mod, lg, lb, tm, rows_per_mod, row0, n_rows):
    n_tok = x1.shape[0]
    off = row0 // tm
    return pl.pallas_call(
        functools.partial(_final_kernel, row0=row0, n_tok=n_tok),
        grid_spec=pltpu.PrefetchScalarGridSpec(
            num_scalar_prefetch=1,
            grid=(n_rows // tm,),
            in_specs=[pl.BlockSpec((tm, D_MODEL), lambda i, *_: (off + i, 0)),
                      pl.BlockSpec((tm, LANES), lambda i, *_: (off + i, 0)),
                      _mod_spec(mod, 5, tm, rows_per_mod),
                      pl.BlockSpec((1, D_MODEL), lambda i, *_: (0, 0)),
                      pl.BlockSpec((1, D_MODEL), lambda i, *_: (0, 0)),
                      pl.BlockSpec(memory_space=pl.ANY)],
            out_specs=pl.BlockSpec((tm, D_MODEL), lambda i, *_: (i, 0)),
            scratch_shapes=[pltpu.VMEM((2, 2, tm, D_MODEL), F32),
                            pltpu.SemaphoreType.DMA((2,))]),
        out_shape=jax.ShapeDtypeStruct((n_rows, D_MODEL), F32),
        compiler_params=_params("arbitrary"),
        name="final",
    )(dest_flat, x1, route, mod, lg, lb, yb)


def _lane_row(vec, offset):
    return jnp.zeros((1, LANES), F32).at[0, offset:offset + vec.shape[0]].set(vec.astype(F32))


def kernel(x_prompt, x_sample, state_conv, state_ssm, c_prompt, c_sample, w_ada, b_ada, w_in, a_ws, a_bs, a_norm_g, a_norm_b, b_conv_w, b_a_log, b_dt_bias, b_onorm_g, w_out, ln1_g, ln1_b, w_router_g, b_router_g, w_router_e, b_router_e, w_gate, w_up, w_down, ln2_g, ln2_b):
    batch, seq, d = x_prompt.shape
    nb, steps, _ = x_sample.shape
    n_p = batch * seq
    n_s = nb * steps
    n_tok = n_p + n_s
    l = 0

    c_rows = batch + nb
    c_pad = (-c_rows) % 8
    c_all = jnp.concatenate([c_prompt, c_sample, jnp.zeros((c_pad, d), F32)], axis=0)
    mod = _ada(c_all, w_ada[l], b_ada[l].reshape(1, -1))
    mod_p = mod[:batch].reshape(batch, 1, 6 * d)
    mod_s = mod[batch:batch + nb]

    w_in_t = jnp.swapaxes(w_in[l], 0, 1)
    w_bd = jnp.pad(w_in_t[MAIN_WIDTH:], ((0, LANES - 2 * B_V_HEADS), (0, 0))).astype(BF16)
    xp = x_prompt.reshape(n_p, d)
    xs = jnp.swapaxes(x_sample, 0, 1).reshape(n_s, d)
    proj_p, bd_p, proj_s, bd_s = _inproj(xp, xs, mod_p, mod_s, w_in_t, w_bd, 512, seq)

    ng = a_norm_g[l].reshape(1, -1)
    nbias = a_norm_b[l].reshape(1, -1)
    bias_tile = jnp.repeat(a_bs[l].T, HEAD_DIM, axis=1)
    a_out_p = _mixa_prompt(proj_p, a_ws[l], bias_tile, ng, nbias, 512)
    coef = jnp.repeat(jnp.transpose(a_ws[l][:, :steps, :steps], (1, 2, 0)).reshape(steps * steps, A_HEADS),
                      HEAD_DIM, axis=1)
    a_out_s, chunkv = _mixa_sample(proj_s, coef, bias_tile[:steps], ng, nbias, steps, nb)

    nega = _lane_row(-jnp.exp(b_a_log[l].astype(F32)), B_V_HEADS)
    dtb = _lane_row(b_dt_bias[l], B_V_HEADS)
    og = b_onorm_g[l].reshape(1, -1)
    b_out_p, ssm_p = _mixb_prompt(proj_p, bd_p, b_conv_w[l], nega, dtb, og, batch, seq)
    qkv0 = 2 * A_WIDTH
    conv_p = proj_p.reshape(batch, seq, MAIN_WIDTH)[:, seq - (B_CONV - 1):, qkv0:qkv0 + B_CONV_CH]

    buf_s = jnp.swapaxes(state_conv[l], 0, 1).reshape((B_CONV - 1) * nb, B_CONV_CH)
    act_s, beta_s, g_s = _mixb_sample_pre(proj_s, buf_s, bd_s, b_conv_w[l], nega, dtb, steps, nb)
    rep = B_V_HEADS // B_QK_HEADS
    act4 = act_s.reshape(steps, nb, B_CONV_CH)
    q8 = jnp.repeat(act4[..., :B_KEY_WIDTH].reshape(steps, nb, B_QK_HEADS, HEAD_DIM), rep, axis=2)
    k8 = jnp.repeat(act4[..., B_KEY_WIDTH:2 * B_KEY_WIDTH].reshape(steps, nb, B_QK_HEADS, HEAD_DIM), rep, axis=2)

    def to_tiles(a):
        a = jnp.transpose(a, (1, 2, 0, 3))
        return jnp.pad(a, ((0, 0), (0, 0), (0, 8 - a.shape[2]), (0, 0)))

    kq_t = jnp.concatenate([to_tiles(k8)[:, :, :4], to_tiles(q8)[:, :, :4]], axis=2)
    v_t = to_tiles(act4[..., 2 * B_KEY_WIDTH:].reshape(steps, nb, B_V_HEADS, HEAD_DIM))
    z_t = to_tiles(proj_s[:, qkv0 + B_CONV_CH:].reshape(steps, nb, B_V_HEADS, HEAD_DIM))

    def gate_tiles(a, off):
        a = a[:, off:off + B_V_HEADS].reshape(steps, nb, B_V_HEADS, 1)
        return to_tiles(jnp.broadcast_to(a, (steps, nb, B_V_HEADS, HEAD_DIM)))

    o_t, ssm_s = _mixb_sample_rec(kq_t, v_t, z_t, gate_tiles(beta_s, 0), gate_tiles(g_s, B_V_HEADS),
                                  state_ssm[l], og, steps, 8)
    b_out_s = jnp.transpose(o_t[:, :, :steps], (2, 0, 1, 3)).reshape(n_s, B_VAL_WIDTH)
    conv_s = jnp.swapaxes(proj_s.reshape(steps, nb, MAIN_WIDTH)[steps - (B_CONV - 1):, :, qkv0:qkv0 + B_CONV_CH], 0, 1)

    w_out_b = w_out[l].astype(BF16)
    lg1 = ln1_g[l].reshape(1, -1)
    lb1 = ln1_b[l].reshape(1, -1)
    w_router = jnp.concatenate([w_router_g[l], w_router_e[l],
                                jnp.zeros((d, LANES - N_GROUPS - N_EXPERTS), F32)], axis=1)
    out_tm = 256
    mod_s_tile = jnp.tile(mod_s, (out_tm // nb, 1))
    r_bias = jnp.concatenate([b_router_g[l], b_router_e[l],
                              jnp.zeros((LANES - N_GROUPS - N_EXPERTS,), F32)]).reshape(1, LANES)
    x1, h2, route, counts_row = _outproj((a_out_p, b_out_p, xp), (a_out_s, b_out_s, xs), mod_p, mod_s_tile,
                                         w_out_b, lg1, lb1, w_router, r_bias, out_tm, seq)

    counts = counts_row[0, N_GROUPS:N_GROUPS + N_EXPERTS].astype(jnp.int32)
    route_t = route[:, :8].T
    eid = route_t[0:2].astype(jnp.int32)
    rank = route_t[4:6].astype(jnp.int32)
    padded = (counts + MOE_ROWS - 1) // MOE_ROWS * MOE_ROWS
    pend = jnp.cumsum(padded)
    pstart = pend - padded
    dest = rank + jnp.sum(jnp.where(eid[None] == jnp.arange(N_EXPERTS, dtype=jnp.int32)[:, None, None],
                                    pstart[:, None, None], 0), axis=0)
    n_blocks = -(-(2 * n_tok) // MOE_ROWS) + N_EXPERTS
    dest_flat = dest.reshape(-1)
    last_block = jnp.where(padded > 0, pend // MOE_ROWS - 1, -1).astype(jnp.int32)
    blk0 = jnp.arange(n_blocks, dtype=jnp.int32) * MOE_ROWS
    block_e = jnp.minimum(jnp.sum(pend[None, :] <= blk0[:, None], axis=1), N_EXPERTS - 1).astype(jnp.int32)
    n_valid = (pend[-1:] // MOE_ROWS).astype(jnp.int32)

    xb = _dispatch(dest_flat, last_block, n_valid, h2, n_blocks)
    yb = _experts(block_e, n_valid, xb, w_gate[l], w_up[l], w_down[l])

    lg2 = ln2_g[l].reshape(1, -1)
    lb2 = ln2_b[l].reshape(1, -1)
    y_p = _final(dest_flat, x1, yb, route, mod_p, lg2, lb2, 512, seq, 0, n_p)
    y_s = _final(dest_flat, x1, yb, route, mod_s, lg2, lb2, nb, 0, n_p, n_s)

    y_prompt = y_p.reshape(batch, seq, d)
    y_sample = jnp.swapaxes(y_s.reshape(steps, nb, d), 0, 1)
    chunkv_s = jnp.swapaxes(chunkv.reshape(steps, nb, A_HEADS, HEAD_DIM), 0, 1)
    return (y_prompt, y_sample, conv_p[None], ssm_p[None], conv_s[None], ssm_s[None], chunkv_s[None])
```
